```python
import math
import jax
import jax.numpy as jnp
from jax import lax
import numpy as np

D_MODEL = 1024
BATCH = 32
SEQ = 2048
DEPTH = 4

HEAD_DIM = 64
N_Q_HEADS = 8
N_KV_HEADS = 2
GROUP = N_Q_HEADS // N_KV_HEADS
WINDOW = 128
ATTN_BLOCK = WINDOW
Q_WIDTH = N_Q_HEADS * HEAD_DIM
KV_WIDTH = N_KV_HEADS * HEAD_DIM
NUM_BUCKETS = 32
MAX_DISTANCE = 128
CONV_CHANNELS = D_MODEL // 2
CONV_WIDTH = 31
EVEN_IN = Q_WIDTH + 2 * KV_WIDTH + 2 * CONV_CHANNELS
EVEN_CAT = Q_WIDTH + CONV_CHANNELS
LRU_WIDTH = D_MODEL
LRU_HEADS = 8
LRU_BLOCK = LRU_WIDTH // LRU_HEADS
LRU_CONV_WIDTH = 4
RG_LRU_C = 8.0
D_FF = 2816
RMS_EPS = 1e-6
LN_EPS = 1e-5
NEG_INF = -1e30
N_EVEN = (DEPTH + 1) // 2
N_ODD = DEPTH // 2

kernel_name = "hybrid_swa_conformer_rglru_macaron"


def _rmsnorm(x, g):
    xf = x.astype(jnp.float32)
    y = xf * lax.rsqrt(jnp.mean(xf * xf, axis=-1, keepdims=True) + RMS_EPS)
    return (y * g.astype(jnp.float32)).astype(x.dtype)


def _layernorm(x, g, b):
    xf = x.astype(jnp.float32)
    mu = jnp.mean(xf, axis=-1, keepdims=True)
    var = jnp.mean(jnp.square(xf - mu), axis=-1, keepdims=True)
    y = (xf - mu) * lax.rsqrt(var + LN_EPS)
    return (y * g.astype(jnp.float32) + b.astype(jnp.float32)).astype(x.dtype)


def _swiglu(x, wg, wu, wd):
    return (jax.nn.silu(x @ wg) * (x @ wu)) @ wd


def _causal_depthwise_conv(x, w, b):
    k_width, chans = w.shape
    y = lax.conv_general_dilated(
        x, w[:, None, :].astype(x.dtype), window_strides=(1,),
        padding=[(k_width - 1, 0)], dimension_numbers=("NWC", "WIO", "NWC"),
        feature_group_count=chans)
    return y + b.astype(x.dtype)


def _t5_bucket(dist):
    n = jnp.maximum(dist, 0)
    max_exact = NUM_BUCKETS // 2
    nf = jnp.maximum(n, max_exact).astype(jnp.float32)
    large = max_exact + (jnp.log(nf / max_exact) / math.log(MAX_DISTANCE / max_exact)
                         * (NUM_BUCKETS - max_exact)).astype(jnp.int32)
    large = jnp.minimum(large, NUM_BUCKETS - 1)
    return jnp.where(n < max_exact, n, large)


def _swa_sink_attention(q, k, v, sinks, rel_bias):
    bsz, seq = q.shape[:2]
    nb = seq // ATTN_BLOCK
    qb = q.reshape(bsz, nb, ATTN_BLOCK, N_KV_HEADS, GROUP, HEAD_DIM)
    kb = k.reshape(bsz, nb, ATTN_BLOCK, N_KV_HEADS, HEAD_DIM)
    vb = v.reshape(bsz, nb, ATTN_BLOCK, N_KV_HEADS, HEAD_DIM)
    kk = jnp.concatenate([jnp.concatenate([jnp.zeros_like(kb[:, :1]), kb[:, :-1]], axis=1), kb], axis=2)
    vv = jnp.concatenate([jnp.concatenate([jnp.zeros_like(vb[:, :1]), vb[:, :-1]], axis=1), vb], axis=2)
    scores = jnp.einsum("bnqhgd,bnshd->bnhgqs", qb, kk,
                        preferred_element_type=jnp.float32) * (1.0 / math.sqrt(HEAD_DIM))
    qi = jnp.arange(ATTN_BLOCK)[:, None]
    sj = jnp.arange(2 * ATTN_BLOCK)[None, :]
    dist = qi + ATTN_BLOCK - sj
    bias = rel_bias.astype(jnp.float32)[_t5_bucket(dist)]
    bias = jnp.transpose(bias, (2, 0, 1)).reshape(N_KV_HEADS, GROUP, ATTN_BLOCK, 2 * ATTN_BLOCK)
    in_window = (dist >= 0) & (dist < WINDOW)
    key_pos = jnp.arange(nb)[:, None, None] * ATTN_BLOCK + sj[None] - ATTN_BLOCK
    mask = in_window[None] & (key_pos >= 0)
    scores = jnp.where(mask[None, :, None, None], scores + bias[None, None], NEG_INF)
    sink = sinks.astype(jnp.float32).reshape(N_KV_HEADS, GROUP)[None, None, :, :, None, None]
    sink = jnp.broadcast_to(sink, scores.shape[:-1] + (1,))
    probs = jax.nn.softmax(jnp.concatenate([scores, sink], axis=-1), axis=-1)[..., :-1]
    out = jnp.einsum("bnhgqs,bnshd->bnqhgd", probs.astype(v.dtype), vv)
    return out.reshape(bsz, seq, Q_WIDTH)


def _attn_conv_mixer(h, w_in, sinks, conv_w, conv_b, ln_g, ln_b, w_out, rel_bias):
    bsz, seq, _ = h.shape
    u = h @ w_in
    o1 = Q_WIDTH
    o2 = o1 + KV_WIDTH
    o3 = o2 + KV_WIDTH
    o4 = o3 + CONV_CHANNELS
    q = u[..., :o1].reshape(bsz, seq, N_Q_HEADS, HEAD_DIM)
    k = u[..., o1:o2].reshape(bsz, seq, N_KV_HEADS, HEAD_DIM)
    v = u[..., o2:o3].reshape(bsz, seq, N_KV_HEADS, HEAD_DIM)
    attn = _swa_sink_attention(q, k, v, sinks, rel_bias)
    glu = u[..., o3:o4] * jax.nn.sigmoid(u[..., o4:])
    c = jax.nn.silu(_layernorm(_causal_depthwise_conv(glu, conv_w, conv_b), ln_g, ln_b))
    return jnp.concatenate([attn, c], axis=-1) @ w_out


def _rg_lru(x, ga_w, ga_b, gx_w, gx_b, lam):
    bsz, seq, width = x.shape
    xh = x.reshape(bsz, seq, LRU_HEADS, LRU_BLOCK)
    r = jax.nn.sigmoid(jnp.einsum("bshi,hij->bshj", xh, ga_w).reshape(bsz, seq, width) + ga_b)
    i = jax.nn.sigmoid(jnp.einsum("bshi,hij->bshj", xh, gx_w).reshape(bsz, seq, width) + gx_b)
    log_a = RG_LRU_C * r.astype(jnp.float32) * jax.nn.log_sigmoid(lam.astype(jnp.float32))
    a = jnp.exp(log_a)
    bx = jnp.sqrt(-jnp.expm1(2.0 * log_a)) * (i * x).astype(jnp.float32)

    def combine(left, right):
        a1, b1 = left
        a2, b2 = right
        return a1 * a2, a2 * b1 + b2

    _, hs = lax.associative_scan(combine, (a, bx), axis=1)
    return hs.astype(x.dtype)


def _recurrent_mixer(h, w_in, conv_w, conv_b, ga_w, ga_b, gx_w, gx_b, lam, w_out):
    u = h @ w_in
    gate = jax.nn.gelu(u[..., :LRU_WIDTH])
    rec = _causal_depthwise_conv(u[..., LRU_WIDTH:], conv_w, conv_b)
    rec = _rg_lru(rec, ga_w, ga_b, gx_w, gx_b, lam)
    return (gate * rec) @ w_out


def _fwd_setup_inputs(seed: int = 0) -> dict:
    key = jax.random.key(seed)
    ks = iter(jax.random.split(key, 40))

    def nrm(shape, scale):
        return jax.random.normal(next(ks), shape, jnp.float32) * scale

    def gain(shape):
        return 1.0 + nrm(shape, 0.02)

    x = nrm((BATCH, SEQ, D_MODEL), 1.0)
    u = jax.random.uniform(next(ks), (N_ODD, LRU_WIDTH), jnp.float32, 0.9, 0.999)
    a0 = u ** (1.0 / RG_LRU_C)
    lru_lambda = jnp.log(a0) - jnp.log1p(-a0)
    return {
        "x": x,
        "norm_ffn1": gain((DEPTH, D_MODEL)),
        "ffn1_wg": nrm((DEPTH, D_MODEL, D_FF), D_MODEL ** -0.5),
        "ffn1_wu": nrm((DEPTH, D_MODEL, D_FF), D_MODEL ** -0.5),
        "ffn1_wd": nrm((DEPTH, D_FF, D_MODEL), D_FF ** -0.5),
        "norm_mix": gain((DEPTH, D_MODEL)),
        "norm_ffn2": gain((DEPTH, D_MODEL)),
        "ffn2_wg": nrm((DEPTH, D_MODEL, D_FF), D_MODEL ** -0.5),
        "ffn2_wu": nrm((DEPTH, D_MODEL, D_FF), D_MODEL ** -0.5),
        "ffn2_wd": nrm((DEPTH, D_FF, D_MODEL), D_FF ** -0.5),
        "rel_bias": nrm((NUM_BUCKETS, N_Q_HEADS), 0.3),
        "even_w_in": nrm((N_EVEN, D_MODEL, EVEN_IN), D_MODEL ** -0.5),
        "attn_sinks": nrm((N_EVEN, N_Q_HEADS), 1.0),
        "conv_b_w": nrm((N_EVEN, CONV_WIDTH, CONV_CHANNELS), CONV_WIDTH ** -0.5),
        "conv_b_b": nrm((N_EVEN, CONV_CHANNELS), 0.02),
        "conv_ln_g": gain((N_EVEN, CONV_CHANNELS)),
        "conv_ln_b": nrm((N_EVEN, CONV_CHANNELS), 0.02),
        "even_w_out": nrm((N_EVEN, EVEN_CAT, D_MODEL), EVEN_CAT ** -0.5),
        "odd_w_in": nrm((N_ODD, D_MODEL, 2 * LRU_WIDTH), D_MODEL ** -0.5),
        "lru_conv_w": nrm((N_ODD, LRU_CONV_WIDTH, LRU_WIDTH), LRU_CONV_WIDTH ** -0.5),
        "lru_conv_b": nrm((N_ODD, LRU_WIDTH), 0.02),
        "gate_a_w": nrm((N_ODD, LRU_HEADS, LRU_BLOCK, LRU_BLOCK), LRU_BLOCK ** -0.5),
        "gate_a_b": nrm((N_ODD, LRU_WIDTH), 0.02),
        "gate_x_w": nrm((N_ODD, LRU_HEADS, LRU_BLOCK, LRU_BLOCK), LRU_BLOCK ** -0.5),
        "gate_x_b": nrm((N_ODD, LRU_WIDTH), 0.02),
        "lru_lambda": lru_lambda,
        "odd_w_out": nrm((N_ODD, LRU_WIDTH, D_MODEL), LRU_WIDTH ** -0.5),
        "norm_final": gain((D_MODEL,)),
    }


def _fwd_reference(x, norm_ffn1, ffn1_wg, ffn1_wu, ffn1_wd, norm_mix, norm_ffn2, ffn2_wg, ffn2_wu, ffn2_wd,
              rel_bias, even_w_in, attn_sinks, conv_b_w, conv_b_b, conv_ln_g, conv_ln_b, even_w_out,
              odd_w_in, lru_conv_w, lru_conv_b, gate_a_w, gate_a_b, gate_x_w, gate_x_b, lru_lambda,
              odd_w_out, norm_final):
    h = x
    for layer in range(DEPTH):
        h = h + 0.5 * _swiglu(_rmsnorm(h, norm_ffn1[layer]), ffn1_wg[layer], ffn1_wu[layer], ffn1_wd[layer])
        hn = _rmsnorm(h, norm_mix[layer])
        if layer % 2 == 0:
            e = layer // 2
            h = h + _attn_conv_mixer(hn, even_w_in[e], attn_sinks[e], conv_b_w[e], conv_b_b[e],
                                     conv_ln_g[e], conv_ln_b[e], even_w_out[e], rel_bias)
        else:
            o = layer // 2
            h = h + _recurrent_mixer(hn, odd_w_in[o], lru_conv_w[o], lru_conv_b[o], gate_a_w[o],
                                     gate_a_b[o], gate_x_w[o], gate_x_b[o], lru_lambda[o], odd_w_out[o])
        h = h + 0.5 * _swiglu(_rmsnorm(h, norm_ffn2[layer]), ffn2_wg[layer], ffn2_wu[layer], ffn2_wd[layer])
    return _rmsnorm(h, norm_final)


import jax as _jax
import jax.numpy as _jnp

TWIN_FORMAT = 'train_step'
FWD_PARAMS = ['x', 'norm_ffn1', 'ffn1_wg', 'ffn1_wu', 'ffn1_wd', 'norm_mix', 'norm_ffn2', 'ffn2_wg', 'ffn2_wu', 'ffn2_wd', 'rel_bias', 'even_w_in', 'attn_sinks', 'conv_b_w', 'conv_b_b', 'conv_ln_g', 'conv_ln_b', 'even_w_out', 'odd_w_in', 'lru_conv_w', 'lru_conv_b', 'gate_a_w', 'gate_a_b', 'gate_x_w', 'gate_x_b', 'lru_lambda', 'odd_w_out', 'norm_final']
TWIN_WEIGHTS = ['norm_ffn1', 'ffn1_wg', 'ffn1_wu', 'ffn1_wd', 'norm_mix', 'norm_ffn2', 'ffn2_wg', 'ffn2_wu', 'ffn2_wd', 'rel_bias', 'even_w_in', 'attn_sinks', 'conv_b_w', 'conv_b_b', 'conv_ln_g', 'conv_ln_b', 'even_w_out', 'odd_w_in', 'lru_conv_w', 'lru_conv_b', 'gate_a_w', 'gate_a_b', 'gate_x_w', 'gate_x_b', 'lru_lambda', 'odd_w_out', 'norm_final']
TWIN_DIFF_INPUT = 'x'
TWIN_INPUTS = ['x', 'norm_ffn1', 'ffn1_wg', 'ffn1_wu', 'ffn1_wd', 'norm_mix', 'norm_ffn2', 'ffn2_wg', 'ffn2_wu', 'ffn2_wd', 'rel_bias', 'even_w_in', 'attn_sinks', 'conv_b_w', 'conv_b_b', 'conv_ln_g', 'conv_ln_b', 'even_w_out', 'odd_w_in', 'lru_conv_w', 'lru_conv_b', 'gate_a_w', 'gate_a_b', 'gate_x_w', 'gate_x_b', 'lru_lambda', 'odd_w_out', 'norm_final', 'loss_target', 'm_norm_ffn1', 'm_ffn1_wg', 'm_ffn1_wu', 'm_ffn1_wd', 'm_norm_mix', 'm_norm_ffn2', 'm_ffn2_wg', 'm_ffn2_wu', 'm_ffn2_wd', 'm_rel_bias', 'm_even_w_in', 'm_attn_sinks', 'm_conv_b_w', 'm_conv_b_b', 'm_conv_ln_g', 'm_conv_ln_b', 'm_even_w_out', 'm_odd_w_in', 'm_lru_conv_w', 'm_lru_conv_b', 'm_gate_a_w', 'm_gate_a_b', 'm_gate_x_w', 'm_gate_x_b', 'm_lru_lambda', 'm_odd_w_out', 'm_norm_final', 'v_norm_ffn1', 'v_ffn1_wg', 'v_ffn1_wu', 'v_ffn1_wd', 'v_norm_mix', 'v_norm_ffn2', 'v_ffn2_wg', 'v_ffn2_wu', 'v_ffn2_wd', 'v_rel_bias', 'v_even_w_in', 'v_attn_sinks', 'v_conv_b_w', 'v_conv_b_b', 'v_conv_ln_g', 'v_conv_ln_b', 'v_even_w_out', 'v_odd_w_in', 'v_lru_conv_w', 'v_lru_conv_b', 'v_gate_a_w', 'v_gate_a_b', 'v_gate_x_w', 'v_gate_x_b', 'v_lru_lambda', 'v_odd_w_out', 'v_norm_final']
TWIN_OUTPUTS = ['loss', 'grad_x', 'grad_norm_ffn1', 'grad_ffn1_wg', 'grad_ffn1_wu', 'grad_ffn1_wd', 'grad_norm_mix', 'grad_norm_ffn2', 'grad_ffn2_wg', 'grad_ffn2_wu', 'grad_ffn2_wd', 'grad_rel_bias', 'grad_even_w_in', 'grad_attn_sinks', 'grad_conv_b_w', 'grad_conv_b_b', 'grad_conv_ln_g', 'grad_conv_ln_b', 'grad_even_w_out', 'grad_odd_w_in', 'grad_lru_conv_w', 'grad_lru_conv_b', 'grad_gate_a_w', 'grad_gate_a_b', 'grad_gate_x_w', 'grad_gate_x_b', 'grad_lru_lambda', 'grad_odd_w_out', 'grad_norm_final', 'delta_norm_ffn1', 'delta_ffn1_wg', 'delta_ffn1_wu', 'delta_ffn1_wd', 'delta_norm_mix', 'delta_norm_ffn2', 'delta_ffn2_wg', 'delta_ffn2_wu', 'delta_ffn2_wd', 'delta_rel_bias', 'delta_even_w_in', 'delta_attn_sinks', 'delta_conv_b_w', 'delta_conv_b_b', 'delta_conv_ln_g', 'delta_conv_ln_b', 'delta_even_w_out', 'delta_odd_w_in', 'delta_lru_conv_w', 'delta_lru_conv_b', 'delta_gate_a_w', 'delta_gate_a_b', 'delta_gate_x_w', 'delta_gate_x_b', 'delta_lru_lambda', 'delta_odd_w_out', 'delta_norm_final', 'new_m_norm_ffn1', 'new_m_ffn1_wg', 'new_m_ffn1_wu', 'new_m_ffn1_wd', 'new_m_norm_mix', 'new_m_norm_ffn2', 'new_m_ffn2_wg', 'new_m_ffn2_wu', 'new_m_ffn2_wd', 'new_m_rel_bias', 'new_m_even_w_in', 'new_m_attn_sinks', 'new_m_conv_b_w', 'new_m_conv_b_b', 'new_m_conv_ln_g', 'new_m_conv_ln_b', 'new_m_even_w_out', 'new_m_odd_w_in', 'new_m_lru_conv_w', 'new_m_lru_conv_b', 'new_m_gate_a_w', 'new_m_gate_a_b', 'new_m_gate_x_w', 'new_m_gate_x_b', 'new_m_lru_lambda', 'new_m_odd_w_out', 'new_m_norm_final', 'new_v_norm_ffn1', 'new_v_ffn1_wg', 'new_v_ffn1_wu', 'new_v_ffn1_wd', 'new_v_norm_mix', 'new_v_norm_ffn2', 'new_v_ffn2_wg', 'new_v_ffn2_wu', 'new_v_ffn2_wd', 'new_v_rel_bias', 'new_v_even_w_in', 'new_v_attn_sinks', 'new_v_conv_b_w', 'new_v_conv_b_b', 'new_v_conv_ln_g', 'new_v_conv_ln_b', 'new_v_even_w_out', 'new_v_odd_w_in', 'new_v_lru_conv_w', 'new_v_lru_conv_b', 'new_v_gate_a_w', 'new_v_gate_a_b', 'new_v_gate_x_w', 'new_v_gate_x_b', 'new_v_lru_lambda', 'new_v_odd_w_out', 'new_v_norm_final']
TWIN_LEAF_KINDS = {'loss': 'loss', 'grad_x': 'grad_x', 'grad_norm_ffn1': 'grad_w', 'grad_ffn1_wg': 'grad_w', 'grad_ffn1_wu': 'grad_w', 'grad_ffn1_wd': 'grad_w', 'grad_norm_mix': 'grad_w', 'grad_norm_ffn2': 'grad_w', 'grad_ffn2_wg': 'grad_w', 'grad_ffn2_wu': 'grad_w', 'grad_ffn2_wd': 'grad_w', 'grad_rel_bias': 'grad_w', 'grad_even_w_in': 'grad_w', 'grad_attn_sinks': 'grad_w', 'grad_conv_b_w': 'grad_w', 'grad_conv_b_b': 'grad_w', 'grad_conv_ln_g': 'grad_w', 'grad_conv_ln_b': 'grad_w', 'grad_even_w_out': 'grad_w', 'grad_odd_w_in': 'grad_w', 'grad_lru_conv_w': 'grad_w', 'grad_lru_conv_b': 'grad_w', 'grad_gate_a_w': 'grad_w', 'grad_gate_a_b': 'grad_w', 'grad_gate_x_w': 'grad_w', 'grad_gate_x_b': 'grad_w', 'grad_lru_lambda': 'grad_w', 'grad_odd_w_out': 'grad_w', 'grad_norm_final': 'grad_w', 'delta_norm_ffn1': 'delta_w', 'delta_ffn1_wg': 'delta_w', 'delta_ffn1_wu': 'delta_w', 'delta_ffn1_wd': 'delta_w', 'delta_norm_mix': 'delta_w', 'delta_norm_ffn2': 'delta_w', 'delta_ffn2_wg': 'delta_w', 'delta_ffn2_wu': 'delta_w', 'delta_ffn2_wd': 'delta_w', 'delta_rel_bias': 'delta_w', 'delta_even_w_in': 'delta_w', 'delta_attn_sinks': 'delta_w', 'delta_conv_b_w': 'delta_w', 'delta_conv_b_b': 'delta_w', 'delta_conv_ln_g': 'delta_w', 'delta_conv_ln_b': 'delta_w', 'delta_even_w_out': 'delta_w', 'delta_odd_w_in': 'delta_w', 'delta_lru_conv_w': 'delta_w', 'delta_lru_conv_b': 'delta_w', 'delta_gate_a_w': 'delta_w', 'delta_gate_a_b': 'delta_w', 'delta_gate_x_w': 'delta_w', 'delta_gate_x_b': 'delta_w', 'delta_lru_lambda': 'delta_w', 'delta_odd_w_out': 'delta_w', 'delta_norm_final': 'delta_w', 'new_m_norm_ffn1': 'new_m', 'new_m_ffn1_wg': 'new_m', 'new_m_ffn1_wu': 'new_m', 'new_m_ffn1_wd': 'new_m', 'new_m_norm_mix': 'new_m', 'new_m_norm_ffn2': 'new_m', 'new_m_ffn2_wg': 'new_m', 'new_m_ffn2_wu': 'new_m', 'new_m_ffn2_wd': 'new_m', 'new_m_rel_bias': 'new_m', 'new_m_even_w_in': 'new_m', 'new_m_attn_sinks': 'new_m', 'new_m_conv_b_w': 'new_m', 'new_m_conv_b_b': 'new_m', 'new_m_conv_ln_g': 'new_m', 'new_m_conv_ln_b': 'new_m', 'new_m_even_w_out': 'new_m', 'new_m_odd_w_in': 'new_m', 'new_m_lru_conv_w': 'new_m', 'new_m_lru_conv_b': 'new_m', 'new_m_gate_a_w': 'new_m', 'new_m_gate_a_b': 'new_m', 'new_m_gate_x_w': 'new_m', 'new_m_gate_x_b': 'new_m', 'new_m_lru_lambda': 'new_m', 'new_m_odd_w_out': 'new_m', 'new_m_norm_final': 'new_m', 'new_v_norm_ffn1': 'new_v', 'new_v_ffn1_wg': 'new_v', 'new_v_ffn1_wu': 'new_v', 'new_v_ffn1_wd': 'new_v', 'new_v_norm_mix': 'new_v', 'new_v_norm_ffn2': 'new_v', 'new_v_ffn2_wg': 'new_v', 'new_v_ffn2_wu': 'new_v', 'new_v_ffn2_wd': 'new_v', 'new_v_rel_bias': 'new_v', 'new_v_even_w_in': 'new_v', 'new_v_attn_sinks': 'new_v', 'new_v_conv_b_w': 'new_v', 'new_v_conv_b_b': 'new_v', 'new_v_conv_ln_g': 'new_v', 'new_v_conv_ln_b': 'new_v', 'new_v_even_w_out': 'new_v', 'new_v_odd_w_in': 'new_v', 'new_v_lru_conv_w': 'new_v', 'new_v_lru_conv_b': 'new_v', 'new_v_gate_a_w': 'new_v', 'new_v_gate_a_b': 'new_v', 'new_v_gate_x_w': 'new_v', 'new_v_gate_x_b': 'new_v', 'new_v_lru_lambda': 'new_v', 'new_v_odd_w_out': 'new_v', 'new_v_norm_final': 'new_v'}


def _forward(args):
    return _fwd_reference(*[args[k] for k in FWD_PARAMS])


def _output_shape():
    out = _jax.eval_shape(lambda: _forward(_fwd_setup_inputs(0)))
    return out.shape, out.dtype

N_MICROBATCH = 1
ADAM_LR = 0.001
ADAM_B1 = 0.9
ADAM_B2 = 0.999
ADAM_EPS = 1e-08
ADAM_WD = 0.01
ADAM_STEP = 10
PER_EXAMPLE_BATCH_AXIS = {'x': 0, 'loss_target': 0}
SHARED_INPUTS = []
_WEIGHT_DTYPES = {'norm_ffn1': _jnp.float32, 'ffn1_wg': _jnp.float32, 'ffn1_wu': _jnp.float32, 'ffn1_wd': _jnp.float32, 'norm_mix': _jnp.float32, 'norm_ffn2': _jnp.float32, 'ffn2_wg': _jnp.float32, 'ffn2_wu': _jnp.float32, 'ffn2_wd': _jnp.float32, 'rel_bias': _jnp.float32, 'even_w_in': _jnp.float32, 'attn_sinks': _jnp.float32, 'conv_b_w': _jnp.float32, 'conv_b_b': _jnp.float32, 'conv_ln_g': _jnp.float32, 'conv_ln_b': _jnp.float32, 'even_w_out': _jnp.float32, 'odd_w_in': _jnp.float32, 'lru_conv_w': _jnp.float32, 'lru_conv_b': _jnp.float32, 'gate_a_w': _jnp.float32, 'gate_a_b': _jnp.float32, 'gate_x_w': _jnp.float32, 'gate_x_b': _jnp.float32, 'lru_lambda': _jnp.float32, 'odd_w_out': _jnp.float32, 'norm_final': _jnp.float32}
MOMENT_SCALE = {'norm_ffn1': 1.071812e-01, 'ffn1_wg': 4.587790e-02, 'ffn1_wu': 4.458970e-02, 'ffn1_wd': 7.378552e-02, 'norm_mix': 1.966260e-01, 'norm_ffn2': 9.116563e-02, 'ffn2_wg': 3.883773e-02, 'ffn2_wu': 3.776095e-02, 'ffn2_wd': 6.256354e-02, 'rel_bias': 8.708481e-02, 'even_w_in': 1.036538e-01, 'attn_sinks': 6.319396e-02, 'conv_b_w': 1.638958e-01, 'conv_b_b': 1.041352e+00, 'conv_ln_g': 4.276950e-01, 'conv_ln_b': 7.081162e-01, 'even_w_out': 2.018137e-01, 'odd_w_in': 1.951452e-01, 'lru_conv_w': 2.635015e-01, 'lru_conv_b': 1.688275e+00, 'gate_a_w': 4.014542e-02, 'gate_a_b': 5.187892e-02, 'gate_x_w': 7.719929e-02, 'gate_x_b': 9.864742e-02, 'lru_lambda': 1.298574e-01, 'odd_w_out': 2.158739e-01, 'norm_final': 6.417711e+01}


def _to_microbatches(a, axis):
    t = _jnp.moveaxis(a, axis, 0)
    t = t.reshape((N_MICROBATCH, t.shape[0] // N_MICROBATCH) + t.shape[1:])
    return _jnp.moveaxis(t, 1, axis + 1)


def setup_inputs(seed: int = 0) -> dict:
    inp = _fwd_setup_inputs(seed)
    key = _jax.random.fold_in(_jax.random.key(seed), 7919)
    shape, _ = _output_shape()
    out = dict(inp)
    out["loss_target"] = _jax.random.normal(_jax.random.fold_in(key, 0), shape, _jnp.float32)
    for i, name in enumerate(TWIN_WEIGHTS):
        w = inp[name].astype(_jnp.float32)
        if MOMENT_SCALE is None:
            s = _jnp.sqrt(_jnp.mean(_jnp.square(w)) + 1e-30)
        else:
            s = MOMENT_SCALE[name]
        km, kv = _jax.random.split(_jax.random.fold_in(key, i + 1))
        out[name] = w
        out["m_" + name] = s * _jax.random.normal(km, w.shape, _jnp.float32)
        out["v_" + name] = (s * s) * _jax.random.uniform(kv, w.shape, _jnp.float32, 0.5, 1.5)
    if N_MICROBATCH > 1:
        for name, axis in PER_EXAMPLE_BATCH_AXIS.items():
            out[name] = _to_microbatches(out[name], axis)
    return {'x': out['x'], 'norm_ffn1': out['norm_ffn1'], 'ffn1_wg': out['ffn1_wg'], 'ffn1_wu': out['ffn1_wu'], 'ffn1_wd': out['ffn1_wd'], 'norm_mix': out['norm_mix'], 'norm_ffn2': out['norm_ffn2'], 'ffn2_wg': out['ffn2_wg'], 'ffn2_wu': out['ffn2_wu'], 'ffn2_wd': out['ffn2_wd'], 'rel_bias': out['rel_bias'], 'even_w_in': out['even_w_in'], 'attn_sinks': out['attn_sinks'], 'conv_b_w': out['conv_b_w'], 'conv_b_b': out['conv_b_b'], 'conv_ln_g': out['conv_ln_g'], 'conv_ln_b': out['conv_ln_b'], 'even_w_out': out['even_w_out'], 'odd_w_in': out['odd_w_in'], 'lru_conv_w': out['lru_conv_w'], 'lru_conv_b': out['lru_conv_b'], 'gate_a_w': out['gate_a_w'], 'gate_a_b': out['gate_a_b'], 'gate_x_w': out['gate_x_w'], 'gate_x_b': out['gate_x_b'], 'lru_lambda': out['lru_lambda'], 'odd_w_out': out['odd_w_out'], 'norm_final': out['norm_final'], 'loss_target': out['loss_target'], 'm_norm_ffn1': out['m_norm_ffn1'], 'm_ffn1_wg': out['m_ffn1_wg'], 'm_ffn1_wu': out['m_ffn1_wu'], 'm_ffn1_wd': out['m_ffn1_wd'], 'm_norm_mix': out['m_norm_mix'], 'm_norm_ffn2': out['m_norm_ffn2'], 'm_ffn2_wg': out['m_ffn2_wg'], 'm_ffn2_wu': out['m_ffn2_wu'], 'm_ffn2_wd': out['m_ffn2_wd'], 'm_rel_bias': out['m_rel_bias'], 'm_even_w_in': out['m_even_w_in'], 'm_attn_sinks': out['m_attn_sinks'], 'm_conv_b_w': out['m_conv_b_w'], 'm_conv_b_b': out['m_conv_b_b'], 'm_conv_ln_g': out['m_conv_ln_g'], 'm_conv_ln_b': out['m_conv_ln_b'], 'm_even_w_out': out['m_even_w_out'], 'm_odd_w_in': out['m_odd_w_in'], 'm_lru_conv_w': out['m_lru_conv_w'], 'm_lru_conv_b': out['m_lru_conv_b'], 'm_gate_a_w': out['m_gate_a_w'], 'm_gate_a_b': out['m_gate_a_b'], 'm_gate_x_w': out['m_gate_x_w'], 'm_gate_x_b': out['m_gate_x_b'], 'm_lru_lambda': out['m_lru_lambda'], 'm_odd_w_out': out['m_odd_w_out'], 'm_norm_final': out['m_norm_final'], 'v_norm_ffn1': out['v_norm_ffn1'], 'v_ffn1_wg': out['v_ffn1_wg'], 'v_ffn1_wu': out['v_ffn1_wu'], 'v_ffn1_wd': out['v_ffn1_wd'], 'v_norm_mix': out['v_norm_mix'], 'v_norm_ffn2': out['v_norm_ffn2'], 'v_ffn2_wg': out['v_ffn2_wg'], 'v_ffn2_wu': out['v_ffn2_wu'], 'v_ffn2_wd': out['v_ffn2_wd'], 'v_rel_bias': out['v_rel_bias'], 'v_even_w_in': out['v_even_w_in'], 'v_attn_sinks': out['v_attn_sinks'], 'v_conv_b_w': out['v_conv_b_w'], 'v_conv_b_b': out['v_conv_b_b'], 'v_conv_ln_g': out['v_conv_ln_g'], 'v_conv_ln_b': out['v_conv_ln_b'], 'v_even_w_out': out['v_even_w_out'], 'v_odd_w_in': out['v_odd_w_in'], 'v_lru_conv_w': out['v_lru_conv_w'], 'v_lru_conv_b': out['v_lru_conv_b'], 'v_gate_a_w': out['v_gate_a_w'], 'v_gate_a_b': out['v_gate_a_b'], 'v_gate_x_w': out['v_gate_x_w'], 'v_gate_x_b': out['v_gate_x_b'], 'v_lru_lambda': out['v_lru_lambda'], 'v_odd_w_out': out['v_odd_w_out'], 'v_norm_final': out['v_norm_final']}


def _loss(weights, diff, rest, loss_target):
    with _jax.named_scope("forward"):
        args = {**rest, TWIN_DIFF_INPUT: diff, **{k: w.astype(_WEIGHT_DTYPES[k]) for k, w in weights.items()}}
        y = _forward(args)
    with _jax.named_scope("loss_head"):
        err = _jnp.square(y.astype(_jnp.float32) - loss_target)
        return 0.5 * _jnp.sum(_jnp.mean(err, axis=-1)) if err.ndim else 0.5 * err


def _adamw(w, g, m, v):
    m = ADAM_B1 * m + (1.0 - ADAM_B1) * g
    v = ADAM_B2 * v + (1.0 - ADAM_B2) * _jnp.square(g)
    m_hat = m / (1.0 - ADAM_B1 ** ADAM_STEP)
    v_hat = v / (1.0 - ADAM_B2 ** ADAM_STEP)
    delta = -ADAM_LR * (m_hat / (_jnp.sqrt(v_hat) + ADAM_EPS) + ADAM_WD * w)
    return delta, m, v


def reference(x, norm_ffn1, ffn1_wg, ffn1_wu, ffn1_wd, norm_mix, norm_ffn2, ffn2_wg, ffn2_wu, ffn2_wd, rel_bias, even_w_in, attn_sinks, conv_b_w, conv_b_b, conv_ln_g, conv_ln_b, even_w_out, odd_w_in, lru_conv_w, lru_conv_b, gate_a_w, gate_a_b, gate_x_w, gate_x_b, lru_lambda, odd_w_out, norm_final, loss_target, m_norm_ffn1, m_ffn1_wg, m_ffn1_wu, m_ffn1_wd, m_norm_mix, m_norm_ffn2, m_ffn2_wg, m_ffn2_wu, m_ffn2_wd, m_rel_bias, m_even_w_in, m_attn_sinks, m_conv_b_w, m_conv_b_b, m_conv_ln_g, m_conv_ln_b, m_even_w_out, m_odd_w_in, m_lru_conv_w, m_lru_conv_b, m_gate_a_w, m_gate_a_b, m_gate_x_w, m_gate_x_b, m_lru_lambda, m_odd_w_out, m_norm_final, v_norm_ffn1, v_ffn1_wg, v_ffn1_wu, v_ffn1_wd, v_norm_mix, v_norm_ffn2, v_ffn2_wg, v_ffn2_wu, v_ffn2_wd, v_rel_bias, v_even_w_in, v_attn_sinks, v_conv_b_w, v_conv_b_b, v_conv_ln_g, v_conv_ln_b, v_even_w_out, v_odd_w_in, v_lru_conv_w, v_lru_conv_b, v_gate_a_w, v_gate_a_b, v_gate_x_w, v_gate_x_b, v_lru_lambda, v_odd_w_out, v_norm_final):
    given = dict(x=x, norm_ffn1=norm_ffn1, ffn1_wg=ffn1_wg, ffn1_wu=ffn1_wu, ffn1_wd=ffn1_wd, norm_mix=norm_mix, norm_ffn2=norm_ffn2, ffn2_wg=ffn2_wg, ffn2_wu=ffn2_wu, ffn2_wd=ffn2_wd, rel_bias=rel_bias, even_w_in=even_w_in, attn_sinks=attn_sinks, conv_b_w=conv_b_w, conv_b_b=conv_b_b, conv_ln_g=conv_ln_g, conv_ln_b=conv_ln_b, even_w_out=even_w_out, odd_w_in=odd_w_in, lru_conv_w=lru_conv_w, lru_conv_b=lru_conv_b, gate_a_w=gate_a_w, gate_a_b=gate_a_b, gate_x_w=gate_x_w, gate_x_b=gate_x_b, lru_lambda=lru_lambda, odd_w_out=odd_w_out, norm_final=norm_final, loss_target=loss_target, m_norm_ffn1=m_norm_ffn1, m_ffn1_wg=m_ffn1_wg, m_ffn1_wu=m_ffn1_wu, m_ffn1_wd=m_ffn1_wd, m_norm_mix=m_norm_mix, m_norm_ffn2=m_norm_ffn2, m_ffn2_wg=m_ffn2_wg, m_ffn2_wu=m_ffn2_wu, m_ffn2_wd=m_ffn2_wd, m_rel_bias=m_rel_bias, m_even_w_in=m_even_w_in, m_attn_sinks=m_attn_sinks, m_conv_b_w=m_conv_b_w, m_conv_b_b=m_conv_b_b, m_conv_ln_g=m_conv_ln_g, m_conv_ln_b=m_conv_ln_b, m_even_w_out=m_even_w_out, m_odd_w_in=m_odd_w_in, m_lru_conv_w=m_lru_conv_w, m_lru_conv_b=m_lru_conv_b, m_gate_a_w=m_gate_a_w, m_gate_a_b=m_gate_a_b, m_gate_x_w=m_gate_x_w, m_gate_x_b=m_gate_x_b, m_lru_lambda=m_lru_lambda, m_odd_w_out=m_odd_w_out, m_norm_final=m_norm_final, v_norm_ffn1=v_norm_ffn1, v_ffn1_wg=v_ffn1_wg, v_ffn1_wu=v_ffn1_wu, v_ffn1_wd=v_ffn1_wd, v_norm_mix=v_norm_mix, v_norm_ffn2=v_norm_ffn2, v_ffn2_wg=v_ffn2_wg, v_ffn2_wu=v_ffn2_wu, v_ffn2_wd=v_ffn2_wd, v_rel_bias=v_rel_bias, v_even_w_in=v_even_w_in, v_attn_sinks=v_attn_sinks, v_conv_b_w=v_conv_b_w, v_conv_b_b=v_conv_b_b, v_conv_ln_g=v_conv_ln_g, v_conv_ln_b=v_conv_ln_b, v_even_w_out=v_even_w_out, v_odd_w_in=v_odd_w_in, v_lru_conv_w=v_lru_conv_w, v_lru_conv_b=v_lru_conv_b, v_gate_a_w=v_gate_a_w, v_gate_a_b=v_gate_a_b, v_gate_x_w=v_gate_x_w, v_gate_x_b=v_gate_x_b, v_lru_lambda=v_lru_lambda, v_odd_w_out=v_odd_w_out, v_norm_final=v_norm_final)
    weights = {n: given[n] for n in TWIN_WEIGHTS}
    shared = {n: given[n] for n in SHARED_INPUTS}
    per_example = {n: given[n] for n in ['x']}
    grad_fn = _jax.value_and_grad(_loss, argnums=(0, 1))

    def one_microbatch(ex, loss_target):
        ex = dict(ex)
        diff = ex.pop(TWIN_DIFF_INPUT)
        return grad_fn(weights, diff, {**shared, **ex}, loss_target)

    if N_MICROBATCH == 1:
        loss, (grad_w, grad_x) = one_microbatch(per_example, given["loss_target"])
    else:
        def body(carry, xs):
            loss_sum, grad_sum = carry
            l_k, (gw_k, gx_k) = one_microbatch(xs[0], xs[1])
            with _jax.named_scope("update"):
                return (loss_sum + l_k, _jax.tree.map(_jnp.add, grad_sum, gw_k)), gx_k

        init = (_jnp.zeros((), _jnp.float32), _jax.tree.map(_jnp.zeros_like, weights))
        (loss, grad_w), grad_x = _jax.lax.scan(body, init, (per_example, given["loss_target"]))
    with _jax.named_scope("update"):
        delta_w, new_m, new_v = {}, {}, {}
        for n in TWIN_WEIGHTS:
            delta_w[n], new_m[n], new_v[n] = _adamw(weights[n], grad_w[n], given["m_" + n], given["v_" + n])
    return (loss, grad_x, *[grad_w[n] for n in TWIN_WEIGHTS], *[delta_w[n] for n in TWIN_WEIGHTS],
            *[new_m[n] for n in TWIN_WEIGHTS], *[new_v[n] for n in TWIN_WEIGHTS])
```

```python
import math
from typing import NamedTuple

import jax
import jax.numpy as jnp
from jax import lax
from jax.experimental import pallas as pl
from jax.experimental.pallas import tpu as pltpu

F32 = jnp.float32
BF16 = jnp.bfloat16
RMS_EPS = 1e-6
LN_EPS = 1e-5
NEG_INF = -1e30
RG_LRU_C = 8.0
ADAM_LR = 0.001
ADAM_B1 = 0.9
ADAM_B2 = 0.999
ADAM_EPS = 1e-08
ADAM_WD = 0.01
ADAM_STEP = 10
N_DEV = 8
VMEM_LIMIT = 56 * 1024 * 1024


class Cfg(NamedTuple):
    d: int = 1024
    f: int = 2816
    s: int = 2048
    bl: int = 4
    hq: int = 8
    hkv: int = 2
    hd: int = 64
    win: int = 128
    cc: int = 512
    cw: int = 31
    lh: int = 8
    lb: int = 128
    lcw: int = 4
    nbuckets: int = 32
    max_dist: int = 128
    depth: int = 4
    tm: int = 512
    tf: int = 256
    tk: int = 512
    tr: int = 256
    ct_f: int = 512
    ct_b: int = 256

    @property
    def qw(self):
        return self.hq * self.hd

    @property
    def kvw(self):
        return self.hkv * self.hd

    @property
    def even_in(self):
        return self.qw + 2 * self.kvw + 2 * self.cc

    @property
    def even_cat(self):
        return self.qw + self.cc

    @property
    def lw(self):
        return self.lh * self.lb


def _cparams(*sem):
    return pltpu.CompilerParams(dimension_semantics=sem, vmem_limit_bytes=VMEM_LIMIT)


def _nt(a, b):
    return lax.dot_general(a, b, (((1,), (1,)), ((), ())), preferred_element_type=F32)


def _nn(a, b):
    return lax.dot_general(a, b, (((1,), (0,)), ((), ())), preferred_element_type=F32)


def _tn(a, b):
    return lax.dot_general(a, b, (((0,), (0,)), ((), ())), preferred_element_type=F32)


def _rstd(h):
    return lax.rsqrt(jnp.mean(h * h, axis=-1, keepdims=True) + RMS_EPS)


def _rms_bwd(h, nw, dxn):
    rstd = _rstd(h)
    dyg = dxn * nw
    dnw = jnp.sum(dxn * h * rstd, axis=0, keepdims=True)
    dx = rstd * (dyg - h * (rstd * rstd) * jnp.mean(dyg * h, axis=-1, keepdims=True))
    return dx, dnw


def _sigmoid(x):
    return 1.0 / (1.0 + jnp.exp(-x))


def ffn_fwd(cfg, h, nw, wts, m0, name):
    t, d = h.shape
    f = wts.shape[1]
    tm, tf = cfg.tm, cfg.tf
    nj = f // tf

    def body(h_ref, nw_ref, wg_ref, wu_ref, wd_ref, ho_ref, xn_ref, g_ref, u_ref, acc_ref):
        j = pl.program_id(1)

        @pl.when(j == 0)
        def _():
            hh = h_ref[...]
            xn_ref[...] = (hh * _rstd(hh) * nw_ref[...]).astype(BF16)
            acc_ref[...] = jnp.zeros_like(acc_ref)

        xn = xn_ref[...]
        g = _nt(xn, wg_ref[...])
        u = _nt(xn, wu_ref[...])
        g_ref[...] = g.astype(BF16)
        u_ref[...] = u.astype(BF16)
        a = (g * _sigmoid(g) * u).astype(BF16)
        acc_ref[...] += _nn(a, wd_ref[...])

        @pl.when(j == nj - 1)
        def _():
            ho_ref[...] = h_ref[...] + 0.5 * acc_ref[...]

    def wspec(m):
        return pl.BlockSpec((None, tf, d), lambda i, j: (m, j, 0))

    return pl.pallas_call(
        body, name=name, grid=(t // tm, nj),
        in_specs=[pl.BlockSpec((tm, d), lambda i, j: (i, 0)), pl.BlockSpec((1, d), lambda i, j: (0, 0)),
                  wspec(m0), wspec(m0 + 1), wspec(m0 + 2)],
        out_specs=[pl.BlockSpec((tm, d), lambda i, j: (i, 0)), pl.BlockSpec((tm, d), lambda i, j: (i, 0)),
                   pl.BlockSpec((tm, tf), lambda i, j: (i, j)), pl.BlockSpec((tm, tf), lambda i, j: (i, j))],
        out_shape=[jax.ShapeDtypeStruct((t, d), F32), jax.ShapeDtypeStruct((t, d), BF16),
                   jax.ShapeDtypeStruct((t, f), BF16), jax.ShapeDtypeStruct((t, f), BF16)],
        scratch_shapes=[pltpu.VMEM((tm, d), F32)],
        compiler_params=_cparams("arbitrary", "arbitrary"),
    )(h, nw, wts, wts, wts)


def ffn_bwd_x(cfg, dh, h, nw, g, u, wts, m0, name):
    t, d = h.shape
    f = wts.shape[1]
    tm, tf = cfg.tm, cfg.tf
    nj = f // tf

    def body(dh_ref, h_ref, nw_ref, g_ref, u_ref, wg_ref, wu_ref, wd_ref,
             dho_ref, dout_ref, dg_ref, du_ref, dnw_ref, acc_ref):
        i, j = pl.program_id(0), pl.program_id(1)

        @pl.when(j == 0)
        def _():
            dout_ref[...] = (0.5 * dh_ref[...]).astype(BF16)
            acc_ref[...] = jnp.zeros_like(acc_ref)

        @pl.when((i == 0) & (j == 0))
        def _():
            dnw_ref[...] = jnp.zeros_like(dnw_ref)

        da = _nt(dout_ref[...], wd_ref[...])
        gg = g_ref[...].astype(F32)
        uu = u_ref[...].astype(F32)
        sig = _sigmoid(gg)
        dg = (da * uu * (sig * (1.0 + gg * (1.0 - sig)))).astype(BF16)
        du = (da * (gg * sig)).astype(BF16)
        dg_ref[...] = dg
        du_ref[...] = du
        acc_ref[...] += _nn(dg, wg_ref[...]) + _nn(du, wu_ref[...])

        @pl.when(j == nj - 1)
        def _():
            dx, dnw = _rms_bwd(h_ref[...], nw_ref[...], acc_ref[...])
            dnw_ref[0:1, :] += dnw
            dho_ref[...] = dh_ref[...] + dx

    def wspec(m):
        return pl.BlockSpec((None, tf, d), lambda i, j: (m, j, 0))

    row = pl.BlockSpec((tm, d), lambda i, j: (i, 0))
    hid = pl.BlockSpec((tm, tf), lambda i, j: (i, j))
    return pl.pallas_call(
        body, name=name, grid=(t // tm, nj),
        in_specs=[row, row, pl.BlockSpec((1, d), lambda i, j: (0, 0)), hid, hid, wspec(m0), wspec(m0 + 1), wspec(m0 + 2)],
        out_specs=[row, row, hid, hid, pl.BlockSpec((8, d), lambda i, j: (0, 0))],
        out_shape=[jax.ShapeDtypeStruct((t, d), F32), jax.ShapeDtypeStruct((t, d), BF16),
                   jax.ShapeDtypeStruct((t, f), BF16), jax.ShapeDtypeStruct((t, f), BF16),
                   jax.ShapeDtypeStruct((8, d), F32)],
        scratch_shapes=[pltpu.VMEM((tm, d), F32)],
        compiler_params=_cparams("arbitrary", "arbitrary"),
    )(dh, h, nw, g, u, wts, wts, wts)


def ffn_bwd_w(cfg, xn, dout, g, u, dg, du, name, slot=0, nslot=1, prev=None):
    t, d = xn.shape
    f = g.shape[1]
    tk, tf = cfg.tk, cfg.tf
    nk = t // tk

    def body(xn_ref, dout_ref, g_ref, u_ref, dg_ref, du_ref, *rest):
        o_ref, acc_ref = rest[-2:]
        k = pl.program_id(1)

        @pl.when(k == 0)
        def _():
            acc_ref[...] = jnp.zeros_like(acc_ref)

        gg = g_ref[...].astype(F32)
        a = (gg * _sigmoid(gg) * u_ref[...].astype(F32)).astype(BF16)
        xn_t = xn_ref[...]
        acc_ref[0] += _tn(dg_ref[...], xn_t)
        acc_ref[1] += _tn(du_ref[...], xn_t)
        acc_ref[2] += _tn(a, dout_ref[...])

        @pl.when(k == nk - 1)
        def _():
            o_ref[...] = acc_ref[...].astype(BF16)

    row = pl.BlockSpec((tk, d), lambda j, k: (k, 0))
    hid = pl.BlockSpec((tk, tf), lambda j, k: (k, j))
    extra = [] if prev is None else [prev]
    return pl.pallas_call(
        body, name=name, grid=(f // tf, nk),
        in_specs=[row, row, hid, hid, hid, hid] + [pl.BlockSpec(memory_space=pl.ANY)] * len(extra),
        out_specs=pl.BlockSpec((3, tf, d), lambda j, k: (slot, j, 0)),
        out_shape=jax.ShapeDtypeStruct((3 * nslot, f, d), BF16),
        scratch_shapes=[pltpu.VMEM((3, tf, d), F32)],
        input_output_aliases={} if prev is None else {6: 0},
        compiler_params=_cparams("arbitrary", "arbitrary"),
    )(xn, dout, g, u, dg, du, *extra)


def _wspec(w, wi):
    if w.ndim == 2:
        return pl.BlockSpec(w.shape, lambda i: (0, 0))
    return pl.BlockSpec((None,) + w.shape[1:], lambda i: (wi, 0, 0))


def norm_proj(cfg, h, nw, w, name, wi=0):
    t, d = h.shape
    n = w.shape[-2]
    tm = cfg.tm

    def body(h_ref, nw_ref, w_ref, u_ref, xn_ref):
        hh = h_ref[...]
        xn = (hh * _rstd(hh) * nw_ref[...]).astype(BF16)
        xn_ref[...] = xn
        u_ref[...] = _nt(xn, w_ref[...]).astype(BF16)

    return pl.pallas_call(
        body, name=name, grid=(t // tm,),
        in_specs=[pl.BlockSpec((tm, d), lambda i: (i, 0)), pl.BlockSpec((1, d), lambda i: (0, 0)),
                  _wspec(w, wi)],
        out_specs=[pl.BlockSpec((tm, n), lambda i: (i, 0)), pl.BlockSpec((tm, d), lambda i: (i, 0))],
        out_shape=[jax.ShapeDtypeStruct((t, n), BF16), jax.ShapeDtypeStruct((t, d), BF16)],
        compiler_params=_cparams("arbitrary"),
    )(h, nw, w)


def proj_residual(cfg, h, parts, w, name, wi=0):
    t, d = h.shape
    tm = cfg.tm
    ks = [p.shape[1] for p in parts]
    offs = [sum(ks[:i]) for i in range(len(ks))]
    np_ = len(parts)

    def body(*refs):
        h_ref, w_ref, ho_ref = refs[0], refs[1 + np_], refs[2 + np_]
        acc = h_ref[...]
        for p_ref, off, k in zip(refs[1:1 + np_], offs, ks):
            acc = acc + _nn(p_ref[...], w_ref[off:off + k, :])
        ho_ref[...] = acc

    return pl.pallas_call(
        body, name=name, grid=(t // tm,),
        in_specs=[pl.BlockSpec((tm, d), lambda i: (i, 0))]
        + [pl.BlockSpec((tm, k), lambda i: (i, 0)) for k in ks]
        + [_wspec(w, wi)],
        out_specs=pl.BlockSpec((tm, d), lambda i: (i, 0)),
        out_shape=jax.ShapeDtypeStruct((t, d), F32),
        compiler_params=_cparams("arbitrary"),
    )(h, *parts, w)


def proj_bwd_act(cfg, dh, w, name, wi=0):
    t, d = dh.shape
    k = w.shape[-2]
    tm = cfg.tm

    def body(dh_ref, w_ref, o_ref):
        o_ref[...] = _nt(dh_ref[...].astype(BF16), w_ref[...]).astype(BF16)

    return pl.pallas_call(
        body, name=name, grid=(t // tm,),
        in_specs=[pl.BlockSpec((tm, d), lambda i: (i, 0)), _wspec(w, wi)],
        out_specs=pl.BlockSpec((tm, k), lambda i: (i, 0)),
        out_shape=jax.ShapeDtypeStruct((t, k), BF16),
        compiler_params=_cparams("arbitrary"),
    )(dh, w)


def grad_weight(cfg, a, b, name):
    t, k = a.shape
    d = b.shape[1]
    tk = cfg.tk
    tr = cfg.tr if k % cfg.tr == 0 else k
    nk = t // tk

    def body(a_ref, b_ref, o_ref, acc_ref):
        kk = pl.program_id(1)

        @pl.when(kk == 0)
        def _():
            acc_ref[...] = jnp.zeros_like(acc_ref)

        acc_ref[...] += _tn(a_ref[...], b_ref[...].astype(BF16))

        @pl.when(kk == nk - 1)
        def _():
            o_ref[...] = acc_ref[...].astype(BF16)

    return pl.pallas_call(
        body, name=name, grid=(k // tr, nk),
        in_specs=[pl.BlockSpec((tk, tr), lambda j, kk: (kk, j)), pl.BlockSpec((tk, d), lambda j, kk: (kk, 0))],
        out_specs=pl.BlockSpec((tr, d), lambda j, kk: (j, 0)),
        out_shape=jax.ShapeDtypeStruct((k, d), BF16),
        scratch_shapes=[pltpu.VMEM((tr, d), F32)],
        compiler_params=_cparams("arbitrary", "arbitrary"),
    )(a, b)


def norm_proj_bwd(cfg, dh, h, nw, parts, w, name, wi=0):
    t, d = h.shape
    tm = cfg.tm
    ks = [p.shape[1] for p in parts]
    offs = [sum(ks[:i]) for i in range(len(ks))]
    np_ = len(parts)

    def body(*refs):
        dh_ref, h_ref, nw_ref = refs[:3]
        w_ref, dho_ref, dnw_ref = refs[3 + np_:]

        @pl.when(pl.program_id(0) == 0)
        def _():
            dnw_ref[...] = jnp.zeros_like(dnw_ref)

        dxn = None
        for p_ref, off, k in zip(refs[3:3 + np_], offs, ks):
            term = _nn(p_ref[...], w_ref[off:off + k, :])
            dxn = term if dxn is None else dxn + term
        dx, dnw = _rms_bwd(h_ref[...], nw_ref[...], dxn)
        dnw_ref[0:1, :] += dnw
        dho_ref[...] = dh_ref[...] + dx

    row = pl.BlockSpec((tm, d), lambda i: (i, 0))
    return pl.pallas_call(
        body, name=name, grid=(t // tm,),
        in_specs=[row, row, pl.BlockSpec((1, d), lambda i: (0, 0))]
        + [pl.BlockSpec((tm, k), lambda i: (i, 0)) for k in ks]
        + [_wspec(w, wi)],
        out_specs=[row, pl.BlockSpec((8, d), lambda i: (0, 0))],
        out_shape=[jax.ShapeDtypeStruct((t, d), F32), jax.ShapeDtypeStruct((8, d), F32)],
        compiler_params=_cparams("arbitrary"),
    )(dh, h, nw, *parts, w)


def loss_head(cfg, h, nf, tgt, name):
    t, d = h.shape
    tm = cfg.tm

    def body(h_ref, nf_ref, tgt_ref, loss_ref, dh_ref, dnf_ref):
        @pl.when(pl.program_id(0) == 0)
        def _():
            loss_ref[...] = jnp.zeros_like(loss_ref)
            dnf_ref[...] = jnp.zeros_like(dnf_ref)

        hh = h_ref[...]
        err = hh * _rstd(hh) * nf_ref[...] - tgt_ref[...]
        row = jnp.sum(err * err, axis=-1, keepdims=True) * (0.5 / d)
        loss_ref[...] += jnp.sum(row, axis=0, keepdims=True)
        dx, dnf = _rms_bwd(hh, nf_ref[...], err * (1.0 / d))
        dnf_ref[0:1, :] += dnf
        dh_ref[...] = dx

    row = pl.BlockSpec((tm, d), lambda i: (i, 0))
    return pl.pallas_call(
        body, name=name, grid=(t // tm,),
        in_specs=[row, pl.BlockSpec((1, d), lambda i: (0, 0)), row],
        out_specs=[pl.BlockSpec((8, 128), lambda i: (0, 0)), row, pl.BlockSpec((8, d), lambda i: (0, 0))],
        out_shape=[jax.ShapeDtypeStruct((8, 128), F32), jax.ShapeDtypeStruct((t, d), F32),
                   jax.ShapeDtypeStruct((8, d), F32)],
        compiler_params=_cparams("arbitrary"),
    )(h, nf, tgt)


def bucket_table(cfg):
    qi = jnp.arange(cfg.win)[:, None]
    sj = jnp.arange(2 * cfg.win)[None, :]
    dist = qi + cfg.win - sj
    n = jnp.maximum(dist, 0)
    max_exact = cfg.nbuckets // 2
    nf = jnp.maximum(n, max_exact).astype(F32)
    large = max_exact + (jnp.log(nf / max_exact) / math.log(cfg.max_dist / max_exact)
                         * (cfg.nbuckets - max_exact)).astype(jnp.int32)
    large = jnp.minimum(large, cfg.nbuckets - 1)
    bucket = jnp.where(n < max_exact, n, large)
    return jnp.where((dist >= 0) & (dist < cfg.win), bucket, -1).astype(jnp.int32)


def bias_build(cfg, rel_bias, buckets, name):
    w = cfg.win

    def body(rb_ref, bk_ref, o_ref):
        bk = bk_ref[...]
        for h in range(cfg.hq):
            acc = jnp.full((w, 2 * w), NEG_INF, F32)
            for b in range(cfg.nbuckets):
                acc = jnp.where(bk == b, rb_ref[b, h], acc)
            o_ref[h] = acc

    return pl.pallas_call(
        body, name=name,
        in_specs=[pl.BlockSpec(memory_space=pltpu.SMEM), pl.BlockSpec(memory_space=pltpu.VMEM)],
        out_specs=pl.BlockSpec(memory_space=pltpu.VMEM),
        out_shape=jax.ShapeDtypeStruct((cfg.hq, w, 2 * w), F32),
    )(rel_bias, buckets)


def bias_grad(cfg, dbias, buckets, name):
    w = cfg.win

    def body(db_ref, bk_ref, o_ref, rows_ref):
        bk = bk_ref[...]
        for h in range(cfg.hq):
            d = db_ref[0, h]
            for e in range(1, dbias.shape[0]):
                d = d + db_ref[e, h]
            for b in range(cfg.nbuckets):
                rows_ref[b:b + 1, :] = jnp.sum(jnp.where(bk == b, d, 0.0), axis=0, keepdims=True)
            o_ref[h] = jnp.broadcast_to(jnp.sum(rows_ref[...], axis=1, keepdims=True), (cfg.nbuckets, 128))

    return pl.pallas_call(
        body, name=name,
        in_specs=[pl.BlockSpec(memory_space=pltpu.VMEM), pl.BlockSpec(memory_space=pltpu.VMEM)],
        out_specs=pl.BlockSpec(memory_space=pltpu.VMEM),
        out_shape=jax.ShapeDtypeStruct((cfg.hq, cfg.nbuckets, 128), F32),
        scratch_shapes=[pltpu.VMEM((cfg.nbuckets, 2 * w), F32)],
    )(dbias, buckets)


def _attn_probs(cfg, qh, kj, bias_h, sink, first_ok):
    s = _nt(qh, kj) * (1.0 / math.sqrt(cfg.hd)) + bias_h
    s = jnp.where(first_ok, s, NEG_INF)
    m = jnp.maximum(jnp.max(s, axis=-1, keepdims=True), sink)
    e = jnp.exp(s - m)
    es = jnp.exp(sink - m)
    inv = 1.0 / (jnp.sum(e, axis=-1, keepdims=True) + es)
    return e * inv, es * inv


def _attn_block_inputs(cfg, n, q_ref, kv_ref):
    w = cfg.win
    r0 = pl.multiple_of(n * w, w)
    rp = pl.multiple_of(jnp.maximum(n - 1, 0) * w, w)
    qb = q_ref[pl.ds(r0, w), :]
    kk = jnp.concatenate([kv_ref[pl.ds(rp, w), :], kv_ref[pl.ds(r0, w), :]], axis=0)
    col = lax.broadcasted_iota(jnp.int32, (w, 2 * w), 1)
    first_ok = (n > 0) | (col >= w)
    return r0, rp, qb, kk, first_ok


def _kv_col_block(cfg):
    assert cfg.qw % (2 * cfg.kvw) == 0
    return cfg.qw // (2 * cfg.kvw)


def attn_fwd(cfg, u, bias, sinks, name):
    t = u.shape[0]
    s, w, hd, g = cfg.s, cfg.win, cfg.hd, cfg.hq // cfg.hkv
    kvb = _kv_col_block(cfg)

    def body(q_ref, kv_ref, bias_ref, sink_ref, o_ref):
        def blk(n, carry):
            r0, _, qb, kk, first_ok = _attn_block_inputs(cfg, n, q_ref, kv_ref)
            outs = []
            for j in range(cfg.hkv):
                kj = kk[:, hd * j:hd * (j + 1)]
                vj = kk[:, cfg.kvw + hd * j:cfg.kvw + hd * (j + 1)]
                for gq in range(g):
                    h = j * g + gq
                    p, _ = _attn_probs(cfg, qb[:, hd * h:hd * (h + 1)], kj, bias_ref[h], sink_ref[h], first_ok)
                    outs.append(_nn(p.astype(BF16), vj))
            o_ref[pl.ds(r0, w), :] = jnp.concatenate(outs, axis=1).astype(BF16)
            return carry

        lax.fori_loop(0, s // w, blk, 0)

    return pl.pallas_call(
        body, name=name, grid=(t // s,),
        in_specs=[pl.BlockSpec((s, cfg.qw), lambda b: (b, 0)), pl.BlockSpec((s, 2 * cfg.kvw), lambda b: (b, kvb)),
                  pl.BlockSpec(bias.shape, lambda b: (0, 0, 0)), pl.BlockSpec(memory_space=pltpu.SMEM)],
        out_specs=pl.BlockSpec((s, cfg.qw), lambda b: (b, 0)),
        out_shape=jax.ShapeDtypeStruct((t, cfg.qw), BF16),
        compiler_params=_cparams("arbitrary"),
    )(u, u, bias, sinks)


def attn_bwd(cfg, u, dcat, bias, sinks, name):
    t = u.shape[0]
    s, w, hd, g = cfg.s, cfg.win, cfg.hd, cfg.hq // cfg.hkv
    kvb = _kv_col_block(cfg)
    scale = 1.0 / math.sqrt(hd)
    assert cfg.hq <= 8

    def body(q_ref, kv_ref, do_ref, bias_ref, sink_ref, du_ref, dbias_ref, dsink_ref, dkv_ref):
        @pl.when(pl.program_id(0) == 0)
        def _():
            dbias_ref[...] = jnp.zeros_like(dbias_ref)
            dsink_ref[...] = jnp.zeros_like(dsink_ref)

        dkv_ref[...] = jnp.zeros_like(dkv_ref)

        def blk(n, carry):
            r0, rp, qb, kk, first_ok = _attn_block_inputs(cfg, n, q_ref, kv_ref)
            dob = do_ref[pl.ds(r0, w), :]
            dqs, dks, dvs = [], [], []
            for j in range(cfg.hkv):
                kj = kk[:, hd * j:hd * (j + 1)]
                vj = kk[:, cfg.kvw + hd * j:cfg.kvw + hd * (j + 1)]
                dk = jnp.zeros((2 * w, hd), F32)
                dv = jnp.zeros((2 * w, hd), F32)
                for gq in range(g):
                    h = j * g + gq
                    qh = qb[:, hd * h:hd * (h + 1)]
                    doh = dob[:, hd * h:hd * (h + 1)]
                    p, ps = _attn_probs(cfg, qh, kj, bias_ref[h], sink_ref[h], first_ok)
                    dp = _nt(doh, vj)
                    delta = jnp.sum(p * dp, axis=-1, keepdims=True)
                    ds = p * (dp - delta)
                    dsink_ref[h:h + 1, :] += jnp.broadcast_to(-jnp.sum(ps * delta, axis=0, keepdims=True), (1, 128))
                    dbias_ref[h] += ds
                    dsb = ds.astype(BF16)
                    dqs.append(_nn(dsb, kj) * scale)
                    dk = dk + _tn(dsb, qh) * scale
                    dv = dv + _tn(p.astype(BF16), doh)
                dks.append(dk)
                dvs.append(dv)
            du_ref[pl.ds(r0, w), 0:cfg.qw] = jnp.concatenate(dqs, axis=1).astype(BF16)
            dkv = jnp.concatenate(dks + dvs, axis=1)
            dkv_ref[pl.ds(rp, w), :] += dkv[:w]
            dkv_ref[pl.ds(r0, w), :] += dkv[w:]
            return carry

        lax.fori_loop(0, s // w, blk, 0)
        du_ref[:, cfg.qw:] = dkv_ref[...].astype(BF16)

    wa = cfg.qw + 2 * cfg.kvw
    return pl.pallas_call(
        body, name=name, grid=(t // s,),
        in_specs=[pl.BlockSpec((s, cfg.qw), lambda b: (b, 0)), pl.BlockSpec((s, 2 * cfg.kvw), lambda b: (b, kvb)),
                  pl.BlockSpec((s, cfg.qw), lambda b: (b, 0)),
                  pl.BlockSpec(bias.shape, lambda b: (0, 0, 0)), pl.BlockSpec(memory_space=pltpu.SMEM)],
        out_specs=[pl.BlockSpec((s, wa), lambda b: (b, 0)), pl.BlockSpec(bias.shape, lambda b: (0, 0, 0)),
                   pl.BlockSpec((8, 128), lambda b: (0, 0))],
        out_shape=[jax.ShapeDtypeStruct((t, wa), BF16), jax.ShapeDtypeStruct(bias.shape, F32),
                   jax.ShapeDtypeStruct((8, 128), F32)],
        scratch_shapes=[pltpu.VMEM((s, 2 * cfg.kvw), F32)],
        compiler_params=_cparams("arbitrary"),
    )(u, u, dcat, bias, sinks)


def _shift_views(win, rc, pad):
    return [win] + [win[j:j + rc + pad - 8] for j in range(1, 8)]


def _tap(views, off, rc):
    a = 8 * (off // 8)
    return views[off % 8][a:a + rc]


def _conv_rows(views, w_ref, cw, pad, rc):
    acc = None
    for k in range(cw):
        term = _tap(views, pad - (cw - 1) + k, rc) * w_ref[k:k + 1, :]
        acc = term if acc is None else acc + term
    return acc


def _conv_rows_t(views, w_ref, cw, rc):
    acc = None
    for k in range(cw):
        term = _tap(views, cw - 1 - k, rc) * w_ref[k:k + 1, :]
        acc = term if acc is None else acc + term
    return acc


def _group_sum(x):
    acc = x[0:8]
    for i in range(1, x.shape[0] // 8):
        acc = acc + x[8 * i:8 * i + 8]
    return acc


def _conv_wgrad(views, dy, acc_ref, cw, pad, rc):
    for k in range(cw):
        acc_ref[k] += _group_sum(dy * _tap(views, pad - (cw - 1) + k, rc))


CONV_RC = 64
CONV_PAD = 32
GLU_RC = 256


def _conv_col_blocks(cfg):
    off = cfg.qw + 2 * cfg.kvw
    bw = math.gcd(off, cfg.cc)
    assert bw % 128 == 0
    n = cfg.cc // bw
    return bw, [off // bw + i for i in range(n)], [(off + cfg.cc) // bw + i for i in range(n)]


def _glu_inputs(a_refs, b_refs, rows):
    ga = jnp.concatenate([r[rows, :] for r in a_refs], axis=1).astype(F32)
    gb = jnp.concatenate([r[rows, :] for r in b_refs], axis=1).astype(F32)
    return ga, gb


def _fill_glu(cfg, a_refs, b_refs, xp_ref):
    xp_ref[0:CONV_PAD, :] = jnp.zeros((CONV_PAD, cfg.cc), F32)

    def fill(i, carry):
        r0 = pl.multiple_of(i * GLU_RC, GLU_RC)
        ga, gb = _glu_inputs(a_refs, b_refs, pl.ds(r0, GLU_RC))
        xp_ref[pl.ds(CONV_PAD + r0, GLU_RC), :] = ga * _sigmoid(gb)
        return carry

    lax.fori_loop(0, cfg.s // GLU_RC, fill, 0)


def _layernorm_stats(cv):
    mu = jnp.mean(cv, axis=-1, keepdims=True)
    xc = cv - mu
    rstd = lax.rsqrt(jnp.mean(xc * xc, axis=-1, keepdims=True) + LN_EPS)
    return xc * rstd, rstd


def conv_fwd(cfg, u, cw_w, cb, lg, lb, name):
    t = u.shape[0]
    s, cc, cw = cfg.s, cfg.cc, cfg.cw
    bw, a_idx, b_idx = _conv_col_blocks(cfg)
    nb = len(a_idx)

    def body(*refs):
        a_refs, b_refs = refs[:nb], refs[nb:2 * nb]
        w_ref, cb_ref, lg_ref, lb_ref, o_ref, xp_ref = refs[2 * nb:]
        _fill_glu(cfg, a_refs, b_refs, xp_ref)

        def chunk(i, carry):
            r0 = pl.multiple_of(i * CONV_RC, CONV_RC)
            views = _shift_views(xp_ref[pl.ds(r0, CONV_RC + CONV_PAD), :], CONV_RC, CONV_PAD)
            cv = _conv_rows(views, w_ref, cw, CONV_PAD, CONV_RC) + cb_ref[...]
            xhat, _ = _layernorm_stats(cv)
            ln = xhat * lg_ref[...] + lb_ref[...]
            o_ref[pl.ds(r0, CONV_RC), :] = (ln * _sigmoid(ln)).astype(BF16)
            return carry

        lax.fori_loop(0, s // CONV_RC, chunk, 0)

    def colspec(j):
        return pl.BlockSpec((s, bw), lambda b: (b, j))

    vec = pl.BlockSpec((1, cc), lambda b: (0, 0))
    return pl.pallas_call(
        body, name=name, grid=(t // s,),
        in_specs=[colspec(j) for j in a_idx + b_idx] + [pl.BlockSpec((cw, cc), lambda b: (0, 0)), vec, vec, vec],
        out_specs=pl.BlockSpec((s, cc), lambda b: (b, 0)),
        out_shape=jax.ShapeDtypeStruct((t, cc), BF16),
        scratch_shapes=[pltpu.VMEM((CONV_PAD + s, cc), F32)],
        compiler_params=_cparams("arbitrary"),
    )(*([u] * (2 * nb)), cw_w, cb, lg, lb)


def conv_bwd(cfg, u, dcat, cw_w, cb, lg, lb, name):
    t = u.shape[0]
    s, cc, cw = cfg.s, cfg.cc, cfg.cw
    bw, a_idx, b_idx = _conv_col_blocks(cfg)
    nb = len(a_idx)
    assert cfg.qw % cc == 0 and cw <= 32

    def body(*refs):
        a_refs, b_refs = refs[:nb], refs[nb:2 * nb]
        dc_ref, w_ref, cb_ref, lg_ref, lb_ref, du_ref, dw_ref, dvec_ref, xp_ref, dcv_ref, dwacc_ref = refs[2 * nb:]

        @pl.when(pl.program_id(0) == 0)
        def _():
            dw_ref[...] = jnp.zeros_like(dw_ref)
            dvec_ref[...] = jnp.zeros_like(dvec_ref)

        _fill_glu(cfg, a_refs, b_refs, xp_ref)
        dcv_ref[s:s + CONV_PAD, :] = jnp.zeros((CONV_PAD, cc), F32)
        dwacc_ref[...] = jnp.zeros_like(dwacc_ref)

        def chunk(i, carry):
            r0 = pl.multiple_of(i * CONV_RC, CONV_RC)
            views = _shift_views(xp_ref[pl.ds(r0, CONV_RC + CONV_PAD), :], CONV_RC, CONV_PAD)
            cv = _conv_rows(views, w_ref, cw, CONV_PAD, CONV_RC) + cb_ref[...]
            xhat, rstd = _layernorm_stats(cv)
            ln = xhat * lg_ref[...] + lb_ref[...]
            sg = _sigmoid(ln)
            dln = dc_ref[pl.ds(r0, CONV_RC), :].astype(F32) * (sg * (1.0 + ln * (1.0 - sg)))
            dxh = dln * lg_ref[...]
            dcv = rstd * (dxh - jnp.mean(dxh, axis=-1, keepdims=True)
                          - xhat * jnp.mean(dxh * xhat, axis=-1, keepdims=True))
            dcv_ref[pl.ds(r0, CONV_RC), :] = dcv
            dvec_ref[0:1, :] += jnp.sum(dcv, axis=0, keepdims=True)
            dvec_ref[1:2, :] += jnp.sum(dln * xhat, axis=0, keepdims=True)
            dvec_ref[2:3, :] += jnp.sum(dln, axis=0, keepdims=True)
            _conv_wgrad(views, dcv, dwacc_ref, cw, CONV_PAD, CONV_RC)
            return carry

        lax.fori_loop(0, s // CONV_RC, chunk, 0)
        for k in range(cw):
            dw_ref[k:k + 1, :] += jnp.sum(dwacc_ref[k], axis=0, keepdims=True)

        def chunk2(i, carry):
            r0 = pl.multiple_of(i * CONV_RC, CONV_RC)
            views = _shift_views(dcv_ref[pl.ds(r0, CONV_RC + CONV_PAD), :], CONV_RC, CONV_PAD)
            dglu = _conv_rows_t(views, w_ref, cw, CONV_RC)
            ga, gb = _glu_inputs(a_refs, b_refs, pl.ds(r0, CONV_RC))
            sgb = _sigmoid(gb)
            du_ref[pl.ds(r0, CONV_RC), 0:cc] = (dglu * sgb).astype(BF16)
            du_ref[pl.ds(r0, CONV_RC), cc:2 * cc] = (dglu * ga * sgb * (1.0 - sgb)).astype(BF16)
            return carry

        lax.fori_loop(0, s // CONV_RC, chunk2, 0)

    def colspec(j):
        return pl.BlockSpec((s, bw), lambda b: (b, j))

    vec = pl.BlockSpec((1, cc), lambda b: (0, 0))
    return pl.pallas_call(
        body, name=name, grid=(t // s,),
        in_specs=[colspec(j) for j in a_idx + b_idx]
        + [pl.BlockSpec((s, cc), lambda b: (b, cfg.qw // cc)), pl.BlockSpec((cw, cc), lambda b: (0, 0)), vec, vec, vec],
        out_specs=[pl.BlockSpec((s, 2 * cc), lambda b: (b, 0)), pl.BlockSpec((32, cc), lambda b: (0, 0)),
                   pl.BlockSpec((8, cc), lambda b: (0, 0))],
        out_shape=[jax.ShapeDtypeStruct((t, 2 * cc), BF16), jax.ShapeDtypeStruct((32, cc), F32),
                   jax.ShapeDtypeStruct((8, cc), F32)],
        scratch_shapes=[pltpu.VMEM((CONV_PAD + s, cc), F32), pltpu.VMEM((s + CONV_PAD, cc), F32),
                        pltpu.VMEM((cw, 8, cc), F32)],
        compiler_params=_cparams("arbitrary"),
    )(*([u] * (2 * nb)), dcat, cw_w, cb, lg, lb)


LRU_RC = 64
LRU_PAD = 8
SCAN_RC = 16
GELU_K = math.sqrt(2.0 / math.pi)


def _expm1_neg(z):
    return jnp.where(z > -0.05, z * (1.0 + z * (0.5 + z * (1.0 / 6.0 + z * (1.0 / 24.0)))), jnp.exp(z) - 1.0)


def _log_sigmoid(x):
    e = jnp.exp(-jnp.abs(x))
    log1p = jnp.where(e < 0.01, e * (1.0 - e * (0.5 - e * (1.0 / 3.0))), jnp.log(1.0 + e))
    return jnp.minimum(x, 0.0) - log1p


def _gelu(x):
    t = jnp.tanh(GELU_K * (x + 0.044715 * x * x * x))
    return 0.5 * x * (1.0 + t), t


def _gelu_grad(x, t):
    return 0.5 * (1.0 + t) + 0.5 * x * (1.0 - t * t) * GELU_K * (1.0 + 3.0 * 0.044715 * x * x)


def _lru_gates(xc, wa_ref, ba, wx_ref, bx, ls):
    nh = xc.shape[1] // 128
    xb = xc.astype(BF16)
    ra = jnp.concatenate([_nn(xb[:, 128 * h:128 * (h + 1)], wa_ref[h]) for h in range(nh)], axis=1) + ba
    ia = jnp.concatenate([_nn(xb[:, 128 * h:128 * (h + 1)], wx_ref[h]) for h in range(nh)], axis=1) + bx
    r = _sigmoid(ra)
    ig = _sigmoid(ia)
    log_a = RG_LRU_C * r * ls
    a = jnp.exp(log_a)
    mult = jnp.sqrt(-_expm1_neg(2.0 * log_a))
    return r, ig, a, mult


def _fill_padded(src_ref, dst_ref, s, ct):
    dst_ref[0:LRU_PAD, :] = jnp.zeros((LRU_PAD, ct), F32)

    def fill(i, carry):
        r0 = pl.multiple_of(i * GLU_RC, GLU_RC)
        dst_ref[pl.ds(LRU_PAD + r0, GLU_RC), :] = src_ref[pl.ds(r0, GLU_RC), :].astype(F32)
        return carry

    lax.fori_loop(0, s // GLU_RC, fill, 0)


def _lru_specs(cfg, ct):
    s, lw = cfg.s, cfg.lw
    nct = lw // ct
    nh = ct // 128
    act = [pl.BlockSpec((s, ct), lambda c, b: (b, c)), pl.BlockSpec((s, ct), lambda c, b: (b, nct + c))]
    vec = pl.BlockSpec((1, ct), lambda c, b: (0, c))
    gate_w = pl.BlockSpec((nh, 128, 128), lambda c, b: (c, 0, 0))
    params = [pl.BlockSpec((cfg.lcw, ct), lambda c, b: (0, c)), vec, gate_w, vec, gate_w, vec, vec]
    return nct, act, params


def lru_fwd(cfg, u, conv_w, conv_b, wa, ba, wx, bx, lam, name):
    t = u.shape[0]
    s, lw, lcw, ct = cfg.s, cfg.lw, cfg.lcw, cfg.ct_f
    nct, act, params = _lru_specs(cfg, ct)

    def body(gi_ref, ri_ref, cw_ref, cb_ref, wa_ref, ba_ref, wx_ref, bx_ref, lam_ref, y_ref, hs_ref,
             xp_ref, a_ref, b_ref):
        _fill_padded(ri_ref, xp_ref, s, ct)
        ls = _log_sigmoid(lam_ref[...])

        def chunk(i, carry):
            r0 = pl.multiple_of(i * LRU_RC, LRU_RC)
            views = _shift_views(xp_ref[pl.ds(r0, LRU_RC + LRU_PAD), :], LRU_RC, LRU_PAD)
            xc = _conv_rows(views, cw_ref, lcw, LRU_PAD, LRU_RC) + cb_ref[...]
            _, ig, a, mult = _lru_gates(xc, wa_ref, ba_ref[...], wx_ref, bx_ref[...], ls)
            a_ref[pl.ds(r0, LRU_RC), :] = a
            b_ref[pl.ds(r0, LRU_RC), :] = mult * (ig * xc)
            return carry

        lax.fori_loop(0, s // LRU_RC, chunk, 0)
        row = lax.broadcasted_iota(jnp.int32, (SCAN_RC, ct), 0)

        def scan(i, h_last):
            rows = pl.ds(pl.multiple_of(i * SCAN_RC, SCAN_RC), SCAN_RC)
            a = a_ref[rows, :]
            b = b_ref[rows, :]
            sft = 1
            while sft < SCAN_RC:
                a_sh = jnp.where(row >= sft, pltpu.roll(a, sft, 0), 1.0)
                b_sh = jnp.where(row >= sft, pltpu.roll(b, sft, 0), 0.0)
                b = a * b_sh + b
                a = a * a_sh
                sft *= 2
            h = a * h_last + b
            gate, _ = _gelu(gi_ref[rows, :].astype(F32))
            y_ref[rows, :] = (gate * h).astype(BF16)
            hs_ref[rows, :] = h.astype(BF16)
            return h[SCAN_RC - 1:SCAN_RC, :]

        lax.fori_loop(0, s // SCAN_RC, scan, jnp.zeros((1, ct), F32))

    out = pl.BlockSpec((s, ct), lambda c, b: (b, c))
    return pl.pallas_call(
        body, name=name, grid=(nct, t // s),
        in_specs=act + params,
        out_specs=[out, out],
        out_shape=[jax.ShapeDtypeStruct((t, lw), BF16), jax.ShapeDtypeStruct((t, lw), BF16)],
        scratch_shapes=[pltpu.VMEM((LRU_PAD + s, ct), F32), pltpu.VMEM((s, ct), F32), pltpu.VMEM((s, ct), F32)],
        compiler_params=_cparams("arbitrary", "arbitrary"),
    )(u, u, conv_w, conv_b, wa, ba, wx, bx, lam)


def lru_bwd(cfg, u, hs, dy, conv_w, conv_b, wa, ba, wx, bx, lam, name):
    t = u.shape[0]
    s, lw, lcw, ct = cfg.s, cfg.lw, cfg.lcw, cfg.ct_b
    nct, act, params = _lru_specs(cfg, ct)
    nh = ct // 128
    nscan = s // SCAN_RC
    assert lcw <= 8

    def body(gi_ref, ri_ref, hs_ref, dy_ref, cw_ref, cb_ref, wa_ref, ba_ref, wx_ref, bx_ref, lam_ref,
             dug_ref, dur_ref, dwa_ref, dwx_ref, dvec_ref, dcw_ref,
             xp_ref, hp_ref, a_ref, g_ref, dxc_ref, dwacc_ref):
        @pl.when(pl.program_id(1) == 0)
        def _():
            dwa_ref[...] = jnp.zeros_like(dwa_ref)
            dwx_ref[...] = jnp.zeros_like(dwx_ref)
            dvec_ref[...] = jnp.zeros_like(dvec_ref)
            dcw_ref[...] = jnp.zeros_like(dcw_ref)

        _fill_padded(ri_ref, xp_ref, s, ct)
        _fill_padded(hs_ref, hp_ref, s, ct)
        dxc_ref[s:s + LRU_PAD, :] = jnp.zeros((LRU_PAD, ct), F32)
        dwacc_ref[...] = jnp.zeros_like(dwacc_ref)
        lam = lam_ref[...]
        ls = _log_sigmoid(lam)

        def gates_at(r0):
            views = _shift_views(xp_ref[pl.ds(r0, LRU_RC + LRU_PAD), :], LRU_RC, LRU_PAD)
            xc = _conv_rows(views, cw_ref, lcw, LRU_PAD, LRU_RC) + cb_ref[...]
            return views, xc, _lru_gates(xc, wa_ref, ba_ref[...], wx_ref, bx_ref[...], ls)

        def chunk(i, carry):
            r0 = pl.multiple_of(i * LRU_RC, LRU_RC)
            rows = pl.ds(r0, LRU_RC)
            _, _, (_, _, a, _) = gates_at(r0)
            a_ref[rows, :] = a
            x = gi_ref[rows, :].astype(F32)
            gate, th = _gelu(x)
            dyv = dy_ref[rows, :].astype(F32)
            g_ref[rows, :] = dyv * gate
            dug_ref[rows, :] = (dyv * hp_ref[pl.ds(LRU_PAD + r0, LRU_RC), :] * _gelu_grad(x, th)).astype(BF16)
            return carry

        lax.fori_loop(0, s // LRU_RC, chunk, 0)
        row = lax.broadcasted_iota(jnp.int32, (SCAN_RC, ct), 0)

        def scan(ii, carry):
            g_next, a_next = carry
            rows = pl.ds(pl.multiple_of((nscan - 1 - ii) * SCAN_RC, SCAN_RC), SCAN_RC)
            a = a_ref[rows, :]
            d = g_ref[rows, :]
            c = jnp.where(row < SCAN_RC - 1, pltpu.roll(a, SCAN_RC - 1, 0), a_next)
            sft = 1
            while sft < SCAN_RC:
                c_sh = jnp.where(row < SCAN_RC - sft, pltpu.roll(c, SCAN_RC - sft, 0), 1.0)
                d_sh = jnp.where(row < SCAN_RC - sft, pltpu.roll(d, SCAN_RC - sft, 0), 0.0)
                d = d + c * d_sh
                c = c * c_sh
                sft *= 2
            g = d + c * g_next
            g_ref[rows, :] = g
            return g[0:1, :], a[0:1, :]

        lax.fori_loop(0, nscan, scan, (jnp.zeros((1, ct), F32), jnp.zeros((1, ct), F32)))

        def chunk3(i, carry):
            r0 = pl.multiple_of(i * LRU_RC, LRU_RC)
            rows = pl.ds(r0, LRU_RC)
            views, xc, (r, ig, a, mult) = gates_at(r0)
            g = g_ref[rows, :]
            h_prev = hp_ref[pl.ds(r0, LRU_RC + LRU_PAD), :][LRU_PAD - 1:LRU_PAD - 1 + LRU_RC]
            dix = g * mult
            di = dix * xc
            dxc = dix * ig
            da = g * h_prev - (g * ig * xc) * a / mult
            dlog_a = da * a
            dr = dlog_a * (RG_LRU_C * ls)
            dra = dr * r * (1.0 - r)
            dia = di * ig * (1.0 - ig)
            xb, drab, diab = xc.astype(BF16), dra.astype(BF16), dia.astype(BF16)
            dxg = []
            for h in range(nh):
                cols = slice(128 * h, 128 * (h + 1))
                dxg.append(_nt(drab[:, cols], wa_ref[h]) + _nt(diab[:, cols], wx_ref[h]))
                dwa_ref[h] += _tn(xb[:, cols], drab[:, cols])
                dwx_ref[h] += _tn(xb[:, cols], diab[:, cols])
            dxc = dxc + jnp.concatenate(dxg, axis=1)
            dvec_ref[0:1, :] += jnp.sum(dra, axis=0, keepdims=True)
            dvec_ref[1:2, :] += jnp.sum(dia, axis=0, keepdims=True)
            dvec_ref[2:3, :] += jnp.sum(dlog_a * r, axis=0, keepdims=True) * (RG_LRU_C * _sigmoid(-lam))
            dvec_ref[3:4, :] += jnp.sum(dxc, axis=0, keepdims=True)
            dxc_ref[rows, :] = dxc
            _conv_wgrad(views, dxc, dwacc_ref, lcw, LRU_PAD, LRU_RC)
            return carry

        lax.fori_loop(0, s // LRU_RC, chunk3, 0)
        for k in range(lcw):
            dcw_ref[k:k + 1, :] += jnp.sum(dwacc_ref[k], axis=0, keepdims=True)

        def chunk4(i, carry):
            r0 = pl.multiple_of(i * LRU_RC, LRU_RC)
            views = _shift_views(dxc_ref[pl.ds(r0, LRU_RC + LRU_PAD), :], LRU_RC, LRU_PAD)
            dur_ref[pl.ds(r0, LRU_RC), :] = _conv_rows_t(views, cw_ref, lcw, LRU_RC).astype(BF16)
            return carry

        lax.fori_loop(0, s // LRU_RC, chunk4, 0)

    blk = pl.BlockSpec((s, ct), lambda c, b: (b, c))
    acc8 = pl.BlockSpec((8, ct), lambda c, b: (0, c))
    gate_w = pl.BlockSpec((nh, 128, 128), lambda c, b: (c, 0, 0))
    return pl.pallas_call(
        body, name=name, grid=(nct, t // s),
        in_specs=act + [blk, blk] + params,
        out_specs=[blk, blk, gate_w, gate_w, acc8, acc8],
        out_shape=[jax.ShapeDtypeStruct((t, lw), BF16), jax.ShapeDtypeStruct((t, lw), BF16),
                   jax.ShapeDtypeStruct((cfg.lh, 128, 128), F32), jax.ShapeDtypeStruct((cfg.lh, 128, 128), F32),
                   jax.ShapeDtypeStruct((8, lw), F32), jax.ShapeDtypeStruct((8, lw), F32)],
        scratch_shapes=[pltpu.VMEM((LRU_PAD + s, ct), F32), pltpu.VMEM((LRU_PAD + s, ct), F32),
                        pltpu.VMEM((s, ct), F32), pltpu.VMEM((s, ct), F32), pltpu.VMEM((s + LRU_PAD, ct), F32),
                        pltpu.VMEM((lcw, 8, ct), F32)],
        compiler_params=_cparams("arbitrary", "arbitrary"),
    )(u, u, hs, dy, conv_w, conv_b, wa, ba, wx, bx, lam)


def local_step(cfg, x, tgt, p):
    buckets = bucket_table(cfg)
    bias = bias_build(cfg, p["rel_bias"], buckets, "bias_build")
    ga_w, gx_w = p["gate_a_w"].astype(BF16), p["gate_x_w"].astype(BF16)
    h = x
    saved = []
    for l in range(cfg.depth):
        i = l // 2
        s = {"h0": h}
        s["h1"], s["xn1"], s["g1"], s["u1"] = ffn_fwd(cfg, h, p["norm_ffn1"][l][None], p["wf"], 6 * l, f"ffn1_fwd_{l}")
        if l % 2 == 0:
            s["um"], s["xnm"] = norm_proj(cfg, s["h1"], p["norm_mix"][l][None], p["even_in"], f"mix_in_{l}", wi=i)
            attn = attn_fwd(cfg, s["um"], bias, p["attn_sinks"][i], f"attn_fwd_{l}")
            c = conv_fwd(cfg, s["um"], p["conv_b_w"][i], p["conv_b_b"][i][None], p["conv_ln_g"][i][None],
                         p["conv_ln_b"][i][None], f"conv_fwd_{l}")
            s["parts"] = [attn, c]
            s["h2"] = proj_residual(cfg, s["h1"], s["parts"], p["even_out"], f"mix_out_{l}", wi=i)
        else:
            s["um"], s["xnm"] = norm_proj(cfg, s["h1"], p["norm_mix"][l][None], p["odd_in"], f"mix_in_{l}", wi=i)
            y, s["hs"] = lru_fwd(cfg, s["um"], p["lru_conv_w"][i], p["lru_conv_b"][i][None], ga_w[i], p["gate_a_b"][i][None],
                                 gx_w[i], p["gate_x_b"][i][None], p["lru_lambda"][i][None], f"lru_fwd_{l}")
            s["parts"] = [y]
            s["h2"] = proj_residual(cfg, s["h1"], s["parts"], p["odd_out"], f"mix_out_{l}", wi=i)
        h, s["xn2"], s["g2"], s["u2"] = ffn_fwd(cfg, s["h2"], p["norm_ffn2"][l][None], p["wf"], 6 * l + 3, f"ffn2_fwd_{l}")
        saved.append(s)

    loss, dh, dnf = loss_head(cfg, h, p["norm_final"][None], tgt, "loss_head")
    big = [None] * cfg.depth
    sm = {k: [None] * cfg.depth for k in ("norm_ffn1", "norm_mix", "norm_ffn2")}
    ne, no = (cfg.depth + 1) // 2, cfg.depth // 2
    for k in ("attn_sinks", "conv_b_w", "conv_b_b", "conv_ln_g", "conv_ln_b", "dbias"):
        sm[k] = [None] * ne
    for k in ("lru_conv_w", "lru_conv_b", "gate_a_w", "gate_a_b", "gate_x_w", "gate_x_b", "lru_lambda"):
        sm[k] = [None] * no
    for l in reversed(range(cfg.depth)):
        i = l // 2
        s = saved[l]
        dh, dout, dg, du, dn = ffn_bwd_x(cfg, dh, s["h2"], p["norm_ffn2"][l][None], s["g2"], s["u2"], p["wf"], 6 * l + 3,
                                         f"ffn2_bwd_x_{l}")
        sm["norm_ffn2"][l] = dn[0]
        gf = ffn_bwd_w(cfg, s["xn2"], dout, s["g2"], s["u2"], dg, du, f"ffn2_bwd_w_{l}", slot=1, nslot=2)
        w_out = p["even_out"] if l % 2 == 0 else p["odd_out"]
        w_in = p["even_in"] if l % 2 == 0 else p["odd_in"]
        dcat = proj_bwd_act(cfg, dh, w_out, f"mix_out_bwd_{l}", wi=i)
        g_out = [grad_weight(cfg, part, dh, f"mix_out_gw{j}_{l}") for j, part in enumerate(s["parts"])]
        if l % 2 == 0:
            du_a, sm["dbias"][i], dsink = attn_bwd(cfg, s["um"], dcat, bias, p["attn_sinks"][i], f"attn_bwd_{l}")
            du_c, dcw, dvec = conv_bwd(cfg, s["um"], dcat, p["conv_b_w"][i], p["conv_b_b"][i][None],
                                       p["conv_ln_g"][i][None], p["conv_ln_b"][i][None], f"conv_bwd_{l}")
            sm["attn_sinks"][i] = dsink[:cfg.hq, 0]
            sm["conv_b_w"][i] = dcw[:cfg.cw]
            sm["conv_b_b"][i], sm["conv_ln_g"][i], sm["conv_ln_b"][i] = dvec[0], dvec[1], dvec[2]
            dparts = [du_a, du_c]
        else:
            dug, dur, dwa, dwx, dvec, dcw = lru_bwd(
                cfg, s["um"], s["hs"], dcat, p["lru_conv_w"][i], p["lru_conv_b"][i][None], ga_w[i], p["gate_a_b"][i][None],
                gx_w[i], p["gate_x_b"][i][None], p["lru_lambda"][i][None], f"lru_bwd_{l}")
            sm["gate_a_w"][i], sm["gate_x_w"][i] = dwa, dwx
            sm["gate_a_b"][i], sm["gate_x_b"][i], sm["lru_lambda"][i], sm["lru_conv_b"][i] = dvec[0], dvec[1], dvec[2], dvec[3]
            sm["lru_conv_w"][i] = dcw[:cfg.lcw]
            dparts = [dug, dur]
        g_in = [grad_weight(cfg, dp, s["xnm"], f"mix_in_gw{j}_{l}") for j, dp in enumerate(dparts)]
        dh, dn = norm_proj_bwd(cfg, dh, s["h1"], p["norm_mix"][l][None], dparts, w_in, f"mix_in_bwd_{l}", wi=i)
        sm["norm_mix"][l] = dn[0]
        dh, dout, dg, du, dn = ffn_bwd_x(cfg, dh, s["h0"], p["norm_ffn1"][l][None], s["g1"], s["u1"], p["wf"], 6 * l,
                                         f"ffn1_bwd_x_{l}")
        sm["norm_ffn1"][l] = dn[0]
        gf = ffn_bwd_w(cfg, s["xn1"], dout, s["g1"], s["u1"], dg, du, f"ffn1_bwd_w_{l}", slot=0, nslot=2, prev=gf)
        big[l] = {"ffn": gf,
                  "in": g_in[0] if len(g_in) == 1 else jnp.concatenate(g_in, axis=0),
                  "out": g_out[0] if len(g_out) == 1 else jnp.concatenate(g_out, axis=0)}
    drb = bias_grad(cfg, jnp.stack(sm.pop("dbias")), buckets, "bias_grad")
    small = {k: jnp.stack(v) for k, v in sm.items()}
    small["rel_bias"] = drb[:, :, 0].T
    small["norm_final"] = dnf[0]
    return loss, dh, big, small


MESH = pl.DeviceIdType.MESH
ANY = pl.BlockSpec(memory_space=pl.ANY)


def _place():
    return lax.axis_index("x"), lax.axis_index("y"), lax.axis_index("c")


def all_gather(srcs, name):
    ng = len(srcs)

    def body(*refs):
        x_refs, o_refs = refs[:ng], refs[ng:2 * ng]
        send_sems, recv_sems, local_sems = refs[2 * ng:]
        x, y, c = _place()
        me, sibling = (x, y, c), (x, y, 1 - c)
        chips = [(1 - x, y), (x, 1 - y), (1 - x, 1 - y)]

        def copy(gi, k, block, to, src=None):
            dst = o_refs[gi].at[:, 4 * block[0] + 2 * block[1] + block[2]]
            return pltpu.make_async_remote_copy(
                src_ref=dst if src is None else src, dst_ref=dst, send_sem=send_sems.at[gi, k],
                recv_sem=recv_sems.at[gi, k], device_id=to, device_id_type=MESH)

        mine = [pltpu.make_async_copy(x_refs[gi], o_refs[gi].at[:, 4 * x + 2 * y + c], local_sems.at[gi])
                for gi in range(ng)]
        for cp in mine:
            cp.start()
        first = []
        for gi in range(ng):
            first.append(copy(gi, 0, me, sibling, src=x_refs[gi]))
            first += [copy(gi, 1 + j, me, (*chip, c), src=x_refs[gi]) for j, chip in enumerate(chips)]
        for cp in first:
            cp.start()
        passed = []
        for j, chip in enumerate(chips):
            for gi in range(ng):
                copy(gi, 1 + j, (*chip, c), me).wait_recv()
                cp = copy(gi, 4 + j, (*chip, c), sibling)
                cp.start()
                passed.append(cp)
        for gi in range(ng):
            copy(gi, 0, sibling, me).wait_recv()
            for j, chip in enumerate(chips):
                copy(gi, 4 + j, (*chip, 1 - c), me).wait_recv()
        for cp in first + passed:
            cp.wait_send()
        for cp in mine:
            cp.wait()

    return pl.pallas_call(
        body, name=name,
        in_specs=[ANY] * ng, out_specs=[ANY] * ng,
        out_shape=[jax.ShapeDtypeStruct((s.shape[0], N_DEV) + s.shape[1:], s.dtype) for s in srcs],
        scratch_shapes=[pltpu.SemaphoreType.DMA((ng, 7)), pltpu.SemaphoreType.DMA((ng, 7)),
                        pltpu.SemaphoreType.DMA((ng,))],
    )(*srcs)


def pair_exchange(bufs, name):
    ng = len(bufs)

    def body(*refs):
        b_refs, o_refs = refs[:ng], refs[ng:2 * ng]
        send_sems, recv_sems = refs[2 * ng:]
        x, y, c = _place()
        copies = [pltpu.make_async_remote_copy(
            src_ref=b_refs[gi].at[:, :, 1 - c], dst_ref=o_refs[gi], send_sem=send_sems.at[gi], recv_sem=recv_sems.at[gi],
            device_id=(x, y, 1 - c), device_id_type=MESH) for gi in range(ng)]
        for cp in copies:
            cp.start()
        for cp in copies:
            cp.wait()

    return pl.pallas_call(
        body, name=name,
        in_specs=[ANY] * ng, out_specs=[ANY] * ng,
        out_shape=[jax.ShapeDtypeStruct(b.shape[:2] + b.shape[3:], b.dtype) for b in bufs],
        scratch_shapes=[pltpu.SemaphoreType.DMA((ng,)), pltpu.SemaphoreType.DMA((ng,))],
    )(*bufs)


def chip_exchange(qs, name):
    ng = len(qs)

    def body(*refs):
        q_refs, o_refs = refs[:ng], refs[ng:2 * ng]
        send_sems, recv_sems = refs[2 * ng:]
        x, y, c = _place()
        chips = [(1 - x, y), (x, 1 - y), (1 - x, 1 - y)]
        copies = [pltpu.make_async_remote_copy(
            src_ref=q_refs[gi].at[:, 2 * chip[0] + chip[1]], dst_ref=o_refs[gi].at[:, j],
            send_sem=send_sems.at[gi, j], recv_sem=recv_sems.at[gi, j],
            device_id=(*chip, c), device_id_type=MESH) for gi in range(ng) for j, chip in enumerate(chips)]
        for cp in copies:
            cp.start()
        for cp in copies:
            cp.wait()

    return pl.pallas_call(
        body, name=name,
        in_specs=[ANY] * ng, out_specs=[ANY] * ng,
        out_shape=[jax.ShapeDtypeStruct((q.shape[0], 3) + q.shape[2:], q.dtype) for q in qs],
        scratch_shapes=[pltpu.SemaphoreType.DMA((ng, 3)), pltpu.SemaphoreType.DMA((ng, 3))],
    )(*qs)


def pair_sum(buf, recv, core, name):
    n, _, _, r, cdim = buf.shape

    def body(core_ref, a_ref, b_ref, o_ref):
        o_ref[...] = (a_ref[...].astype(F32) + b_ref[...].astype(F32)).astype(BF16)

    blk = pl.BlockSpec((None, None, r, cdim), lambda i, k, core_ref: (i, k, 0, 0))
    return pl.pallas_call(
        body, name=name,
        grid_spec=pltpu.PrefetchScalarGridSpec(
            num_scalar_prefetch=1, grid=(n, 4),
            in_specs=[pl.BlockSpec((None, None, None, r, cdim), lambda i, k, core_ref: (i, k, core_ref[0], 0, 0)), blk],
            out_specs=blk),
        out_shape=jax.ShapeDtypeStruct((n, 4, r, cdim), BF16),
    )(core, buf, recv)


def chip_sum(q, recv, chip, name):
    n, _, r, cdim = q.shape

    def body(chip_ref, a_ref, b_ref, o_ref):
        acc = a_ref[...].astype(F32)
        for j in range(3):
            acc = acc + b_ref[j].astype(F32)
        o_ref[...] = acc

    return pl.pallas_call(
        body, name=name,
        grid_spec=pltpu.PrefetchScalarGridSpec(
            num_scalar_prefetch=1, grid=(n,),
            in_specs=[pl.BlockSpec((None, None, r, cdim), lambda i, chip_ref: (i, chip_ref[0], 0, 0)),
                      pl.BlockSpec((None, 3, r, cdim), lambda i, chip_ref: (i, 0, 0, 0))],
            out_specs=pl.BlockSpec((None, r, cdim), lambda i, chip_ref: (i, 0, 0))),
        out_shape=jax.ShapeDtypeStruct((n, r, cdim), F32),
    )(chip, q, recv)


def sum_blocks(a, name):
    def body(a_ref, o_ref):
        acc = a_ref[0]
        for d in range(1, a.shape[0]):
            acc = acc + a_ref[d]
        o_ref[...] = acc

    return pl.pallas_call(body, name=name, out_shape=jax.ShapeDtypeStruct(a.shape[1:], F32),
                          compiler_params=pltpu.CompilerParams(vmem_limit_bytes=VMEM_LIMIT))(a)


def adamw(w, g, m, v, name):
    c1 = 1.0 / (1.0 - ADAM_B1 ** ADAM_STEP)
    c2 = 1.0 / (1.0 - ADAM_B2 ** ADAM_STEP)

    def body(w_ref, g_ref, m_ref, v_ref, d_ref, mo_ref, vo_ref):
        gg = g_ref[...]
        m2 = ADAM_B1 * m_ref[...] + (1.0 - ADAM_B1) * gg
        v2 = ADAM_B2 * v_ref[...] + (1.0 - ADAM_B2) * (gg * gg)
        mo_ref[...] = m2
        vo_ref[...] = v2
        d_ref[...] = -ADAM_LR * ((m2 * c1) / (jnp.sqrt(v2 * c2) + ADAM_EPS) + ADAM_WD * w_ref[...])

    out_shape = [jax.ShapeDtypeStruct(w.shape, F32)] * 3
    if w.ndim == 2:
        return pl.pallas_call(body, name=name, out_shape=out_shape,
                              compiler_params=pltpu.CompilerParams(vmem_limit_bytes=VMEM_LIMIT))(w, g, m, v)
    blk = pl.BlockSpec((None,) + w.shape[1:], lambda i: (i, 0, 0))
    return pl.pallas_call(body, name=name, grid=(w.shape[0],), in_specs=[blk] * 4, out_specs=[blk] * 3,
                          out_shape=out_shape, compiler_params=_cparams("arbitrary"))(w, g, m, v)


WEIGHTS = ("norm_ffn1", "ffn1_wg", "ffn1_wu", "ffn1_wd", "norm_mix", "norm_ffn2", "ffn2_wg", "ffn2_wu", "ffn2_wd",
           "rel_bias", "even_w_in", "attn_sinks", "conv_b_w", "conv_b_b", "conv_ln_g", "conv_ln_b", "even_w_out",
           "odd_w_in", "lru_conv_w", "lru_conv_b", "gate_a_w", "gate_a_b", "gate_x_w", "gate_x_b", "lru_lambda",
           "odd_w_out", "norm_final")
BIG = ("ffn1_wg", "ffn1_wu", "ffn1_wd", "ffn2_wg", "ffn2_wu", "ffn2_wd", "even_w_in", "even_w_out", "odd_w_in", "odd_w_out")
SMALL = tuple(n for n in WEIGHTS if n not in BIG)
SMALL_SHARDED = ("conv_b_w", "lru_conv_w", "lru_conv_b", "gate_a_b", "gate_x_b", "lru_lambda")
PACK_ALIGN = 1024


def _pack(arrays):
    parts = []
    for a in arrays:
        flat = a.reshape(-1)
        parts.append(jnp.pad(flat, (0, -flat.shape[0] % PACK_ALIGN)))
    return jnp.concatenate(parts).reshape(-1, 128)


def _unpack(packed, shapes, lead=()):
    flat = packed.reshape(lead + (-1,))
    out, off = [], 0
    for shp in shapes:
        size = math.prod(shp)
        out.append(flat[..., off:off + size].reshape(lead + tuple(shp)))
        off += size + (-size % PACK_ALIGN)
    return out


def _unshard_last(blocks):
    nd = blocks.ndim
    moved = jnp.moveaxis(blocks, 0, nd - 2)
    return moved.reshape(moved.shape[:-2] + (-1,))


def _step(cfg, x, weights, loss_target, ms, vs):
    w = dict(zip(WEIGHTS, weights))
    m = dict(zip(WEIGHTS, ms))
    v = dict(zip(WEIGHTS, vs))
    px, py, pc = _place()
    dev = 4 * px + 2 * py + pc
    core = jnp.reshape(pc, (1,)).astype(jnp.int32)
    chip = jnp.reshape(2 * px + py, (1,)).astype(jnp.int32)
    d = cfg.d
    t = cfg.bl * cfg.s

    def rows(name):
        a = w[name]
        return (a if name.endswith(("wd", "w_out")) else a.transpose(0, 2, 1)).astype(BF16)

    ffn_src = jnp.stack([rows(n) for n in BIG[:6]], axis=1).reshape(cfg.depth * 6, cfg.f // N_DEV, d)
    small_src = _pack([w[n] for n in SMALL_SHARDED])[None]
    gathered = all_gather([ffn_src, rows("even_w_in"), rows("even_w_out"), rows("odd_w_in"), rows("odd_w_out"), small_src],
                          "all_gather_weights")
    full = [g.reshape(g.shape[0], -1, g.shape[-1]) for g in gathered[:5]]
    p = {n: w[n] for n in SMALL if n not in SMALL_SHARDED}
    p.update(wf=full[0], even_in=full[1], even_out=full[2], odd_in=full[3], odd_out=full[4])
    for n, blocks in zip(SMALL_SHARDED, _unpack(gathered[5][0], [w[n].shape for n in SMALL_SHARDED], lead=(N_DEV,))):
        p[n] = _unshard_last(blocks)

    lossp, gx, big, small = local_step(cfg, x.reshape(t, d), loss_target.reshape(t, d), p)
    loss = lax.psum(lossp[0, 0], ("x", "y", "c"))

    shard_rows = []
    for l in range(cfg.depth):
        bufs = [big[l][k] for k in ("ffn", "in", "out")]
        bufs = [b if b.ndim == 3 else b[None] for b in bufs]
        bufs = [b.reshape(b.shape[0], 4, 2, b.shape[1] // N_DEV, d) for b in bufs]
        recv = pair_exchange(bufs, f"rs_pair_{l}")
        qs = [pair_sum(b, r, core, f"rs_pair_sum{j}_{l}") for j, (b, r) in enumerate(zip(bufs, recv))]
        recv = chip_exchange(qs, f"rs_chip_{l}")
        shard_rows.append([chip_sum(q, r, chip, f"rs_chip_sum{j}_{l}") for j, (q, r) in enumerate(zip(qs, recv))])

    grads = {}
    ffn_g = jnp.stack([sr[0] for sr in shard_rows])
    for j, n in enumerate(BIG[:6]):
        grads[n] = ffn_g[:, j] if n.endswith("wd") else ffn_g[:, j].transpose(0, 2, 1)
    grads["even_w_in"] = jnp.stack([shard_rows[l][1][0] for l in range(0, cfg.depth, 2)]).transpose(0, 2, 1)
    grads["even_w_out"] = jnp.stack([shard_rows[l][2][0] for l in range(0, cfg.depth, 2)])
    grads["odd_w_in"] = jnp.stack([shard_rows[l][1][0] for l in range(1, cfg.depth, 2)]).transpose(0, 2, 1)
    grads["odd_w_out"] = jnp.stack([shard_rows[l][2][0] for l in range(1, cfg.depth, 2)])

    full_shapes = [small[n].shape for n in SMALL]
    parts = all_gather([_pack([small[n] for n in SMALL])[None]], "all_gather_small_grads")[0][0]
    for n, g in zip(SMALL, _unpack(sum_blocks(parts, "sum_small_grads"), full_shapes)):
        if n in SMALL_SHARDED:
            width = w[n].shape[-1]
            g = lax.dynamic_slice_in_dim(g, dev * width, width, axis=g.ndim - 1)
        grads[n] = g

    delta, new_m, new_v = {}, {}, {}
    for n in BIG:
        delta[n], new_m[n], new_v[n] = adamw(w[n], grads[n], m[n], v[n], f"adamw_{n}")
    shapes = [w[n].shape for n in SMALL]
    packed = adamw(*[_pack([src[n] for n in SMALL]) for src in (w, grads, m, v)], "adamw_small")
    for out, pk in zip((delta, new_m, new_v), packed):
        out.update(zip(SMALL, _unpack(pk, shapes)))

    return (loss, gx.reshape(x.shape), *[grads[n] for n in WEIGHTS], *[delta[n] for n in WEIGHTS],
            *[new_m[n] for n in WEIGHTS], *[new_v[n] for n in WEIGHTS])


def kernel(x, norm_ffn1, ffn1_wg, ffn1_wu, ffn1_wd, norm_mix, norm_ffn2, ffn2_wg, ffn2_wu, ffn2_wd, rel_bias, even_w_in, attn_sinks, conv_b_w, conv_b_b, conv_ln_g, conv_ln_b, even_w_out, odd_w_in, lru_conv_w, lru_conv_b, gate_a_w, gate_a_b, gate_x_w, gate_x_b, lru_lambda, odd_w_out, norm_final, loss_target, m_norm_ffn1, m_ffn1_wg, m_ffn1_wu, m_ffn1_wd, m_norm_mix, m_norm_ffn2, m_ffn2_wg, m_ffn2_wu, m_ffn2_wd, m_rel_bias, m_even_w_in, m_attn_sinks, m_conv_b_w, m_conv_b_b, m_conv_ln_g, m_conv_ln_b, m_even_w_out, m_odd_w_in, m_lru_conv_w, m_lru_conv_b, m_gate_a_w, m_gate_a_b, m_gate_x_w, m_gate_x_b, m_lru_lambda, m_odd_w_out, m_norm_final, v_norm_ffn1, v_ffn1_wg, v_ffn1_wu, v_ffn1_wd, v_norm_mix, v_norm_ffn2, v_ffn2_wg, v_ffn2_wu, v_ffn2_wd, v_rel_bias, v_even_w_in, v_attn_sinks, v_conv_b_w, v_conv_b_b, v_conv_ln_g, v_conv_ln_b, v_even_w_out, v_odd_w_in, v_lru_conv_w, v_lru_conv_b, v_gate_a_w, v_gate_a_b, v_gate_x_w, v_gate_x_b, v_lru_lambda, v_odd_w_out, v_norm_final):
    weights = (norm_ffn1, ffn1_wg, ffn1_wu, ffn1_wd, norm_mix, norm_ffn2, ffn2_wg, ffn2_wu, ffn2_wd, rel_bias, even_w_in, attn_sinks, conv_b_w, conv_b_b, conv_ln_g, conv_ln_b, even_w_out, odd_w_in, lru_conv_w, lru_conv_b, gate_a_w, gate_a_b, gate_x_w, gate_x_b, lru_lambda, odd_w_out, norm_final)
    ms = (m_norm_ffn1, m_ffn1_wg, m_ffn1_wu, m_ffn1_wd, m_norm_mix, m_norm_ffn2, m_ffn2_wg, m_ffn2_wu, m_ffn2_wd, m_rel_bias, m_even_w_in, m_attn_sinks, m_conv_b_w, m_conv_b_b, m_conv_ln_g, m_conv_ln_b, m_even_w_out, m_odd_w_in, m_lru_conv_w, m_lru_conv_b, m_gate_a_w, m_gate_a_b, m_gate_x_w, m_gate_x_b, m_lru_lambda, m_odd_w_out, m_norm_final)
    vs = (v_norm_ffn1, v_ffn1_wg, v_ffn1_wu, v_ffn1_wd, v_norm_mix, v_norm_ffn2, v_ffn2_wg, v_ffn2_wu, v_ffn2_wd, v_rel_bias, v_even_w_in, v_attn_sinks, v_conv_b_w, v_conv_b_b, v_conv_ln_g, v_conv_ln_b, v_even_w_out, v_odd_w_in, v_lru_conv_w, v_lru_conv_b, v_gate_a_w, v_gate_a_b, v_gate_x_w, v_gate_x_b, v_lru_lambda, v_odd_w_out, v_norm_final)
    return _step(Cfg(), x, weights, loss_target, ms, vs)
```

```python
import math
from typing import NamedTuple

import jax
import jax.numpy as jnp
from jax import lax
from jax.experimental import pallas as pl
from jax.experimental.pallas import tpu as pltpu

F32 = jnp.float32
BF16 = jnp.bfloat16
RMS_EPS = 1e-6
LN_EPS = 1e-5
NEG_INF = -1e30
RG_LRU_C = 8.0
ADAM_LR = 0.001
ADAM_B1 = 0.9
ADAM_B2 = 0.999
ADAM_EPS = 1e-08
ADAM_WD = 0.01
ADAM_STEP = 10
N_DEV = 8
VMEM_LIMIT = 56 * 1024 * 1024


class Cfg(NamedTuple):
    d: int = 1024
    f: int = 2816
    s: int = 2048
    bl: int = 4
    hq: int = 8
    hkv: int = 2
    hd: int = 64
    win: int = 128
    cc: int = 512
    cw: int = 31
    lh: int = 8
    lb: int = 128
    lcw: int = 4
    nbuckets: int = 32
    max_dist: int = 128
    depth: int = 4
    tm: int = 512
    tm_ffn: int = 1024
    tf: int = 256
    tk_ffn: int = 256
    tf_w: int = 1408
    tk: int = 512
    tr: int = 256
    ct_f: int = 512
    ct_b: int = 256

    @property
    def qw(self):
        return self.hq * self.hd

    @property
    def kvw(self):
        return self.hkv * self.hd

    @property
    def even_in(self):
        return self.qw + 2 * self.kvw + 2 * self.cc

    @property
    def even_cat(self):
        return self.qw + self.cc

    @property
    def lw(self):
        return self.lh * self.lb


def _cparams(*sem):
    return pltpu.CompilerParams(dimension_semantics=sem, vmem_limit_bytes=VMEM_LIMIT)


def _nt(a, b):
    return lax.dot_general(a, b, (((1,), (1,)), ((), ())), preferred_element_type=F32)


def _nn(a, b):
    return lax.dot_general(a, b, (((1,), (0,)), ((), ())), preferred_element_type=F32)


def _tn(a, b):
    return lax.dot_general(a, b, (((0,), (0,)), ((), ())), preferred_element_type=F32)


def _rstd(h):
    return lax.rsqrt(jnp.mean(h * h, axis=-1, keepdims=True) + RMS_EPS)


def _rms_bwd(h, nw, dxn):
    rstd = _rstd(h)
    dyg = dxn * nw
    dnw = jnp.sum(dxn * h * rstd, axis=0, keepdims=True)
    dx = rstd * (dyg - h * (rstd * rstd) * jnp.mean(dyg * h, axis=-1, keepdims=True))
    return dx, dnw


def _sigmoid(x):
    return 0.5 * jnp.tanh(0.5 * x) + 0.5


FFN_SLABS = 4


def _ffn_wspecs(tf, d, gu, md):
    return [pl.BlockSpec((2, tf, d), lambda i, j: (gu, j, 0)), pl.BlockSpec((None, tf, d), lambda i, j: (md, j, 0))]


class Riders(NamedTuple):
    inputs: tuple
    out_shape: tuple
    scratch: tuple
    start: object
    wait: object


def _ride(riders, grid, n_in, n_out, body):
    if riders is None:
        return body, [], [], [], [], []
    ni, no, ns = len(riders.inputs), len(riders.out_shape), len(riders.scratch)

    def full(*refs):
        ins, rin = refs[:n_in], refs[n_in:n_in + ni]
        outs = refs[n_in + ni:n_in + ni + n_out]
        rout = refs[n_in + ni + n_out:n_in + ni + n_out + no]
        rest = refs[n_in + ni + n_out + no:]
        scratch, sems = rest[:len(rest) - ns], rest[len(rest) - ns:]
        first = last = None
        for axis, size in enumerate(grid):
            pid = pl.program_id(axis)
            first = (pid == 0) if first is None else first & (pid == 0)
            last = (pid == size - 1) if last is None else last & (pid == size - 1)

        @pl.when(first)
        def _():
            riders.start(rin, rout, sems)

        body(*ins, *outs, *scratch)

        @pl.when(last)
        def _():
            riders.wait(rin, rout, sems)

    any_spec = pl.BlockSpec(memory_space=pl.ANY)
    return full, list(riders.inputs), [any_spec] * ni, [any_spec] * no, list(riders.out_shape), list(riders.scratch)


def ffn_fwd(cfg, h, nw, wts, gu, md, name, riders=None):
    t, d = h.shape
    f = wts.shape[1]
    tm, tf = cfg.tm_ffn, cfg.tf
    nj = f // tf

    def body(h_ref, nw_ref, wgu_ref, wd_ref, ho_ref, xn_ref, g_ref, u_ref, acc_ref):
        j = pl.program_id(1)

        @pl.when(j == 0)
        def _():
            hh = h_ref[...]
            xn_ref[...] = (hh * _rstd(hh) * nw_ref[...]).astype(BF16)
            acc_ref[...] = jnp.zeros_like(acc_ref)

        gu = _nt(xn_ref[...], wgu_ref[...].reshape(2 * tf, d))
        g, u = gu[:, :tf], gu[:, tf:]
        g_ref[...] = g.astype(BF16)
        u_ref[...] = u.astype(BF16)
        acc_ref[...] += _nn((g * _sigmoid(g) * u).astype(BF16), wd_ref[...])

        @pl.when(j == nj - 1)
        def _():
            ho_ref[...] = h_ref[...] + 0.5 * acc_ref[...]

    row = pl.BlockSpec((tm, d), lambda i, j: (i, 0))
    hid = pl.BlockSpec((tm, tf), lambda i, j: (i, j))
    grid = (t // tm, nj)
    full, r_args, r_in, r_out, r_shape, r_scratch = _ride(riders, grid, 4, 4, body)
    return pl.pallas_call(
        full, name=name, grid=grid,
        in_specs=[row, pl.BlockSpec((1, d), lambda i, j: (0, 0))] + _ffn_wspecs(tf, d, gu, md) + r_in,
        out_specs=[row, row, hid, hid] + r_out,
        out_shape=[jax.ShapeDtypeStruct((t, d), F32), jax.ShapeDtypeStruct((t, d), BF16),
                   jax.ShapeDtypeStruct((t, f), BF16), jax.ShapeDtypeStruct((t, f), BF16)] + r_shape,
        scratch_shapes=[pltpu.VMEM((tm, d), F32)] + r_scratch,
        compiler_params=_cparams("arbitrary", "arbitrary"),
    )(h, nw, wts, wts, *r_args)


def ffn_bwd_x(cfg, dh, h, nw, g, u, wts, gu, md, name, riders=None):
    t, d = h.shape
    f = wts.shape[1]
    tm, tf = cfg.tm_ffn, cfg.tf
    nj = f // tf

    def body(dh_ref, h_ref, nw_ref, g_ref, u_ref, wgu_ref, wd_ref,
             dho_ref, dout_ref, dg_ref, du_ref, dnw_ref, acc_ref):
        i, j = pl.program_id(0), pl.program_id(1)

        @pl.when(j == 0)
        def _():
            dout_ref[...] = (0.5 * dh_ref[...]).astype(BF16)
            acc_ref[...] = jnp.zeros_like(acc_ref)

        @pl.when((i == 0) & (j == 0))
        def _():
            dnw_ref[...] = jnp.zeros_like(dnw_ref)

        w2 = wgu_ref[...].reshape(2 * tf, d)
        for sl in range(FFN_SLABS):
            rows = pl.ds(sl * (tm // FFN_SLABS), tm // FFN_SLABS)
            da = _nt(dout_ref[rows, :], wd_ref[...])
            gg = g_ref[rows, :].astype(F32)
            sig = _sigmoid(gg)
            dg = (da * u_ref[rows, :].astype(F32) * (sig * (1.0 + gg * (1.0 - sig)))).astype(BF16)
            du = (da * (gg * sig)).astype(BF16)
            dg_ref[rows, :] = dg
            du_ref[rows, :] = du
            acc_ref[rows, :] += _nn(jnp.concatenate([dg, du], axis=1), w2)

        @pl.when(j == nj - 1)
        def _():
            dx, dnw = _rms_bwd(h_ref[...], nw_ref[...], acc_ref[...])
            dnw_ref[0:1, :] += dnw
            dho_ref[...] = dh_ref[...] + dx

    row = pl.BlockSpec((tm, d), lambda i, j: (i, 0))
    hid = pl.BlockSpec((tm, tf), lambda i, j: (i, j))
    grid = (t // tm, nj)
    full, r_args, r_in, r_out, r_shape, r_scratch = _ride(riders, grid, 7, 5, body)
    return pl.pallas_call(
        full, name=name, grid=grid,
        in_specs=[row, row, pl.BlockSpec((1, d), lambda i, j: (0, 0)), hid, hid] + _ffn_wspecs(tf, d, gu, md) + r_in,
        out_specs=[row, row, hid, hid, pl.BlockSpec((8, d), lambda i, j: (0, 0))] + r_out,
        out_shape=[jax.ShapeDtypeStruct((t, d), F32), jax.ShapeDtypeStruct((t, d), BF16),
                   jax.ShapeDtypeStruct((t, f), BF16), jax.ShapeDtypeStruct((t, f), BF16),
                   jax.ShapeDtypeStruct((8, d), F32)] + r_shape,
        scratch_shapes=[pltpu.VMEM((tm, d), F32)] + r_scratch,
        compiler_params=_cparams("arbitrary", "arbitrary"),
    )(dh, h, nw, g, u, wts, wts, *r_args)


def ffn_bwd_w(cfg, xn, dout, g, u, dg, du, name, riders=None):
    t, d = xn.shape
    f = g.shape[1]
    tk, tf = cfg.tk_ffn, cfg.tf_w
    nk = t // tk

    def body(xn_ref, dout_ref, g_ref, u_ref, dg_ref, du_ref, o_ref, acc_ref):
        k = pl.program_id(1)

        @pl.when(k == 0)
        def _():
            acc_ref[...] = jnp.zeros_like(acc_ref)

        gg = g_ref[...].astype(F32)
        a = (gg * _sigmoid(gg) * u_ref[...].astype(F32)).astype(BF16)
        xn_t = xn_ref[...]
        acc_ref[0] += _tn(dg_ref[...], xn_t)
        acc_ref[1] += _tn(du_ref[...], xn_t)
        acc_ref[2] += _tn(a, dout_ref[...])

        @pl.when(k == nk - 1)
        def _():
            o_ref[...] = acc_ref[...].astype(BF16)

    row = pl.BlockSpec((tk, d), lambda j, k: (k, 0))
    hid = pl.BlockSpec((tk, tf), lambda j, k: (k, j))
    grid = (f // tf, nk)
    full, r_args, r_in, r_out, r_shape, r_scratch = _ride(riders, grid, 6, 1, body)
    return pl.pallas_call(
        full, name=name, grid=grid,
        in_specs=[row, row, hid, hid, hid, hid] + r_in,
        out_specs=[pl.BlockSpec((3, tf, d), lambda j, k: (0, j, 0))] + r_out,
        out_shape=[jax.ShapeDtypeStruct((3, f, d), BF16)] + r_shape,
        scratch_shapes=[pltpu.VMEM((3, tf, d), F32)] + r_scratch,
        compiler_params=_cparams("arbitrary", "arbitrary"),
    )(xn, dout, g, u, dg, du, *r_args)


def _wspec(w, wi):
    if w.ndim == 2:
        return pl.BlockSpec(w.shape, lambda i: (0, 0))
    return pl.BlockSpec((None,) + w.shape[1:], lambda i: (wi, 0, 0))


def norm_proj(cfg, h, nw, w, name, wi=0):
    t, d = h.shape
    n = w.shape[-2]
    tm = cfg.tm

    def body(h_ref, nw_ref, w_ref, u_ref, xn_ref):
        hh = h_ref[...]
        xn = (hh * _rstd(hh) * nw_ref[...]).astype(BF16)
        xn_ref[...] = xn
        u_ref[...] = _nt(xn, w_ref[...]).astype(BF16)

    return pl.pallas_call(
        body, name=name, grid=(t // tm,),
        in_specs=[pl.BlockSpec((tm, d), lambda i: (i, 0)), pl.BlockSpec((1, d), lambda i: (0, 0)),
                  _wspec(w, wi)],
        out_specs=[pl.BlockSpec((tm, n), lambda i: (i, 0)), pl.BlockSpec((tm, d), lambda i: (i, 0))],
        out_shape=[jax.ShapeDtypeStruct((t, n), BF16), jax.ShapeDtypeStruct((t, d), BF16)],
        compiler_params=_cparams("arbitrary"),
    )(h, nw, w)


def proj_residual(cfg, h, parts, w, name, wi=0):
    t, d = h.shape
    tm = cfg.tm
    ks = [p.shape[1] for p in parts]
    offs = [sum(ks[:i]) for i in range(len(ks))]
    np_ = len(parts)

    def body(*refs):
        h_ref, w_ref, ho_ref = refs[0], refs[1 + np_], refs[2 + np_]
        acc = h_ref[...]
        for p_ref, off, k in zip(refs[1:1 + np_], offs, ks):
            acc = acc + _nn(p_ref[...], w_ref[off:off + k, :])
        ho_ref[...] = acc

    return pl.pallas_call(
        body, name=name, grid=(t // tm,),
        in_specs=[pl.BlockSpec((tm, d), lambda i: (i, 0))]
        + [pl.BlockSpec((tm, k), lambda i: (i, 0)) for k in ks]
        + [_wspec(w, wi)],
        out_specs=pl.BlockSpec((tm, d), lambda i: (i, 0)),
        out_shape=jax.ShapeDtypeStruct((t, d), F32),
        compiler_params=_cparams("arbitrary"),
    )(h, *parts, w)


def proj_bwd_act(cfg, dh, w, name, wi=0):
    t, d = dh.shape
    k = w.shape[-2]
    tm = cfg.tm

    def body(dh_ref, w_ref, o_ref):
        o_ref[...] = _nt(dh_ref[...].astype(BF16), w_ref[...]).astype(BF16)

    return pl.pallas_call(
        body, name=name, grid=(t // tm,),
        in_specs=[pl.BlockSpec((tm, d), lambda i: (i, 0)), _wspec(w, wi)],
        out_specs=pl.BlockSpec((tm, k), lambda i: (i, 0)),
        out_shape=jax.ShapeDtypeStruct((t, k), BF16),
        compiler_params=_cparams("arbitrary"),
    )(dh, w)


def grad_weight(cfg, a, b, name):
    t, k = a.shape
    d = b.shape[1]
    tk = cfg.tk
    tr = cfg.tr if k % cfg.tr == 0 else k
    nk = t // tk

    def body(a_ref, b_ref, o_ref, acc_ref):
        kk = pl.program_id(1)

        @pl.when(kk == 0)
        def _():
            acc_ref[...] = jnp.zeros_like(acc_ref)

        acc_ref[...] += _tn(a_ref[...], b_ref[...].astype(BF16))

        @pl.when(kk == nk - 1)
        def _():
            o_ref[...] = acc_ref[...].astype(BF16)

    return pl.pallas_call(
        body, name=name, grid=(k // tr, nk),
        in_specs=[pl.BlockSpec((tk, tr), lambda j, kk: (kk, j)), pl.BlockSpec((tk, d), lambda j, kk: (kk, 0))],
        out_specs=pl.BlockSpec((tr, d), lambda j, kk: (j, 0)),
        out_shape=jax.ShapeDtypeStruct((k, d), BF16),
        scratch_shapes=[pltpu.VMEM((tr, d), F32)],
        compiler_params=_cparams("arbitrary", "arbitrary"),
    )(a, b)


def norm_proj_bwd(cfg, dh, h, nw, parts, w, name, wi=0):
    t, d = h.shape
    tm = cfg.tm
    ks = [p.shape[1] for p in parts]
    offs = [sum(ks[:i]) for i in range(len(ks))]
    np_ = len(parts)

    def body(*refs):
        dh_ref, h_ref, nw_ref = refs[:3]
        w_ref, dho_ref, dnw_ref = refs[3 + np_:]

        @pl.when(pl.program_id(0) == 0)
        def _():
            dnw_ref[...] = jnp.zeros_like(dnw_ref)

        dxn = None
        for p_ref, off, k in zip(refs[3:3 + np_], offs, ks):
            term = _nn(p_ref[...], w_ref[off:off + k, :])
            dxn = term if dxn is None else dxn + term
        dx, dnw = _rms_bwd(h_ref[...], nw_ref[...], dxn)
        dnw_ref[0:1, :] += dnw
        dho_ref[...] = dh_ref[...] + dx

    row = pl.BlockSpec((tm, d), lambda i: (i, 0))
    return pl.pallas_call(
        body, name=name, grid=(t // tm,),
        in_specs=[row, row, pl.BlockSpec((1, d), lambda i: (0, 0))]
        + [pl.BlockSpec((tm, k), lambda i: (i, 0)) for k in ks]
        + [_wspec(w, wi)],
        out_specs=[row, pl.BlockSpec((8, d), lambda i: (0, 0))],
        out_shape=[jax.ShapeDtypeStruct((t, d), F32), jax.ShapeDtypeStruct((8, d), F32)],
        compiler_params=_cparams("arbitrary"),
    )(dh, h, nw, *parts, w)


def loss_head(cfg, h, nf, tgt, name):
    t, d = h.shape
    tm = cfg.tm

    def body(h_ref, nf_ref, tgt_ref, loss_ref, dh_ref, dnf_ref):
        @pl.when(pl.program_id(0) == 0)
        def _():
            loss_ref[...] = jnp.zeros_like(loss_ref)
            dnf_ref[...] = jnp.zeros_like(dnf_ref)

        hh = h_ref[...]
        err = hh * _rstd(hh) * nf_ref[...] - tgt_ref[...]
        row = jnp.sum(err * err, axis=-1, keepdims=True) * (0.5 / d)
        loss_ref[...] += jnp.sum(row, axis=0, keepdims=True)
        dx, dnf = _rms_bwd(hh, nf_ref[...], err * (1.0 / d))
        dnf_ref[0:1, :] += dnf
        dh_ref[...] = dx

    row = pl.BlockSpec((tm, d), lambda i: (i, 0))
    return pl.pallas_call(
        body, name=name, grid=(t // tm,),
        in_specs=[row, pl.BlockSpec((1, d), lambda i: (0, 0)), row],
        out_specs=[pl.BlockSpec((8, 128), lambda i: (0, 0)), row, pl.BlockSpec((8, d), lambda i: (0, 0))],
        out_shape=[jax.ShapeDtypeStruct((8, 128), F32), jax.ShapeDtypeStruct((t, d), F32),
                   jax.ShapeDtypeStruct((8, d), F32)],
        compiler_params=_cparams("arbitrary"),
    )(h, nf, tgt)


def bucket_table(cfg):
    qi = jnp.arange(cfg.win)[:, None]
    sj = jnp.arange(2 * cfg.win)[None, :]
    dist = qi + cfg.win - sj
    n = jnp.maximum(dist, 0)
    max_exact = cfg.nbuckets // 2
    nf = jnp.maximum(n, max_exact).astype(F32)
    large = max_exact + (jnp.log(nf / max_exact) / math.log(cfg.max_dist / max_exact)
                         * (cfg.nbuckets - max_exact)).astype(jnp.int32)
    large = jnp.minimum(large, cfg.nbuckets - 1)
    bucket = jnp.where(n < max_exact, n, large)
    return jnp.where((dist >= 0) & (dist < cfg.win), bucket, -1).astype(jnp.int32)


def bias_build(cfg, rel_bias, buckets, name):
    w = cfg.win

    def body(rb_ref, bk_ref, o_ref):
        bk = bk_ref[...]
        for h in range(cfg.hq):
            acc = jnp.full((w, 2 * w), NEG_INF, F32)
            for b in range(cfg.nbuckets):
                acc = jnp.where(bk == b, rb_ref[b, h], acc)
            o_ref[h] = acc

    return pl.pallas_call(
        body, name=name,
        in_specs=[pl.BlockSpec(memory_space=pltpu.SMEM), pl.BlockSpec(memory_space=pltpu.VMEM)],
        out_specs=pl.BlockSpec(memory_space=pltpu.VMEM),
        out_shape=jax.ShapeDtypeStruct((cfg.hq, w, 2 * w), F32),
    )(rel_bias, buckets)


def bias_grad(cfg, dbias, buckets, name):
    w = cfg.win

    def body(db_ref, bk_ref, o_ref, rows_ref):
        bk = bk_ref[...]
        for h in range(cfg.hq):
            d = db_ref[0, h]
            for e in range(1, dbias.shape[0]):
                d = d + db_ref[e, h]
            for b in range(cfg.nbuckets):
                rows_ref[b:b + 1, :] = jnp.sum(jnp.where(bk == b, d, 0.0), axis=0, keepdims=True)
            o_ref[h] = jnp.broadcast_to(jnp.sum(rows_ref[...], axis=1, keepdims=True), (cfg.nbuckets, 128))

    return pl.pallas_call(
        body, name=name,
        in_specs=[pl.BlockSpec(memory_space=pltpu.VMEM), pl.BlockSpec(memory_space=pltpu.VMEM)],
        out_specs=pl.BlockSpec(memory_space=pltpu.VMEM),
        out_shape=jax.ShapeDtypeStruct((cfg.hq, cfg.nbuckets, 128), F32),
        scratch_shapes=[pltpu.VMEM((cfg.nbuckets, 2 * w), F32)],
    )(dbias, buckets)


def _attn_probs(cfg, qh, kj, bias_h, sink, first_ok):
    s = _nt(qh, kj) * (1.0 / math.sqrt(cfg.hd)) + bias_h
    s = jnp.where(first_ok, s, NEG_INF)
    m = jnp.maximum(jnp.max(s, axis=-1, keepdims=True), sink)
    e = jnp.exp(s - m)
    es = jnp.exp(sink - m)
    inv = 1.0 / (jnp.sum(e, axis=-1, keepdims=True) + es)
    return e * inv, es * inv


def _attn_block_inputs(cfg, n, q_ref, kv_ref):
    w = cfg.win
    r0 = pl.multiple_of(n * w, w)
    rp = pl.multiple_of(jnp.maximum(n - 1, 0) * w, w)
    qb = q_ref[pl.ds(r0, w), :]
    kk = jnp.concatenate([kv_ref[pl.ds(rp, w), :], kv_ref[pl.ds(r0, w), :]], axis=0)
    col = lax.broadcasted_iota(jnp.int32, (w, 2 * w), 1)
    first_ok = (n > 0) | (col >= w)
    return r0, rp, qb, kk, first_ok


def _kv_col_block(cfg):
    assert cfg.qw % (2 * cfg.kvw) == 0
    return cfg.qw // (2 * cfg.kvw)


def attn_fwd(cfg, u, bias, sinks, name):
    t = u.shape[0]
    s, w, hd, g = cfg.s, cfg.win, cfg.hd, cfg.hq // cfg.hkv
    kvb = _kv_col_block(cfg)

    def body(q_ref, kv_ref, bias_ref, sink_ref, o_ref):
        def blk(n, carry):
            r0, _, qb, kk, first_ok = _attn_block_inputs(cfg, n, q_ref, kv_ref)
            outs = []
            for j in range(cfg.hkv):
                kj = kk[:, hd * j:hd * (j + 1)]
                vj = kk[:, cfg.kvw + hd * j:cfg.kvw + hd * (j + 1)]
                for gq in range(g):
                    h = j * g + gq
                    p, _ = _attn_probs(cfg, qb[:, hd * h:hd * (h + 1)], kj, bias_ref[h], sink_ref[h], first_ok)
                    outs.append(_nn(p.astype(BF16), vj))
            o_ref[pl.ds(r0, w), :] = jnp.concatenate(outs, axis=1).astype(BF16)
            return carry

        lax.fori_loop(0, s // w, blk, 0)

    return pl.pallas_call(
        body, name=name, grid=(t // s,),
        in_specs=[pl.BlockSpec((s, cfg.qw), lambda b: (b, 0)), pl.BlockSpec((s, 2 * cfg.kvw), lambda b: (b, kvb)),
                  pl.BlockSpec(bias.shape, lambda b: (0, 0, 0)), pl.BlockSpec(memory_space=pltpu.SMEM)],
        out_specs=pl.BlockSpec((s, cfg.qw), lambda b: (b, 0)),
        out_shape=jax.ShapeDtypeStruct((t, cfg.qw), BF16),
        compiler_params=_cparams("arbitrary"),
    )(u, u, bias, sinks)


def attn_bwd(cfg, u, dcat, bias, sinks, name):
    t = u.shape[0]
    s, w, hd, g = cfg.s, cfg.win, cfg.hd, cfg.hq // cfg.hkv
    kvb = _kv_col_block(cfg)
    scale = 1.0 / math.sqrt(hd)
    assert cfg.hq <= 8

    def body(q_ref, kv_ref, do_ref, bias_ref, sink_ref, du_ref, dbias_ref, dsink_ref, dkv_ref):
        @pl.when(pl.program_id(0) == 0)
        def _():
            dbias_ref[...] = jnp.zeros_like(dbias_ref)
            dsink_ref[...] = jnp.zeros_like(dsink_ref)

        dkv_ref[...] = jnp.zeros_like(dkv_ref)

        def blk(n, carry):
            r0, rp, qb, kk, first_ok = _attn_block_inputs(cfg, n, q_ref, kv_ref)
            dob = do_ref[pl.ds(r0, w), :]
            dqs, dks, dvs = [], [], []
            for j in range(cfg.hkv):
                kj = kk[:, hd * j:hd * (j + 1)]
                vj = kk[:, cfg.kvw + hd * j:cfg.kvw + hd * (j + 1)]
                dk = jnp.zeros((2 * w, hd), F32)
                dv = jnp.zeros((2 * w, hd), F32)
                for gq in range(g):
                    h = j * g + gq
                    qh = qb[:, hd * h:hd * (h + 1)]
                    doh = dob[:, hd * h:hd * (h + 1)]
                    p, ps = _attn_probs(cfg, qh, kj, bias_ref[h], sink_ref[h], first_ok)
                    dp = _nt(doh, vj)
                    delta = jnp.sum(p * dp, axis=-1, keepdims=True)
                    ds = p * (dp - delta)
                    dsink_ref[h:h + 1, :] += jnp.broadcast_to(-jnp.sum(ps * delta, axis=0, keepdims=True), (1, 128))
                    dbias_ref[h] += ds
                    dsb = ds.astype(BF16)
                    dqs.append(_nn(dsb, kj) * scale)
                    dk = dk + _tn(dsb, qh) * scale
                    dv = dv + _tn(p.astype(BF16), doh)
                dks.append(dk)
                dvs.append(dv)
            du_ref[pl.ds(r0, w), 0:cfg.qw] = jnp.concatenate(dqs, axis=1).astype(BF16)
            dkv = jnp.concatenate(dks + dvs, axis=1)
            dkv_ref[pl.ds(rp, w), :] += dkv[:w]
            dkv_ref[pl.ds(r0, w), :] += dkv[w:]
            return carry

        lax.fori_loop(0, s // w, blk, 0)
        du_ref[:, cfg.qw:] = dkv_ref[...].astype(BF16)

    wa = cfg.qw + 2 * cfg.kvw
    return pl.pallas_call(
        body, name=name, grid=(t // s,),
        in_specs=[pl.BlockSpec((s, cfg.qw), lambda b: (b, 0)), pl.BlockSpec((s, 2 * cfg.kvw), lambda b: (b, kvb)),
                  pl.BlockSpec((s, cfg.qw), lambda b: (b, 0)),
                  pl.BlockSpec(bias.shape, lambda b: (0, 0, 0)), pl.BlockSpec(memory_space=pltpu.SMEM)],
        out_specs=[pl.BlockSpec((s, wa), lambda b: (b, 0)), pl.BlockSpec(bias.shape, lambda b: (0, 0, 0)),
                   pl.BlockSpec((8, 128), lambda b: (0, 0))],
        out_shape=[jax.ShapeDtypeStruct((t, wa), BF16), jax.ShapeDtypeStruct(bias.shape, F32),
                   jax.ShapeDtypeStruct((8, 128), F32)],
        scratch_shapes=[pltpu.VMEM((s, 2 * cfg.kvw), F32)],
        compiler_params=_cparams("arbitrary"),
    )(u, u, dcat, bias, sinks)


def _shift_views(win, rc, pad):
    return [win] + [win[j:j + rc + pad - 8] for j in range(1, 8)]


def _tap(views, off, rc):
    a = 8 * (off // 8)
    return views[off % 8][a:a + rc]


def _conv_rows(views, w_ref, cw, pad, rc):
    acc = None
    for k in range(cw):
        term = _tap(views, pad - (cw - 1) + k, rc) * w_ref[k:k + 1, :]
        acc = term if acc is None else acc + term
    return acc


def _conv_rows_t(views, w_ref, cw, rc):
    acc = None
    for k in range(cw):
        term = _tap(views, cw - 1 - k, rc) * w_ref[k:k + 1, :]
        acc = term if acc is None else acc + term
    return acc


def _group_sum(x):
    acc = x[0:8]
    for i in range(1, x.shape[0] // 8):
        acc = acc + x[8 * i:8 * i + 8]
    return acc


def _conv_wgrad(views, dy, acc_ref, cw, pad, rc):
    for k in range(cw):
        acc_ref[k] += _group_sum(dy * _tap(views, pad - (cw - 1) + k, rc))


CONV_RC = 64
CONV_PAD = 32
GLU_RC = 256


def _conv_col_blocks(cfg):
    off = cfg.qw + 2 * cfg.kvw
    bw = math.gcd(off, cfg.cc)
    assert bw % 128 == 0
    n = cfg.cc // bw
    return bw, [off // bw + i for i in range(n)], [(off + cfg.cc) // bw + i for i in range(n)]


def _glu_inputs(a_refs, b_refs, rows):
    ga = jnp.concatenate([r[rows, :] for r in a_refs], axis=1).astype(F32)
    gb = jnp.concatenate([r[rows, :] for r in b_refs], axis=1).astype(F32)
    return ga, gb


def _fill_glu(cfg, a_refs, b_refs, xp_ref):
    xp_ref[0:CONV_PAD, :] = jnp.zeros((CONV_PAD, cfg.cc), F32)

    def fill(i, carry):
        r0 = pl.multiple_of(i * GLU_RC, GLU_RC)
        ga, gb = _glu_inputs(a_refs, b_refs, pl.ds(r0, GLU_RC))
        xp_ref[pl.ds(CONV_PAD + r0, GLU_RC), :] = ga * _sigmoid(gb)
        return carry

    lax.fori_loop(0, cfg.s // GLU_RC, fill, 0)


def _layernorm_stats(cv):
    mu = jnp.mean(cv, axis=-1, keepdims=True)
    xc = cv - mu
    rstd = lax.rsqrt(jnp.mean(xc * xc, axis=-1, keepdims=True) + LN_EPS)
    return xc * rstd, rstd


def conv_fwd(cfg, u, cw_w, cb, lg, lb, name):
    t = u.shape[0]
    s, cc, cw = cfg.s, cfg.cc, cfg.cw
    bw, a_idx, b_idx = _conv_col_blocks(cfg)
    nb = len(a_idx)

    def body(*refs):
        a_refs, b_refs = refs[:nb], refs[nb:2 * nb]
        w_ref, cb_ref, lg_ref, lb_ref, o_ref, xp_ref = refs[2 * nb:]
        _fill_glu(cfg, a_refs, b_refs, xp_ref)

        def chunk(i, carry):
            r0 = pl.multiple_of(i * CONV_RC, CONV_RC)
            views = _shift_views(xp_ref[pl.ds(r0, CONV_RC + CONV_PAD), :], CONV_RC, CONV_PAD)
            cv = _conv_rows(views, w_ref, cw, CONV_PAD, CONV_RC) + cb_ref[...]
            xhat, _ = _layernorm_stats(cv)
            ln = xhat * lg_ref[...] + lb_ref[...]
            o_ref[pl.ds(r0, CONV_RC), :] = (ln * _sigmoid(ln)).astype(BF16)
            return carry

        lax.fori_loop(0, s // CONV_RC, chunk, 0)

    def colspec(j):
        return pl.BlockSpec((s, bw), lambda b: (b, j))

    vec = pl.BlockSpec((1, cc), lambda b: (0, 0))
    return pl.pallas_call(
        body, name=name, grid=(t // s,),
        in_specs=[colspec(j) for j in a_idx + b_idx] + [pl.BlockSpec((cw, cc), lambda b: (0, 0)), vec, vec, vec],
        out_specs=pl.BlockSpec((s, cc), lambda b: (b, 0)),
        out_shape=jax.ShapeDtypeStruct((t, cc), BF16),
        scratch_shapes=[pltpu.VMEM((CONV_PAD + s, cc), F32)],
        compiler_params=_cparams("arbitrary"),
    )(*([u] * (2 * nb)), cw_w, cb, lg, lb)


def conv_bwd(cfg, u, dcat, cw_w, cb, lg, lb, name):
    t = u.shape[0]
    s, cc, cw = cfg.s, cfg.cc, cfg.cw
    bw, a_idx, b_idx = _conv_col_blocks(cfg)
    nb = len(a_idx)
    assert cfg.qw % cc == 0 and cw <= 32

    def body(*refs):
        a_refs, b_refs = refs[:nb], refs[nb:2 * nb]
        dc_ref, w_ref, cb_ref, lg_ref, lb_ref, du_ref, dw_ref, dvec_ref, xp_ref, dcv_ref, dwacc_ref = refs[2 * nb:]

        @pl.when(pl.program_id(0) == 0)
        def _():
            dw_ref[...] = jnp.zeros_like(dw_ref)
            dvec_ref[...] = jnp.zeros_like(dvec_ref)

        _fill_glu(cfg, a_refs, b_refs, xp_ref)
        dcv_ref[s:s + CONV_PAD, :] = jnp.zeros((CONV_PAD, cc), F32)
        dwacc_ref[...] = jnp.zeros_like(dwacc_ref)

        def chunk(i, carry):
            r0 = pl.multiple_of(i * CONV_RC, CONV_RC)
            views = _shift_views(xp_ref[pl.ds(r0, CONV_RC + CONV_PAD), :], CONV_RC, CONV_PAD)
            cv = _conv_rows(views, w_ref, cw, CONV_PAD, CONV_RC) + cb_ref[...]
            xhat, rstd = _layernorm_stats(cv)
            ln = xhat * lg_ref[...] + lb_ref[...]
            sg = _sigmoid(ln)
            dln = dc_ref[pl.ds(r0, CONV_RC), :].astype(F32) * (sg * (1.0 + ln * (1.0 - sg)))
            dxh = dln * lg_ref[...]
            dcv = rstd * (dxh - jnp.mean(dxh, axis=-1, keepdims=True)
                          - xhat * jnp.mean(dxh * xhat, axis=-1, keepdims=True))
            dcv_ref[pl.ds(r0, CONV_RC), :] = dcv
            dvec_ref[0:1, :] += jnp.sum(dcv, axis=0, keepdims=True)
            dvec_ref[1:2, :] += jnp.sum(dln * xhat, axis=0, keepdims=True)
            dvec_ref[2:3, :] += jnp.sum(dln, axis=0, keepdims=True)
            _conv_wgrad(views, dcv, dwacc_ref, cw, CONV_PAD, CONV_RC)
            return carry

        lax.fori_loop(0, s // CONV_RC, chunk, 0)
        for k in range(cw):
            dw_ref[k:k + 1, :] += jnp.sum(dwacc_ref[k], axis=0, keepdims=True)

        def chunk2(i, carry):
            r0 = pl.multiple_of(i * CONV_RC, CONV_RC)
            views = _shift_views(dcv_ref[pl.ds(r0, CONV_RC + CONV_PAD), :], CONV_RC, CONV_PAD)
            dglu = _conv_rows_t(views, w_ref, cw, CONV_RC)
            ga, gb = _glu_inputs(a_refs, b_refs, pl.ds(r0, CONV_RC))
            sgb = _sigmoid(gb)
            du_ref[pl.ds(r0, CONV_RC), 0:cc] = (dglu * sgb).astype(BF16)
            du_ref[pl.ds(r0, CONV_RC), cc:2 * cc] = (dglu * ga * sgb * (1.0 - sgb)).astype(BF16)
            return carry

        lax.fori_loop(0, s // CONV_RC, chunk2, 0)

    def colspec(j):
        return pl.BlockSpec((s, bw), lambda b: (b, j))

    vec = pl.BlockSpec((1, cc), lambda b: (0, 0))
    return pl.pallas_call(
        body, name=name, grid=(t // s,),
        in_specs=[colspec(j) for j in a_idx + b_idx]
        + [pl.BlockSpec((s, cc), lambda b: (b, cfg.qw // cc)), pl.BlockSpec((cw, cc), lambda b: (0, 0)), vec, vec, vec],
        out_specs=[pl.BlockSpec((s, 2 * cc), lambda b: (b, 0)), pl.BlockSpec((32, cc), lambda b: (0, 0)),
                   pl.BlockSpec((8, cc), lambda b: (0, 0))],
        out_shape=[jax.ShapeDtypeStruct((t, 2 * cc), BF16), jax.ShapeDtypeStruct((32, cc), F32),
                   jax.ShapeDtypeStruct((8, cc), F32)],
        scratch_shapes=[pltpu.VMEM((CONV_PAD + s, cc), F32), pltpu.VMEM((s + CONV_PAD, cc), F32),
                        pltpu.VMEM((cw, 8, cc), F32)],
        compiler_params=_cparams("arbitrary"),
    )(*([u] * (2 * nb)), dcat, cw_w, cb, lg, lb)


LRU_RC = 64
LRU_PAD = 8
SCAN_RC = 16
GELU_K = math.sqrt(2.0 / math.pi)


def _expm1_neg(z):
    return jnp.where(z > -0.05, z * (1.0 + z * (0.5 + z * (1.0 / 6.0 + z * (1.0 / 24.0)))), jnp.exp(z) - 1.0)


def _log_sigmoid(x):
    e = jnp.exp(-jnp.abs(x))
    log1p = jnp.where(e < 0.01, e * (1.0 - e * (0.5 - e * (1.0 / 3.0))), jnp.log(1.0 + e))
    return jnp.minimum(x, 0.0) - log1p


def _gelu(x):
    t = jnp.tanh(GELU_K * (x + 0.044715 * x * x * x))
    return 0.5 * x * (1.0 + t), t


def _gelu_grad(x, t):
    return 0.5 * (1.0 + t) + 0.5 * x * (1.0 - t * t) * GELU_K * (1.0 + 3.0 * 0.044715 * x * x)


def _lru_gates(xc, wa_ref, ba, wx_ref, bx, ls):
    nh = xc.shape[1] // 128
    xb = xc.astype(BF16)
    ra = jnp.concatenate([_nn(xb[:, 128 * h:128 * (h + 1)], wa_ref[h]) for h in range(nh)], axis=1) + ba
    ia = jnp.concatenate([_nn(xb[:, 128 * h:128 * (h + 1)], wx_ref[h]) for h in range(nh)], axis=1) + bx
    r = _sigmoid(ra)
    ig = _sigmoid(ia)
    log_a = RG_LRU_C * r * ls
    a = jnp.exp(log_a)
    mult = jnp.sqrt(-_expm1_neg(2.0 * log_a))
    return r, ig, a, mult


def _fill_padded(src_ref, dst_ref, s, ct):
    dst_ref[0:LRU_PAD, :] = jnp.zeros((LRU_PAD, ct), F32)

    def fill(i, carry):
        r0 = pl.multiple_of(i * GLU_RC, GLU_RC)
        dst_ref[pl.ds(LRU_PAD + r0, GLU_RC), :] = src_ref[pl.ds(r0, GLU_RC), :].astype(F32)
        return carry

    lax.fori_loop(0, s // GLU_RC, fill, 0)


def _lru_specs(cfg, ct):
    s, lw = cfg.s, cfg.lw
    nct = lw // ct
    nh = ct // 128
    act = [pl.BlockSpec((s, ct), lambda c, b: (b, c)), pl.BlockSpec((s, ct), lambda c, b: (b, nct + c))]
    vec = pl.BlockSpec((1, ct), lambda c, b: (0, c))
    gate_w = pl.BlockSpec((nh, 128, 128), lambda c, b: (c, 0, 0))
    params = [pl.BlockSpec((cfg.lcw, ct), lambda c, b: (0, c)), vec, gate_w, vec, gate_w, vec, vec]
    return nct, act, params


def lru_fwd(cfg, u, conv_w, conv_b, wa, ba, wx, bx, lam, name):
    t = u.shape[0]
    s, lw, lcw, ct = cfg.s, cfg.lw, cfg.lcw, cfg.ct_f
    nct, act, params = _lru_specs(cfg, ct)

    def body(gi_ref, ri_ref, cw_ref, cb_ref, wa_ref, ba_ref, wx_ref, bx_ref, lam_ref, y_ref, hs_ref,
             xp_ref, a_ref, b_ref):
        _fill_padded(ri_ref, xp_ref, s, ct)
        ls = _log_sigmoid(lam_ref[...])

        def chunk(i, carry):
            r0 = pl.multiple_of(i * LRU_RC, LRU_RC)
            views = _shift_views(xp_ref[pl.ds(r0, LRU_RC + LRU_PAD), :], LRU_RC, LRU_PAD)
            xc = _conv_rows(views, cw_ref, lcw, LRU_PAD, LRU_RC) + cb_ref[...]
            _, ig, a, mult = _lru_gates(xc, wa_ref, ba_ref[...], wx_ref, bx_ref[...], ls)
            a_ref[pl.ds(r0, LRU_RC), :] = a
            b_ref[pl.ds(r0, LRU_RC), :] = mult * (ig * xc)
            return carry

        lax.fori_loop(0, s // LRU_RC, chunk, 0)
        row = lax.broadcasted_iota(jnp.int32, (SCAN_RC, ct), 0)

        def scan(i, h_last):
            rows = pl.ds(pl.multiple_of(i * SCAN_RC, SCAN_RC), SCAN_RC)
            a = a_ref[rows, :]
            b = b_ref[rows, :]
            sft = 1
            while sft < SCAN_RC:
                a_sh = jnp.where(row >= sft, pltpu.roll(a, sft, 0), 1.0)
                b_sh = jnp.where(row >= sft, pltpu.roll(b, sft, 0), 0.0)
                b = a * b_sh + b
                a = a * a_sh
                sft *= 2
            h = a * h_last + b
            gate, _ = _gelu(gi_ref[rows, :].astype(F32))
            y_ref[rows, :] = (gate * h).astype(BF16)
            hs_ref[rows, :] = h.astype(BF16)
            return h[SCAN_RC - 1:SCAN_RC, :]

        lax.fori_loop(0, s // SCAN_RC, scan, jnp.zeros((1, ct), F32))

    out = pl.BlockSpec((s, ct), lambda c, b: (b, c))
    return pl.pallas_call(
        body, name=name, grid=(nct, t // s),
        in_specs=act + params,
        out_specs=[out, out],
        out_shape=[jax.ShapeDtypeStruct((t, lw), BF16), jax.ShapeDtypeStruct((t, lw), BF16)],
        scratch_shapes=[pltpu.VMEM((LRU_PAD + s, ct), F32), pltpu.VMEM((s, ct), F32), pltpu.VMEM((s, ct), F32)],
        compiler_params=_cparams("arbitrary", "arbitrary"),
    )(u, u, conv_w, conv_b, wa, ba, wx, bx, lam)


def lru_bwd(cfg, u, hs, dy, conv_w, conv_b, wa, ba, wx, bx, lam, name):
    t = u.shape[0]
    s, lw, lcw, ct = cfg.s, cfg.lw, cfg.lcw, cfg.ct_b
    nct, act, params = _lru_specs(cfg, ct)
    nh = ct // 128
    nscan = s // SCAN_RC
    assert lcw <= 8

    def body(gi_ref, ri_ref, hs_ref, dy_ref, cw_ref, cb_ref, wa_ref, ba_ref, wx_ref, bx_ref, lam_ref,
             dug_ref, dur_ref, dwa_ref, dwx_ref, dvec_ref, dcw_ref,
             xp_ref, hp_ref, a_ref, g_ref, dxc_ref, dwacc_ref):
        @pl.when(pl.program_id(1) == 0)
        def _():
            dwa_ref[...] = jnp.zeros_like(dwa_ref)
            dwx_ref[...] = jnp.zeros_like(dwx_ref)
            dvec_ref[...] = jnp.zeros_like(dvec_ref)
            dcw_ref[...] = jnp.zeros_like(dcw_ref)

        _fill_padded(ri_ref, xp_ref, s, ct)
        _fill_padded(hs_ref, hp_ref, s, ct)
        dxc_ref[s:s + LRU_PAD, :] = jnp.zeros((LRU_PAD, ct), F32)
        dwacc_ref[...] = jnp.zeros_like(dwacc_ref)
        lam = lam_ref[...]
        ls = _log_sigmoid(lam)

        def gates_at(r0):
            views = _shift_views(xp_ref[pl.ds(r0, LRU_RC + LRU_PAD), :], LRU_RC, LRU_PAD)
            xc = _conv_rows(views, cw_ref, lcw, LRU_PAD, LRU_RC) + cb_ref[...]
            return views, xc, _lru_gates(xc, wa_ref, ba_ref[...], wx_ref, bx_ref[...], ls)

        def chunk(i, carry):
            r0 = pl.multiple_of(i * LRU_RC, LRU_RC)
            rows = pl.ds(r0, LRU_RC)
            _, _, (_, _, a, _) = gates_at(r0)
            a_ref[rows, :] = a
            x = gi_ref[rows, :].astype(F32)
            gate, th = _gelu(x)
            dyv = dy_ref[rows, :].astype(F32)
            g_ref[rows, :] = dyv * gate
            dug_ref[rows, :] = (dyv * hp_ref[pl.ds(LRU_PAD + r0, LRU_RC), :] * _gelu_grad(x, th)).astype(BF16)
            return carry

        lax.fori_loop(0, s // LRU_RC, chunk, 0)
        row = lax.broadcasted_iota(jnp.int32, (SCAN_RC, ct), 0)

        def scan(ii, carry):
            g_next, a_next = carry
            rows = pl.ds(pl.multiple_of((nscan - 1 - ii) * SCAN_RC, SCAN_RC), SCAN_RC)
            a = a_ref[rows, :]
            d = g_ref[rows, :]
            c = jnp.where(row < SCAN_RC - 1, pltpu.roll(a, SCAN_RC - 1, 0), a_next)
            sft = 1
            while sft < SCAN_RC:
                c_sh = jnp.where(row < SCAN_RC - sft, pltpu.roll(c, SCAN_RC - sft, 0), 1.0)
                d_sh = jnp.where(row < SCAN_RC - sft, pltpu.roll(d, SCAN_RC - sft, 0), 0.0)
                d = d + c * d_sh
                c = c * c_sh
                sft *= 2
            g = d + c * g_next
            g_ref[rows, :] = g
            return g[0:1, :], a[0:1, :]

        lax.fori_loop(0, nscan, scan, (jnp.zeros((1, ct), F32), jnp.zeros((1, ct), F32)))

        def chunk3(i, carry):
            r0 = pl.multiple_of(i * LRU_RC, LRU_RC)
            rows = pl.ds(r0, LRU_RC)
            views, xc, (r, ig, a, mult) = gates_at(r0)
            g = g_ref[rows, :]
            h_prev = hp_ref[pl.ds(r0, LRU_RC + LRU_PAD), :][LRU_PAD - 1:LRU_PAD - 1 + LRU_RC]
            dix = g * mult
            di = dix * xc
            dxc = dix * ig
            da = g * h_prev - (g * ig * xc) * a / mult
            dlog_a = da * a
            dr = dlog_a * (RG_LRU_C * ls)
            dra = dr * r * (1.0 - r)
            dia = di * ig * (1.0 - ig)
            xb, drab, diab = xc.astype(BF16), dra.astype(BF16), dia.astype(BF16)
            dxg = []
            for h in range(nh):
                cols = slice(128 * h, 128 * (h + 1))
                dxg.append(_nt(drab[:, cols], wa_ref[h]) + _nt(diab[:, cols], wx_ref[h]))
                dwa_ref[h] += _tn(xb[:, cols], drab[:, cols])
                dwx_ref[h] += _tn(xb[:, cols], diab[:, cols])
            dxc = dxc + jnp.concatenate(dxg, axis=1)
            dvec_ref[0:1, :] += jnp.sum(dra, axis=0, keepdims=True)
            dvec_ref[1:2, :] += jnp.sum(dia, axis=0, keepdims=True)
            dvec_ref[2:3, :] += jnp.sum(dlog_a * r, axis=0, keepdims=True) * (RG_LRU_C * _sigmoid(-lam))
            dvec_ref[3:4, :] += jnp.sum(dxc, axis=0, keepdims=True)
            dxc_ref[rows, :] = dxc
            _conv_wgrad(views, dxc, dwacc_ref, lcw, LRU_PAD, LRU_RC)
            return carry

        lax.fori_loop(0, s // LRU_RC, chunk3, 0)
        for k in range(lcw):
            dcw_ref[k:k + 1, :] += jnp.sum(dwacc_ref[k], axis=0, keepdims=True)

        def chunk4(i, carry):
            r0 = pl.multiple_of(i * LRU_RC, LRU_RC)
            views = _shift_views(dxc_ref[pl.ds(r0, LRU_RC + LRU_PAD), :], LRU_RC, LRU_PAD)
            dur_ref[pl.ds(r0, LRU_RC), :] = _conv_rows_t(views, cw_ref, lcw, LRU_RC).astype(BF16)
            return carry

        lax.fori_loop(0, s // LRU_RC, chunk4, 0)

    blk = pl.BlockSpec((s, ct), lambda c, b: (b, c))
    acc8 = pl.BlockSpec((8, ct), lambda c, b: (0, c))
    gate_w = pl.BlockSpec((nh, 128, 128), lambda c, b: (c, 0, 0))
    return pl.pallas_call(
        body, name=name, grid=(nct, t // s),
        in_specs=act + [blk, blk] + params,
        out_specs=[blk, blk, gate_w, gate_w, acc8, acc8],
        out_shape=[jax.ShapeDtypeStruct((t, lw), BF16), jax.ShapeDtypeStruct((t, lw), BF16),
                   jax.ShapeDtypeStruct((cfg.lh, 128, 128), F32), jax.ShapeDtypeStruct((cfg.lh, 128, 128), F32),
                   jax.ShapeDtypeStruct((8, lw), F32), jax.ShapeDtypeStruct((8, lw), F32)],
        scratch_shapes=[pltpu.VMEM((LRU_PAD + s, ct), F32), pltpu.VMEM((LRU_PAD + s, ct), F32),
                        pltpu.VMEM((s, ct), F32), pltpu.VMEM((s, ct), F32), pltpu.VMEM((s + LRU_PAD, ct), F32),
                        pltpu.VMEM((lcw, 8, ct), F32)],
        compiler_params=_cparams("arbitrary", "arbitrary"),
    )(u, u, hs, dy, conv_w, conv_b, wa, ba, wx, bx, lam)


def local_step(cfg, x, tgt, p, shards=None):
    buckets = bucket_table(cfg)
    bias = bias_build(cfg, p["rel_bias"], buckets, "bias_build")
    ga_w, gx_w = p["gate_a_w"].astype(BF16), p["gate_x_w"].astype(BF16)
    wf = dict(p["wf"])
    dist = shards is not None

    def ffn_forward(l, k, h, nw):
        riders = gather_riders([shards[(l + 1, k)]]) if dist and l + 1 < cfg.depth else None
        outs = ffn_fwd(cfg, h, nw, wf[(l, k)], 0, 2, f"ffn{k + 1}_fwd_{l}", riders=riders)
        if riders is not None:
            wf[(l + 1, k)] = outs[4].reshape(3, -1, cfg.d)
        return outs[:4]

    def blocks(g):
        g = g if g.ndim == 3 else g[None]
        return g.reshape(g.shape[0], N_DEV, g.shape[1] // N_DEV, g.shape[2])

    h = x
    saved = []
    for l in range(cfg.depth):
        i = l // 2
        s = {"h0": h}
        s["h1"], s["xn1"], s["g1"], s["u1"] = ffn_forward(l, 0, h, p["norm_ffn1"][l][None])
        if l % 2 == 0:
            s["um"], s["xnm"] = norm_proj(cfg, s["h1"], p["norm_mix"][l][None], p["even_in"], f"mix_in_{l}", wi=i)
            attn = attn_fwd(cfg, s["um"], bias, p["attn_sinks"][i], f"attn_fwd_{l}")
            c = conv_fwd(cfg, s["um"], p["conv_b_w"][i], p["conv_b_b"][i][None], p["conv_ln_g"][i][None],
                         p["conv_ln_b"][i][None], f"conv_fwd_{l}")
            s["parts"] = [attn, c]
            s["h2"] = proj_residual(cfg, s["h1"], s["parts"], p["even_out"], f"mix_out_{l}", wi=i)
        else:
            s["um"], s["xnm"] = norm_proj(cfg, s["h1"], p["norm_mix"][l][None], p["odd_in"], f"mix_in_{l}", wi=i)
            y, s["hs"] = lru_fwd(cfg, s["um"], p["lru_conv_w"][i], p["lru_conv_b"][i][None], ga_w[i], p["gate_a_b"][i][None],
                                 gx_w[i], p["gate_x_b"][i][None], p["lru_lambda"][i][None], f"lru_fwd_{l}")
            s["parts"] = [y]
            s["h2"] = proj_residual(cfg, s["h1"], s["parts"], p["odd_out"], f"mix_out_{l}", wi=i)
        h, s["xn2"], s["g2"], s["u2"] = ffn_forward(l, 1, s["h2"], p["norm_ffn2"][l][None])
        saved.append(s)

    loss, dh, dnf = loss_head(cfg, h, p["norm_final"][None], tgt, "loss_head")
    big = [None] * cfg.depth
    sm = {k: [None] * cfg.depth for k in ("norm_ffn1", "norm_mix", "norm_ffn2")}
    ne, no = (cfg.depth + 1) // 2, cfg.depth // 2
    for k in ("attn_sinks", "conv_b_w", "conv_b_b", "conv_ln_g", "conv_ln_b", "dbias"):
        sm[k] = [None] * ne
    for k in ("lru_conv_w", "lru_conv_b", "gate_a_w", "gate_a_b", "gate_x_w", "gate_x_b", "lru_lambda"):
        sm[k] = [None] * no
    pending = None
    for l in reversed(range(cfg.depth)):
        i = l // 2
        s = saved[l]
        riders = scatter_riders([pending]) if pending is not None else None
        outs = ffn_bwd_x(cfg, dh, s["h2"], p["norm_ffn2"][l][None], s["g2"], s["u2"], wf[(l, 1)], 0, 2,
                         f"ffn2_bwd_x_{l}", riders=riders)
        dh, dout, dg, du, dn = outs[:5]
        if riders is not None:
            big[l + 1]["f1"] = (pending, outs[5])
        sm["norm_ffn2"][l] = dn[0]
        gf2 = blocks(ffn_bwd_w(cfg, s["xn2"], dout, s["g2"], s["u2"], dg, du, f"ffn2_bwd_w_{l}")[0])
        w_out = p["even_out"] if l % 2 == 0 else p["odd_out"]
        w_in = p["even_in"] if l % 2 == 0 else p["odd_in"]
        dcat = proj_bwd_act(cfg, dh, w_out, f"mix_out_bwd_{l}", wi=i)
        g_out = [grad_weight(cfg, part, dh, f"mix_out_gw{j}_{l}") for j, part in enumerate(s["parts"])]
        if l % 2 == 0:
            du_a, sm["dbias"][i], dsink = attn_bwd(cfg, s["um"], dcat, bias, p["attn_sinks"][i], f"attn_bwd_{l}")
            du_c, dcw, dvec = conv_bwd(cfg, s["um"], dcat, p["conv_b_w"][i], p["conv_b_b"][i][None],
                                       p["conv_ln_g"][i][None], p["conv_ln_b"][i][None], f"conv_bwd_{l}")
            sm["attn_sinks"][i] = dsink[:cfg.hq, 0]
            sm["conv_b_w"][i] = dcw[:cfg.cw]
            sm["conv_b_b"][i], sm["conv_ln_g"][i], sm["conv_ln_b"][i] = dvec[0], dvec[1], dvec[2]
            dparts = [du_a, du_c]
        else:
            dug, dur, dwa, dwx, dvec, dcw = lru_bwd(
                cfg, s["um"], s["hs"], dcat, p["lru_conv_w"][i], p["lru_conv_b"][i][None], ga_w[i], p["gate_a_b"][i][None],
                gx_w[i], p["gate_x_b"][i][None], p["lru_lambda"][i][None], f"lru_bwd_{l}")
            sm["gate_a_w"][i], sm["gate_x_w"][i] = dwa, dwx
            sm["gate_a_b"][i], sm["gate_x_b"][i], sm["lru_lambda"][i], sm["lru_conv_b"][i] = dvec[0], dvec[1], dvec[2], dvec[3]
            sm["lru_conv_w"][i] = dcw[:cfg.lcw]
            dparts = [dug, dur]
        g_in = [grad_weight(cfg, dp, s["xnm"], f"mix_in_gw{j}_{l}") for j, dp in enumerate(dparts)]
        dh, dn = norm_proj_bwd(cfg, dh, s["h1"], p["norm_mix"][l][None], dparts, w_in, f"mix_in_bwd_{l}", wi=i)
        sm["norm_mix"][l] = dn[0]
        g_in = blocks(g_in[0] if len(g_in) == 1 else jnp.concatenate(g_in, axis=0))
        g_out = blocks(g_out[0] if len(g_out) == 1 else jnp.concatenate(g_out, axis=0))
        riders = scatter_riders([gf2]) if dist else None
        outs = ffn_bwd_x(cfg, dh, s["h0"], p["norm_ffn1"][l][None], s["g1"], s["u1"], wf[(l, 0)], 0, 2,
                         f"ffn1_bwd_x_{l}", riders=riders)
        dh, dout, dg, du, dn = outs[:5]
        sm["norm_ffn1"][l] = dn[0]
        riders = scatter_riders([g_in, g_out]) if dist else None
        gouts = ffn_bwd_w(cfg, s["xn1"], dout, s["g1"], s["u1"], dg, du, f"ffn1_bwd_w_{l}", riders=riders)
        gf1 = blocks(gouts[0])
        big[l] = {"f1": (gf1, None), "f2": (gf2, outs[5] if dist else None),
                  "in": (g_in, gouts[1] if dist else None), "out": (g_out, gouts[2] if dist else None)}
        pending = gf1 if dist and l > 0 else None
    drb = bias_grad(cfg, jnp.stack(sm.pop("dbias")), buckets, "bias_grad")
    small = {k: jnp.stack(v) for k, v in sm.items()}
    small["rel_bias"] = drb[:, :, 0].T
    small["norm_final"] = dnf[0]
    return loss, dh, big, small


MESH = pl.DeviceIdType.MESH
ANY = pl.BlockSpec(memory_space=pl.ANY)


def _place():
    return lax.axis_index("x"), lax.axis_index("y"), lax.axis_index("c")


FLIPS = ((0, 0, 1), (0, 1, 0), (0, 1, 1), (1, 0, 0), (1, 0, 1), (1, 1, 0), (1, 1, 1))


def _peer(place, flip):
    return tuple(1 - v if f else v for v, f in zip(place, flip))


def _dev_index(place):
    return 4 * place[0] + 2 * place[1] + place[2]


def _remote(src, dst, send_sems, recv_sems, g, k, peer):
    return pltpu.make_async_remote_copy(src_ref=src, dst_ref=dst, send_sem=send_sems.at[g, k], recv_sem=recv_sems.at[g, k],
                                        device_id=peer, device_id_type=MESH)


def gather_riders(srcs):
    ng = len(srcs)

    def copies(in_refs, out_refs, sems):
        send_sems, recv_sems, local_sems = sems
        me = _place()
        local, sends, recvs = [], [], []
        for g in range(ng):
            mine = out_refs[g].at[:, _dev_index(me)]
            local.append(pltpu.make_async_copy(in_refs[g], mine, local_sems.at[g]))
            for k, flip in enumerate(FLIPS):
                peer = _peer(me, flip)
                sends.append(_remote(in_refs[g], mine, send_sems, recv_sems, g, k, peer))
                recvs.append(_remote(in_refs[g], out_refs[g].at[:, _dev_index(peer)], send_sems, recv_sems, g, k, peer))
        return local, sends, recvs

    def start(in_refs, out_refs, sems):
        local, sends, _ = copies(in_refs, out_refs, sems)
        for cp in local + sends:
            cp.start()

    def wait(in_refs, out_refs, sems):
        local, sends, recvs = copies(in_refs, out_refs, sems)
        for cp in sends:
            cp.wait_send()
        for cp in recvs:
            cp.wait_recv()
        for cp in local:
            cp.wait()

    return Riders(tuple(srcs), tuple(jax.ShapeDtypeStruct((s.shape[0], N_DEV) + s.shape[1:], s.dtype) for s in srcs),
                  (pltpu.SemaphoreType.DMA((ng, 7)), pltpu.SemaphoreType.DMA((ng, 7)), pltpu.SemaphoreType.DMA((ng,))),
                  start, wait)


def scatter_riders(bufs):
    ng = len(bufs)

    def copies(in_refs, out_refs, sems):
        send_sems, recv_sems = sems
        me = _place()
        return [_remote(in_refs[g].at[:, _dev_index(_peer(me, flip))], out_refs[g].at[:, k], send_sems, recv_sems, g, k,
                        _peer(me, flip)) for g in range(ng) for k, flip in enumerate(FLIPS)]

    def start(in_refs, out_refs, sems):
        for cp in copies(in_refs, out_refs, sems):
            cp.start()

    def wait(in_refs, out_refs, sems):
        for cp in copies(in_refs, out_refs, sems):
            cp.wait()

    return Riders(tuple(bufs), tuple(jax.ShapeDtypeStruct((b.shape[0], 7) + b.shape[2:], b.dtype) for b in bufs),
                  (pltpu.SemaphoreType.DMA((ng, 7)), pltpu.SemaphoreType.DMA((ng, 7))), start, wait)


def shard_sum(buf, recv, dev, name):
    n, _, r, cdim = buf.shape

    def body(dev_ref, a_ref, b_ref, o_ref):
        acc = a_ref[...].astype(F32)
        for k in range(7):
            acc = acc + b_ref[k].astype(F32)
        o_ref[...] = acc

    return pl.pallas_call(
        body, name=name,
        grid_spec=pltpu.PrefetchScalarGridSpec(
            num_scalar_prefetch=1, grid=(n,),
            in_specs=[pl.BlockSpec((None, None, r, cdim), lambda i, dev_ref: (i, dev_ref[0], 0, 0)),
                      pl.BlockSpec((None, 7, r, cdim), lambda i, dev_ref: (i, 0, 0, 0))],
            out_specs=pl.BlockSpec((None, r, cdim), lambda i, dev_ref: (i, 0, 0))),
        out_shape=jax.ShapeDtypeStruct((n, r, cdim), F32),
    )(dev, buf, recv)


def all_gather(srcs, name):
    ng = len(srcs)

    def body(*refs):
        x_refs, o_refs = refs[:ng], refs[ng:2 * ng]
        send_sems, recv_sems, local_sems = refs[2 * ng:]
        x, y, c = _place()
        me, sibling = (x, y, c), (x, y, 1 - c)
        chips = [(1 - x, y), (x, 1 - y), (1 - x, 1 - y)]

        def copy(gi, k, block, to, src=None):
            dst = o_refs[gi].at[:, 4 * block[0] + 2 * block[1] + block[2]]
            return pltpu.make_async_remote_copy(
                src_ref=dst if src is None else src, dst_ref=dst, send_sem=send_sems.at[gi, k],
                recv_sem=recv_sems.at[gi, k], device_id=to, device_id_type=MESH)

        mine = [pltpu.make_async_copy(x_refs[gi], o_refs[gi].at[:, 4 * x + 2 * y + c], local_sems.at[gi])
                for gi in range(ng)]
        for cp in mine:
            cp.start()
        first = []
        for gi in range(ng):
            first.append(copy(gi, 0, me, sibling, src=x_refs[gi]))
            first += [copy(gi, 1 + j, me, (*chip, c), src=x_refs[gi]) for j, chip in enumerate(chips)]
        for cp in first:
            cp.start()
        passed = []
        for j, chip in enumerate(chips):
            for gi in range(ng):
                copy(gi, 1 + j, (*chip, c), me).wait_recv()
                cp = copy(gi, 4 + j, (*chip, c), sibling)
                cp.start()
                passed.append(cp)
        for gi in range(ng):
            copy(gi, 0, sibling, me).wait_recv()
            for j, chip in enumerate(chips):
                copy(gi, 4 + j, (*chip, 1 - c), me).wait_recv()
        for cp in first + passed:
            cp.wait_send()
        for cp in mine:
            cp.wait()

    return pl.pallas_call(
        body, name=name,
        in_specs=[ANY] * ng, out_specs=[ANY] * ng,
        out_shape=[jax.ShapeDtypeStruct((s.shape[0], N_DEV) + s.shape[1:], s.dtype) for s in srcs],
        scratch_shapes=[pltpu.SemaphoreType.DMA((ng, 7)), pltpu.SemaphoreType.DMA((ng, 7)),
                        pltpu.SemaphoreType.DMA((ng,))],
    )(*srcs)


def pair_exchange(bufs, name):
    ng = len(bufs)

    def body(*refs):
        b_refs, o_refs = refs[:ng], refs[ng:2 * ng]
        send_sems, recv_sems = refs[2 * ng:]
        x, y, c = _place()
        copies = [pltpu.make_async_remote_copy(
            src_ref=b_refs[gi].at[:, :, 1 - c], dst_ref=o_refs[gi], send_sem=send_sems.at[gi], recv_sem=recv_sems.at[gi],
            device_id=(x, y, 1 - c), device_id_type=MESH) for gi in range(ng)]
        for cp in copies:
            cp.start()
        for cp in copies:
            cp.wait()

    return pl.pallas_call(
        body, name=name,
        in_specs=[ANY] * ng, out_specs=[ANY] * ng,
        out_shape=[jax.ShapeDtypeStruct(b.shape[:2] + b.shape[3:], b.dtype) for b in bufs],
        scratch_shapes=[pltpu.SemaphoreType.DMA((ng,)), pltpu.SemaphoreType.DMA((ng,))],
    )(*bufs)


def chip_exchange(qs, name):
    ng = len(qs)

    def body(*refs):
        q_refs, o_refs = refs[:ng], refs[ng:2 * ng]
        send_sems, recv_sems = refs[2 * ng:]
        x, y, c = _place()
        chips = [(1 - x, y), (x, 1 - y), (1 - x, 1 - y)]
        copies = [pltpu.make_async_remote_copy(
            src_ref=q_refs[gi].at[:, 2 * chip[0] + chip[1]], dst_ref=o_refs[gi].at[:, j],
            send_sem=send_sems.at[gi, j], recv_sem=recv_sems.at[gi, j],
            device_id=(*chip, c), device_id_type=MESH) for gi in range(ng) for j, chip in enumerate(chips)]
        for cp in copies:
            cp.start()
        for cp in copies:
            cp.wait()

    return pl.pallas_call(
        body, name=name,
        in_specs=[ANY] * ng, out_specs=[ANY] * ng,
        out_shape=[jax.ShapeDtypeStruct((q.shape[0], 3) + q.shape[2:], q.dtype) for q in qs],
        scratch_shapes=[pltpu.SemaphoreType.DMA((ng, 3)), pltpu.SemaphoreType.DMA((ng, 3))],
    )(*qs)


def pair_sum(buf, recv, core, name):
    n, _, _, r, cdim = buf.shape

    def body(core_ref, a_ref, b_ref, o_ref):
        o_ref[...] = (a_ref[...].astype(F32) + b_ref[...].astype(F32)).astype(BF16)

    blk = pl.BlockSpec((None, None, r, cdim), lambda i, k, core_ref: (i, k, 0, 0))
    return pl.pallas_call(
        body, name=name,
        grid_spec=pltpu.PrefetchScalarGridSpec(
            num_scalar_prefetch=1, grid=(n, 4),
            in_specs=[pl.BlockSpec((None, None, None, r, cdim), lambda i, k, core_ref: (i, k, core_ref[0], 0, 0)), blk],
            out_specs=blk),
        out_shape=jax.ShapeDtypeStruct((n, 4, r, cdim), BF16),
    )(core, buf, recv)


def chip_sum(q, recv, chip, name):
    n, _, r, cdim = q.shape

    def body(chip_ref, a_ref, b_ref, o_ref):
        acc = a_ref[...].astype(F32)
        for j in range(3):
            acc = acc + b_ref[j].astype(F32)
        o_ref[...] = acc

    return pl.pallas_call(
        body, name=name,
        grid_spec=pltpu.PrefetchScalarGridSpec(
            num_scalar_prefetch=1, grid=(n,),
            in_specs=[pl.BlockSpec((None, None, r, cdim), lambda i, chip_ref: (i, chip_ref[0], 0, 0)),
                      pl.BlockSpec((None, 3, r, cdim), lambda i, chip_ref: (i, 0, 0, 0))],
            out_specs=pl.BlockSpec((None, r, cdim), lambda i, chip_ref: (i, 0, 0))),
        out_shape=jax.ShapeDtypeStruct((n, r, cdim), F32),
    )(chip, q, recv)


def sum_blocks(a, name):
    def body(a_ref, o_ref):
        acc = a_ref[0]
        for d in range(1, a.shape[0]):
            acc = acc + a_ref[d]
        o_ref[...] = acc

    return pl.pallas_call(body, name=name, out_shape=jax.ShapeDtypeStruct(a.shape[1:], F32),
                          compiler_params=pltpu.CompilerParams(vmem_limit_bytes=VMEM_LIMIT))(a)


def adamw(w, g, m, v, name):
    c1 = 1.0 / (1.0 - ADAM_B1 ** ADAM_STEP)
    c2 = 1.0 / (1.0 - ADAM_B2 ** ADAM_STEP)

    def body(w_ref, g_ref, m_ref, v_ref, d_ref, mo_ref, vo_ref):
        gg = g_ref[...]
        m2 = ADAM_B1 * m_ref[...] + (1.0 - ADAM_B1) * gg
        v2 = ADAM_B2 * v_ref[...] + (1.0 - ADAM_B2) * (gg * gg)
        mo_ref[...] = m2
        vo_ref[...] = v2
        d_ref[...] = -ADAM_LR * ((m2 * c1) / (jnp.sqrt(v2 * c2) + ADAM_EPS) + ADAM_WD * w_ref[...])

    out_shape = [jax.ShapeDtypeStruct(w.shape, F32)] * 3
    if w.ndim == 2:
        return pl.pallas_call(body, name=name, out_shape=out_shape,
                              compiler_params=pltpu.CompilerParams(vmem_limit_bytes=VMEM_LIMIT))(w, g, m, v)
    blk = pl.BlockSpec((None,) + w.shape[1:], lambda i: (i, 0, 0))
    return pl.pallas_call(body, name=name, grid=(w.shape[0],), in_specs=[blk] * 4, out_specs=[blk] * 3,
                          out_shape=out_shape, compiler_params=_cparams("arbitrary"))(w, g, m, v)


WEIGHTS = ("norm_ffn1", "ffn1_wg", "ffn1_wu", "ffn1_wd", "norm_mix", "norm_ffn2", "ffn2_wg", "ffn2_wu", "ffn2_wd",
           "rel_bias", "even_w_in", "attn_sinks", "conv_b_w", "conv_b_b", "conv_ln_g", "conv_ln_b", "even_w_out",
           "odd_w_in", "lru_conv_w", "lru_conv_b", "gate_a_w", "gate_a_b", "gate_x_w", "gate_x_b", "lru_lambda",
           "odd_w_out", "norm_final")
BIG = ("ffn1_wg", "ffn1_wu", "ffn1_wd", "ffn2_wg", "ffn2_wu", "ffn2_wd", "even_w_in", "even_w_out", "odd_w_in", "odd_w_out")
SMALL = tuple(n for n in WEIGHTS if n not in BIG)
SMALL_SHARDED = ("conv_b_w", "lru_conv_w", "lru_conv_b", "gate_a_b", "gate_x_b", "lru_lambda")
PACK_ALIGN = 1024


def _pack(arrays):
    parts = []
    for a in arrays:
        flat = a.reshape(-1)
        parts.append(jnp.pad(flat, (0, -flat.shape[0] % PACK_ALIGN)))
    return jnp.concatenate(parts).reshape(-1, 128)


def _unpack(packed, shapes, lead=()):
    flat = packed.reshape(lead + (-1,))
    out, off = [], 0
    for shp in shapes:
        size = math.prod(shp)
        out.append(flat[..., off:off + size].reshape(lead + tuple(shp)))
        off += size + (-size % PACK_ALIGN)
    return out


def _unshard_last(blocks):
    nd = blocks.ndim
    moved = jnp.moveaxis(blocks, 0, nd - 2)
    return moved.reshape(moved.shape[:-2] + (-1,))


def _step(cfg, x, weights, loss_target, ms, vs):
    w = dict(zip(WEIGHTS, weights))
    m = dict(zip(WEIGHTS, ms))
    v = dict(zip(WEIGHTS, vs))
    px, py, pc = _place()
    dev = 4 * px + 2 * py + pc
    core = jnp.reshape(pc, (1,)).astype(jnp.int32)
    chip = jnp.reshape(2 * px + py, (1,)).astype(jnp.int32)
    d = cfg.d
    t = cfg.bl * cfg.s

    def rows(name):
        a = w[name]
        return (a if name.endswith(("wd", "w_out")) else a.transpose(0, 2, 1)).astype(BF16)

    r3 = {n: rows(n) for n in BIG[:6]}
    shards = {(l, k): jnp.stack([r3[f"ffn{k + 1}_{mat}"][l] for mat in ("wg", "wu", "wd")])
              for l in range(cfg.depth) for k in range(2)}
    small_src = _pack([w[n] for n in SMALL_SHARDED])[None]
    gathered = all_gather([shards.pop((0, 0)), shards.pop((0, 1)), rows("even_w_in"), rows("even_w_out"), rows("odd_w_in"),
                           rows("odd_w_out"), small_src], "all_gather_weights")
    full = [g.reshape(g.shape[0], -1, g.shape[-1]) for g in gathered[:6]]
    p = {n: w[n] for n in SMALL if n not in SMALL_SHARDED}
    p.update(wf={(0, 0): full[0], (0, 1): full[1]}, even_in=full[2], even_out=full[3], odd_in=full[4], odd_out=full[5])
    for n, blocks in zip(SMALL_SHARDED, _unpack(gathered[6][0], [w[n].shape for n in SMALL_SHARDED], lead=(N_DEV,))):
        p[n] = _unshard_last(blocks)

    lossp, gx, big, small = local_step(cfg, x.reshape(t, d), loss_target.reshape(t, d), p, shards)
    loss = lax.psum(lossp[0, 0], ("x", "y", "c"))

    dev1 = jnp.reshape(dev, (1,)).astype(jnp.int32)
    shard_rows = [{} for _ in range(cfg.depth)]
    for l in range(cfg.depth):
        for key, (buf, recv) in big[l].items():
            if recv is not None:
                shard_rows[l][key] = shard_sum(buf, recv, dev1, f"rs_sum_{key}_{l}")
    left = [(l, key, buf) for l in range(cfg.depth) for key, (buf, recv) in big[l].items() if recv is None]
    bufs = [buf.reshape(buf.shape[0], 4, 2, buf.shape[2], d) for _, _, buf in left]
    recv = pair_exchange(bufs, "rs_pair")
    qs = [pair_sum(b, r, core, f"rs_pair_sum{j}") for j, (b, r) in enumerate(zip(bufs, recv))]
    recv = chip_exchange(qs, "rs_chip")
    for j, ((l, key, _), q, r) in enumerate(zip(left, qs, recv)):
        shard_rows[l][key] = chip_sum(q, r, chip, f"rs_chip_sum{j}")

    grads = {}
    for k in range(2):
        ffn_g = jnp.stack([shard_rows[l][f"f{k + 1}"] for l in range(cfg.depth)])
        grads[f"ffn{k + 1}_wg"] = ffn_g[:, 0].transpose(0, 2, 1)
        grads[f"ffn{k + 1}_wu"] = ffn_g[:, 1].transpose(0, 2, 1)
        grads[f"ffn{k + 1}_wd"] = ffn_g[:, 2]
    grads["even_w_in"] = jnp.stack([shard_rows[l]["in"][0] for l in range(0, cfg.depth, 2)]).transpose(0, 2, 1)
    grads["even_w_out"] = jnp.stack([shard_rows[l]["out"][0] for l in range(0, cfg.depth, 2)])
    grads["odd_w_in"] = jnp.stack([shard_rows[l]["in"][0] for l in range(1, cfg.depth, 2)]).transpose(0, 2, 1)
    grads["odd_w_out"] = jnp.stack([shard_rows[l]["out"][0] for l in range(1, cfg.depth, 2)])

    full_shapes = [small[n].shape for n in SMALL]
    parts = all_gather([_pack([small[n] for n in SMALL])[None]], "all_gather_small_grads")[0][0]
    for n, g in zip(SMALL, _unpack(sum_blocks(parts, "sum_small_grads"), full_shapes)):
        if n in SMALL_SHARDED:
            width = w[n].shape[-1]
            g = lax.dynamic_slice_in_dim(g, dev * width, width, axis=g.ndim - 1)
        grads[n] = g

    delta, new_m, new_v = {}, {}, {}
    for n in BIG:
        delta[n], new_m[n], new_v[n] = adamw(w[n], grads[n], m[n], v[n], f"adamw_{n}")
    shapes = [w[n].shape for n in SMALL]
    packed = adamw(*[_pack([src[n] for n in SMALL]) for src in (w, grads, m, v)], "adamw_small")
    for out, pk in zip((delta, new_m, new_v), packed):
        out.update(zip(SMALL, _unpack(pk, shapes)))

    return (loss, gx.reshape(x.shape), *[grads[n] for n in WEIGHTS], *[delta[n] for n in WEIGHTS],
            *[new_m[n] for n in WEIGHTS], *[new_v[n] for n in WEIGHTS])


def kernel(x, norm_ffn1, ffn1_wg, ffn1_wu, ffn1_wd, norm_mix, norm_ffn2, ffn2_wg, ffn2_wu, ffn2_wd, rel_bias, even_w_in, attn_sinks, conv_b_w, conv_b_b, conv_ln_g, conv_ln_b, even_w_out, odd_w_in, lru_conv_w, lru_conv_b, gate_a_w, gate_a_b, gate_x_w, gate_x_b, lru_lambda, odd_w_out, norm_final, loss_target, m_norm_ffn1, m_ffn1_wg, m_ffn1_wu, m_ffn1_wd, m_norm_mix, m_norm_ffn2, m_ffn2_wg, m_ffn2_wu, m_ffn2_wd, m_rel_bias, m_even_w_in, m_attn_sinks, m_conv_b_w, m_conv_b_b, m_conv_ln_g, m_conv_ln_b, m_even_w_out, m_odd_w_in, m_lru_conv_w, m_lru_conv_b, m_gate_a_w, m_gate_a_b, m_gate_x_w, m_gate_x_b, m_lru_lambda, m_odd_w_out, m_norm_final, v_norm_ffn1, v_ffn1_wg, v_ffn1_wu, v_ffn1_wd, v_norm_mix, v_norm_ffn2, v_ffn2_wg, v_ffn2_wu, v_ffn2_wd, v_rel_bias, v_even_w_in, v_attn_sinks, v_conv_b_w, v_conv_b_b, v_conv_ln_g, v_conv_ln_b, v_even_w_out, v_odd_w_in, v_lru_conv_w, v_lru_conv_b, v_gate_a_w, v_gate_a_b, v_gate_x_w, v_gate_x_b, v_lru_lambda, v_odd_w_out, v_norm_final):
    weights = (norm_ffn1, ffn1_wg, ffn1_wu, ffn1_wd, norm_mix, norm_ffn2, ffn2_wg, ffn2_wu, ffn2_wd, rel_bias, even_w_in, attn_sinks, conv_b_w, conv_b_b, conv_ln_g, conv_ln_b, even_w_out, odd_w_in, lru_conv_w, lru_conv_b, gate_a_w, gate_a_b, gate_x_w, gate_x_b, lru_lambda, odd_w_out, norm_final)
    ms = (m_norm_ffn1, m_ffn1_wg, m_ffn1_wu, m_ffn1_wd, m_norm_mix, m_norm_ffn2, m_ffn2_wg, m_ffn2_wu, m_ffn2_wd, m_rel_bias, m_even_w_in, m_attn_sinks, m_conv_b_w, m_conv_b_b, m_conv_ln_g, m_conv_ln_b, m_even_w_out, m_odd_w_in, m_lru_conv_w, m_lru_conv_b, m_gate_a_w, m_gate_a_b, m_gate_x_w, m_gate_x_b, m_lru_lambda, m_odd_w_out, m_norm_final)
    vs = (v_norm_ffn1, v_ffn1_wg, v_ffn1_wu, v_ffn1_wd, v_norm_mix, v_norm_ffn2, v_ffn2_wg, v_ffn2_wu, v_ffn2_wd, v_rel_bias, v_even_w_in, v_attn_sinks, v_conv_b_w, v_conv_b_b, v_conv_ln_g, v_conv_ln_b, v_even_w_out, v_odd_w_in, v_lru_conv_w, v_lru_conv_b, v_gate_a_w, v_gate_a_b, v_gate_x_w, v_gate_x_b, v_lru_lambda, v_odd_w_out, v_norm_final)
    return _step(Cfg(), x, weights, loss_target, ms, vs)
```

```python
import math
from typing import NamedTuple

import jax
import jax.numpy as jnp
from jax import lax
from jax.experimental import pallas as pl
from jax.experimental.pallas import tpu as pltpu

F32 = jnp.float32
BF16 = jnp.bfloat16
RMS_EPS = 1e-6
LN_EPS = 1e-5
NEG_INF = -1e30
RG_LRU_C = 8.0
ADAM_LR = 0.001
ADAM_B1 = 0.9
ADAM_B2 = 0.999
ADAM_EPS = 1e-08
ADAM_WD = 0.01
ADAM_STEP = 10
N_DEV = 8
VMEM_LIMIT = 56 * 1024 * 1024


class Cfg(NamedTuple):
    d: int = 1024
    f: int = 2816
    s: int = 2048
    bl: int = 4
    hq: int = 8
    hkv: int = 2
    hd: int = 64
    win: int = 128
    cc: int = 512
    cw: int = 31
    lh: int = 8
    lb: int = 128
    lcw: int = 4
    nbuckets: int = 32
    max_dist: int = 128
    depth: int = 4
    tm: int = 512
    tm_ffn: int = 1024
    tf: int = 256
    tk_ffn: int = 256
    tf_w: int = 1408
    tk: int = 1024
    ct_f: int = 512
    ct_b: int = 256

    @property
    def qw(self):
        return self.hq * self.hd

    @property
    def kvw(self):
        return self.hkv * self.hd

    @property
    def even_in(self):
        return self.qw + 2 * self.kvw + 2 * self.cc

    @property
    def even_cat(self):
        return self.qw + self.cc

    @property
    def lw(self):
        return self.lh * self.lb


def _cparams(*sem):
    return pltpu.CompilerParams(dimension_semantics=sem, vmem_limit_bytes=VMEM_LIMIT)


def _nt(a, b):
    return lax.dot_general(a, b, (((1,), (1,)), ((), ())), preferred_element_type=F32)


def _nn(a, b):
    return lax.dot_general(a, b, (((1,), (0,)), ((), ())), preferred_element_type=F32)


def _tn(a, b):
    return lax.dot_general(a, b, (((0,), (0,)), ((), ())), preferred_element_type=F32)


def _rstd(h):
    return lax.rsqrt(jnp.mean(h * h, axis=-1, keepdims=True) + RMS_EPS)


def _rms_bwd(h, nw, dxn):
    rstd = _rstd(h)
    dyg = dxn * nw
    dnw = jnp.sum(dxn * h * rstd, axis=0, keepdims=True)
    dx = rstd * (dyg - h * (rstd * rstd) * jnp.mean(dyg * h, axis=-1, keepdims=True))
    return dx, dnw


def _sigmoid(x):
    return 0.5 * jnp.tanh(0.5 * x) + 0.5


FFN_SLABS = 4


def _ffn_wspecs(tf, d, gu, md):
    return [pl.BlockSpec((2, tf, d), lambda i, j: (gu, j, 0)), pl.BlockSpec((None, tf, d), lambda i, j: (md, j, 0))]


class Riders(NamedTuple):
    inputs: tuple
    out_shape: tuple
    scratch: tuple
    start: object
    wait: object


def _ride(riders, grid, n_in, n_out, body):
    if riders is None:
        return body, [], [], [], [], []
    ni, no, ns = len(riders.inputs), len(riders.out_shape), len(riders.scratch)

    def full(*refs):
        ins, rin = refs[:n_in], refs[n_in:n_in + ni]
        outs = refs[n_in + ni:n_in + ni + n_out]
        rout = refs[n_in + ni + n_out:n_in + ni + n_out + no]
        rest = refs[n_in + ni + n_out + no:]
        scratch, sems = rest[:len(rest) - ns], rest[len(rest) - ns:]
        first = last = None
        for axis, size in enumerate(grid):
            pid = pl.program_id(axis)
            first = (pid == 0) if first is None else first & (pid == 0)
            last = (pid == size - 1) if last is None else last & (pid == size - 1)

        @pl.when(first)
        def _():
            riders.start(rin, rout, sems)

        body(*ins, *outs, *scratch)

        @pl.when(last)
        def _():
            riders.wait(rin, rout, sems)

    any_spec = pl.BlockSpec(memory_space=pl.ANY)
    return full, list(riders.inputs), [any_spec] * ni, [any_spec] * no, list(riders.out_shape), list(riders.scratch)


def ffn_fwd(cfg, h, nw, wts, gu, md, name, riders=None):
    t, d = h.shape
    f = wts.shape[1]
    tm, tf = cfg.tm_ffn, cfg.tf
    nj = f // tf

    def body(h_ref, nw_ref, wgu_ref, wd_ref, ho_ref, xn_ref, g_ref, u_ref, acc_ref):
        j = pl.program_id(1)

        @pl.when(j == 0)
        def _():
            hh = h_ref[...]
            xn_ref[...] = (hh * _rstd(hh) * nw_ref[...]).astype(BF16)
            acc_ref[...] = jnp.zeros_like(acc_ref)

        gu = _nt(xn_ref[...], wgu_ref[...].reshape(2 * tf, d))
        g, u = gu[:, :tf], gu[:, tf:]
        g_ref[...] = g.astype(BF16)
        u_ref[...] = u.astype(BF16)
        acc_ref[...] += _nn((g * _sigmoid(g) * u).astype(BF16), wd_ref[...])

        @pl.when(j == nj - 1)
        def _():
            ho_ref[...] = h_ref[...] + 0.5 * acc_ref[...]

    row = pl.BlockSpec((tm, d), lambda i, j: (i, 0))
    hid = pl.BlockSpec((tm, tf), lambda i, j: (i, j))
    grid = (t // tm, nj)
    full, r_args, r_in, r_out, r_shape, r_scratch = _ride(riders, grid, 4, 4, body)
    return pl.pallas_call(
        full, name=name, grid=grid,
        in_specs=[row, pl.BlockSpec((1, d), lambda i, j: (0, 0))] + _ffn_wspecs(tf, d, gu, md) + r_in,
        out_specs=[row, row, hid, hid] + r_out,
        out_shape=[jax.ShapeDtypeStruct((t, d), F32), jax.ShapeDtypeStruct((t, d), BF16),
                   jax.ShapeDtypeStruct((t, f), BF16), jax.ShapeDtypeStruct((t, f), BF16)] + r_shape,
        scratch_shapes=[pltpu.VMEM((tm, d), F32)] + r_scratch,
        compiler_params=_cparams("arbitrary", "arbitrary"),
    )(h, nw, wts, wts, *r_args)


def ffn_bwd_x(cfg, dh, h, nw, g, u, wts, gu, md, name, riders=None):
    t, d = h.shape
    f = wts.shape[1]
    tm, tf = cfg.tm_ffn, cfg.tf
    nj = f // tf

    def body(dh_ref, h_ref, nw_ref, g_ref, u_ref, wgu_ref, wd_ref,
             dho_ref, dout_ref, dg_ref, du_ref, dnw_ref, acc_ref, da_ref):
        i, j = pl.program_id(0), pl.program_id(1)

        @pl.when(j == 0)
        def _():
            dout_ref[...] = (0.5 * dh_ref[...]).astype(BF16)
            acc_ref[...] = jnp.zeros_like(acc_ref)
            da_ref[1] = jnp.zeros((tm, tf), F32)

        @pl.when((i == 0) & (j == 0))
        def _():
            dnw_ref[...] = jnp.zeros_like(dnw_ref)

        slot = lax.rem(j, 2)
        da = da_ref[1 - slot]
        da_ref[slot] = _nt(dout_ref[...], wd_ref[...])
        gg = g_ref[...].astype(F32)
        sig = _sigmoid(gg)
        dg = (da * u_ref[...].astype(F32) * (sig * (1.0 + gg * (1.0 - sig)))).astype(BF16)
        du = (da * (gg * sig)).astype(BF16)
        dg_ref[...] = dg
        du_ref[...] = du
        acc_ref[...] += _nn(jnp.concatenate([dg, du], axis=1), wgu_ref[...].reshape(2 * tf, d))

        @pl.when(j == nj)
        def _():
            dx, dnw = _rms_bwd(h_ref[...], nw_ref[...], acc_ref[...])
            dnw_ref[0:1, :] += dnw
            dho_ref[...] = dh_ref[...] + dx

    row = pl.BlockSpec((tm, d), lambda i, j: (i, 0))
    prev = pl.BlockSpec((tm, tf), lambda i, j: (i, jnp.maximum(j - 1, 0)))
    wspecs = [pl.BlockSpec((2, tf, d), lambda i, j: (gu, jnp.maximum(j - 1, 0), 0)),
              pl.BlockSpec((None, tf, d), lambda i, j: (md, jnp.minimum(j, nj - 1), 0))]
    grid = (t // tm, nj + 1)
    full, r_args, r_in, r_out, r_shape, r_scratch = _ride(riders, grid, 7, 5, body)
    return pl.pallas_call(
        full, name=name, grid=grid,
        in_specs=[row, row, pl.BlockSpec((1, d), lambda i, j: (0, 0)), prev, prev] + wspecs + r_in,
        out_specs=[row, row, prev, prev, pl.BlockSpec((8, d), lambda i, j: (0, 0))] + r_out,
        out_shape=[jax.ShapeDtypeStruct((t, d), F32), jax.ShapeDtypeStruct((t, d), BF16),
                   jax.ShapeDtypeStruct((t, f), BF16), jax.ShapeDtypeStruct((t, f), BF16),
                   jax.ShapeDtypeStruct((8, d), F32)] + r_shape,
        scratch_shapes=[pltpu.VMEM((tm, d), F32), pltpu.VMEM((2, tm, tf), F32)] + r_scratch,
        compiler_params=_cparams("arbitrary", "arbitrary"),
    )(dh, h, nw, g, u, wts, wts, *r_args)


def ffn_bwd_w(cfg, xn, dout, g, u, dg, du, name, riders=None):
    t, d = xn.shape
    f = g.shape[1]
    tk, tf = cfg.tk_ffn, cfg.tf_w
    nk = t // tk

    def body(xn_ref, dout_ref, g_ref, u_ref, dg_ref, du_ref, o_ref, acc_ref):
        k = pl.program_id(1)

        @pl.when(k == 0)
        def _():
            acc_ref[...] = jnp.zeros_like(acc_ref)

        gg = g_ref[...].astype(F32)
        a = (gg * _sigmoid(gg) * u_ref[...].astype(F32)).astype(BF16)
        xn_t = xn_ref[...]
        acc_ref[0] += _tn(dg_ref[...], xn_t)
        acc_ref[1] += _tn(du_ref[...], xn_t)
        acc_ref[2] += _tn(a, dout_ref[...])

        @pl.when(k == nk - 1)
        def _():
            o_ref[...] = acc_ref[...].astype(BF16)

    row = pl.BlockSpec((tk, d), lambda j, k: (k, 0))
    hid = pl.BlockSpec((tk, tf), lambda j, k: (k, j))
    grid = (f // tf, nk)
    full, r_args, r_in, r_out, r_shape, r_scratch = _ride(riders, grid, 6, 1, body)
    return pl.pallas_call(
        full, name=name, grid=grid,
        in_specs=[row, row, hid, hid, hid, hid] + r_in,
        out_specs=[pl.BlockSpec((3, tf, d), lambda j, k: (0, j, 0))] + r_out,
        out_shape=[jax.ShapeDtypeStruct((3, f, d), BF16)] + r_shape,
        scratch_shapes=[pltpu.VMEM((3, tf, d), F32)] + r_scratch,
        compiler_params=_cparams("arbitrary", "arbitrary"),
    )(xn, dout, g, u, dg, du, *r_args)


def _wspec(w, wi):
    if w.ndim == 2:
        return pl.BlockSpec(w.shape, lambda i: (0, 0))
    return pl.BlockSpec((None,) + w.shape[1:], lambda i: (wi, 0, 0))


def norm_proj(cfg, h, nw, w, name, wi=0):
    t, d = h.shape
    n = w.shape[-2]
    tm = cfg.tm

    def body(h_ref, nw_ref, w_ref, u_ref, xn_ref):
        hh = h_ref[...]
        xn = (hh * _rstd(hh) * nw_ref[...]).astype(BF16)
        xn_ref[...] = xn
        u_ref[...] = _nt(xn, w_ref[...]).astype(BF16)

    return pl.pallas_call(
        body, name=name, grid=(t // tm,),
        in_specs=[pl.BlockSpec((tm, d), lambda i: (i, 0)), pl.BlockSpec((1, d), lambda i: (0, 0)),
                  _wspec(w, wi)],
        out_specs=[pl.BlockSpec((tm, n), lambda i: (i, 0)), pl.BlockSpec((tm, d), lambda i: (i, 0))],
        out_shape=[jax.ShapeDtypeStruct((t, n), BF16), jax.ShapeDtypeStruct((t, d), BF16)],
        compiler_params=_cparams("arbitrary"),
    )(h, nw, w)


def proj_residual(cfg, h, parts, w, name, wi=0):
    t, d = h.shape
    tm = cfg.tm
    ks = [p.shape[1] for p in parts]
    offs = [sum(ks[:i]) for i in range(len(ks))]
    np_ = len(parts)

    def body(*refs):
        h_ref, w_ref, ho_ref = refs[0], refs[1 + np_], refs[2 + np_]
        acc = h_ref[...]
        for p_ref, off, k in zip(refs[1:1 + np_], offs, ks):
            acc = acc + _nn(p_ref[...], w_ref[off:off + k, :])
        ho_ref[...] = acc

    return pl.pallas_call(
        body, name=name, grid=(t // tm,),
        in_specs=[pl.BlockSpec((tm, d), lambda i: (i, 0))]
        + [pl.BlockSpec((tm, k), lambda i: (i, 0)) for k in ks]
        + [_wspec(w, wi)],
        out_specs=pl.BlockSpec((tm, d), lambda i: (i, 0)),
        out_shape=jax.ShapeDtypeStruct((t, d), F32),
        compiler_params=_cparams("arbitrary"),
    )(h, *parts, w)


def proj_bwd_act(cfg, dh, w, name, wi=0):
    t, d = dh.shape
    k = w.shape[-2]
    tm = cfg.tm

    def body(dh_ref, w_ref, o_ref):
        o_ref[...] = _nt(dh_ref[...].astype(BF16), w_ref[...]).astype(BF16)

    return pl.pallas_call(
        body, name=name, grid=(t // tm,),
        in_specs=[pl.BlockSpec((tm, d), lambda i: (i, 0)), _wspec(w, wi)],
        out_specs=pl.BlockSpec((tm, k), lambda i: (i, 0)),
        out_shape=jax.ShapeDtypeStruct((t, k), BF16),
        compiler_params=_cparams("arbitrary"),
    )(dh, w)


def grad_weight(cfg, parts, b, name):
    t, d = b.shape
    tk = cfg.tk
    nk = t // tk
    ks = [p.shape[1] for p in parts]
    offs = [sum(ks[:i]) for i in range(len(ks))]
    np_ = len(parts)

    def body(*refs):
        b_ref, o_ref, acc_ref = refs[np_:]
        kk = pl.program_id(0)

        @pl.when(kk == 0)
        def _():
            acc_ref[...] = jnp.zeros_like(acc_ref)

        bb = b_ref[...].astype(BF16)
        for a_ref, off, k in zip(refs[:np_], offs, ks):
            acc_ref[off:off + k, :] += _tn(a_ref[...], bb)

        @pl.when(kk == nk - 1)
        def _():
            o_ref[...] = acc_ref[...].astype(BF16)

    return pl.pallas_call(
        body, name=name, grid=(nk,),
        in_specs=[pl.BlockSpec((tk, k), lambda kk: (kk, 0)) for k in ks] + [pl.BlockSpec((tk, d), lambda kk: (kk, 0))],
        out_specs=pl.BlockSpec((sum(ks), d), lambda kk: (0, 0)),
        out_shape=jax.ShapeDtypeStruct((sum(ks), d), BF16),
        scratch_shapes=[pltpu.VMEM((sum(ks), d), F32)],
        compiler_params=_cparams("arbitrary"),
    )(*parts, b)


def norm_proj_bwd(cfg, dh, h, nw, parts, w, name, wi=0):
    t, d = h.shape
    tm = cfg.tm
    ks = [p.shape[1] for p in parts]
    offs = [sum(ks[:i]) for i in range(len(ks))]
    np_ = len(parts)

    def body(*refs):
        dh_ref, h_ref, nw_ref = refs[:3]
        w_ref, dho_ref, dnw_ref = refs[3 + np_:]

        @pl.when(pl.program_id(0) == 0)
        def _():
            dnw_ref[...] = jnp.zeros_like(dnw_ref)

        dxn = None
        for p_ref, off, k in zip(refs[3:3 + np_], offs, ks):
            term = _nn(p_ref[...], w_ref[off:off + k, :])
            dxn = term if dxn is None else dxn + term
        dx, dnw = _rms_bwd(h_ref[...], nw_ref[...], dxn)
        dnw_ref[0:1, :] += dnw
        dho_ref[...] = dh_ref[...] + dx

    row = pl.BlockSpec((tm, d), lambda i: (i, 0))
    return pl.pallas_call(
        body, name=name, grid=(t // tm,),
        in_specs=[row, row, pl.BlockSpec((1, d), lambda i: (0, 0))]
        + [pl.BlockSpec((tm, k), lambda i: (i, 0)) for k in ks]
        + [_wspec(w, wi)],
        out_specs=[row, pl.BlockSpec((8, d), lambda i: (0, 0))],
        out_shape=[jax.ShapeDtypeStruct((t, d), F32), jax.ShapeDtypeStruct((8, d), F32)],
        compiler_params=_cparams("arbitrary"),
    )(dh, h, nw, *parts, w)


def loss_head(cfg, h, nf, tgt, name):
    t, d = h.shape
    tm = cfg.tm

    def body(h_ref, nf_ref, tgt_ref, loss_ref, dh_ref, dnf_ref):
        @pl.when(pl.program_id(0) == 0)
        def _():
            loss_ref[...] = jnp.zeros_like(loss_ref)
            dnf_ref[...] = jnp.zeros_like(dnf_ref)

        hh = h_ref[...]
        err = hh * _rstd(hh) * nf_ref[...] - tgt_ref[...]
        row = jnp.sum(err * err, axis=-1, keepdims=True) * (0.5 / d)
        loss_ref[...] += jnp.sum(row, axis=0, keepdims=True)
        dx, dnf = _rms_bwd(hh, nf_ref[...], err * (1.0 / d))
        dnf_ref[0:1, :] += dnf
        dh_ref[...] = dx

    row = pl.BlockSpec((tm, d), lambda i: (i, 0))
    return pl.pallas_call(
        body, name=name, grid=(t // tm,),
        in_specs=[row, pl.BlockSpec((1, d), lambda i: (0, 0)), row],
        out_specs=[pl.BlockSpec((8, 128), lambda i: (0, 0)), row, pl.BlockSpec((8, d), lambda i: (0, 0))],
        out_shape=[jax.ShapeDtypeStruct((8, 128), F32), jax.ShapeDtypeStruct((t, d), F32),
                   jax.ShapeDtypeStruct((8, d), F32)],
        compiler_params=_cparams("arbitrary"),
    )(h, nf, tgt)


def bucket_table(cfg):
    qi = jnp.arange(cfg.win)[:, None]
    sj = jnp.arange(2 * cfg.win)[None, :]
    dist = qi + cfg.win - sj
    n = jnp.maximum(dist, 0)
    max_exact = cfg.nbuckets // 2
    nf = jnp.maximum(n, max_exact).astype(F32)
    large = max_exact + (jnp.log(nf / max_exact) / math.log(cfg.max_dist / max_exact)
                         * (cfg.nbuckets - max_exact)).astype(jnp.int32)
    large = jnp.minimum(large, cfg.nbuckets - 1)
    bucket = jnp.where(n < max_exact, n, large)
    return jnp.where((dist >= 0) & (dist < cfg.win), bucket, -1).astype(jnp.int32)


def bias_build(cfg, rel_bias, buckets, name):
    w = cfg.win

    def body(rb_ref, bk_ref, o_ref):
        bk = bk_ref[...]
        for h in range(cfg.hq):
            acc = jnp.full((w, 2 * w), NEG_INF, F32)
            for b in range(cfg.nbuckets):
                acc = jnp.where(bk == b, rb_ref[b, h], acc)
            o_ref[h] = acc

    return pl.pallas_call(
        body, name=name,
        in_specs=[pl.BlockSpec(memory_space=pltpu.SMEM), pl.BlockSpec(memory_space=pltpu.VMEM)],
        out_specs=pl.BlockSpec(memory_space=pltpu.VMEM),
        out_shape=jax.ShapeDtypeStruct((cfg.hq, w, 2 * w), F32),
    )(rel_bias, buckets)


def bias_grad(cfg, dbias, buckets, name):
    w = cfg.win

    def body(db_ref, bk_ref, o_ref, rows_ref):
        bk = bk_ref[...]
        for h in range(cfg.hq):
            d = db_ref[0, h]
            for e in range(1, dbias.shape[0]):
                d = d + db_ref[e, h]
            for b in range(cfg.nbuckets):
                rows_ref[b:b + 1, :] = jnp.sum(jnp.where(bk == b, d, 0.0), axis=0, keepdims=True)
            o_ref[h] = jnp.broadcast_to(jnp.sum(rows_ref[...], axis=1, keepdims=True), (cfg.nbuckets, 128))

    return pl.pallas_call(
        body, name=name,
        in_specs=[pl.BlockSpec(memory_space=pltpu.VMEM), pl.BlockSpec(memory_space=pltpu.VMEM)],
        out_specs=pl.BlockSpec(memory_space=pltpu.VMEM),
        out_shape=jax.ShapeDtypeStruct((cfg.hq, cfg.nbuckets, 128), F32),
        scratch_shapes=[pltpu.VMEM((cfg.nbuckets, 2 * w), F32)],
    )(dbias, buckets)


def _attn_probs(cfg, qh, kj, bias_h, sink, first_ok):
    s = _nt(qh, kj) * (1.0 / math.sqrt(cfg.hd)) + bias_h
    s = jnp.where(first_ok, s, NEG_INF)
    m = jnp.maximum(jnp.max(s, axis=-1, keepdims=True), sink)
    e = jnp.exp(s - m)
    es = jnp.exp(sink - m)
    inv = 1.0 / (jnp.sum(e, axis=-1, keepdims=True) + es)
    return e * inv, es * inv


def _attn_block_inputs(cfg, n, q_ref, kv_ref):
    w = cfg.win
    r0 = pl.multiple_of(n * w, w)
    rp = pl.multiple_of(jnp.maximum(n - 1, 0) * w, w)
    qb = q_ref[pl.ds(r0, w), :]
    kk = jnp.concatenate([kv_ref[pl.ds(rp, w), :], kv_ref[pl.ds(r0, w), :]], axis=0)
    col = lax.broadcasted_iota(jnp.int32, (w, 2 * w), 1)
    first_ok = (n > 0) | (col >= w)
    return r0, rp, qb, kk, first_ok


def _kv_col_block(cfg):
    assert cfg.qw % (2 * cfg.kvw) == 0
    return cfg.qw // (2 * cfg.kvw)


def attn_fwd(cfg, u, bias, sinks, name):
    t = u.shape[0]
    s, w, hd, g = cfg.s, cfg.win, cfg.hd, cfg.hq // cfg.hkv
    kvb = _kv_col_block(cfg)

    def body(q_ref, kv_ref, bias_ref, sink_ref, o_ref):
        def blk(n, carry):
            r0, _, qb, kk, first_ok = _attn_block_inputs(cfg, n, q_ref, kv_ref)
            outs = []
            for j in range(cfg.hkv):
                kj = kk[:, hd * j:hd * (j + 1)]
                vj = kk[:, cfg.kvw + hd * j:cfg.kvw + hd * (j + 1)]
                for gq in range(g):
                    h = j * g + gq
                    p, _ = _attn_probs(cfg, qb[:, hd * h:hd * (h + 1)], kj, bias_ref[h], sink_ref[h], first_ok)
                    outs.append(_nn(p.astype(BF16), vj))
            o_ref[pl.ds(r0, w), :] = jnp.concatenate(outs, axis=1).astype(BF16)
            return carry

        lax.fori_loop(0, s // w, blk, 0)

    return pl.pallas_call(
        body, name=name, grid=(t // s,),
        in_specs=[pl.BlockSpec((s, cfg.qw), lambda b: (b, 0)), pl.BlockSpec((s, 2 * cfg.kvw), lambda b: (b, kvb)),
                  pl.BlockSpec(bias.shape, lambda b: (0, 0, 0)), pl.BlockSpec(memory_space=pltpu.SMEM)],
        out_specs=pl.BlockSpec((s, cfg.qw), lambda b: (b, 0)),
        out_shape=jax.ShapeDtypeStruct((t, cfg.qw), BF16),
        compiler_params=_cparams("arbitrary"),
    )(u, u, bias, sinks)


def attn_bwd(cfg, u, dcat, bias, sinks, name):
    t = u.shape[0]
    s, w, hd, g = cfg.s, cfg.win, cfg.hd, cfg.hq // cfg.hkv
    kvb = _kv_col_block(cfg)
    scale = 1.0 / math.sqrt(hd)
    assert cfg.hq <= 8

    def body(q_ref, kv_ref, do_ref, bias_ref, sink_ref, du_ref, dbias_ref, dsink_ref, dkv_ref):
        @pl.when(pl.program_id(0) == 0)
        def _():
            dbias_ref[...] = jnp.zeros_like(dbias_ref)
            dsink_ref[...] = jnp.zeros_like(dsink_ref)

        dkv_ref[...] = jnp.zeros_like(dkv_ref)

        def blk(n, carry):
            r0, rp, qb, kk, first_ok = _attn_block_inputs(cfg, n, q_ref, kv_ref)
            dob = do_ref[pl.ds(r0, w), :]
            dqs, dks, dvs = [], [], []
            for j in range(cfg.hkv):
                kj = kk[:, hd * j:hd * (j + 1)]
                vj = kk[:, cfg.kvw + hd * j:cfg.kvw + hd * (j + 1)]
                dk = jnp.zeros((2 * w, hd), F32)
                dv = jnp.zeros((2 * w, hd), F32)
                for gq in range(g):
                    h = j * g + gq
                    qh = qb[:, hd * h:hd * (h + 1)]
                    doh = dob[:, hd * h:hd * (h + 1)]
                    p, ps = _attn_probs(cfg, qh, kj, bias_ref[h], sink_ref[h], first_ok)
                    dp = _nt(doh, vj)
                    delta = jnp.sum(p * dp, axis=-1, keepdims=True)
                    ds = p * (dp - delta)
                    dsink_ref[h:h + 1, :] += jnp.broadcast_to(-jnp.sum(ps * delta, axis=0, keepdims=True), (1, 128))
                    dbias_ref[h] += ds
                    dsb = ds.astype(BF16)
                    dqs.append(_nn(dsb, kj) * scale)
                    dk = dk + _tn(dsb, qh) * scale
                    dv = dv + _tn(p.astype(BF16), doh)
                dks.append(dk)
                dvs.append(dv)
            du_ref[pl.ds(r0, w), 0:cfg.qw] = jnp.concatenate(dqs, axis=1).astype(BF16)
            dkv = jnp.concatenate(dks + dvs, axis=1)
            dkv_ref[pl.ds(rp, w), :] += dkv[:w]
            dkv_ref[pl.ds(r0, w), :] += dkv[w:]
            return carry

        lax.fori_loop(0, s // w, blk, 0)
        du_ref[:, cfg.qw:] = dkv_ref[...].astype(BF16)

    wa = cfg.qw + 2 * cfg.kvw
    return pl.pallas_call(
        body, name=name, grid=(t // s,),
        in_specs=[pl.BlockSpec((s, cfg.qw), lambda b: (b, 0)), pl.BlockSpec((s, 2 * cfg.kvw), lambda b: (b, kvb)),
                  pl.BlockSpec((s, cfg.qw), lambda b: (b, 0)),
                  pl.BlockSpec(bias.shape, lambda b: (0, 0, 0)), pl.BlockSpec(memory_space=pltpu.SMEM)],
        out_specs=[pl.BlockSpec((s, wa), lambda b: (b, 0)), pl.BlockSpec(bias.shape, lambda b: (0, 0, 0)),
                   pl.BlockSpec((8, 128), lambda b: (0, 0))],
        out_shape=[jax.ShapeDtypeStruct((t, wa), BF16), jax.ShapeDtypeStruct(bias.shape, F32),
                   jax.ShapeDtypeStruct((8, 128), F32)],
        scratch_shapes=[pltpu.VMEM((s, 2 * cfg.kvw), F32)],
        compiler_params=_cparams("arbitrary"),
    )(u, u, dcat, bias, sinks)


def _shift_views(win, rc, pad):
    return [win] + [win[j:j + rc + pad - 8] for j in range(1, 8)]


def _tap(views, off, rc):
    a = 8 * (off // 8)
    return views[off % 8][a:a + rc]


def _conv_rows(views, w_ref, cw, pad, rc):
    acc = None
    for k in range(cw):
        term = _tap(views, pad - (cw - 1) + k, rc) * w_ref[k:k + 1, :]
        acc = term if acc is None else acc + term
    return acc


def _conv_rows_t(views, w_ref, cw, rc):
    acc = None
    for k in range(cw):
        term = _tap(views, cw - 1 - k, rc) * w_ref[k:k + 1, :]
        acc = term if acc is None else acc + term
    return acc


def _group_sum(x):
    acc = x[0:8]
    for i in range(1, x.shape[0] // 8):
        acc = acc + x[8 * i:8 * i + 8]
    return acc


def _conv_wgrad(views, dy, acc_ref, cw, pad, rc):
    for k in range(cw):
        acc_ref[k] += _group_sum(dy * _tap(views, pad - (cw - 1) + k, rc))


CONV_RC = 64
CONV_PAD = 32
GLU_RC = 256


def _conv_col_blocks(cfg):
    off = cfg.qw + 2 * cfg.kvw
    bw = math.gcd(off, cfg.cc)
    assert bw % 128 == 0
    n = cfg.cc // bw
    return bw, [off // bw + i for i in range(n)], [(off + cfg.cc) // bw + i for i in range(n)]


def _glu_inputs(a_refs, b_refs, rows):
    ga = jnp.concatenate([r[rows, :] for r in a_refs], axis=1).astype(F32)
    gb = jnp.concatenate([r[rows, :] for r in b_refs], axis=1).astype(F32)
    return ga, gb


def _fill_glu(cfg, a_refs, b_refs, xp_ref):
    xp_ref[0:CONV_PAD, :] = jnp.zeros((CONV_PAD, cfg.cc), F32)

    def fill(i, carry):
        r0 = pl.multiple_of(i * GLU_RC, GLU_RC)
        ga, gb = _glu_inputs(a_refs, b_refs, pl.ds(r0, GLU_RC))
        xp_ref[pl.ds(CONV_PAD + r0, GLU_RC), :] = ga * _sigmoid(gb)
        return carry

    lax.fori_loop(0, cfg.s // GLU_RC, fill, 0)


def _layernorm_stats(cv):
    mu = jnp.mean(cv, axis=-1, keepdims=True)
    xc = cv - mu
    rstd = lax.rsqrt(jnp.mean(xc * xc, axis=-1, keepdims=True) + LN_EPS)
    return xc * rstd, rstd


def conv_fwd(cfg, u, cw_w, cb, lg, lb, name):
    t = u.shape[0]
    s, cc, cw = cfg.s, cfg.cc, cfg.cw
    bw, a_idx, b_idx = _conv_col_blocks(cfg)
    nb = len(a_idx)

    def body(*refs):
        a_refs, b_refs = refs[:nb], refs[nb:2 * nb]
        w_ref, cb_ref, lg_ref, lb_ref, o_ref, xp_ref = refs[2 * nb:]
        _fill_glu(cfg, a_refs, b_refs, xp_ref)

        def chunk(i, carry):
            r0 = pl.multiple_of(i * CONV_RC, CONV_RC)
            views = _shift_views(xp_ref[pl.ds(r0, CONV_RC + CONV_PAD), :], CONV_RC, CONV_PAD)
            cv = _conv_rows(views, w_ref, cw, CONV_PAD, CONV_RC) + cb_ref[...]
            xhat, _ = _layernorm_stats(cv)
            ln = xhat * lg_ref[...] + lb_ref[...]
            o_ref[pl.ds(r0, CONV_RC), :] = (ln * _sigmoid(ln)).astype(BF16)
            return carry

        lax.fori_loop(0, s // CONV_RC, chunk, 0)

    def colspec(j):
        return pl.BlockSpec((s, bw), lambda b: (b, j))

    vec = pl.BlockSpec((1, cc), lambda b: (0, 0))
    return pl.pallas_call(
        body, name=name, grid=(t // s,),
        in_specs=[colspec(j) for j in a_idx + b_idx] + [pl.BlockSpec((cw, cc), lambda b: (0, 0)), vec, vec, vec],
        out_specs=pl.BlockSpec((s, cc), lambda b: (b, 0)),
        out_shape=jax.ShapeDtypeStruct((t, cc), BF16),
        scratch_shapes=[pltpu.VMEM((CONV_PAD + s, cc), F32)],
        compiler_params=_cparams("arbitrary"),
    )(*([u] * (2 * nb)), cw_w, cb, lg, lb)


def conv_bwd(cfg, u, dcat, cw_w, cb, lg, lb, name):
    t = u.shape[0]
    s, cc, cw = cfg.s, cfg.cc, cfg.cw
    bw, a_idx, b_idx = _conv_col_blocks(cfg)
    nb = len(a_idx)
    assert cfg.qw % cc == 0 and cw <= 32

    def body(*refs):
        a_refs, b_refs = refs[:nb], refs[nb:2 * nb]
        dc_ref, w_ref, cb_ref, lg_ref, lb_ref, du_ref, dw_ref, dvec_ref, xp_ref, dcv_ref, dwacc_ref = refs[2 * nb:]

        @pl.when(pl.program_id(0) == 0)
        def _():
            dw_ref[...] = jnp.zeros_like(dw_ref)
            dvec_ref[...] = jnp.zeros_like(dvec_ref)

        _fill_glu(cfg, a_refs, b_refs, xp_ref)
        dcv_ref[s:s + CONV_PAD, :] = jnp.zeros((CONV_PAD, cc), F32)
        dwacc_ref[...] = jnp.zeros_like(dwacc_ref)

        def chunk(i, carry):
            r0 = pl.multiple_of(i * CONV_RC, CONV_RC)
            views = _shift_views(xp_ref[pl.ds(r0, CONV_RC + CONV_PAD), :], CONV_RC, CONV_PAD)
            cv = _conv_rows(views, w_ref, cw, CONV_PAD, CONV_RC) + cb_ref[...]
            xhat, rstd = _layernorm_stats(cv)
            ln = xhat * lg_ref[...] + lb_ref[...]
            sg = _sigmoid(ln)
            dln = dc_ref[pl.ds(r0, CONV_RC), :].astype(F32) * (sg * (1.0 + ln * (1.0 - sg)))
            dxh = dln * lg_ref[...]
            dcv = rstd * (dxh - jnp.mean(dxh, axis=-1, keepdims=True)
                          - xhat * jnp.mean(dxh * xhat, axis=-1, keepdims=True))
            dcv_ref[pl.ds(r0, CONV_RC), :] = dcv
            dvec_ref[0:1, :] += jnp.sum(dcv, axis=0, keepdims=True)
            dvec_ref[1:2, :] += jnp.sum(dln * xhat, axis=0, keepdims=True)
            dvec_ref[2:3, :] += jnp.sum(dln, axis=0, keepdims=True)
            _conv_wgrad(views, dcv, dwacc_ref, cw, CONV_PAD, CONV_RC)
            return carry

        lax.fori_loop(0, s // CONV_RC, chunk, 0)
        for k in range(cw):
            dw_ref[k:k + 1, :] += jnp.sum(dwacc_ref[k], axis=0, keepdims=True)

        def chunk2(i, carry):
            r0 = pl.multiple_of(i * CONV_RC, CONV_RC)
            views = _shift_views(dcv_ref[pl.ds(r0, CONV_RC + CONV_PAD), :], CONV_RC, CONV_PAD)
            dglu = _conv_rows_t(views, w_ref, cw, CONV_RC)
            ga, gb = _glu_inputs(a_refs, b_refs, pl.ds(r0, CONV_RC))
            sgb = _sigmoid(gb)
            du_ref[pl.ds(r0, CONV_RC), 0:cc] = (dglu * sgb).astype(BF16)
            du_ref[pl.ds(r0, CONV_RC), cc:2 * cc] = (dglu * ga * sgb * (1.0 - sgb)).astype(BF16)
            return carry

        lax.fori_loop(0, s // CONV_RC, chunk2, 0)

    def colspec(j):
        return pl.BlockSpec((s, bw), lambda b: (b, j))

    vec = pl.BlockSpec((1, cc), lambda b: (0, 0))
    return pl.pallas_call(
        body, name=name, grid=(t // s,),
        in_specs=[colspec(j) for j in a_idx + b_idx]
        + [pl.BlockSpec((s, cc), lambda b: (b, cfg.qw // cc)), pl.BlockSpec((cw, cc), lambda b: (0, 0)), vec, vec, vec],
        out_specs=[pl.BlockSpec((s, 2 * cc), lambda b: (b, 0)), pl.BlockSpec((32, cc), lambda b: (0, 0)),
                   pl.BlockSpec((8, cc), lambda b: (0, 0))],
        out_shape=[jax.ShapeDtypeStruct((t, 2 * cc), BF16), jax.ShapeDtypeStruct((32, cc), F32),
                   jax.ShapeDtypeStruct((8, cc), F32)],
        scratch_shapes=[pltpu.VMEM((CONV_PAD + s, cc), F32), pltpu.VMEM((s + CONV_PAD, cc), F32),
                        pltpu.VMEM((cw, 8, cc), F32)],
        compiler_params=_cparams("arbitrary"),
    )(*([u] * (2 * nb)), dcat, cw_w, cb, lg, lb)


LRU_RC = 64
LRU_PAD = 8
SCAN_RC = 16
GELU_K = math.sqrt(2.0 / math.pi)


def _expm1_neg(z):
    return jnp.where(z > -0.05, z * (1.0 + z * (0.5 + z * (1.0 / 6.0 + z * (1.0 / 24.0)))), jnp.exp(z) - 1.0)


def _log_sigmoid(x):
    e = jnp.exp(-jnp.abs(x))
    log1p = jnp.where(e < 0.01, e * (1.0 - e * (0.5 - e * (1.0 / 3.0))), jnp.log(1.0 + e))
    return jnp.minimum(x, 0.0) - log1p


def _gelu(x):
    t = jnp.tanh(GELU_K * (x + 0.044715 * x * x * x))
    return 0.5 * x * (1.0 + t), t


def _gelu_grad(x, t):
    return 0.5 * (1.0 + t) + 0.5 * x * (1.0 - t * t) * GELU_K * (1.0 + 3.0 * 0.044715 * x * x)


def _lru_gates(xc, wa_ref, ba, wx_ref, bx, ls):
    nh = xc.shape[1] // 128
    xb = xc.astype(BF16)
    ra = jnp.concatenate([_nn(xb[:, 128 * h:128 * (h + 1)], wa_ref[h]) for h in range(nh)], axis=1) + ba
    ia = jnp.concatenate([_nn(xb[:, 128 * h:128 * (h + 1)], wx_ref[h]) for h in range(nh)], axis=1) + bx
    r = _sigmoid(ra)
    ig = _sigmoid(ia)
    log_a = RG_LRU_C * r * ls
    a = jnp.exp(log_a)
    mult = jnp.sqrt(-_expm1_neg(2.0 * log_a))
    return r, ig, a, mult


def _fill_padded(src_ref, dst_ref, s, ct):
    dst_ref[0:LRU_PAD, :] = jnp.zeros((LRU_PAD, ct), F32)

    def fill(i, carry):
        r0 = pl.multiple_of(i * GLU_RC, GLU_RC)
        dst_ref[pl.ds(LRU_PAD + r0, GLU_RC), :] = src_ref[pl.ds(r0, GLU_RC), :].astype(F32)
        return carry

    lax.fori_loop(0, s // GLU_RC, fill, 0)


def _lru_specs(cfg, ct):
    s, lw = cfg.s, cfg.lw
    nct = lw // ct
    nh = ct // 128
    act = [pl.BlockSpec((s, ct), lambda c, b: (b, c)), pl.BlockSpec((s, ct), lambda c, b: (b, nct + c))]
    vec = pl.BlockSpec((1, ct), lambda c, b: (0, c))
    gate_w = pl.BlockSpec((nh, 128, 128), lambda c, b: (c, 0, 0))
    params = [pl.BlockSpec((cfg.lcw, ct), lambda c, b: (0, c)), vec, gate_w, vec, gate_w, vec, vec]
    return nct, act, params


def lru_fwd(cfg, u, conv_w, conv_b, wa, ba, wx, bx, lam, name):
    t = u.shape[0]
    s, lw, lcw, ct = cfg.s, cfg.lw, cfg.lcw, cfg.ct_f
    nct, act, params = _lru_specs(cfg, ct)

    def body(gi_ref, ri_ref, cw_ref, cb_ref, wa_ref, ba_ref, wx_ref, bx_ref, lam_ref, y_ref, hs_ref,
             xp_ref, a_ref, b_ref):
        _fill_padded(ri_ref, xp_ref, s, ct)
        ls = _log_sigmoid(lam_ref[...])

        def chunk(i, carry):
            r0 = pl.multiple_of(i * LRU_RC, LRU_RC)
            views = _shift_views(xp_ref[pl.ds(r0, LRU_RC + LRU_PAD), :], LRU_RC, LRU_PAD)
            xc = _conv_rows(views, cw_ref, lcw, LRU_PAD, LRU_RC) + cb_ref[...]
            _, ig, a, mult = _lru_gates(xc, wa_ref, ba_ref[...], wx_ref, bx_ref[...], ls)
            a_ref[pl.ds(r0, LRU_RC), :] = a
            b_ref[pl.ds(r0, LRU_RC), :] = mult * (ig * xc)
            return carry

        lax.fori_loop(0, s // LRU_RC, chunk, 0)
        row = lax.broadcasted_iota(jnp.int32, (SCAN_RC, ct), 0)

        def scan(i, h_last):
            rows = pl.ds(pl.multiple_of(i * SCAN_RC, SCAN_RC), SCAN_RC)
            a = a_ref[rows, :]
            b = b_ref[rows, :]
            sft = 1
            while sft < SCAN_RC:
                a_sh = jnp.where(row >= sft, pltpu.roll(a, sft, 0), 1.0)
                b_sh = jnp.where(row >= sft, pltpu.roll(b, sft, 0), 0.0)
                b = a * b_sh + b
                a = a * a_sh
                sft *= 2
            h = a * h_last + b
            gate, _ = _gelu(gi_ref[rows, :].astype(F32))
            y_ref[rows, :] = (gate * h).astype(BF16)
            hs_ref[rows, :] = h.astype(BF16)
            return h[SCAN_RC - 1:SCAN_RC, :]

        lax.fori_loop(0, s // SCAN_RC, scan, jnp.zeros((1, ct), F32))

    out = pl.BlockSpec((s, ct), lambda c, b: (b, c))
    return pl.pallas_call(
        body, name=name, grid=(nct, t // s),
        in_specs=act + params,
        out_specs=[out, out],
        out_shape=[jax.ShapeDtypeStruct((t, lw), BF16), jax.ShapeDtypeStruct((t, lw), BF16)],
        scratch_shapes=[pltpu.VMEM((LRU_PAD + s, ct), F32), pltpu.VMEM((s, ct), F32), pltpu.VMEM((s, ct), F32)],
        compiler_params=_cparams("arbitrary", "arbitrary"),
    )(u, u, conv_w, conv_b, wa, ba, wx, bx, lam)


def lru_bwd(cfg, u, hs, dy, conv_w, conv_b, wa, ba, wx, bx, lam, name):
    t = u.shape[0]
    s, lw, lcw, ct = cfg.s, cfg.lw, cfg.lcw, cfg.ct_b
    nct, act, params = _lru_specs(cfg, ct)
    nh = ct // 128
    nscan = s // SCAN_RC
    assert lcw <= 8

    def body(gi_ref, ri_ref, hs_ref, dy_ref, cw_ref, cb_ref, wa_ref, ba_ref, wx_ref, bx_ref, lam_ref,
             dug_ref, dur_ref, dwa_ref, dwx_ref, dvec_ref, dcw_ref,
             xp_ref, hp_ref, a_ref, g_ref, dxc_ref, dwacc_ref):
        @pl.when(pl.program_id(1) == 0)
        def _():
            dwa_ref[...] = jnp.zeros_like(dwa_ref)
            dwx_ref[...] = jnp.zeros_like(dwx_ref)
            dvec_ref[...] = jnp.zeros_like(dvec_ref)
            dcw_ref[...] = jnp.zeros_like(dcw_ref)

        _fill_padded(ri_ref, xp_ref, s, ct)
        _fill_padded(hs_ref, hp_ref, s, ct)
        dxc_ref[s:s + LRU_PAD, :] = jnp.zeros((LRU_PAD, ct), F32)
        dwacc_ref[...] = jnp.zeros_like(dwacc_ref)
        lam = lam_ref[...]
        ls = _log_sigmoid(lam)

        def gates_at(r0):
            views = _shift_views(xp_ref[pl.ds(r0, LRU_RC + LRU_PAD), :], LRU_RC, LRU_PAD)
            xc = _conv_rows(views, cw_ref, lcw, LRU_PAD, LRU_RC) + cb_ref[...]
            return views, xc, _lru_gates(xc, wa_ref, ba_ref[...], wx_ref, bx_ref[...], ls)

        def chunk(i, carry):
            r0 = pl.multiple_of(i * LRU_RC, LRU_RC)
            rows = pl.ds(r0, LRU_RC)
            _, _, (_, _, a, _) = gates_at(r0)
            a_ref[rows, :] = a
            x = gi_ref[rows, :].astype(F32)
            gate, th = _gelu(x)
            dyv = dy_ref[rows, :].astype(F32)
            g_ref[rows, :] = dyv * gate
            dug_ref[rows, :] = (dyv * hp_ref[pl.ds(LRU_PAD + r0, LRU_RC), :] * _gelu_grad(x, th)).astype(BF16)
            return carry

        lax.fori_loop(0, s // LRU_RC, chunk, 0)
        row = lax.broadcasted_iota(jnp.int32, (SCAN_RC, ct), 0)

        def scan(ii, carry):
            g_next, a_next = carry
            rows = pl.ds(pl.multiple_of((nscan - 1 - ii) * SCAN_RC, SCAN_RC), SCAN_RC)
            a = a_ref[rows, :]
            d = g_ref[rows, :]
            c = jnp.where(row < SCAN_RC - 1, pltpu.roll(a, SCAN_RC - 1, 0), a_next)
            sft = 1
            while sft < SCAN_RC:
                c_sh = jnp.where(row < SCAN_RC - sft, pltpu.roll(c, SCAN_RC - sft, 0), 1.0)
                d_sh = jnp.where(row < SCAN_RC - sft, pltpu.roll(d, SCAN_RC - sft, 0), 0.0)
                d = d + c * d_sh
                c = c * c_sh
                sft *= 2
            g = d + c * g_next
            g_ref[rows, :] = g
            return g[0:1, :], a[0:1, :]

        lax.fori_loop(0, nscan, scan, (jnp.zeros((1, ct), F32), jnp.zeros((1, ct), F32)))

        def chunk3(i, carry):
            r0 = pl.multiple_of(i * LRU_RC, LRU_RC)
            rows = pl.ds(r0, LRU_RC)
            views, xc, (r, ig, a, mult) = gates_at(r0)
            g = g_ref[rows, :]
            h_prev = hp_ref[pl.ds(r0, LRU_RC + LRU_PAD), :][LRU_PAD - 1:LRU_PAD - 1 + LRU_RC]
            dix = g * mult
            di = dix * xc
            dxc = dix * ig
            da = g * h_prev - (g * ig * xc) * a / mult
            dlog_a = da * a
            dr = dlog_a * (RG_LRU_C * ls)
            dra = dr * r * (1.0 - r)
            dia = di * ig * (1.0 - ig)
            xb, drab, diab = xc.astype(BF16), dra.astype(BF16), dia.astype(BF16)
            dxg = []
            for h in range(nh):
                cols = slice(128 * h, 128 * (h + 1))
                dxg.append(_nt(drab[:, cols], wa_ref[h]) + _nt(diab[:, cols], wx_ref[h]))
                dwa_ref[h] += _tn(xb[:, cols], drab[:, cols])
                dwx_ref[h] += _tn(xb[:, cols], diab[:, cols])
            dxc = dxc + jnp.concatenate(dxg, axis=1)
            dvec_ref[0:1, :] += jnp.sum(dra, axis=0, keepdims=True)
            dvec_ref[1:2, :] += jnp.sum(dia, axis=0, keepdims=True)
            dvec_ref[2:3, :] += jnp.sum(dlog_a * r, axis=0, keepdims=True) * (RG_LRU_C * _sigmoid(-lam))
            dvec_ref[3:4, :] += jnp.sum(dxc, axis=0, keepdims=True)
            dxc_ref[rows, :] = dxc
            _conv_wgrad(views, dxc, dwacc_ref, lcw, LRU_PAD, LRU_RC)
            return carry

        lax.fori_loop(0, s // LRU_RC, chunk3, 0)
        for k in range(lcw):
            dcw_ref[k:k + 1, :] += jnp.sum(dwacc_ref[k], axis=0, keepdims=True)

        def chunk4(i, carry):
            r0 = pl.multiple_of(i * LRU_RC, LRU_RC)
            views = _shift_views(dxc_ref[pl.ds(r0, LRU_RC + LRU_PAD), :], LRU_RC, LRU_PAD)
            dur_ref[pl.ds(r0, LRU_RC), :] = _conv_rows_t(views, cw_ref, lcw, LRU_RC).astype(BF16)
            return carry

        lax.fori_loop(0, s // LRU_RC, chunk4, 0)

    blk = pl.BlockSpec((s, ct), lambda c, b: (b, c))
    acc8 = pl.BlockSpec((8, ct), lambda c, b: (0, c))
    gate_w = pl.BlockSpec((nh, 128, 128), lambda c, b: (c, 0, 0))
    return pl.pallas_call(
        body, name=name, grid=(nct, t // s),
        in_specs=act + [blk, blk] + params,
        out_specs=[blk, blk, gate_w, gate_w, acc8, acc8],
        out_shape=[jax.ShapeDtypeStruct((t, lw), BF16), jax.ShapeDtypeStruct((t, lw), BF16),
                   jax.ShapeDtypeStruct((cfg.lh, 128, 128), F32), jax.ShapeDtypeStruct((cfg.lh, 128, 128), F32),
                   jax.ShapeDtypeStruct((8, lw), F32), jax.ShapeDtypeStruct((8, lw), F32)],
        scratch_shapes=[pltpu.VMEM((LRU_PAD + s, ct), F32), pltpu.VMEM((LRU_PAD + s, ct), F32),
                        pltpu.VMEM((s, ct), F32), pltpu.VMEM((s, ct), F32), pltpu.VMEM((s + LRU_PAD, ct), F32),
                        pltpu.VMEM((lcw, 8, ct), F32)],
        compiler_params=_cparams("arbitrary", "arbitrary"),
    )(u, u, hs, dy, conv_w, conv_b, wa, ba, wx, bx, lam)


def local_step(cfg, x, tgt, p, shards=None):
    buckets = bucket_table(cfg)
    bias = bias_build(cfg, p["rel_bias"], buckets, "bias_build")
    ga_w, gx_w = p["gate_a_w"].astype(BF16), p["gate_x_w"].astype(BF16)
    wf = dict(p["wf"])
    dist = shards is not None

    def ffn_forward(l, k, h, nw):
        riders = gather_riders([shards[(l + 1, k)]]) if dist and l + 1 < cfg.depth else None
        outs = ffn_fwd(cfg, h, nw, wf[(l, k)], 0, 2, f"ffn{k + 1}_fwd_{l}", riders=riders)
        if riders is not None:
            wf[(l + 1, k)] = outs[4].reshape(3, -1, cfg.d)
        return outs[:4]

    def blocks(g):
        g = g if g.ndim == 3 else g[None]
        return g.reshape(g.shape[0], N_DEV, g.shape[1] // N_DEV, g.shape[2])

    h = x
    saved = []
    for l in range(cfg.depth):
        i = l // 2
        s = {"h0": h}
        s["h1"], s["xn1"], s["g1"], s["u1"] = ffn_forward(l, 0, h, p["norm_ffn1"][l][None])
        if l % 2 == 0:
            s["um"], s["xnm"] = norm_proj(cfg, s["h1"], p["norm_mix"][l][None], p["even_in"], f"mix_in_{l}", wi=i)
            attn = attn_fwd(cfg, s["um"], bias, p["attn_sinks"][i], f"attn_fwd_{l}")
            c = conv_fwd(cfg, s["um"], p["conv_b_w"][i], p["conv_b_b"][i][None], p["conv_ln_g"][i][None],
                         p["conv_ln_b"][i][None], f"conv_fwd_{l}")
            s["parts"] = [attn, c]
            s["h2"] = proj_residual(cfg, s["h1"], s["parts"], p["even_out"], f"mix_out_{l}", wi=i)
        else:
            s["um"], s["xnm"] = norm_proj(cfg, s["h1"], p["norm_mix"][l][None], p["odd_in"], f"mix_in_{l}", wi=i)
            y, s["hs"] = lru_fwd(cfg, s["um"], p["lru_conv_w"][i], p["lru_conv_b"][i][None], ga_w[i], p["gate_a_b"][i][None],
                                 gx_w[i], p["gate_x_b"][i][None], p["lru_lambda"][i][None], f"lru_fwd_{l}")
            s["parts"] = [y]
            s["h2"] = proj_residual(cfg, s["h1"], s["parts"], p["odd_out"], f"mix_out_{l}", wi=i)
        h, s["xn2"], s["g2"], s["u2"] = ffn_forward(l, 1, s["h2"], p["norm_ffn2"][l][None])
        saved.append(s)

    loss, dh, dnf = loss_head(cfg, h, p["norm_final"][None], tgt, "loss_head")
    big = [None] * cfg.depth
    sm = {k: [None] * cfg.depth for k in ("norm_ffn1", "norm_mix", "norm_ffn2")}
    ne, no = (cfg.depth + 1) // 2, cfg.depth // 2
    for k in ("attn_sinks", "conv_b_w", "conv_b_b", "conv_ln_g", "conv_ln_b", "dbias"):
        sm[k] = [None] * ne
    for k in ("lru_conv_w", "lru_conv_b", "gate_a_w", "gate_a_b", "gate_x_w", "gate_x_b", "lru_lambda"):
        sm[k] = [None] * no
    pending = None
    for l in reversed(range(cfg.depth)):
        i = l // 2
        s = saved[l]
        riders = scatter_riders([pending]) if pending is not None else None
        outs = ffn_bwd_x(cfg, dh, s["h2"], p["norm_ffn2"][l][None], s["g2"], s["u2"], wf[(l, 1)], 0, 2,
                         f"ffn2_bwd_x_{l}", riders=riders)
        dh, dout, dg, du, dn = outs[:5]
        if riders is not None:
            big[l + 1]["f1"] = (pending, outs[5])
        sm["norm_ffn2"][l] = dn[0]
        gf2 = blocks(ffn_bwd_w(cfg, s["xn2"], dout, s["g2"], s["u2"], dg, du, f"ffn2_bwd_w_{l}")[0])
        w_out = p["even_out"] if l % 2 == 0 else p["odd_out"]
        w_in = p["even_in"] if l % 2 == 0 else p["odd_in"]
        dcat = proj_bwd_act(cfg, dh, w_out, f"mix_out_bwd_{l}", wi=i)
        g_out = blocks(grad_weight(cfg, s["parts"], dh, f"mix_out_gw_{l}"))
        if l % 2 == 0:
            du_a, sm["dbias"][i], dsink = attn_bwd(cfg, s["um"], dcat, bias, p["attn_sinks"][i], f"attn_bwd_{l}")
            du_c, dcw, dvec = conv_bwd(cfg, s["um"], dcat, p["conv_b_w"][i], p["conv_b_b"][i][None],
                                       p["conv_ln_g"][i][None], p["conv_ln_b"][i][None], f"conv_bwd_{l}")
            sm["attn_sinks"][i] = dsink[:cfg.hq, 0]
            sm["conv_b_w"][i] = dcw[:cfg.cw]
            sm["conv_b_b"][i], sm["conv_ln_g"][i], sm["conv_ln_b"][i] = dvec[0], dvec[1], dvec[2]
            dparts = [du_a, du_c]
        else:
            dug, dur, dwa, dwx, dvec, dcw = lru_bwd(
                cfg, s["um"], s["hs"], dcat, p["lru_conv_w"][i], p["lru_conv_b"][i][None], ga_w[i], p["gate_a_b"][i][None],
                gx_w[i], p["gate_x_b"][i][None], p["lru_lambda"][i][None], f"lru_bwd_{l}")
            sm["gate_a_w"][i], sm["gate_x_w"][i] = dwa, dwx
            sm["gate_a_b"][i], sm["gate_x_b"][i], sm["lru_lambda"][i], sm["lru_conv_b"][i] = dvec[0], dvec[1], dvec[2], dvec[3]
            sm["lru_conv_w"][i] = dcw[:cfg.lcw]
            dparts = [dug, dur]
        g_in = blocks(grad_weight(cfg, dparts, s["xnm"], f"mix_in_gw_{l}"))
        dh, dn = norm_proj_bwd(cfg, dh, s["h1"], p["norm_mix"][l][None], dparts, w_in, f"mix_in_bwd_{l}", wi=i)
        sm["norm_mix"][l] = dn[0]
        riders = scatter_riders([gf2]) if dist else None
        outs = ffn_bwd_x(cfg, dh, s["h0"], p["norm_ffn1"][l][None], s["g1"], s["u1"], wf[(l, 0)], 0, 2,
                         f"ffn1_bwd_x_{l}", riders=riders)
        dh, dout, dg, du, dn = outs[:5]
        sm["norm_ffn1"][l] = dn[0]
        riders = scatter_riders([g_in, g_out]) if dist else None
        gouts = ffn_bwd_w(cfg, s["xn1"], dout, s["g1"], s["u1"], dg, du, f"ffn1_bwd_w_{l}", riders=riders)
        gf1 = blocks(gouts[0])
        big[l] = {"f1": (gf1, None), "f2": (gf2, outs[5] if dist else None),
                  "in": (g_in, gouts[1] if dist else None), "out": (g_out, gouts[2] if dist else None)}
        pending = gf1 if dist and l > 0 else None
    drb = bias_grad(cfg, jnp.stack(sm.pop("dbias")), buckets, "bias_grad")
    small = {k: jnp.stack(v) for k, v in sm.items()}
    small["rel_bias"] = drb[:, :, 0].T
    small["norm_final"] = dnf[0]
    return loss, dh, big, small


MESH = pl.DeviceIdType.MESH
ANY = pl.BlockSpec(memory_space=pl.ANY)


def _place():
    return lax.axis_index("x"), lax.axis_index("y"), lax.axis_index("c")


FLIPS = ((0, 0, 1), (0, 1, 0), (0, 1, 1), (1, 0, 0), (1, 0, 1), (1, 1, 0), (1, 1, 1))


def _peer(place, flip):
    return tuple(1 - v if f else v for v, f in zip(place, flip))


def _dev_index(place):
    return 4 * place[0] + 2 * place[1] + place[2]


def _remote(src, dst, send_sems, recv_sems, g, k, peer):
    return pltpu.make_async_remote_copy(src_ref=src, dst_ref=dst, send_sem=send_sems.at[g, k], recv_sem=recv_sems.at[g, k],
                                        device_id=peer, device_id_type=MESH)


def gather_riders(srcs):
    ng = len(srcs)

    def copies(in_refs, out_refs, sems):
        send_sems, recv_sems, local_sems = sems
        me = _place()
        local, sends, recvs = [], [], []
        for g in range(ng):
            mine = out_refs[g].at[:, _dev_index(me)]
            local.append(pltpu.make_async_copy(in_refs[g], mine, local_sems.at[g]))
            for k, flip in enumerate(FLIPS):
                peer = _peer(me, flip)
                sends.append(_remote(in_refs[g], mine, send_sems, recv_sems, g, k, peer))
                recvs.append(_remote(in_refs[g], out_refs[g].at[:, _dev_index(peer)], send_sems, recv_sems, g, k, peer))
        return local, sends, recvs

    def start(in_refs, out_refs, sems):
        local, sends, _ = copies(in_refs, out_refs, sems)
        for cp in local + sends:
            cp.start()

    def wait(in_refs, out_refs, sems):
        local, sends, recvs = copies(in_refs, out_refs, sems)
        for cp in sends:
            cp.wait_send()
        for cp in recvs:
            cp.wait_recv()
        for cp in local:
            cp.wait()

    return Riders(tuple(srcs), tuple(jax.ShapeDtypeStruct((s.shape[0], N_DEV) + s.shape[1:], s.dtype) for s in srcs),
                  (pltpu.SemaphoreType.DMA((ng, 7)), pltpu.SemaphoreType.DMA((ng, 7)), pltpu.SemaphoreType.DMA((ng,))),
                  start, wait)


def scatter_riders(bufs):
    ng = len(bufs)

    def copies(in_refs, out_refs, sems):
        send_sems, recv_sems = sems
        me = _place()
        return [_remote(in_refs[g].at[:, _dev_index(_peer(me, flip))], out_refs[g].at[:, k], send_sems, recv_sems, g, k,
                        _peer(me, flip)) for g in range(ng) for k, flip in enumerate(FLIPS)]

    def start(in_refs, out_refs, sems):
        for cp in copies(in_refs, out_refs, sems):
            cp.start()

    def wait(in_refs, out_refs, sems):
        for cp in copies(in_refs, out_refs, sems):
            cp.wait()

    return Riders(tuple(bufs), tuple(jax.ShapeDtypeStruct((b.shape[0], 7) + b.shape[2:], b.dtype) for b in bufs),
                  (pltpu.SemaphoreType.DMA((ng, 7)), pltpu.SemaphoreType.DMA((ng, 7))), start, wait)


def shard_sum(buf, recv, dev, name):
    n, _, r, cdim = buf.shape

    def body(dev_ref, a_ref, b_ref, o_ref):
        acc = a_ref[...].astype(F32)
        for k in range(7):
            acc = acc + b_ref[k].astype(F32)
        o_ref[...] = acc

    return pl.pallas_call(
        body, name=name,
        grid_spec=pltpu.PrefetchScalarGridSpec(
            num_scalar_prefetch=1, grid=(n,),
            in_specs=[pl.BlockSpec((None, None, r, cdim), lambda i, dev_ref: (i, dev_ref[0], 0, 0)),
                      pl.BlockSpec((None, 7, r, cdim), lambda i, dev_ref: (i, 0, 0, 0))],
            out_specs=pl.BlockSpec((None, r, cdim), lambda i, dev_ref: (i, 0, 0))),
        out_shape=jax.ShapeDtypeStruct((n, r, cdim), F32),
    )(dev, buf, recv)


def all_gather(srcs, name):
    ng = len(srcs)

    def body(*refs):
        x_refs, o_refs = refs[:ng], refs[ng:2 * ng]
        send_sems, recv_sems, local_sems = refs[2 * ng:]
        x, y, c = _place()
        me, sibling = (x, y, c), (x, y, 1 - c)
        chips = [(1 - x, y), (x, 1 - y), (1 - x, 1 - y)]

        def copy(gi, k, block, to, src=None):
            dst = o_refs[gi].at[:, 4 * block[0] + 2 * block[1] + block[2]]
            return pltpu.make_async_remote_copy(
                src_ref=dst if src is None else src, dst_ref=dst, send_sem=send_sems.at[gi, k],
                recv_sem=recv_sems.at[gi, k], device_id=to, device_id_type=MESH)

        mine = [pltpu.make_async_copy(x_refs[gi], o_refs[gi].at[:, 4 * x + 2 * y + c], local_sems.at[gi])
                for gi in range(ng)]
        for cp in mine:
            cp.start()
        first = []
        for gi in range(ng):
            first.append(copy(gi, 0, me, sibling, src=x_refs[gi]))
            first += [copy(gi, 1 + j, me, (*chip, c), src=x_refs[gi]) for j, chip in enumerate(chips)]
        for cp in first:
            cp.start()
        passed = []
        for j, chip in enumerate(chips):
            for gi in range(ng):
                copy(gi, 1 + j, (*chip, c), me).wait_recv()
                cp = copy(gi, 4 + j, (*chip, c), sibling)
                cp.start()
                passed.append(cp)
        for gi in range(ng):
            copy(gi, 0, sibling, me).wait_recv()
            for j, chip in enumerate(chips):
                copy(gi, 4 + j, (*chip, 1 - c), me).wait_recv()
        for cp in first + passed:
            cp.wait_send()
        for cp in mine:
            cp.wait()

    return pl.pallas_call(
        body, name=name,
        in_specs=[ANY] * ng, out_specs=[ANY] * ng,
        out_shape=[jax.ShapeDtypeStruct((s.shape[0], N_DEV) + s.shape[1:], s.dtype) for s in srcs],
        scratch_shapes=[pltpu.SemaphoreType.DMA((ng, 7)), pltpu.SemaphoreType.DMA((ng, 7)),
                        pltpu.SemaphoreType.DMA((ng,))],
    )(*srcs)


def pair_exchange(bufs, name):
    ng = len(bufs)

    def body(*refs):
        b_refs, o_refs = refs[:ng], refs[ng:2 * ng]
        send_sems, recv_sems = refs[2 * ng:]
        x, y, c = _place()
        copies = [pltpu.make_async_remote_copy(
            src_ref=b_refs[gi].at[:, :, 1 - c], dst_ref=o_refs[gi], send_sem=send_sems.at[gi], recv_sem=recv_sems.at[gi],
            device_id=(x, y, 1 - c), device_id_type=MESH) for gi in range(ng)]
        for cp in copies:
            cp.start()
        for cp in copies:
            cp.wait()

    return pl.pallas_call(
        body, name=name,
        in_specs=[ANY] * ng, out_specs=[ANY] * ng,
        out_shape=[jax.ShapeDtypeStruct(b.shape[:2] + b.shape[3:], b.dtype) for b in bufs],
        scratch_shapes=[pltpu.SemaphoreType.DMA((ng,)), pltpu.SemaphoreType.DMA((ng,))],
    )(*bufs)


def chip_exchange(qs, name):
    ng = len(qs)

    def body(*refs):
        q_refs, o_refs = refs[:ng], refs[ng:2 * ng]
        send_sems, recv_sems = refs[2 * ng:]
        x, y, c = _place()
        chips = [(1 - x, y), (x, 1 - y), (1 - x, 1 - y)]
        copies = [pltpu.make_async_remote_copy(
            src_ref=q_refs[gi].at[:, 2 * chip[0] + chip[1]], dst_ref=o_refs[gi].at[:, j],
            send_sem=send_sems.at[gi, j], recv_sem=recv_sems.at[gi, j],
            device_id=(*chip, c), device_id_type=MESH) for gi in range(ng) for j, chip in enumerate(chips)]
        for cp in copies:
            cp.start()
        for cp in copies:
            cp.wait()

    return pl.pallas_call(
        body, name=name,
        in_specs=[ANY] * ng, out_specs=[ANY] * ng,
        out_shape=[jax.ShapeDtypeStruct((q.shape[0], 3) + q.shape[2:], q.dtype) for q in qs],
        scratch_shapes=[pltpu.SemaphoreType.DMA((ng, 3)), pltpu.SemaphoreType.DMA((ng, 3))],
    )(*qs)


def pair_sum(buf, recv, core, name):
    n, _, _, r, cdim = buf.shape

    def body(core_ref, a_ref, b_ref, o_ref):
        o_ref[...] = (a_ref[...].astype(F32) + b_ref[...].astype(F32)).astype(BF16)

    blk = pl.BlockSpec((None, None, r, cdim), lambda i, k, core_ref: (i, k, 0, 0))
    return pl.pallas_call(
        body, name=name,
        grid_spec=pltpu.PrefetchScalarGridSpec(
            num_scalar_prefetch=1, grid=(n, 4),
            in_specs=[pl.BlockSpec((None, None, None, r, cdim), lambda i, k, core_ref: (i, k, core_ref[0], 0, 0)), blk],
            out_specs=blk),
        out_shape=jax.ShapeDtypeStruct((n, 4, r, cdim), BF16),
    )(core, buf, recv)


def chip_sum(q, recv, chip, name):
    n, _, r, cdim = q.shape

    def body(chip_ref, a_ref, b_ref, o_ref):
        acc = a_ref[...].astype(F32)
        for j in range(3):
            acc = acc + b_ref[j].astype(F32)
        o_ref[...] = acc

    return pl.pallas_call(
        body, name=name,
        grid_spec=pltpu.PrefetchScalarGridSpec(
            num_scalar_prefetch=1, grid=(n,),
            in_specs=[pl.BlockSpec((None, None, r, cdim), lambda i, chip_ref: (i, chip_ref[0], 0, 0)),
                      pl.BlockSpec((None, 3, r, cdim), lambda i, chip_ref: (i, 0, 0, 0))],
            out_specs=pl.BlockSpec((None, r, cdim), lambda i, chip_ref: (i, 0, 0))),
        out_shape=jax.ShapeDtypeStruct((n, r, cdim), F32),
    )(chip, q, recv)


def sum_blocks(a, name):
    def body(a_ref, o_ref):
        acc = a_ref[0]
        for d in range(1, a.shape[0]):
            acc = acc + a_ref[d]
        o_ref[...] = acc

    return pl.pallas_call(body, name=name, out_shape=jax.ShapeDtypeStruct(a.shape[1:], F32),
                          compiler_params=pltpu.CompilerParams(vmem_limit_bytes=VMEM_LIMIT))(a)


def adamw(w, g, m, v, name):
    c1 = 1.0 / (1.0 - ADAM_B1 ** ADAM_STEP)
    c2 = 1.0 / (1.0 - ADAM_B2 ** ADAM_STEP)

    def body(w_ref, g_ref, m_ref, v_ref, d_ref, mo_ref, vo_ref):
        gg = g_ref[...]
        m2 = ADAM_B1 * m_ref[...] + (1.0 - ADAM_B1) * gg
        v2 = ADAM_B2 * v_ref[...] + (1.0 - ADAM_B2) * (gg * gg)
        mo_ref[...] = m2
        vo_ref[...] = v2
        d_ref[...] = -ADAM_LR * ((m2 * c1) / (jnp.sqrt(v2 * c2) + ADAM_EPS) + ADAM_WD * w_ref[...])

    out_shape = [jax.ShapeDtypeStruct(w.shape, F32)] * 3
    if w.ndim == 2:
        return pl.pallas_call(body, name=name, out_shape=out_shape,
                              compiler_params=pltpu.CompilerParams(vmem_limit_bytes=VMEM_LIMIT))(w, g, m, v)
    blk = pl.BlockSpec((None,) + w.shape[1:], lambda i: (i, 0, 0))
    return pl.pallas_call(body, name=name, grid=(w.shape[0],), in_specs=[blk] * 4, out_specs=[blk] * 3,
                          out_shape=out_shape, compiler_params=_cparams("arbitrary"))(w, g, m, v)


WEIGHTS = ("norm_ffn1", "ffn1_wg", "ffn1_wu", "ffn1_wd", "norm_mix", "norm_ffn2", "ffn2_wg", "ffn2_wu", "ffn2_wd",
           "rel_bias", "even_w_in", "attn_sinks", "conv_b_w", "conv_b_b", "conv_ln_g", "conv_ln_b", "even_w_out",
           "odd_w_in", "lru_conv_w", "lru_conv_b", "gate_a_w", "gate_a_b", "gate_x_w", "gate_x_b", "lru_lambda",
           "odd_w_out", "norm_final")
BIG = ("ffn1_wg", "ffn1_wu", "ffn1_wd", "ffn2_wg", "ffn2_wu", "ffn2_wd", "even_w_in", "even_w_out", "odd_w_in", "odd_w_out")
SMALL = tuple(n for n in WEIGHTS if n not in BIG)
SMALL_SHARDED = ("conv_b_w", "lru_conv_w", "lru_conv_b", "gate_a_b", "gate_x_b", "lru_lambda")
PACK_ALIGN = 1024


def _pack(arrays):
    parts = []
    for a in arrays:
        flat = a.reshape(-1)
        parts.append(jnp.pad(flat, (0, -flat.shape[0] % PACK_ALIGN)))
    return jnp.concatenate(parts).reshape(-1, 128)


def _unpack(packed, shapes, lead=()):
    flat = packed.reshape(lead + (-1,))
    out, off = [], 0
    for shp in shapes:
        size = math.prod(shp)
        out.append(flat[..., off:off + size].reshape(lead + tuple(shp)))
        off += size + (-size % PACK_ALIGN)
    return out


def _unshard_last(blocks):
    nd = blocks.ndim
    moved = jnp.moveaxis(blocks, 0, nd - 2)
    return moved.reshape(moved.shape[:-2] + (-1,))


def _step(cfg, x, weights, loss_target, ms, vs):
    w = dict(zip(WEIGHTS, weights))
    m = dict(zip(WEIGHTS, ms))
    v = dict(zip(WEIGHTS, vs))
    px, py, pc = _place()
    dev = 4 * px + 2 * py + pc
    core = jnp.reshape(pc, (1,)).astype(jnp.int32)
    chip = jnp.reshape(2 * px + py, (1,)).astype(jnp.int32)
    d = cfg.d
    t = cfg.bl * cfg.s

    def rows(name):
        a = w[name]
        return (a if name.endswith(("wd", "w_out")) else a.transpose(0, 2, 1)).astype(BF16)

    r3 = {n: rows(n) for n in BIG[:6]}
    shards = {(l, k): jnp.stack([r3[f"ffn{k + 1}_{mat}"][l] for mat in ("wg", "wu", "wd")])
              for l in range(cfg.depth) for k in range(2)}
    small_src = _pack([w[n] for n in SMALL_SHARDED])[None]
    gathered = all_gather([shards.pop((0, 0)), shards.pop((0, 1)), rows("even_w_in"), rows("even_w_out"), rows("odd_w_in"),
                           rows("odd_w_out"), small_src], "all_gather_weights")
    full = [g.reshape(g.shape[0], -1, g.shape[-1]) for g in gathered[:6]]
    p = {n: w[n] for n in SMALL if n not in SMALL_SHARDED}
    p.update(wf={(0, 0): full[0], (0, 1): full[1]}, even_in=full[2], even_out=full[3], odd_in=full[4], odd_out=full[5])
    for n, blocks in zip(SMALL_SHARDED, _unpack(gathered[6][0], [w[n].shape for n in SMALL_SHARDED], lead=(N_DEV,))):
        p[n] = _unshard_last(blocks)

    lossp, gx, big, small = local_step(cfg, x.reshape(t, d), loss_target.reshape(t, d), p, shards)
    loss = lax.psum(lossp[0, 0], ("x", "y", "c"))

    dev1 = jnp.reshape(dev, (1,)).astype(jnp.int32)
    shard_rows = [{} for _ in range(cfg.depth)]
    for l in range(cfg.depth):
        for key, (buf, recv) in big[l].items():
            if recv is not None:
                shard_rows[l][key] = shard_sum(buf, recv, dev1, f"rs_sum_{key}_{l}")
    left = [(l, key, buf) for l in range(cfg.depth) for key, (buf, recv) in big[l].items() if recv is None]
    bufs = [buf.reshape(buf.shape[0], 4, 2, buf.shape[2], d) for _, _, buf in left]
    recv = pair_exchange(bufs, "rs_pair")
    qs = [pair_sum(b, r, core, f"rs_pair_sum{j}") for j, (b, r) in enumerate(zip(bufs, recv))]
    recv = chip_exchange(qs, "rs_chip")
    for j, ((l, key, _), q, r) in enumerate(zip(left, qs, recv)):
        shard_rows[l][key] = chip_sum(q, r, chip, f"rs_chip_sum{j}")

    g_rows = {}
    for k in range(2):
        ffn_g = jnp.stack([shard_rows[l][f"f{k + 1}"] for l in range(cfg.depth)])
        for j, mat in enumerate(("wg", "wu", "wd")):
            g_rows[f"ffn{k + 1}_{mat}"] = ffn_g[:, j]
    g_rows["even_w_in"] = jnp.stack([shard_rows[l]["in"][0] for l in range(0, cfg.depth, 2)])
    g_rows["even_w_out"] = jnp.stack([shard_rows[l]["out"][0] for l in range(0, cfg.depth, 2)])
    g_rows["odd_w_in"] = jnp.stack([shard_rows[l]["in"][0] for l in range(1, cfg.depth, 2)])
    g_rows["odd_w_out"] = jnp.stack([shard_rows[l]["out"][0] for l in range(1, cfg.depth, 2)])
    grads = {}

    full_shapes = [small[n].shape for n in SMALL]
    parts = all_gather([_pack([small[n] for n in SMALL])[None]], "all_gather_small_grads")[0][0]
    for n, g in zip(SMALL, _unpack(sum_blocks(parts, "sum_small_grads"), full_shapes)):
        if n in SMALL_SHARDED:
            width = w[n].shape[-1]
            g = lax.dynamic_slice_in_dim(g, dev * width, width, axis=g.ndim - 1)
        grads[n] = g

    delta, new_m, new_v = {}, {}, {}
    for n in BIG:
        if n.endswith(("wd", "w_out")):
            grads[n] = g_rows[n]
            delta[n], new_m[n], new_v[n] = adamw(w[n], grads[n], m[n], v[n], f"adamw_{n}")
        elif w[n].shape[-1] % 128 == 0:
            grads[n] = g_rows[n].transpose(0, 2, 1)
            delta[n], new_m[n], new_v[n] = adamw(w[n], grads[n], m[n], v[n], f"adamw_{n}")
        else:
            outs = adamw(w[n].transpose(0, 2, 1), g_rows[n], m[n].transpose(0, 2, 1), v[n].transpose(0, 2, 1), f"adamw_{n}")
            delta[n], new_m[n], new_v[n] = [o.transpose(0, 2, 1) for o in outs]
            grads[n] = g_rows[n].transpose(0, 2, 1)
    shapes = [w[n].shape for n in SMALL]
    packed = adamw(*[_pack([src[n] for n in SMALL]) for src in (w, grads, m, v)], "adamw_small")
    for out, pk in zip((delta, new_m, new_v), packed):
        out.update(zip(SMALL, _unpack(pk, shapes)))

    return (loss, gx.reshape(x.shape), *[grads[n] for n in WEIGHTS], *[delta[n] for n in WEIGHTS],
            *[new_m[n] for n in WEIGHTS], *[new_v[n] for n in WEIGHTS])


def kernel(x, norm_ffn1, ffn1_wg, ffn1_wu, ffn1_wd, norm_mix, norm_ffn2, ffn2_wg, ffn2_wu, ffn2_wd, rel_bias, even_w_in, attn_sinks, conv_b_w, conv_b_b, conv_ln_g, conv_ln_b, even_w_out, odd_w_in, lru_conv_w, lru_conv_b, gate_a_w, gate_a_b, gate_x_w, gate_x_b, lru_lambda, odd_w_out, norm_final, loss_target, m_norm_ffn1, m_ffn1_wg, m_ffn1_wu, m_ffn1_wd, m_norm_mix, m_norm_ffn2, m_ffn2_wg, m_ffn2_wu, m_ffn2_wd, m_rel_bias, m_even_w_in, m_attn_sinks, m_conv_b_w, m_conv_b_b, m_conv_ln_g, m_conv_ln_b, m_even_w_out, m_odd_w_in, m_lru_conv_w, m_lru_conv_b, m_gate_a_w, m_gate_a_b, m_gate_x_w, m_gate_x_b, m_lru_lambda, m_odd_w_out, m_norm_final, v_norm_ffn1, v_ffn1_wg, v_ffn1_wu, v_ffn1_wd, v_norm_mix, v_norm_ffn2, v_ffn2_wg, v_ffn2_wu, v_ffn2_wd, v_rel_bias, v_even_w_in, v_attn_sinks, v_conv_b_w, v_conv_b_b, v_conv_ln_g, v_conv_ln_b, v_even_w_out, v_odd_w_in, v_lru_conv_w, v_lru_conv_b, v_gate_a_w, v_gate_a_b, v_gate_x_w, v_gate_x_b, v_lru_lambda, v_odd_w_out, v_norm_final):
    weights = (norm_ffn1, ffn1_wg, ffn1_wu, ffn1_wd, norm_mix, norm_ffn2, ffn2_wg, ffn2_wu, ffn2_wd, rel_bias, even_w_in, attn_sinks, conv_b_w, conv_b_b, conv_ln_g, conv_ln_b, even_w_out, odd_w_in, lru_conv_w, lru_conv_b, gate_a_w, gate_a_b, gate_x_w, gate_x_b, lru_lambda, odd_w_out, norm_final)
    ms = (m_norm_ffn1, m_ffn1_wg, m_ffn1_wu, m_ffn1_wd, m_norm_mix, m_norm_ffn2, m_ffn2_wg, m_ffn2_wu, m_ffn2_wd, m_rel_bias, m_even_w_in, m_attn_sinks, m_conv_b_w, m_conv_b_b, m_conv_ln_g, m_conv_ln_b, m_even_w_out, m_odd_w_in, m_lru_conv_w, m_lru_conv_b, m_gate_a_w, m_gate_a_b, m_gate_x_w, m_gate_x_b, m_lru_lambda, m_odd_w_out, m_norm_final)
    vs = (v_norm_ffn1, v_ffn1_wg, v_ffn1_wu, v_ffn1_wd, v_norm_mix, v_norm_ffn2, v_ffn2_wg, v_ffn2_wu, v_ffn2_wd, v_rel_bias, v_even_w_in, v_attn_sinks, v_conv_b_w, v_conv_b_b, v_conv_ln_g, v_conv_ln_b, v_even_w_out, v_odd_w_in, v_lru_conv_w, v_lru_conv_b, v_gate_a_w, v_gate_a_b, v_gate_x_w, v_gate_x_b, v_lru_lambda, v_odd_w_out, v_norm_final)
    return _step(Cfg(), x, weights, loss_target, ms, vs)
```

```python
import math
from typing import NamedTuple

import jax
import jax.numpy as jnp
from jax import lax
from jax.experimental import pallas as pl
from jax.experimental.pallas import tpu as pltpu

F32 = jnp.float32
BF16 = jnp.bfloat16
RMS_EPS = 1e-6
LN_EPS = 1e-5
NEG_INF = -1e30
RG_LRU_C = 8.0
ADAM_LR = 0.001
ADAM_B1 = 0.9
ADAM_B2 = 0.999
ADAM_EPS = 1e-08
ADAM_WD = 0.01
ADAM_STEP = 10
N_DEV = 8
VMEM_LIMIT = 56 * 1024 * 1024


class Cfg(NamedTuple):
    d: int = 1024
    f: int = 2816
    s: int = 2048
    bl: int = 4
    hq: int = 8
    hkv: int = 2
    hd: int = 64
    win: int = 128
    cc: int = 512
    cw: int = 31
    lh: int = 8
    lb: int = 128
    lcw: int = 4
    nbuckets: int = 32
    max_dist: int = 128
    depth: int = 4
    tm: int = 512
    tm_ffn: int = 1024
    tf: int = 256
    tk_ffn: int = 256
    tf_w: int = 1408
    tk: int = 1024
    ct_f: int = 256
    ct_b: int = 256

    @property
    def qw(self):
        return self.hq * self.hd

    @property
    def kvw(self):
        return self.hkv * self.hd

    @property
    def even_in(self):
        return self.qw + 2 * self.kvw + 2 * self.cc

    @property
    def even_cat(self):
        return self.qw + self.cc

    @property
    def lw(self):
        return self.lh * self.lb


def _cparams(*sem):
    return pltpu.CompilerParams(dimension_semantics=sem, vmem_limit_bytes=VMEM_LIMIT)


def _nt(a, b):
    return lax.dot_general(a, b, (((1,), (1,)), ((), ())), preferred_element_type=F32)


def _nn(a, b):
    return lax.dot_general(a, b, (((1,), (0,)), ((), ())), preferred_element_type=F32)


def _tn(a, b):
    return lax.dot_general(a, b, (((0,), (0,)), ((), ())), preferred_element_type=F32)


def _rstd(h):
    return lax.rsqrt(jnp.mean(h * h, axis=-1, keepdims=True) + RMS_EPS)


def _rms_bwd(h, nw, dxn):
    rstd = _rstd(h)
    dyg = dxn * nw
    dnw = jnp.sum(dxn * h * rstd, axis=0, keepdims=True)
    dx = rstd * (dyg - h * (rstd * rstd) * jnp.mean(dyg * h, axis=-1, keepdims=True))
    return dx, dnw


def _sigmoid(x):
    return 0.5 * jnp.tanh(0.5 * x) + 0.5


FFN_SLABS = 4


def _ffn_wspecs(tf, d, gu, md):
    return [pl.BlockSpec((2, tf, d), lambda i, j: (gu, j, 0)), pl.BlockSpec((None, tf, d), lambda i, j: (md, j, 0))]


class Riders(NamedTuple):
    inputs: tuple
    out_shape: tuple
    scratch: tuple
    start: object
    wait: object


def _ride(riders, grid, n_in, n_out, body):
    if riders is None:
        return body, [], [], [], [], []
    ni, no, ns = len(riders.inputs), len(riders.out_shape), len(riders.scratch)

    def full(*refs):
        ins, rin = refs[:n_in], refs[n_in:n_in + ni]
        outs = refs[n_in + ni:n_in + ni + n_out]
        rout = refs[n_in + ni + n_out:n_in + ni + n_out + no]
        rest = refs[n_in + ni + n_out + no:]
        scratch, sems = rest[:len(rest) - ns], rest[len(rest) - ns:]
        first = last = None
        for axis, size in enumerate(grid):
            pid = pl.program_id(axis)
            first = (pid == 0) if first is None else first & (pid == 0)
            last = (pid == size - 1) if last is None else last & (pid == size - 1)

        @pl.when(first)
        def _():
            riders.start(rin, rout, sems)

        body(*ins, *outs, *scratch)

        @pl.when(last)
        def _():
            riders.wait(rin, rout, sems)

    any_spec = pl.BlockSpec(memory_space=pl.ANY)
    return full, list(riders.inputs), [any_spec] * ni, [any_spec] * no, list(riders.out_shape), list(riders.scratch)


def ffn_fwd(cfg, h, nw, wts, gu, md, name, riders=None):
    t, d = h.shape
    f = wts.shape[1]
    tm, tf = cfg.tm_ffn, cfg.tf
    nj = f // tf

    def body(h_ref, nw_ref, wgu_ref, wd_ref, ho_ref, xn_ref, g_ref, u_ref, acc_ref):
        j = pl.program_id(1)

        @pl.when(j == 0)
        def _():
            hh = h_ref[...]
            xn_ref[...] = (hh * _rstd(hh) * nw_ref[...]).astype(BF16)
            acc_ref[...] = jnp.zeros_like(acc_ref)

        gu = _nt(xn_ref[...], wgu_ref[...].reshape(2 * tf, d))
        g, u = gu[:, :tf], gu[:, tf:]
        g_ref[...] = g.astype(BF16)
        u_ref[...] = u.astype(BF16)
        acc_ref[...] += _nn((g * _sigmoid(g) * u).astype(BF16), wd_ref[...])

        @pl.when(j == nj - 1)
        def _():
            ho_ref[...] = h_ref[...] + 0.5 * acc_ref[...]

    row = pl.BlockSpec((tm, d), lambda i, j: (i, 0))
    hid = pl.BlockSpec((tm, tf), lambda i, j: (i, j))
    grid = (t // tm, nj)
    full, r_args, r_in, r_out, r_shape, r_scratch = _ride(riders, grid, 4, 4, body)
    return pl.pallas_call(
        full, name=name, grid=grid,
        in_specs=[row, pl.BlockSpec((1, d), lambda i, j: (0, 0))] + _ffn_wspecs(tf, d, gu, md) + r_in,
        out_specs=[row, row, hid, hid] + r_out,
        out_shape=[jax.ShapeDtypeStruct((t, d), F32), jax.ShapeDtypeStruct((t, d), BF16),
                   jax.ShapeDtypeStruct((t, f), BF16), jax.ShapeDtypeStruct((t, f), BF16)] + r_shape,
        scratch_shapes=[pltpu.VMEM((tm, d), F32)] + r_scratch,
        compiler_params=_cparams("arbitrary", "arbitrary"),
    )(h, nw, wts, wts, *r_args)


def ffn_bwd_x(cfg, dh, h, nw, g, u, wts, gu, md, name, riders=None):
    t, d = h.shape
    f = wts.shape[1]
    tm, tf = cfg.tm_ffn, cfg.tf
    nj = f // tf

    def body(dh_ref, h_ref, nw_ref, g_ref, u_ref, wgu_ref, wd_ref,
             dho_ref, dout_ref, dg_ref, du_ref, dnw_ref, acc_ref, da_ref):
        i, j = pl.program_id(0), pl.program_id(1)

        @pl.when(j == 0)
        def _():
            dout_ref[...] = (0.5 * dh_ref[...]).astype(BF16)
            acc_ref[...] = jnp.zeros_like(acc_ref)
            da_ref[1] = jnp.zeros((tm, tf), F32)

        @pl.when((i == 0) & (j == 0))
        def _():
            dnw_ref[...] = jnp.zeros_like(dnw_ref)

        slot = lax.rem(j, 2)
        da = da_ref[1 - slot]
        da_ref[slot] = _nt(dout_ref[...], wd_ref[...])
        gg = g_ref[...].astype(F32)
        sig = _sigmoid(gg)
        dg = (da * u_ref[...].astype(F32) * (sig * (1.0 + gg * (1.0 - sig)))).astype(BF16)
        du = (da * (gg * sig)).astype(BF16)
        dg_ref[...] = dg
        du_ref[...] = du
        acc_ref[...] += _nn(jnp.concatenate([dg, du], axis=1), wgu_ref[...].reshape(2 * tf, d))

        @pl.when(j == nj)
        def _():
            dx, dnw = _rms_bwd(h_ref[...], nw_ref[...], acc_ref[...])
            dnw_ref[0:1, :] += dnw
            dho_ref[...] = dh_ref[...] + dx

    row = pl.BlockSpec((tm, d), lambda i, j: (i, 0))
    prev = pl.BlockSpec((tm, tf), lambda i, j: (i, jnp.maximum(j - 1, 0)))
    wspecs = [pl.BlockSpec((2, tf, d), lambda i, j: (gu, jnp.maximum(j - 1, 0), 0)),
              pl.BlockSpec((None, tf, d), lambda i, j: (md, jnp.minimum(j, nj - 1), 0))]
    grid = (t // tm, nj + 1)
    full, r_args, r_in, r_out, r_shape, r_scratch = _ride(riders, grid, 7, 5, body)
    return pl.pallas_call(
        full, name=name, grid=grid,
        in_specs=[row, row, pl.BlockSpec((1, d), lambda i, j: (0, 0)), prev, prev] + wspecs + r_in,
        out_specs=[row, row, prev, prev, pl.BlockSpec((8, d), lambda i, j: (0, 0))] + r_out,
        out_shape=[jax.ShapeDtypeStruct((t, d), F32), jax.ShapeDtypeStruct((t, d), BF16),
                   jax.ShapeDtypeStruct((t, f), BF16), jax.ShapeDtypeStruct((t, f), BF16),
                   jax.ShapeDtypeStruct((8, d), F32)] + r_shape,
        scratch_shapes=[pltpu.VMEM((tm, d), F32), pltpu.VMEM((2, tm, tf), F32)] + r_scratch,
        compiler_params=_cparams("arbitrary", "arbitrary"),
    )(dh, h, nw, g, u, wts, wts, *r_args)


def ffn_bwd_w(cfg, xn, dout, g, u, dg, du, name, riders=None):
    t, d = xn.shape
    f = g.shape[1]
    tk, tf = cfg.tk_ffn, cfg.tf_w
    nk = t // tk

    def body(xn_ref, dout_ref, g_ref, u_ref, dg_ref, du_ref, o_ref, acc_ref):
        k = pl.program_id(1)

        @pl.when(k == 0)
        def _():
            acc_ref[...] = jnp.zeros_like(acc_ref)

        gg = g_ref[...].astype(F32)
        a = (gg * _sigmoid(gg) * u_ref[...].astype(F32)).astype(BF16)
        xn_t = xn_ref[...]
        acc_ref[0] += _tn(dg_ref[...], xn_t)
        acc_ref[1] += _tn(du_ref[...], xn_t)
        acc_ref[2] += _tn(a, dout_ref[...])

        @pl.when(k == nk - 1)
        def _():
            o_ref[...] = acc_ref[...].astype(BF16)

    row = pl.BlockSpec((tk, d), lambda j, k: (k, 0))
    hid = pl.BlockSpec((tk, tf), lambda j, k: (k, j))
    grid = (f // tf, nk)
    full, r_args, r_in, r_out, r_shape, r_scratch = _ride(riders, grid, 6, 1, body)
    return pl.pallas_call(
        full, name=name, grid=grid,
        in_specs=[row, row, hid, hid, hid, hid] + r_in,
        out_specs=[pl.BlockSpec((3, tf, d), lambda j, k: (0, j, 0))] + r_out,
        out_shape=[jax.ShapeDtypeStruct((3, f, d), BF16)] + r_shape,
        scratch_shapes=[pltpu.VMEM((3, tf, d), F32)] + r_scratch,
        compiler_params=_cparams("arbitrary", "arbitrary"),
    )(xn, dout, g, u, dg, du, *r_args)


def _wspec(w, wi):
    if w.ndim == 2:
        return pl.BlockSpec(w.shape, lambda i: (0, 0))
    return pl.BlockSpec((None,) + w.shape[1:], lambda i: (wi, 0, 0))


def norm_proj(cfg, h, nw, w, name, wi=0):
    t, d = h.shape
    n = w.shape[-2]
    tm = cfg.tm

    def body(h_ref, nw_ref, w_ref, u_ref, xn_ref):
        hh = h_ref[...]
        xn = (hh * _rstd(hh) * nw_ref[...]).astype(BF16)
        xn_ref[...] = xn
        u_ref[...] = _nt(xn, w_ref[...]).astype(BF16)

    return pl.pallas_call(
        body, name=name, grid=(t // tm,),
        in_specs=[pl.BlockSpec((tm, d), lambda i: (i, 0)), pl.BlockSpec((1, d), lambda i: (0, 0)),
                  _wspec(w, wi)],
        out_specs=[pl.BlockSpec((tm, n), lambda i: (i, 0)), pl.BlockSpec((tm, d), lambda i: (i, 0))],
        out_shape=[jax.ShapeDtypeStruct((t, n), BF16), jax.ShapeDtypeStruct((t, d), BF16)],
        compiler_params=_cparams("arbitrary"),
    )(h, nw, w)


def proj_residual(cfg, h, parts, w, name, wi=0):
    t, d = h.shape
    tm = cfg.tm
    ks = [p.shape[1] for p in parts]
    offs = [sum(ks[:i]) for i in range(len(ks))]
    np_ = len(parts)

    def body(*refs):
        h_ref, w_ref, ho_ref = refs[0], refs[1 + np_], refs[2 + np_]
        acc = h_ref[...]
        for p_ref, off, k in zip(refs[1:1 + np_], offs, ks):
            acc = acc + _nn(p_ref[...], w_ref[off:off + k, :])
        ho_ref[...] = acc

    return pl.pallas_call(
        body, name=name, grid=(t // tm,),
        in_specs=[pl.BlockSpec((tm, d), lambda i: (i, 0))]
        + [pl.BlockSpec((tm, k), lambda i: (i, 0)) for k in ks]
        + [_wspec(w, wi)],
        out_specs=pl.BlockSpec((tm, d), lambda i: (i, 0)),
        out_shape=jax.ShapeDtypeStruct((t, d), F32),
        compiler_params=_cparams("arbitrary"),
    )(h, *parts, w)


def proj_bwd_act(cfg, dh, w, name, wi=0):
    t, d = dh.shape
    k = w.shape[-2]
    tm = cfg.tm

    def body(dh_ref, w_ref, o_ref):
        o_ref[...] = _nt(dh_ref[...].astype(BF16), w_ref[...]).astype(BF16)

    return pl.pallas_call(
        body, name=name, grid=(t // tm,),
        in_specs=[pl.BlockSpec((tm, d), lambda i: (i, 0)), _wspec(w, wi)],
        out_specs=pl.BlockSpec((tm, k), lambda i: (i, 0)),
        out_shape=jax.ShapeDtypeStruct((t, k), BF16),
        compiler_params=_cparams("arbitrary"),
    )(dh, w)


def grad_weight(cfg, parts, b, name):
    t, d = b.shape
    tk = cfg.tk
    nk = t // tk
    ks = [p.shape[1] for p in parts]
    offs = [sum(ks[:i]) for i in range(len(ks))]
    np_ = len(parts)

    def body(*refs):
        b_ref, o_ref, acc_ref = refs[np_:]
        kk = pl.program_id(0)

        @pl.when(kk == 0)
        def _():
            acc_ref[...] = jnp.zeros_like(acc_ref)

        bb = b_ref[...].astype(BF16)
        for a_ref, off, k in zip(refs[:np_], offs, ks):
            acc_ref[off:off + k, :] += _tn(a_ref[...], bb)

        @pl.when(kk == nk - 1)
        def _():
            o_ref[...] = acc_ref[...].astype(BF16)

    return pl.pallas_call(
        body, name=name, grid=(nk,),
        in_specs=[pl.BlockSpec((tk, k), lambda kk: (kk, 0)) for k in ks] + [pl.BlockSpec((tk, d), lambda kk: (kk, 0))],
        out_specs=pl.BlockSpec((sum(ks), d), lambda kk: (0, 0)),
        out_shape=jax.ShapeDtypeStruct((sum(ks), d), BF16),
        scratch_shapes=[pltpu.VMEM((sum(ks), d), F32)],
        compiler_params=_cparams("arbitrary"),
    )(*parts, b)


def norm_proj_bwd(cfg, dh, h, nw, parts, w, name, wi=0):
    t, d = h.shape
    tm = cfg.tm
    ks = [p.shape[1] for p in parts]
    offs = [sum(ks[:i]) for i in range(len(ks))]
    np_ = len(parts)

    def body(*refs):
        dh_ref, h_ref, nw_ref = refs[:3]
        w_ref, dho_ref, dnw_ref = refs[3 + np_:]

        @pl.when(pl.program_id(0) == 0)
        def _():
            dnw_ref[...] = jnp.zeros_like(dnw_ref)

        dxn = None
        for p_ref, off, k in zip(refs[3:3 + np_], offs, ks):
            term = _nn(p_ref[...], w_ref[off:off + k, :])
            dxn = term if dxn is None else dxn + term
        dx, dnw = _rms_bwd(h_ref[...], nw_ref[...], dxn)
        dnw_ref[0:1, :] += dnw
        dho_ref[...] = dh_ref[...] + dx

    row = pl.BlockSpec((tm, d), lambda i: (i, 0))
    return pl.pallas_call(
        body, name=name, grid=(t // tm,),
        in_specs=[row, row, pl.BlockSpec((1, d), lambda i: (0, 0))]
        + [pl.BlockSpec((tm, k), lambda i: (i, 0)) for k in ks]
        + [_wspec(w, wi)],
        out_specs=[row, pl.BlockSpec((8, d), lambda i: (0, 0))],
        out_shape=[jax.ShapeDtypeStruct((t, d), F32), jax.ShapeDtypeStruct((8, d), F32)],
        compiler_params=_cparams("arbitrary"),
    )(dh, h, nw, *parts, w)


def loss_head(cfg, h, nf, tgt, name):
    t, d = h.shape
    tm = cfg.tm

    def body(h_ref, nf_ref, tgt_ref, loss_ref, dh_ref, dnf_ref):
        @pl.when(pl.program_id(0) == 0)
        def _():
            loss_ref[...] = jnp.zeros_like(loss_ref)
            dnf_ref[...] = jnp.zeros_like(dnf_ref)

        hh = h_ref[...]
        err = hh * _rstd(hh) * nf_ref[...] - tgt_ref[...]
        row = jnp.sum(err * err, axis=-1, keepdims=True) * (0.5 / d)
        loss_ref[...] += jnp.sum(row, axis=0, keepdims=True)
        dx, dnf = _rms_bwd(hh, nf_ref[...], err * (1.0 / d))
        dnf_ref[0:1, :] += dnf
        dh_ref[...] = dx

    row = pl.BlockSpec((tm, d), lambda i: (i, 0))
    return pl.pallas_call(
        body, name=name, grid=(t // tm,),
        in_specs=[row, pl.BlockSpec((1, d), lambda i: (0, 0)), row],
        out_specs=[pl.BlockSpec((8, 128), lambda i: (0, 0)), row, pl.BlockSpec((8, d), lambda i: (0, 0))],
        out_shape=[jax.ShapeDtypeStruct((8, 128), F32), jax.ShapeDtypeStruct((t, d), F32),
                   jax.ShapeDtypeStruct((8, d), F32)],
        compiler_params=_cparams("arbitrary"),
    )(h, nf, tgt)


def bucket_table(cfg):
    qi = jnp.arange(cfg.win)[:, None]
    sj = jnp.arange(2 * cfg.win)[None, :]
    dist = qi + cfg.win - sj
    n = jnp.maximum(dist, 0)
    max_exact = cfg.nbuckets // 2
    nf = jnp.maximum(n, max_exact).astype(F32)
    large = max_exact + (jnp.log(nf / max_exact) / math.log(cfg.max_dist / max_exact)
                         * (cfg.nbuckets - max_exact)).astype(jnp.int32)
    large = jnp.minimum(large, cfg.nbuckets - 1)
    bucket = jnp.where(n < max_exact, n, large)
    return jnp.where((dist >= 0) & (dist < cfg.win), bucket, -1).astype(jnp.int32)


def bias_build(cfg, rel_bias, buckets, name):
    w = cfg.win

    def body(rb_ref, bk_ref, o_ref):
        bk = bk_ref[...]
        for h in range(cfg.hq):
            acc = jnp.full((w, 2 * w), NEG_INF, F32)
            for b in range(cfg.nbuckets):
                acc = jnp.where(bk == b, rb_ref[b, h], acc)
            o_ref[h] = acc

    return pl.pallas_call(
        body, name=name,
        in_specs=[pl.BlockSpec(memory_space=pltpu.SMEM), pl.BlockSpec(memory_space=pltpu.VMEM)],
        out_specs=pl.BlockSpec(memory_space=pltpu.VMEM),
        out_shape=jax.ShapeDtypeStruct((cfg.hq, w, 2 * w), F32),
    )(rel_bias, buckets)


def bias_grad(cfg, dbias, buckets, name):
    w = cfg.win

    def body(db_ref, bk_ref, o_ref, rows_ref):
        bk = bk_ref[...]
        for h in range(cfg.hq):
            d = db_ref[0, h]
            for e in range(1, dbias.shape[0]):
                d = d + db_ref[e, h]
            for b in range(cfg.nbuckets):
                rows_ref[b:b + 1, :] = jnp.sum(jnp.where(bk == b, d, 0.0), axis=0, keepdims=True)
            o_ref[h] = jnp.broadcast_to(jnp.sum(rows_ref[...], axis=1, keepdims=True), (cfg.nbuckets, 128))

    return pl.pallas_call(
        body, name=name,
        in_specs=[pl.BlockSpec(memory_space=pltpu.VMEM), pl.BlockSpec(memory_space=pltpu.VMEM)],
        out_specs=pl.BlockSpec(memory_space=pltpu.VMEM),
        out_shape=jax.ShapeDtypeStruct((cfg.hq, cfg.nbuckets, 128), F32),
        scratch_shapes=[pltpu.VMEM((cfg.nbuckets, 2 * w), F32)],
    )(dbias, buckets)


def _attn_probs(cfg, qh, kj, bias_h, sink, first_ok):
    s = _nt(qh, kj) * (1.0 / math.sqrt(cfg.hd)) + bias_h
    s = jnp.where(first_ok, s, NEG_INF)
    m = jnp.maximum(jnp.max(s, axis=-1, keepdims=True), sink)
    e = jnp.exp(s - m)
    es = jnp.exp(sink - m)
    inv = 1.0 / (jnp.sum(e, axis=-1, keepdims=True) + es)
    return e * inv, es * inv


def _attn_block_inputs(cfg, n, q_ref, kv_ref):
    w = cfg.win
    r0 = pl.multiple_of(n * w, w)
    rp = pl.multiple_of(jnp.maximum(n - 1, 0) * w, w)
    qb = q_ref[pl.ds(r0, w), :]
    kk = jnp.concatenate([kv_ref[pl.ds(rp, w), :], kv_ref[pl.ds(r0, w), :]], axis=0)
    col = lax.broadcasted_iota(jnp.int32, (w, 2 * w), 1)
    first_ok = (n > 0) | (col >= w)
    return r0, rp, qb, kk, first_ok


def _kv_col_block(cfg):
    assert cfg.qw % (2 * cfg.kvw) == 0
    return cfg.qw // (2 * cfg.kvw)


def attn_fwd(cfg, u, bias, sinks, name):
    t = u.shape[0]
    s, w, hd, g = cfg.s, cfg.win, cfg.hd, cfg.hq // cfg.hkv
    kvb = _kv_col_block(cfg)

    def body(q_ref, kv_ref, bias_ref, sink_ref, o_ref):
        def blk(n, carry):
            r0, _, qb, kk, first_ok = _attn_block_inputs(cfg, n, q_ref, kv_ref)
            outs = []
            for j in range(cfg.hkv):
                kj = kk[:, hd * j:hd * (j + 1)]
                vj = kk[:, cfg.kvw + hd * j:cfg.kvw + hd * (j + 1)]
                for gq in range(g):
                    h = j * g + gq
                    p, _ = _attn_probs(cfg, qb[:, hd * h:hd * (h + 1)], kj, bias_ref[h], sink_ref[h], first_ok)
                    outs.append(_nn(p.astype(BF16), vj))
            o_ref[pl.ds(r0, w), :] = jnp.concatenate(outs, axis=1).astype(BF16)
            return carry

        lax.fori_loop(0, s // w, blk, 0, unroll=2)

    return pl.pallas_call(
        body, name=name, grid=(t // s,),
        in_specs=[pl.BlockSpec((s, cfg.qw), lambda b: (b, 0)), pl.BlockSpec((s, 2 * cfg.kvw), lambda b: (b, kvb)),
                  pl.BlockSpec(bias.shape, lambda b: (0, 0, 0)), pl.BlockSpec(memory_space=pltpu.SMEM)],
        out_specs=pl.BlockSpec((s, cfg.qw), lambda b: (b, 0)),
        out_shape=jax.ShapeDtypeStruct((t, cfg.qw), BF16),
        compiler_params=_cparams("arbitrary"),
    )(u, u, bias, sinks)


def attn_bwd(cfg, u, dcat, bias, sinks, name):
    t = u.shape[0]
    s, w, hd, g = cfg.s, cfg.win, cfg.hd, cfg.hq // cfg.hkv
    kvb = _kv_col_block(cfg)
    scale = 1.0 / math.sqrt(hd)
    assert cfg.hq <= 8

    def body(q_ref, kv_ref, do_ref, bias_ref, sink_ref, du_ref, dbias_ref, dsink_ref, dkv_ref):
        @pl.when(pl.program_id(0) == 0)
        def _():
            dbias_ref[...] = jnp.zeros_like(dbias_ref)
            dsink_ref[...] = jnp.zeros_like(dsink_ref)

        dkv_ref[...] = jnp.zeros_like(dkv_ref)

        def blk(n, carry):
            r0, rp, qb, kk, first_ok = _attn_block_inputs(cfg, n, q_ref, kv_ref)
            dob = do_ref[pl.ds(r0, w), :]
            dqs, dks, dvs = [], [], []
            for j in range(cfg.hkv):
                kj = kk[:, hd * j:hd * (j + 1)]
                vj = kk[:, cfg.kvw + hd * j:cfg.kvw + hd * (j + 1)]
                dk = jnp.zeros((2 * w, hd), F32)
                dv = jnp.zeros((2 * w, hd), F32)
                for gq in range(g):
                    h = j * g + gq
                    qh = qb[:, hd * h:hd * (h + 1)]
                    doh = dob[:, hd * h:hd * (h + 1)]
                    p, ps = _attn_probs(cfg, qh, kj, bias_ref[h], sink_ref[h], first_ok)
                    dp = _nt(doh, vj)
                    delta = jnp.sum(p * dp, axis=-1, keepdims=True)
                    ds = p * (dp - delta)
                    dsink_ref[h:h + 1, :] += jnp.broadcast_to(-jnp.sum(ps * delta, axis=0, keepdims=True), (1, 128))
                    dbias_ref[h] += ds
                    dsb = ds.astype(BF16)
                    dqs.append(_nn(dsb, kj) * scale)
                    dk = dk + _tn(dsb, qh) * scale
                    dv = dv + _tn(p.astype(BF16), doh)
                dks.append(dk)
                dvs.append(dv)
            du_ref[pl.ds(r0, w), 0:cfg.qw] = jnp.concatenate(dqs, axis=1).astype(BF16)
            dkv = jnp.concatenate(dks + dvs, axis=1)
            dkv_ref[pl.ds(rp, w), :] += dkv[:w]
            dkv_ref[pl.ds(r0, w), :] += dkv[w:]
            return carry

        lax.fori_loop(0, s // w, blk, 0, unroll=2)
        du_ref[:, cfg.qw:] = dkv_ref[...].astype(BF16)

    wa = cfg.qw + 2 * cfg.kvw
    return pl.pallas_call(
        body, name=name, grid=(t // s,),
        in_specs=[pl.BlockSpec((s, cfg.qw), lambda b: (b, 0)), pl.BlockSpec((s, 2 * cfg.kvw), lambda b: (b, kvb)),
                  pl.BlockSpec((s, cfg.qw), lambda b: (b, 0)),
                  pl.BlockSpec(bias.shape, lambda b: (0, 0, 0)), pl.BlockSpec(memory_space=pltpu.SMEM)],
        out_specs=[pl.BlockSpec((s, wa), lambda b: (b, 0)), pl.BlockSpec(bias.shape, lambda b: (0, 0, 0)),
                   pl.BlockSpec((8, 128), lambda b: (0, 0))],
        out_shape=[jax.ShapeDtypeStruct((t, wa), BF16), jax.ShapeDtypeStruct(bias.shape, F32),
                   jax.ShapeDtypeStruct((8, 128), F32)],
        scratch_shapes=[pltpu.VMEM((s, 2 * cfg.kvw), F32)],
        compiler_params=_cparams("arbitrary"),
    )(u, u, dcat, bias, sinks)


def _shift_views(win, rc, pad):
    return [win] + [win[j:j + rc + pad - 8] for j in range(1, 8)]


def _tap(views, off, rc):
    a = 8 * (off // 8)
    return views[off % 8][a:a + rc]


def _conv_rows(views, w_ref, cw, pad, rc, lanes=slice(None)):
    acc = None
    for k in range(cw):
        term = _tap(views, pad - (cw - 1) + k, rc) * w_ref[k:k + 1, lanes]
        acc = term if acc is None else acc + term
    return acc


def _conv_rows_t(views, w_ref, cw, rc, lanes=slice(None)):
    acc = None
    for k in range(cw):
        term = _tap(views, cw - 1 - k, rc) * w_ref[k:k + 1, lanes]
        acc = term if acc is None else acc + term
    return acc


def _group_sum(x):
    acc = x[0:8]
    for i in range(1, x.shape[0] // 8):
        acc = acc + x[8 * i:8 * i + 8]
    return acc


def _conv_wgrad(views, dy, acc_ref, cw, pad, rc, lanes=slice(None)):
    for k in range(cw):
        acc_ref[k, :, lanes] += _group_sum(dy * _tap(views, pad - (cw - 1) + k, rc))


LANE_TILE = 128


def _lane_tiles(width):
    return [slice(c0, c0 + LANE_TILE) for c0 in range(0, width, LANE_TILE)]


def _conv_rows_tiled(x_ref, r0, w_ref, cw, pad, rc):
    return jnp.concatenate([_conv_rows(_shift_views(x_ref[pl.ds(r0, rc + pad), lanes], rc, pad), w_ref, cw, pad, rc, lanes)
                            for lanes in _lane_tiles(x_ref.shape[1])], axis=1)


def _conv_rows_t_tiled(x_ref, r0, w_ref, cw, pad, rc):
    return jnp.concatenate([_conv_rows_t(_shift_views(x_ref[pl.ds(r0, rc + pad), lanes], rc, pad), w_ref, cw, rc, lanes)
                            for lanes in _lane_tiles(x_ref.shape[1])], axis=1)


def _conv_wgrad_tiled(x_ref, r0, dy, acc_ref, cw, pad, rc):
    for lanes in _lane_tiles(x_ref.shape[1]):
        _conv_wgrad(_shift_views(x_ref[pl.ds(r0, rc + pad), lanes], rc, pad), dy[:, lanes], acc_ref, cw, pad, rc, lanes)


CONV_RC = 64
CONV_PAD = 32
GLU_RC = 256


def _conv_col_blocks(cfg):
    off = cfg.qw + 2 * cfg.kvw
    bw = math.gcd(off, cfg.cc)
    assert bw % 128 == 0
    n = cfg.cc // bw
    return bw, [off // bw + i for i in range(n)], [(off + cfg.cc) // bw + i for i in range(n)]


def _glu_inputs(a_refs, b_refs, rows):
    ga = jnp.concatenate([r[rows, :] for r in a_refs], axis=1).astype(F32)
    gb = jnp.concatenate([r[rows, :] for r in b_refs], axis=1).astype(F32)
    return ga, gb


def _fill_glu(cfg, a_refs, b_refs, xp_ref):
    xp_ref[0:CONV_PAD, :] = jnp.zeros((CONV_PAD, cfg.cc), F32)

    def fill(i, carry):
        r0 = pl.multiple_of(i * GLU_RC, GLU_RC)
        ga, gb = _glu_inputs(a_refs, b_refs, pl.ds(r0, GLU_RC))
        xp_ref[pl.ds(CONV_PAD + r0, GLU_RC), :] = ga * _sigmoid(gb)
        return carry

    lax.fori_loop(0, cfg.s // GLU_RC, fill, 0)


def _layernorm_stats(cv):
    mu = jnp.mean(cv, axis=-1, keepdims=True)
    xc = cv - mu
    rstd = lax.rsqrt(jnp.mean(xc * xc, axis=-1, keepdims=True) + LN_EPS)
    return xc * rstd, rstd


def conv_fwd(cfg, u, cw_w, cb, lg, lb, name):
    t = u.shape[0]
    s, cc, cw = cfg.s, cfg.cc, cfg.cw
    bw, a_idx, b_idx = _conv_col_blocks(cfg)
    nb = len(a_idx)

    def body(*refs):
        a_refs, b_refs = refs[:nb], refs[nb:2 * nb]
        w_ref, cb_ref, lg_ref, lb_ref, o_ref, cv_ref, xp_ref = refs[2 * nb:]
        _fill_glu(cfg, a_refs, b_refs, xp_ref)

        def chunk(i, carry):
            r0 = pl.multiple_of(i * CONV_RC, CONV_RC)
            cv = _conv_rows_tiled(xp_ref, r0, w_ref, cw, CONV_PAD, CONV_RC) + cb_ref[...]
            cv_ref[pl.ds(r0, CONV_RC), :] = cv
            xhat, _ = _layernorm_stats(cv)
            ln = xhat * lg_ref[...] + lb_ref[...]
            o_ref[pl.ds(r0, CONV_RC), :] = (ln * _sigmoid(ln)).astype(BF16)
            return carry

        lax.fori_loop(0, s // CONV_RC, chunk, 0, unroll=2)

    def colspec(j):
        return pl.BlockSpec((s, bw), lambda b: (b, j))

    vec = pl.BlockSpec((1, cc), lambda b: (0, 0))
    return pl.pallas_call(
        body, name=name, grid=(t // s,),
        in_specs=[colspec(j) for j in a_idx + b_idx] + [pl.BlockSpec((cw, cc), lambda b: (0, 0)), vec, vec, vec],
        out_specs=[pl.BlockSpec((s, cc), lambda b: (b, 0))] * 2,
        out_shape=[jax.ShapeDtypeStruct((t, cc), BF16), jax.ShapeDtypeStruct((t, cc), F32)],
        scratch_shapes=[pltpu.VMEM((CONV_PAD + s, cc), F32)],
        compiler_params=_cparams("arbitrary"),
    )(*([u] * (2 * nb)), cw_w, cb, lg, lb)


def conv_bwd(cfg, u, cv_saved, dcat, cw_w, cb, lg, lb, name):
    t = u.shape[0]
    s, cc, cw = cfg.s, cfg.cc, cfg.cw
    bw, a_idx, b_idx = _conv_col_blocks(cfg)
    nb = len(a_idx)
    assert cfg.qw % cc == 0 and cw <= 32

    def body(*refs):
        a_refs, b_refs = refs[:nb], refs[nb:2 * nb]
        cv_ref, dc_ref, w_ref, cb_ref, lg_ref, lb_ref, du_ref, dw_ref, dvec_ref, xp_ref, dcv_ref, dwacc_ref = refs[2 * nb:]

        @pl.when(pl.program_id(0) == 0)
        def _():
            dw_ref[...] = jnp.zeros_like(dw_ref)
            dvec_ref[...] = jnp.zeros_like(dvec_ref)

        _fill_glu(cfg, a_refs, b_refs, xp_ref)
        dcv_ref[s:s + CONV_PAD, :] = jnp.zeros((CONV_PAD, cc), F32)
        dwacc_ref[...] = jnp.zeros_like(dwacc_ref)

        def chunk(i, carry):
            r0 = pl.multiple_of(i * CONV_RC, CONV_RC)
            xhat, rstd = _layernorm_stats(cv_ref[pl.ds(r0, CONV_RC), :])
            ln = xhat * lg_ref[...] + lb_ref[...]
            sg = _sigmoid(ln)
            dln = dc_ref[pl.ds(r0, CONV_RC), :].astype(F32) * (sg * (1.0 + ln * (1.0 - sg)))
            dxh = dln * lg_ref[...]
            dcv = rstd * (dxh - jnp.mean(dxh, axis=-1, keepdims=True)
                          - xhat * jnp.mean(dxh * xhat, axis=-1, keepdims=True))
            dcv_ref[pl.ds(r0, CONV_RC), :] = dcv
            dvec_ref[0:1, :] += jnp.sum(dcv, axis=0, keepdims=True)
            dvec_ref[1:2, :] += jnp.sum(dln * xhat, axis=0, keepdims=True)
            dvec_ref[2:3, :] += jnp.sum(dln, axis=0, keepdims=True)
            _conv_wgrad_tiled(xp_ref, r0, dcv, dwacc_ref, cw, CONV_PAD, CONV_RC)
            return carry

        lax.fori_loop(0, s // CONV_RC, chunk, 0, unroll=2)
        for k in range(cw):
            dw_ref[k:k + 1, :] += jnp.sum(dwacc_ref[k], axis=0, keepdims=True)

        def chunk2(i, carry):
            r0 = pl.multiple_of(i * CONV_RC, CONV_RC)
            dglu = _conv_rows_t_tiled(dcv_ref, r0, w_ref, cw, CONV_PAD, CONV_RC)
            ga, gb = _glu_inputs(a_refs, b_refs, pl.ds(r0, CONV_RC))
            sgb = _sigmoid(gb)
            du_ref[pl.ds(r0, CONV_RC), 0:cc] = (dglu * sgb).astype(BF16)
            du_ref[pl.ds(r0, CONV_RC), cc:2 * cc] = (dglu * ga * sgb * (1.0 - sgb)).astype(BF16)
            return carry

        lax.fori_loop(0, s // CONV_RC, chunk2, 0)

    def colspec(j):
        return pl.BlockSpec((s, bw), lambda b: (b, j))

    vec = pl.BlockSpec((1, cc), lambda b: (0, 0))
    return pl.pallas_call(
        body, name=name, grid=(t // s,),
        in_specs=[colspec(j) for j in a_idx + b_idx]
        + [pl.BlockSpec((s, cc), lambda b: (b, 0)), pl.BlockSpec((s, cc), lambda b: (b, cfg.qw // cc)),
           pl.BlockSpec((cw, cc), lambda b: (0, 0)), vec, vec, vec],
        out_specs=[pl.BlockSpec((s, 2 * cc), lambda b: (b, 0)), pl.BlockSpec((32, cc), lambda b: (0, 0)),
                   pl.BlockSpec((8, cc), lambda b: (0, 0))],
        out_shape=[jax.ShapeDtypeStruct((t, 2 * cc), BF16), jax.ShapeDtypeStruct((32, cc), F32),
                   jax.ShapeDtypeStruct((8, cc), F32)],
        scratch_shapes=[pltpu.VMEM((CONV_PAD + s, cc), F32), pltpu.VMEM((s + CONV_PAD, cc), F32),
                        pltpu.VMEM((cw, 8, cc), F32)],
        compiler_params=_cparams("arbitrary"),
    )(*([u] * (2 * nb)), cv_saved, dcat, cw_w, cb, lg, lb)


LRU_RC = 64
LRU_PAD = 8
SCAN_RC = 16
SCAN_UNROLL = 4
GELU_K = math.sqrt(2.0 / math.pi)


def _expm1_neg(z):
    return jnp.where(z > -0.05, z * (1.0 + z * (0.5 + z * (1.0 / 6.0 + z * (1.0 / 24.0)))), jnp.exp(z) - 1.0)


def _log_sigmoid(x):
    e = jnp.exp(-jnp.abs(x))
    log1p = jnp.where(e < 0.01, e * (1.0 - e * (0.5 - e * (1.0 / 3.0))), jnp.log(1.0 + e))
    return jnp.minimum(x, 0.0) - log1p


def _gelu(x):
    t = jnp.tanh(GELU_K * (x + 0.044715 * x * x * x))
    return 0.5 * x * (1.0 + t), t


def _gelu_grad(x, t):
    return 0.5 * (1.0 + t) + 0.5 * x * (1.0 - t * t) * GELU_K * (1.0 + 3.0 * 0.044715 * x * x)


def _lru_gates(xc, wa_ref, ba, wx_ref, bx, ls):
    nh = xc.shape[1] // 128
    xb = xc.astype(BF16)
    ra = jnp.concatenate([_nn(xb[:, 128 * h:128 * (h + 1)], wa_ref[h]) for h in range(nh)], axis=1) + ba
    ia = jnp.concatenate([_nn(xb[:, 128 * h:128 * (h + 1)], wx_ref[h]) for h in range(nh)], axis=1) + bx
    r = _sigmoid(ra)
    ig = _sigmoid(ia)
    log_a = RG_LRU_C * r * ls
    return r, ig, log_a


def _lru_decay(log_a):
    return jnp.exp(log_a), jnp.sqrt(-_expm1_neg(2.0 * log_a))


def _fill_padded(src_ref, dst_ref, s, ct):
    dst_ref[0:LRU_PAD, :] = jnp.zeros((LRU_PAD, ct), F32)

    def fill(i, carry):
        r0 = pl.multiple_of(i * GLU_RC, GLU_RC)
        dst_ref[pl.ds(LRU_PAD + r0, GLU_RC), :] = src_ref[pl.ds(r0, GLU_RC), :].astype(F32)
        return carry

    lax.fori_loop(0, s // GLU_RC, fill, 0)


def _lru_specs(cfg, ct):
    s, lw = cfg.s, cfg.lw
    nct = lw // ct
    nh = ct // 128
    act = [pl.BlockSpec((s, ct), lambda c, b: (b, c)), pl.BlockSpec((s, ct), lambda c, b: (b, nct + c))]
    vec = pl.BlockSpec((1, ct), lambda c, b: (0, c))
    gate_w = pl.BlockSpec((nh, 128, 128), lambda c, b: (c, 0, 0))
    params = [pl.BlockSpec((cfg.lcw, ct), lambda c, b: (0, c)), vec, gate_w, vec, gate_w, vec, vec]
    return nct, act, params


def lru_fwd(cfg, u, conv_w, conv_b, wa, ba, wx, bx, lam, name):
    t = u.shape[0]
    s, lw, lcw, ct = cfg.s, cfg.lw, cfg.lcw, cfg.ct_f
    nct, act, params = _lru_specs(cfg, ct)

    def body(gi_ref, ri_ref, cw_ref, cb_ref, wa_ref, ba_ref, wx_ref, bx_ref, lam_ref,
             y_ref, hs_ref, r_ref, ig_ref, xc_ref, la_ref, xp_ref, a_ref, b_ref):
        _fill_padded(ri_ref, xp_ref, s, ct)
        ls = _log_sigmoid(lam_ref[...])

        def chunk(i, carry):
            r0 = pl.multiple_of(i * LRU_RC, LRU_RC)
            rows = pl.ds(r0, LRU_RC)
            views = _shift_views(xp_ref[pl.ds(r0, LRU_RC + LRU_PAD), :], LRU_RC, LRU_PAD)
            xc = _conv_rows(views, cw_ref, lcw, LRU_PAD, LRU_RC) + cb_ref[...]
            r, ig, log_a = _lru_gates(xc, wa_ref, ba_ref[...], wx_ref, bx_ref[...], ls)
            a, mult = _lru_decay(log_a)
            a_ref[rows, :] = a
            b_ref[rows, :] = mult * (ig * xc)
            r_ref[rows, :] = r.astype(BF16)
            ig_ref[rows, :] = ig.astype(BF16)
            xc_ref[rows, :] = xc.astype(BF16)
            la_ref[rows, :] = log_a
            return carry

        lax.fori_loop(0, s // LRU_RC, chunk, 0, unroll=2)
        row = lax.broadcasted_iota(jnp.int32, (SCAN_RC, ct), 0)

        def scan(i, h_last):
            for sub in range(SCAN_UNROLL):
                rows = pl.ds(pl.multiple_of((i * SCAN_UNROLL + sub) * SCAN_RC, SCAN_RC), SCAN_RC)
                a = a_ref[rows, :]
                b = b_ref[rows, :]
                sft = 1
                while sft < SCAN_RC:
                    a_sh = jnp.where(row >= sft, pltpu.roll(a, sft, 0), 1.0)
                    b_sh = jnp.where(row >= sft, pltpu.roll(b, sft, 0), 0.0)
                    b = a * b_sh + b
                    a = a * a_sh
                    sft *= 2
                h = a * h_last + b
                gate, _ = _gelu(gi_ref[rows, :].astype(F32))
                y_ref[rows, :] = (gate * h).astype(BF16)
                hs_ref[rows, :] = h.astype(BF16)
                h_last = h[SCAN_RC - 1:SCAN_RC, :]
            return h_last

        lax.fori_loop(0, s // (SCAN_RC * SCAN_UNROLL), scan, jnp.zeros((1, ct), F32))

    out = pl.BlockSpec((s, ct), lambda c, b: (b, c))
    return pl.pallas_call(
        body, name=name, grid=(nct, t // s),
        in_specs=act + params,
        out_specs=[out] * 6,
        out_shape=[jax.ShapeDtypeStruct((t, lw), BF16)] * 5 + [jax.ShapeDtypeStruct((t, lw), F32)],
        scratch_shapes=[pltpu.VMEM((LRU_PAD + s, ct), F32), pltpu.VMEM((s, ct), F32), pltpu.VMEM((s, ct), F32)],
        compiler_params=_cparams("arbitrary", "arbitrary"),
    )(u, u, conv_w, conv_b, wa, ba, wx, bx, lam)


def lru_bwd(cfg, u, saved, dy, conv_w, conv_b, wa, ba, wx, bx, lam, name):
    t = u.shape[0]
    s, lw, lcw, ct = cfg.s, cfg.lw, cfg.lcw, cfg.ct_b
    nct, act, params = _lru_specs(cfg, ct)
    nh = ct // 128
    nscan = s // SCAN_RC
    assert lcw <= 8

    def body(gi_ref, ri_ref, hs_ref, r_ref, ig_ref, xc_ref, la_ref, dy_ref,
             cw_ref, cb_ref, wa_ref, ba_ref, wx_ref, bx_ref, lam_ref,
             dug_ref, dur_ref, dwa_ref, dwx_ref, dvec_ref, dcw_ref,
             xp_ref, hp_ref, a_ref, g_ref, dxc_ref, dwacc_ref):
        @pl.when(pl.program_id(1) == 0)
        def _():
            dwa_ref[...] = jnp.zeros_like(dwa_ref)
            dwx_ref[...] = jnp.zeros_like(dwx_ref)
            dvec_ref[...] = jnp.zeros_like(dvec_ref)
            dcw_ref[...] = jnp.zeros_like(dcw_ref)

        _fill_padded(ri_ref, xp_ref, s, ct)
        _fill_padded(hs_ref, hp_ref, s, ct)
        dxc_ref[s:s + LRU_PAD, :] = jnp.zeros((LRU_PAD, ct), F32)
        dwacc_ref[...] = jnp.zeros_like(dwacc_ref)
        lam = lam_ref[...]
        ls = _log_sigmoid(lam)

        def chunk(i, carry):
            r0 = pl.multiple_of(i * LRU_RC, LRU_RC)
            rows = pl.ds(r0, LRU_RC)
            a_ref[rows, :] = jnp.exp(la_ref[rows, :])
            x = gi_ref[rows, :].astype(F32)
            gate, th = _gelu(x)
            dyv = dy_ref[rows, :].astype(F32)
            g_ref[rows, :] = dyv * gate
            dug_ref[rows, :] = (dyv * hp_ref[pl.ds(LRU_PAD + r0, LRU_RC), :] * _gelu_grad(x, th)).astype(BF16)
            return carry

        lax.fori_loop(0, s // LRU_RC, chunk, 0, unroll=2)
        row = lax.broadcasted_iota(jnp.int32, (SCAN_RC, ct), 0)

        def scan(ii, carry):
            g_next, a_next = carry
            for sub in range(SCAN_UNROLL):
                step = nscan - 1 - (ii * SCAN_UNROLL + sub)
                rows = pl.ds(pl.multiple_of(step * SCAN_RC, SCAN_RC), SCAN_RC)
                a = a_ref[rows, :]
                d = g_ref[rows, :]
                c = jnp.where(row < SCAN_RC - 1, pltpu.roll(a, SCAN_RC - 1, 0), a_next)
                sft = 1
                while sft < SCAN_RC:
                    c_sh = jnp.where(row < SCAN_RC - sft, pltpu.roll(c, SCAN_RC - sft, 0), 1.0)
                    d_sh = jnp.where(row < SCAN_RC - sft, pltpu.roll(d, SCAN_RC - sft, 0), 0.0)
                    d = d + c * d_sh
                    c = c * c_sh
                    sft *= 2
                g = d + c * g_next
                g_ref[rows, :] = g
                g_next, a_next = g[0:1, :], a[0:1, :]
            return g_next, a_next

        lax.fori_loop(0, nscan // SCAN_UNROLL, scan, (jnp.zeros((1, ct), F32), jnp.zeros((1, ct), F32)))

        def chunk3(i, carry):
            r0 = pl.multiple_of(i * LRU_RC, LRU_RC)
            rows = pl.ds(r0, LRU_RC)
            r, ig, xc = r_ref[rows, :].astype(F32), ig_ref[rows, :].astype(F32), xc_ref[rows, :].astype(F32)
            a, mult = _lru_decay(la_ref[rows, :])
            g = g_ref[rows, :]
            h_prev = hp_ref[pl.ds(r0, LRU_RC + LRU_PAD), :][LRU_PAD - 1:LRU_PAD - 1 + LRU_RC]
            dix = g * mult
            di = dix * xc
            dxc = dix * ig
            da = g * h_prev - (g * ig * xc) * a / mult
            dlog_a = da * a
            dr = dlog_a * (RG_LRU_C * ls)
            dra = dr * r * (1.0 - r)
            dia = di * ig * (1.0 - ig)
            xb, drab, diab = xc.astype(BF16), dra.astype(BF16), dia.astype(BF16)
            dxg = []
            for h in range(nh):
                cols = slice(128 * h, 128 * (h + 1))
                dxg.append(_nt(drab[:, cols], wa_ref[h]) + _nt(diab[:, cols], wx_ref[h]))
                dwa_ref[h] += _tn(xb[:, cols], drab[:, cols])
                dwx_ref[h] += _tn(xb[:, cols], diab[:, cols])
            dxc = dxc + jnp.concatenate(dxg, axis=1)
            dvec_ref[0:1, :] += jnp.sum(dra, axis=0, keepdims=True)
            dvec_ref[1:2, :] += jnp.sum(dia, axis=0, keepdims=True)
            dvec_ref[2:3, :] += jnp.sum(dlog_a * r, axis=0, keepdims=True) * (RG_LRU_C * _sigmoid(-lam))
            dvec_ref[3:4, :] += jnp.sum(dxc, axis=0, keepdims=True)
            dxc_ref[rows, :] = dxc
            views = _shift_views(xp_ref[pl.ds(r0, LRU_RC + LRU_PAD), :], LRU_RC, LRU_PAD)
            _conv_wgrad(views, dxc, dwacc_ref, lcw, LRU_PAD, LRU_RC)
            return carry

        lax.fori_loop(0, s // LRU_RC, chunk3, 0, unroll=2)
        for k in range(lcw):
            dcw_ref[k:k + 1, :] += jnp.sum(dwacc_ref[k], axis=0, keepdims=True)

        def chunk4(i, carry):
            r0 = pl.multiple_of(i * LRU_RC, LRU_RC)
            views = _shift_views(dxc_ref[pl.ds(r0, LRU_RC + LRU_PAD), :], LRU_RC, LRU_PAD)
            dur_ref[pl.ds(r0, LRU_RC), :] = _conv_rows_t(views, cw_ref, lcw, LRU_RC).astype(BF16)
            return carry

        lax.fori_loop(0, s // LRU_RC, chunk4, 0, unroll=2)

    blk = pl.BlockSpec((s, ct), lambda c, b: (b, c))
    acc8 = pl.BlockSpec((8, ct), lambda c, b: (0, c))
    gate_w = pl.BlockSpec((nh, 128, 128), lambda c, b: (c, 0, 0))
    return pl.pallas_call(
        body, name=name, grid=(nct, t // s),
        in_specs=act + [blk] * 6 + params,
        out_specs=[blk, blk, gate_w, gate_w, acc8, acc8],
        out_shape=[jax.ShapeDtypeStruct((t, lw), BF16), jax.ShapeDtypeStruct((t, lw), BF16),
                   jax.ShapeDtypeStruct((cfg.lh, 128, 128), F32), jax.ShapeDtypeStruct((cfg.lh, 128, 128), F32),
                   jax.ShapeDtypeStruct((8, lw), F32), jax.ShapeDtypeStruct((8, lw), F32)],
        scratch_shapes=[pltpu.VMEM((LRU_PAD + s, ct), F32), pltpu.VMEM((LRU_PAD + s, ct), F32),
                        pltpu.VMEM((s, ct), F32), pltpu.VMEM((s, ct), F32), pltpu.VMEM((s + LRU_PAD, ct), F32),
                        pltpu.VMEM((lcw, 8, ct), F32)],
        compiler_params=_cparams("arbitrary", "arbitrary"),
    )(u, u, *saved, dy, conv_w, conv_b, wa, ba, wx, bx, lam)


def local_step(cfg, x, tgt, p, shards=None):
    buckets = bucket_table(cfg)
    bias = bias_build(cfg, p["rel_bias"], buckets, "bias_build")
    ga_w, gx_w = p["gate_a_w"].astype(BF16), p["gate_x_w"].astype(BF16)
    wf = dict(p["wf"])
    dist = shards is not None

    def ffn_forward(l, k, h, nw):
        riders = gather_riders([shards[(l + 1, k)]]) if dist and l + 1 < cfg.depth else None
        outs = ffn_fwd(cfg, h, nw, wf[(l, k)], 0, 2, f"ffn{k + 1}_fwd_{l}", riders=riders)
        if riders is not None:
            wf[(l + 1, k)] = outs[4].reshape(3, -1, cfg.d)
        return outs[:4]

    def blocks(g):
        g = g if g.ndim == 3 else g[None]
        return g.reshape(g.shape[0], N_DEV, g.shape[1] // N_DEV, g.shape[2])

    h = x
    saved = []
    for l in range(cfg.depth):
        i = l // 2
        s = {"h0": h}
        s["h1"], s["xn1"], s["g1"], s["u1"] = ffn_forward(l, 0, h, p["norm_ffn1"][l][None])
        if l % 2 == 0:
            s["um"], s["xnm"] = norm_proj(cfg, s["h1"], p["norm_mix"][l][None], p["even_in"], f"mix_in_{l}", wi=i)
            attn = attn_fwd(cfg, s["um"], bias, p["attn_sinks"][i], f"attn_fwd_{l}")
            c, s["cv"] = conv_fwd(cfg, s["um"], p["conv_b_w"][i], p["conv_b_b"][i][None], p["conv_ln_g"][i][None],
                                  p["conv_ln_b"][i][None], f"conv_fwd_{l}")
            s["parts"] = [attn, c]
            s["h2"] = proj_residual(cfg, s["h1"], s["parts"], p["even_out"], f"mix_out_{l}", wi=i)
        else:
            s["um"], s["xnm"] = norm_proj(cfg, s["h1"], p["norm_mix"][l][None], p["odd_in"], f"mix_in_{l}", wi=i)
            y, *s["saved"] = lru_fwd(cfg, s["um"], p["lru_conv_w"][i], p["lru_conv_b"][i][None], ga_w[i], p["gate_a_b"][i][None],
                                 gx_w[i], p["gate_x_b"][i][None], p["lru_lambda"][i][None], f"lru_fwd_{l}")
            s["parts"] = [y]
            s["h2"] = proj_residual(cfg, s["h1"], s["parts"], p["odd_out"], f"mix_out_{l}", wi=i)
        h, s["xn2"], s["g2"], s["u2"] = ffn_forward(l, 1, s["h2"], p["norm_ffn2"][l][None])
        saved.append(s)

    loss, dh, dnf = loss_head(cfg, h, p["norm_final"][None], tgt, "loss_head")
    big = [None] * cfg.depth
    sm = {k: [None] * cfg.depth for k in ("norm_ffn1", "norm_mix", "norm_ffn2")}
    ne, no = (cfg.depth + 1) // 2, cfg.depth // 2
    for k in ("attn_sinks", "conv_b_w", "conv_b_b", "conv_ln_g", "conv_ln_b", "dbias"):
        sm[k] = [None] * ne
    for k in ("lru_conv_w", "lru_conv_b", "gate_a_w", "gate_a_b", "gate_x_w", "gate_x_b", "lru_lambda"):
        sm[k] = [None] * no
    pending = None
    for l in reversed(range(cfg.depth)):
        i = l // 2
        s = saved[l]
        riders = scatter_riders([pending]) if pending is not None else None
        outs = ffn_bwd_x(cfg, dh, s["h2"], p["norm_ffn2"][l][None], s["g2"], s["u2"], wf[(l, 1)], 0, 2,
                         f"ffn2_bwd_x_{l}", riders=riders)
        dh, dout, dg, du, dn = outs[:5]
        if riders is not None:
            big[l + 1]["f1"] = (pending, outs[5])
        sm["norm_ffn2"][l] = dn[0]
        gf2 = blocks(ffn_bwd_w(cfg, s["xn2"], dout, s["g2"], s["u2"], dg, du, f"ffn2_bwd_w_{l}")[0])
        w_out = p["even_out"] if l % 2 == 0 else p["odd_out"]
        w_in = p["even_in"] if l % 2 == 0 else p["odd_in"]
        dcat = proj_bwd_act(cfg, dh, w_out, f"mix_out_bwd_{l}", wi=i)
        g_out = blocks(grad_weight(cfg, s["parts"], dh, f"mix_out_gw_{l}"))
        if l % 2 == 0:
            du_a, sm["dbias"][i], dsink = attn_bwd(cfg, s["um"], dcat, bias, p["attn_sinks"][i], f"attn_bwd_{l}")
            du_c, dcw, dvec = conv_bwd(cfg, s["um"], s["cv"], dcat, p["conv_b_w"][i], p["conv_b_b"][i][None],
                                       p["conv_ln_g"][i][None], p["conv_ln_b"][i][None], f"conv_bwd_{l}")
            sm["attn_sinks"][i] = dsink[:cfg.hq, 0]
            sm["conv_b_w"][i] = dcw[:cfg.cw]
            sm["conv_b_b"][i], sm["conv_ln_g"][i], sm["conv_ln_b"][i] = dvec[0], dvec[1], dvec[2]
            dparts = [du_a, du_c]
        else:
            dug, dur, dwa, dwx, dvec, dcw = lru_bwd(
                cfg, s["um"], s["saved"], dcat, p["lru_conv_w"][i], p["lru_conv_b"][i][None], ga_w[i], p["gate_a_b"][i][None],
                gx_w[i], p["gate_x_b"][i][None], p["lru_lambda"][i][None], f"lru_bwd_{l}")
            sm["gate_a_w"][i], sm["gate_x_w"][i] = dwa, dwx
            sm["gate_a_b"][i], sm["gate_x_b"][i], sm["lru_lambda"][i], sm["lru_conv_b"][i] = dvec[0], dvec[1], dvec[2], dvec[3]
            sm["lru_conv_w"][i] = dcw[:cfg.lcw]
            dparts = [dug, dur]
        g_in = blocks(grad_weight(cfg, dparts, s["xnm"], f"mix_in_gw_{l}"))
        dh, dn = norm_proj_bwd(cfg, dh, s["h1"], p["norm_mix"][l][None], dparts, w_in, f"mix_in_bwd_{l}", wi=i)
        sm["norm_mix"][l] = dn[0]
        riders = scatter_riders([gf2]) if dist else None
        outs = ffn_bwd_x(cfg, dh, s["h0"], p["norm_ffn1"][l][None], s["g1"], s["u1"], wf[(l, 0)], 0, 2,
                         f"ffn1_bwd_x_{l}", riders=riders)
        dh, dout, dg, du, dn = outs[:5]
        sm["norm_ffn1"][l] = dn[0]
        riders = scatter_riders([g_in, g_out]) if dist else None
        gouts = ffn_bwd_w(cfg, s["xn1"], dout, s["g1"], s["u1"], dg, du, f"ffn1_bwd_w_{l}", riders=riders)
        gf1 = blocks(gouts[0])
        big[l] = {"f1": (gf1, None), "f2": (gf2, outs[5] if dist else None),
                  "in": (g_in, gouts[1] if dist else None), "out": (g_out, gouts[2] if dist else None)}
        pending = gf1 if dist and l > 0 else None
    drb = bias_grad(cfg, jnp.stack(sm.pop("dbias")), buckets, "bias_grad")
    small = {k: jnp.stack(v) for k, v in sm.items()}
    small["rel_bias"] = drb[:, :, 0].T
    small["norm_final"] = dnf[0]
    return loss, dh, big, small


MESH = pl.DeviceIdType.MESH
ANY = pl.BlockSpec(memory_space=pl.ANY)


def _place():
    return lax.axis_index("x"), lax.axis_index("y"), lax.axis_index("c")


FLIPS = ((0, 0, 1), (0, 1, 0), (0, 1, 1), (1, 0, 0), (1, 0, 1), (1, 1, 0), (1, 1, 1))


def _peer(place, flip):
    return tuple(1 - v if f else v for v, f in zip(place, flip))


def _dev_index(place):
    return 4 * place[0] + 2 * place[1] + place[2]


def _remote(src, dst, send_sems, recv_sems, g, k, peer):
    return pltpu.make_async_remote_copy(src_ref=src, dst_ref=dst, send_sem=send_sems.at[g, k], recv_sem=recv_sems.at[g, k],
                                        device_id=peer, device_id_type=MESH)


def gather_riders(srcs):
    ng = len(srcs)

    def copies(in_refs, out_refs, sems):
        send_sems, recv_sems, local_sems = sems
        me = _place()
        local, sends, recvs = [], [], []
        for g in range(ng):
            mine = out_refs[g].at[:, _dev_index(me)]
            local.append(pltpu.make_async_copy(in_refs[g], mine, local_sems.at[g]))
            for k, flip in enumerate(FLIPS):
                peer = _peer(me, flip)
                sends.append(_remote(in_refs[g], mine, send_sems, recv_sems, g, k, peer))
                recvs.append(_remote(in_refs[g], out_refs[g].at[:, _dev_index(peer)], send_sems, recv_sems, g, k, peer))
        return local, sends, recvs

    def start(in_refs, out_refs, sems):
        local, sends, _ = copies(in_refs, out_refs, sems)
        for cp in local + sends:
            cp.start()

    def wait(in_refs, out_refs, sems):
        local, sends, recvs = copies(in_refs, out_refs, sems)
        for cp in sends:
            cp.wait_send()
        for cp in recvs:
            cp.wait_recv()
        for cp in local:
            cp.wait()

    return Riders(tuple(srcs), tuple(jax.ShapeDtypeStruct((s.shape[0], N_DEV) + s.shape[1:], s.dtype) for s in srcs),
                  (pltpu.SemaphoreType.DMA((ng, 7)), pltpu.SemaphoreType.DMA((ng, 7)), pltpu.SemaphoreType.DMA((ng,))),
                  start, wait)


def scatter_riders(bufs):
    ng = len(bufs)

    def copies(in_refs, out_refs, sems):
        send_sems, recv_sems = sems
        me = _place()
        return [_remote(in_refs[g].at[:, _dev_index(_peer(me, flip))], out_refs[g].at[:, k], send_sems, recv_sems, g, k,
                        _peer(me, flip)) for g in range(ng) for k, flip in enumerate(FLIPS)]

    def start(in_refs, out_refs, sems):
        for cp in copies(in_refs, out_refs, sems):
            cp.start()

    def wait(in_refs, out_refs, sems):
        for cp in copies(in_refs, out_refs, sems):
            cp.wait()

    return Riders(tuple(bufs), tuple(jax.ShapeDtypeStruct((b.shape[0], 7) + b.shape[2:], b.dtype) for b in bufs),
                  (pltpu.SemaphoreType.DMA((ng, 7)), pltpu.SemaphoreType.DMA((ng, 7))), start, wait)


def shard_sum(buf, recv, dev, name):
    n, _, r, cdim = buf.shape

    def body(dev_ref, a_ref, b_ref, o_ref):
        acc = a_ref[...].astype(F32)
        for k in range(7):
            acc = acc + b_ref[k].astype(F32)
        o_ref[...] = acc

    return pl.pallas_call(
        body, name=name,
        grid_spec=pltpu.PrefetchScalarGridSpec(
            num_scalar_prefetch=1, grid=(n,),
            in_specs=[pl.BlockSpec((None, None, r, cdim), lambda i, dev_ref: (i, dev_ref[0], 0, 0)),
                      pl.BlockSpec((None, 7, r, cdim), lambda i, dev_ref: (i, 0, 0, 0))],
            out_specs=pl.BlockSpec((None, r, cdim), lambda i, dev_ref: (i, 0, 0))),
        out_shape=jax.ShapeDtypeStruct((n, r, cdim), F32),
    )(dev, buf, recv)


def all_gather(srcs, name):
    ng = len(srcs)

    def body(*refs):
        x_refs, o_refs = refs[:ng], refs[ng:2 * ng]
        send_sems, recv_sems, local_sems = refs[2 * ng:]
        x, y, c = _place()
        me, sibling = (x, y, c), (x, y, 1 - c)
        chips = [(1 - x, y), (x, 1 - y), (1 - x, 1 - y)]

        def copy(gi, k, block, to, src=None):
            dst = o_refs[gi].at[:, 4 * block[0] + 2 * block[1] + block[2]]
            return pltpu.make_async_remote_copy(
                src_ref=dst if src is None else src, dst_ref=dst, send_sem=send_sems.at[gi, k],
                recv_sem=recv_sems.at[gi, k], device_id=to, device_id_type=MESH)

        mine = [pltpu.make_async_copy(x_refs[gi], o_refs[gi].at[:, 4 * x + 2 * y + c], local_sems.at[gi])
                for gi in range(ng)]
        for cp in mine:
            cp.start()
        first = []
        for gi in range(ng):
            first.append(copy(gi, 0, me, sibling, src=x_refs[gi]))
            first += [copy(gi, 1 + j, me, (*chip, c), src=x_refs[gi]) for j, chip in enumerate(chips)]
        for cp in first:
            cp.start()
        passed = []
        for j, chip in enumerate(chips):
            for gi in range(ng):
                copy(gi, 1 + j, (*chip, c), me).wait_recv()
                cp = copy(gi, 4 + j, (*chip, c), sibling)
                cp.start()
                passed.append(cp)
        for gi in range(ng):
            copy(gi, 0, sibling, me).wait_recv()
            for j, chip in enumerate(chips):
                copy(gi, 4 + j, (*chip, 1 - c), me).wait_recv()
        for cp in first + passed:
            cp.wait_send()
        for cp in mine:
            cp.wait()

    return pl.pallas_call(
        body, name=name,
        in_specs=[ANY] * ng, out_specs=[ANY] * ng,
        out_shape=[jax.ShapeDtypeStruct((s.shape[0], N_DEV) + s.shape[1:], s.dtype) for s in srcs],
        scratch_shapes=[pltpu.SemaphoreType.DMA((ng, 7)), pltpu.SemaphoreType.DMA((ng, 7)),
                        pltpu.SemaphoreType.DMA((ng,))],
    )(*srcs)


def pair_exchange(bufs, name):
    ng = len(bufs)

    def body(*refs):
        b_refs, o_refs = refs[:ng], refs[ng:2 * ng]
        send_sems, recv_sems = refs[2 * ng:]
        x, y, c = _place()
        copies = [pltpu.make_async_remote_copy(
            src_ref=b_refs[gi].at[:, :, 1 - c], dst_ref=o_refs[gi], send_sem=send_sems.at[gi], recv_sem=recv_sems.at[gi],
            device_id=(x, y, 1 - c), device_id_type=MESH) for gi in range(ng)]
        for cp in copies:
            cp.start()
        for cp in copies:
            cp.wait()

    return pl.pallas_call(
        body, name=name,
        in_specs=[ANY] * ng, out_specs=[ANY] * ng,
        out_shape=[jax.ShapeDtypeStruct(b.shape[:2] + b.shape[3:], b.dtype) for b in bufs],
        scratch_shapes=[pltpu.SemaphoreType.DMA((ng,)), pltpu.SemaphoreType.DMA((ng,))],
    )(*bufs)


def chip_exchange(qs, name):
    ng = len(qs)

    def body(*refs):
        q_refs, o_refs = refs[:ng], refs[ng:2 * ng]
        send_sems, recv_sems = refs[2 * ng:]
        x, y, c = _place()
        chips = [(1 - x, y), (x, 1 - y), (1 - x, 1 - y)]
        copies = [pltpu.make_async_remote_copy(
            src_ref=q_refs[gi].at[:, 2 * chip[0] + chip[1]], dst_ref=o_refs[gi].at[:, j],
            send_sem=send_sems.at[gi, j], recv_sem=recv_sems.at[gi, j],
            device_id=(*chip, c), device_id_type=MESH) for gi in range(ng) for j, chip in enumerate(chips)]
        for cp in copies:
            cp.start()
        for cp in copies:
            cp.wait()

    return pl.pallas_call(
        body, name=name,
        in_specs=[ANY] * ng, out_specs=[ANY] * ng,
        out_shape=[jax.ShapeDtypeStruct((q.shape[0], 3) + q.shape[2:], q.dtype) for q in qs],
        scratch_shapes=[pltpu.SemaphoreType.DMA((ng, 3)), pltpu.SemaphoreType.DMA((ng, 3))],
    )(*qs)


def pair_sum(buf, recv, core, name):
    n, _, _, r, cdim = buf.shape

    def body(core_ref, a_ref, b_ref, o_ref):
        o_ref[...] = (a_ref[...].astype(F32) + b_ref[...].astype(F32)).astype(BF16)

    blk = pl.BlockSpec((None, None, r, cdim), lambda i, k, core_ref: (i, k, 0, 0))
    return pl.pallas_call(
        body, name=name,
        grid_spec=pltpu.PrefetchScalarGridSpec(
            num_scalar_prefetch=1, grid=(n, 4),
            in_specs=[pl.BlockSpec((None, None, None, r, cdim), lambda i, k, core_ref: (i, k, core_ref[0], 0, 0)), blk],
            out_specs=blk),
        out_shape=jax.ShapeDtypeStruct((n, 4, r, cdim), BF16),
    )(core, buf, recv)


def chip_sum(q, recv, chip, name):
    n, _, r, cdim = q.shape

    def body(chip_ref, a_ref, b_ref, o_ref):
        acc = a_ref[...].astype(F32)
        for j in range(3):
            acc = acc + b_ref[j].astype(F32)
        o_ref[...] = acc

    return pl.pallas_call(
        body, name=name,
        grid_spec=pltpu.PrefetchScalarGridSpec(
            num_scalar_prefetch=1, grid=(n,),
            in_specs=[pl.BlockSpec((None, None, r, cdim), lambda i, chip_ref: (i, chip_ref[0], 0, 0)),
                      pl.BlockSpec((None, 3, r, cdim), lambda i, chip_ref: (i, 0, 0, 0))],
            out_specs=pl.BlockSpec((None, r, cdim), lambda i, chip_ref: (i, 0, 0))),
        out_shape=jax.ShapeDtypeStruct((n, r, cdim), F32),
    )(chip, q, recv)


def sum_blocks(a, name):
    def body(a_ref, o_ref):
        acc = a_ref[0]
        for d in range(1, a.shape[0]):
            acc = acc + a_ref[d]
        o_ref[...] = acc

    return pl.pallas_call(body, name=name, out_shape=jax.ShapeDtypeStruct(a.shape[1:], F32),
                          compiler_params=pltpu.CompilerParams(vmem_limit_bytes=VMEM_LIMIT))(a)


def adamw(w, g, m, v, name):
    c1 = 1.0 / (1.0 - ADAM_B1 ** ADAM_STEP)
    c2 = 1.0 / (1.0 - ADAM_B2 ** ADAM_STEP)

    def body(w_ref, g_ref, m_ref, v_ref, d_ref, mo_ref, vo_ref):
        gg = g_ref[...]
        m2 = ADAM_B1 * m_ref[...] + (1.0 - ADAM_B1) * gg
        v2 = ADAM_B2 * v_ref[...] + (1.0 - ADAM_B2) * (gg * gg)
        mo_ref[...] = m2
        vo_ref[...] = v2
        d_ref[...] = -ADAM_LR * ((m2 * c1) / (jnp.sqrt(v2 * c2) + ADAM_EPS) + ADAM_WD * w_ref[...])

    out_shape = [jax.ShapeDtypeStruct(w.shape, F32)] * 3
    if w.ndim == 2:
        return pl.pallas_call(body, name=name, out_shape=out_shape,
                              compiler_params=pltpu.CompilerParams(vmem_limit_bytes=VMEM_LIMIT))(w, g, m, v)
    blk = pl.BlockSpec((None,) + w.shape[1:], lambda i: (i, 0, 0))
    return pl.pallas_call(body, name=name, grid=(w.shape[0],), in_specs=[blk] * 4, out_specs=[blk] * 3,
                          out_shape=out_shape, compiler_params=_cparams("arbitrary"))(w, g, m, v)


WEIGHTS = ("norm_ffn1", "ffn1_wg", "ffn1_wu", "ffn1_wd", "norm_mix", "norm_ffn2", "ffn2_wg", "ffn2_wu", "ffn2_wd",
           "rel_bias", "even_w_in", "attn_sinks", "conv_b_w", "conv_b_b", "conv_ln_g", "conv_ln_b", "even_w_out",
           "odd_w_in", "lru_conv_w", "lru_conv_b", "gate_a_w", "gate_a_b", "gate_x_w", "gate_x_b", "lru_lambda",
           "odd_w_out", "norm_final")
BIG = ("ffn1_wg", "ffn1_wu", "ffn1_wd", "ffn2_wg", "ffn2_wu", "ffn2_wd", "even_w_in", "even_w_out", "odd_w_in", "odd_w_out")
SMALL = tuple(n for n in WEIGHTS if n not in BIG)
SMALL_SHARDED = ("conv_b_w", "lru_conv_w", "lru_conv_b", "gate_a_b", "gate_x_b", "lru_lambda")
PACK_ALIGN = 1024


def _pack(arrays):
    parts = []
    for a in arrays:
        flat = a.reshape(-1)
        parts.append(jnp.pad(flat, (0, -flat.shape[0] % PACK_ALIGN)))
    return jnp.concatenate(parts).reshape(-1, 128)


def _unpack(packed, shapes, lead=()):
    flat = packed.reshape(lead + (-1,))
    out, off = [], 0
    for shp in shapes:
        size = math.prod(shp)
        out.append(flat[..., off:off + size].reshape(lead + tuple(shp)))
        off += size + (-size % PACK_ALIGN)
    return out


def _unshard_last(blocks):
    nd = blocks.ndim
    moved = jnp.moveaxis(blocks, 0, nd - 2)
    return moved.reshape(moved.shape[:-2] + (-1,))


def _step(cfg, x, weights, loss_target, ms, vs):
    w = dict(zip(WEIGHTS, weights))
    m = dict(zip(WEIGHTS, ms))
    v = dict(zip(WEIGHTS, vs))
    px, py, pc = _place()
    dev = 4 * px + 2 * py + pc
    core = jnp.reshape(pc, (1,)).astype(jnp.int32)
    chip = jnp.reshape(2 * px + py, (1,)).astype(jnp.int32)
    d = cfg.d
    t = cfg.bl * cfg.s

    def rows(name):
        a = w[name]
        return (a if name.endswith(("wd", "w_out")) else a.transpose(0, 2, 1)).astype(BF16)

    r3 = {n: rows(n) for n in BIG[:6]}
    shards = {(l, k): jnp.stack([r3[f"ffn{k + 1}_{mat}"][l] for mat in ("wg", "wu", "wd")])
              for l in range(cfg.depth) for k in range(2)}
    small_src = _pack([w[n] for n in SMALL_SHARDED])[None]
    gathered = all_gather([shards.pop((0, 0)), shards.pop((0, 1)), rows("even_w_in"), rows("even_w_out"), rows("odd_w_in"),
                           rows("odd_w_out"), small_src], "all_gather_weights")
    full = [g.reshape(g.shape[0], -1, g.shape[-1]) for g in gathered[:6]]
    p = {n: w[n] for n in SMALL if n not in SMALL_SHARDED}
    p.update(wf={(0, 0): full[0], (0, 1): full[1]}, even_in=full[2], even_out=full[3], odd_in=full[4], odd_out=full[5])
    for n, blocks in zip(SMALL_SHARDED, _unpack(gathered[6][0], [w[n].shape for n in SMALL_SHARDED], lead=(N_DEV,))):
        p[n] = _unshard_last(blocks)

    lossp, gx, big, small = local_step(cfg, x.reshape(t, d), loss_target.reshape(t, d), p, shards)
    loss = lax.psum(lossp[0, 0], ("x", "y", "c"))

    dev1 = jnp.reshape(dev, (1,)).astype(jnp.int32)
    shard_rows = [{} for _ in range(cfg.depth)]
    for l in range(cfg.depth):
        for key, (buf, recv) in big[l].items():
            if recv is not None:
                shard_rows[l][key] = shard_sum(buf, recv, dev1, f"rs_sum_{key}_{l}")
    left = [(l, key, buf) for l in range(cfg.depth) for key, (buf, recv) in big[l].items() if recv is None]
    bufs = [buf.reshape(buf.shape[0], 4, 2, buf.shape[2], d) for _, _, buf in left]
    recv = pair_exchange(bufs, "rs_pair")
    qs = [pair_sum(b, r, core, f"rs_pair_sum{j}") for j, (b, r) in enumerate(zip(bufs, recv))]
    recv = chip_exchange(qs, "rs_chip")
    for j, ((l, key, _), q, r) in enumerate(zip(left, qs, recv)):
        shard_rows[l][key] = chip_sum(q, r, chip, f"rs_chip_sum{j}")

    g_rows = {}
    for k in range(2):
        ffn_g = jnp.stack([shard_rows[l][f"f{k + 1}"] for l in range(cfg.depth)])
        for j, mat in enumerate(("wg", "wu", "wd")):
            g_rows[f"ffn{k + 1}_{mat}"] = ffn_g[:, j]
    g_rows["even_w_in"] = jnp.stack([shard_rows[l]["in"][0] for l in range(0, cfg.depth, 2)])
    g_rows["even_w_out"] = jnp.stack([shard_rows[l]["out"][0] for l in range(0, cfg.depth, 2)])
    g_rows["odd_w_in"] = jnp.stack([shard_rows[l]["in"][0] for l in range(1, cfg.depth, 2)])
    g_rows["odd_w_out"] = jnp.stack([shard_rows[l]["out"][0] for l in range(1, cfg.depth, 2)])
    grads = {}

    full_shapes = [small[n].shape for n in SMALL]
    parts = all_gather([_pack([small[n] for n in SMALL])[None]], "all_gather_small_grads")[0][0]
    for n, g in zip(SMALL, _unpack(sum_blocks(parts, "sum_small_grads"), full_shapes)):
        if n in SMALL_SHARDED:
            width = w[n].shape[-1]
            g = lax.dynamic_slice_in_dim(g, dev * width, width, axis=g.ndim - 1)
        grads[n] = g

    delta, new_m, new_v = {}, {}, {}
    for n in BIG:
        if n.endswith(("wd", "w_out")):
            grads[n] = g_rows[n]
            delta[n], new_m[n], new_v[n] = adamw(w[n], grads[n], m[n], v[n], f"adamw_{n}")
        elif w[n].shape[-1] % 128 == 0:
            grads[n] = g_rows[n].transpose(0, 2, 1)
            delta[n], new_m[n], new_v[n] = adamw(w[n], grads[n], m[n], v[n], f"adamw_{n}")
        else:
            outs = adamw(w[n].transpose(0, 2, 1), g_rows[n], m[n].transpose(0, 2, 1), v[n].transpose(0, 2, 1), f"adamw_{n}")
            delta[n], new_m[n], new_v[n] = [o.transpose(0, 2, 1) for o in outs]
            grads[n] = g_rows[n].transpose(0, 2, 1)
    shapes = [w[n].shape for n in SMALL]
    packed = adamw(*[_pack([src[n] for n in SMALL]) for src in (w, grads, m, v)], "adamw_small")
    for out, pk in zip((delta, new_m, new_v), packed):
        out.update(zip(SMALL, _unpack(pk, shapes)))

    return (loss, gx.reshape(x.shape), *[grads[n] for n in WEIGHTS], *[delta[n] for n in WEIGHTS],
            *[new_m[n] for n in WEIGHTS], *[new_v[n] for n in WEIGHTS])


def kernel(x, norm_ffn1, ffn1_wg, ffn1_wu, ffn1_wd, norm_mix, norm_ffn2, ffn2_wg, ffn2_wu, ffn2_wd, rel_bias, even_w_in, attn_sinks, conv_b_w, conv_b_b, conv_ln_g, conv_ln_b, even_w_out, odd_w_in, lru_conv_w, lru_conv_b, gate_a_w, gate_a_b, gate_x_w, gate_x_b, lru_lambda, odd_w_out, norm_final, loss_target, m_norm_ffn1, m_ffn1_wg, m_ffn1_wu, m_ffn1_wd, m_norm_mix, m_norm_ffn2, m_ffn2_wg, m_ffn2_wu, m_ffn2_wd, m_rel_bias, m_even_w_in, m_attn_sinks, m_conv_b_w, m_conv_b_b, m_conv_ln_g, m_conv_ln_b, m_even_w_out, m_odd_w_in, m_lru_conv_w, m_lru_conv_b, m_gate_a_w, m_gate_a_b, m_gate_x_w, m_gate_x_b, m_lru_lambda, m_odd_w_out, m_norm_final, v_norm_ffn1, v_ffn1_wg, v_ffn1_wu, v_ffn1_wd, v_norm_mix, v_norm_ffn2, v_ffn2_wg, v_ffn2_wu, v_ffn2_wd, v_rel_bias, v_even_w_in, v_attn_sinks, v_conv_b_w, v_conv_b_b, v_conv_ln_g, v_conv_ln_b, v_even_w_out, v_odd_w_in, v_lru_conv_w, v_lru_conv_b, v_gate_a_w, v_gate_a_b, v_gate_x_w, v_gate_x_b, v_lru_lambda, v_odd_w_out, v_norm_final):
    weights = (norm_ffn1, ffn1_wg, ffn1_wu, ffn1_wd, norm_mix, norm_ffn2, ffn2_wg, ffn2_wu, ffn2_wd, rel_bias, even_w_in, attn_sinks, conv_b_w, conv_b_b, conv_ln_g, conv_ln_b, even_w_out, odd_w_in, lru_conv_w, lru_conv_b, gate_a_w, gate_a_b, gate_x_w, gate_x_b, lru_lambda, odd_w_out, norm_final)
    ms = (m_norm_ffn1, m_ffn1_wg, m_ffn1_wu, m_ffn1_wd, m_norm_mix, m_norm_ffn2, m_ffn2_wg, m_ffn2_wu, m_ffn2_wd, m_rel_bias, m_even_w_in, m_attn_sinks, m_conv_b_w, m_conv_b_b, m_conv_ln_g, m_conv_ln_b, m_even_w_out, m_odd_w_in, m_lru_conv_w, m_lru_conv_b, m_gate_a_w, m_gate_a_b, m_gate_x_w, m_gate_x_b, m_lru_lambda, m_odd_w_out, m_norm_final)
    vs = (v_norm_ffn1, v_ffn1_wg, v_ffn1_wu, v_ffn1_wd, v_norm_mix, v_norm_ffn2, v_ffn2_wg, v_ffn2_wu, v_ffn2_wd, v_rel_bias, v_even_w_in, v_attn_sinks, v_conv_b_w, v_conv_b_b, v_conv_ln_g, v_conv_ln_b, v_even_w_out, v_odd_w_in, v_lru_conv_w, v_lru_conv_b, v_gate_a_w, v_gate_a_b, v_gate_x_w, v_gate_x_b, v_lru_lambda, v_odd_w_out, v_norm_final)
    return _step(Cfg(), x, weights, loss_target, ms, vs)
```

```python
import math
from typing import NamedTuple

import jax
import jax.numpy as jnp
from jax import lax
from jax.experimental import pallas as pl
from jax.experimental.pallas import tpu as pltpu

F32 = jnp.float32
BF16 = jnp.bfloat16
RMS_EPS = 1e-6
LN_EPS = 1e-5
NEG_INF = -1e30
RG_LRU_C = 8.0
ADAM_LR = 0.001
ADAM_B1 = 0.9
ADAM_B2 = 0.999
ADAM_EPS = 1e-08
ADAM_WD = 0.01
ADAM_STEP = 10
N_DEV = 8
VMEM_LIMIT = 56 * 1024 * 1024


class Cfg(NamedTuple):
    d: int = 1024
    f: int = 2816
    s: int = 2048
    bl: int = 4
    hq: int = 8
    hkv: int = 2
    hd: int = 64
    win: int = 128
    cc: int = 512
    cw: int = 31
    lh: int = 8
    lb: int = 128
    lcw: int = 4
    nbuckets: int = 32
    max_dist: int = 128
    depth: int = 4
    tm: int = 512
    tm_ffn: int = 1024
    tf: int = 256
    tk_ffn: int = 512
    tf_w: int = 1408
    tk: int = 1024
    ct_f: int = 256
    ct_b: int = 256

    @property
    def qw(self):
        return self.hq * self.hd

    @property
    def kvw(self):
        return self.hkv * self.hd

    @property
    def even_in(self):
        return self.qw + 2 * self.kvw + 2 * self.cc

    @property
    def even_cat(self):
        return self.qw + self.cc

    @property
    def lw(self):
        return self.lh * self.lb


def _cparams(*sem):
    return pltpu.CompilerParams(dimension_semantics=sem, vmem_limit_bytes=VMEM_LIMIT)


def _nt(a, b):
    return lax.dot_general(a, b, (((1,), (1,)), ((), ())), preferred_element_type=F32)


def _nn(a, b):
    return lax.dot_general(a, b, (((1,), (0,)), ((), ())), preferred_element_type=F32)


def _tn(a, b):
    return lax.dot_general(a, b, (((0,), (0,)), ((), ())), preferred_element_type=F32)


def _rstd(h):
    return lax.rsqrt(jnp.mean(h * h, axis=-1, keepdims=True) + RMS_EPS)


def _rms_bwd(h, nw, dxn):
    rstd = _rstd(h)
    dyg = dxn * nw
    dnw = jnp.sum(dxn * h * rstd, axis=0, keepdims=True)
    dx = rstd * (dyg - h * (rstd * rstd) * jnp.mean(dyg * h, axis=-1, keepdims=True))
    return dx, dnw


def _sigmoid(x):
    return 0.5 * jnp.tanh(0.5 * x) + 0.5


FFN_SLABS = 4


def _ffn_wspecs(tf, d, gu, md):
    return [pl.BlockSpec((2, tf, d), lambda i, j: (gu, j, 0)), pl.BlockSpec((None, tf, d), lambda i, j: (md, j, 0))]


class Riders(NamedTuple):
    inputs: tuple
    out_shape: tuple
    scratch: tuple
    start: object
    wait: object


def _ride(riders, grid, n_in, n_out, body):
    if riders is None:
        return body, [], [], [], [], []
    ni, no, ns = len(riders.inputs), len(riders.out_shape), len(riders.scratch)

    def full(*refs):
        ins, rin = refs[:n_in], refs[n_in:n_in + ni]
        outs = refs[n_in + ni:n_in + ni + n_out]
        rout = refs[n_in + ni + n_out:n_in + ni + n_out + no]
        rest = refs[n_in + ni + n_out + no:]
        scratch, sems = rest[:len(rest) - ns], rest[len(rest) - ns:]
        first = last = None
        for axis, size in enumerate(grid):
            pid = pl.program_id(axis)
            first = (pid == 0) if first is None else first & (pid == 0)
            last = (pid == size - 1) if last is None else last & (pid == size - 1)

        @pl.when(first)
        def _():
            riders.start(rin, rout, sems)

        body(*ins, *outs, *scratch)

        @pl.when(last)
        def _():
            riders.wait(rin, rout, sems)

    any_spec = pl.BlockSpec(memory_space=pl.ANY)
    return full, list(riders.inputs), [any_spec] * ni, [any_spec] * no, list(riders.out_shape), list(riders.scratch)


def ffn_fwd(cfg, h, nw, wts, gu, md, name, riders=None):
    t, d = h.shape
    f = wts.shape[1]
    tm, tf = cfg.tm_ffn, cfg.tf
    nj = f // tf

    def body(h_ref, nw_ref, wgu_ref, wd_ref, ho_ref, xn_ref, g_ref, u_ref, acc_ref):
        j = pl.program_id(1)

        @pl.when(j == 0)
        def _():
            hh = h_ref[...]
            xn_ref[...] = (hh * _rstd(hh) * nw_ref[...]).astype(BF16)
            acc_ref[...] = jnp.zeros_like(acc_ref)

        gu = _nt(xn_ref[...], wgu_ref[...].reshape(2 * tf, d))
        g, u = gu[:, :tf], gu[:, tf:]
        g_ref[...] = g.astype(BF16)
        u_ref[...] = u.astype(BF16)
        acc_ref[...] += _nn((g * _sigmoid(g) * u).astype(BF16), wd_ref[...])

        @pl.when(j == nj - 1)
        def _():
            ho_ref[...] = h_ref[...] + 0.5 * acc_ref[...]

    row = pl.BlockSpec((tm, d), lambda i, j: (i, 0))
    hid = pl.BlockSpec((tm, tf), lambda i, j: (i, j))
    grid = (t // tm, nj)
    full, r_args, r_in, r_out, r_shape, r_scratch = _ride(riders, grid, 4, 4, body)
    return pl.pallas_call(
        full, name=name, grid=grid,
        in_specs=[row, pl.BlockSpec((1, d), lambda i, j: (0, 0))] + _ffn_wspecs(tf, d, gu, md) + r_in,
        out_specs=[row, row, hid, hid] + r_out,
        out_shape=[jax.ShapeDtypeStruct((t, d), F32), jax.ShapeDtypeStruct((t, d), BF16),
                   jax.ShapeDtypeStruct((t, f), BF16), jax.ShapeDtypeStruct((t, f), BF16)] + r_shape,
        scratch_shapes=[pltpu.VMEM((tm, d), F32)] + r_scratch,
        compiler_params=_cparams("arbitrary", "arbitrary"),
    )(h, nw, wts, wts, *r_args)


def ffn_bwd_x(cfg, dh, h, nw, g, u, wts, gu, md, name, riders=None):
    t, d = h.shape
    f = wts.shape[1]
    tm, tf = cfg.tm_ffn, cfg.tf
    nj = f // tf

    def body(dh_ref, h_ref, nw_ref, g_ref, u_ref, wgu_ref, wd_ref,
             dho_ref, dout_ref, dg_ref, du_ref, dnw_ref, acc_ref, da_ref):
        i, j = pl.program_id(0), pl.program_id(1)

        @pl.when(j == 0)
        def _():
            dout_ref[...] = (0.5 * dh_ref[...]).astype(BF16)
            acc_ref[...] = jnp.zeros_like(acc_ref)
            da_ref[1] = jnp.zeros((tm, tf), F32)

        @pl.when((i == 0) & (j == 0))
        def _():
            dnw_ref[...] = jnp.zeros_like(dnw_ref)

        slot = lax.rem(j, 2)
        da = da_ref[1 - slot]
        da_ref[slot] = _nt(dout_ref[...], wd_ref[...])
        gg = g_ref[...].astype(F32)
        sig = _sigmoid(gg)
        dg = (da * u_ref[...].astype(F32) * (sig * (1.0 + gg * (1.0 - sig)))).astype(BF16)
        du = (da * (gg * sig)).astype(BF16)
        dg_ref[...] = dg
        du_ref[...] = du
        acc_ref[...] += _nn(jnp.concatenate([dg, du], axis=1), wgu_ref[...].reshape(2 * tf, d))

        @pl.when(j == nj)
        def _():
            dx, dnw = _rms_bwd(h_ref[...], nw_ref[...], acc_ref[...])
            dnw_ref[0:1, :] += dnw
            dho_ref[...] = dh_ref[...] + dx

    row = pl.BlockSpec((tm, d), lambda i, j: (i, 0))
    prev = pl.BlockSpec((tm, tf), lambda i, j: (i, jnp.maximum(j - 1, 0)))
    wspecs = [pl.BlockSpec((2, tf, d), lambda i, j: (gu, jnp.maximum(j - 1, 0), 0)),
              pl.BlockSpec((None, tf, d), lambda i, j: (md, jnp.minimum(j, nj - 1), 0))]
    grid = (t // tm, nj + 1)
    full, r_args, r_in, r_out, r_shape, r_scratch = _ride(riders, grid, 7, 5, body)
    return pl.pallas_call(
        full, name=name, grid=grid,
        in_specs=[row, row, pl.BlockSpec((1, d), lambda i, j: (0, 0)), prev, prev] + wspecs + r_in,
        out_specs=[row, row, prev, prev, pl.BlockSpec((8, d), lambda i, j: (0, 0))] + r_out,
        out_shape=[jax.ShapeDtypeStruct((t, d), F32), jax.ShapeDtypeStruct((t, d), BF16),
                   jax.ShapeDtypeStruct((t, f), BF16), jax.ShapeDtypeStruct((t, f), BF16),
                   jax.ShapeDtypeStruct((8, d), F32)] + r_shape,
        scratch_shapes=[pltpu.VMEM((tm, d), F32), pltpu.VMEM((2, tm, tf), F32)] + r_scratch,
        compiler_params=_cparams("arbitrary", "arbitrary"),
    )(dh, h, nw, g, u, wts, wts, *r_args)


def ffn_bwd_w(cfg, xn, dout, g, u, dg, du, name, riders=None):
    t, d = xn.shape
    f = g.shape[1]
    tk, tf = cfg.tk_ffn, cfg.tf_w
    nk = t // tk

    def body(xn_ref, dout_ref, g_ref, u_ref, dg_ref, du_ref, o_ref, acc_ref):
        k = pl.program_id(1)

        @pl.when(k == 0)
        def _():
            acc_ref[...] = jnp.zeros_like(acc_ref)

        gg = g_ref[...].astype(F32)
        a = (gg * _sigmoid(gg) * u_ref[...].astype(F32)).astype(BF16)
        xn_t = xn_ref[...]
        acc_ref[0] += _tn(dg_ref[...], xn_t)
        acc_ref[1] += _tn(du_ref[...], xn_t)
        acc_ref[2] += _tn(a, dout_ref[...])

        @pl.when(k == nk - 1)
        def _():
            o_ref[...] = acc_ref[...].astype(BF16)

    row = pl.BlockSpec((tk, d), lambda j, k: (k, 0))
    hid = pl.BlockSpec((tk, tf), lambda j, k: (k, j))
    grid = (f // tf, nk)
    full, r_args, r_in, r_out, r_shape, r_scratch = _ride(riders, grid, 6, 1, body)
    return pl.pallas_call(
        full, name=name, grid=grid,
        in_specs=[row, row, hid, hid, hid, hid] + r_in,
        out_specs=[pl.BlockSpec((3, tf, d), lambda j, k: (0, j, 0), pipeline_mode=pl.Buffered(1))] + r_out,
        out_shape=[jax.ShapeDtypeStruct((3, f, d), BF16)] + r_shape,
        scratch_shapes=[pltpu.VMEM((3, tf, d), F32)] + r_scratch,
        compiler_params=_cparams("arbitrary", "arbitrary"),
    )(xn, dout, g, u, dg, du, *r_args)


def _wspec(w, wi):
    if w.ndim == 2:
        return pl.BlockSpec(w.shape, lambda i: (0, 0))
    return pl.BlockSpec((None,) + w.shape[1:], lambda i: (wi, 0, 0))


def norm_proj(cfg, h, nw, w, name, wi=0):
    t, d = h.shape
    n = w.shape[-2]
    tm = cfg.tm

    def body(h_ref, nw_ref, w_ref, u_ref, xn_ref):
        hh = h_ref[...]
        xn = (hh * _rstd(hh) * nw_ref[...]).astype(BF16)
        xn_ref[...] = xn
        u_ref[...] = _nt(xn, w_ref[...]).astype(BF16)

    return pl.pallas_call(
        body, name=name, grid=(t // tm,),
        in_specs=[pl.BlockSpec((tm, d), lambda i: (i, 0)), pl.BlockSpec((1, d), lambda i: (0, 0)),
                  _wspec(w, wi)],
        out_specs=[pl.BlockSpec((tm, n), lambda i: (i, 0)), pl.BlockSpec((tm, d), lambda i: (i, 0))],
        out_shape=[jax.ShapeDtypeStruct((t, n), BF16), jax.ShapeDtypeStruct((t, d), BF16)],
        compiler_params=_cparams("arbitrary"),
    )(h, nw, w)


def proj_residual(cfg, h, parts, w, name, wi=0):
    t, d = h.shape
    tm = cfg.tm
    ks = [p.shape[1] for p in parts]
    offs = [sum(ks[:i]) for i in range(len(ks))]
    np_ = len(parts)

    def body(*refs):
        h_ref, w_ref, ho_ref = refs[0], refs[1 + np_], refs[2 + np_]
        acc = h_ref[...]
        for p_ref, off, k in zip(refs[1:1 + np_], offs, ks):
            acc = acc + _nn(p_ref[...], w_ref[off:off + k, :])
        ho_ref[...] = acc

    return pl.pallas_call(
        body, name=name, grid=(t // tm,),
        in_specs=[pl.BlockSpec((tm, d), lambda i: (i, 0))]
        + [pl.BlockSpec((tm, k), lambda i: (i, 0)) for k in ks]
        + [_wspec(w, wi)],
        out_specs=pl.BlockSpec((tm, d), lambda i: (i, 0)),
        out_shape=jax.ShapeDtypeStruct((t, d), F32),
        compiler_params=_cparams("arbitrary"),
    )(h, *parts, w)


def proj_bwd_act(cfg, dh, w, name, wi=0):
    t, d = dh.shape
    k = w.shape[-2]
    tm = cfg.tm

    def body(dh_ref, w_ref, o_ref):
        o_ref[...] = _nt(dh_ref[...].astype(BF16), w_ref[...]).astype(BF16)

    return pl.pallas_call(
        body, name=name, grid=(t // tm,),
        in_specs=[pl.BlockSpec((tm, d), lambda i: (i, 0)), _wspec(w, wi)],
        out_specs=pl.BlockSpec((tm, k), lambda i: (i, 0)),
        out_shape=jax.ShapeDtypeStruct((t, k), BF16),
        compiler_params=_cparams("arbitrary"),
    )(dh, w)


def grad_weight(cfg, parts, b, name):
    t, d = b.shape
    tk = cfg.tk
    nk = t // tk
    ks = [p.shape[1] for p in parts]
    offs = [sum(ks[:i]) for i in range(len(ks))]
    np_ = len(parts)

    def body(*refs):
        b_ref, o_ref, acc_ref = refs[np_:]
        kk = pl.program_id(0)

        @pl.when(kk == 0)
        def _():
            acc_ref[...] = jnp.zeros_like(acc_ref)

        bb = b_ref[...].astype(BF16)
        for a_ref, off, k in zip(refs[:np_], offs, ks):
            acc_ref[off:off + k, :] += _tn(a_ref[...], bb)

        @pl.when(kk == nk - 1)
        def _():
            o_ref[...] = acc_ref[...].astype(BF16)

    return pl.pallas_call(
        body, name=name, grid=(nk,),
        in_specs=[pl.BlockSpec((tk, k), lambda kk: (kk, 0)) for k in ks] + [pl.BlockSpec((tk, d), lambda kk: (kk, 0))],
        out_specs=pl.BlockSpec((sum(ks), d), lambda kk: (0, 0)),
        out_shape=jax.ShapeDtypeStruct((sum(ks), d), BF16),
        scratch_shapes=[pltpu.VMEM((sum(ks), d), F32)],
        compiler_params=_cparams("arbitrary"),
    )(*parts, b)


def norm_proj_bwd(cfg, dh, h, nw, parts, w, name, wi=0):
    t, d = h.shape
    tm = cfg.tm
    ks = [p.shape[1] for p in parts]
    offs = [sum(ks[:i]) for i in range(len(ks))]
    np_ = len(parts)

    def body(*refs):
        dh_ref, h_ref, nw_ref = refs[:3]
        w_ref, dho_ref, dnw_ref = refs[3 + np_:]

        @pl.when(pl.program_id(0) == 0)
        def _():
            dnw_ref[...] = jnp.zeros_like(dnw_ref)

        dxn = None
        for p_ref, off, k in zip(refs[3:3 + np_], offs, ks):
            term = _nn(p_ref[...], w_ref[off:off + k, :])
            dxn = term if dxn is None else dxn + term
        dx, dnw = _rms_bwd(h_ref[...], nw_ref[...], dxn)
        dnw_ref[0:1, :] += dnw
        dho_ref[...] = dh_ref[...] + dx

    row = pl.BlockSpec((tm, d), lambda i: (i, 0))
    return pl.pallas_call(
        body, name=name, grid=(t // tm,),
        in_specs=[row, row, pl.BlockSpec((1, d), lambda i: (0, 0))]
        + [pl.BlockSpec((tm, k), lambda i: (i, 0)) for k in ks]
        + [_wspec(w, wi)],
        out_specs=[row, pl.BlockSpec((8, d), lambda i: (0, 0))],
        out_shape=[jax.ShapeDtypeStruct((t, d), F32), jax.ShapeDtypeStruct((8, d), F32)],
        compiler_params=_cparams("arbitrary"),
    )(dh, h, nw, *parts, w)


def loss_head(cfg, h, nf, tgt, name):
    t, d = h.shape
    tm = cfg.tm

    def body(h_ref, nf_ref, tgt_ref, loss_ref, dh_ref, dnf_ref):
        @pl.when(pl.program_id(0) == 0)
        def _():
            loss_ref[...] = jnp.zeros_like(loss_ref)
            dnf_ref[...] = jnp.zeros_like(dnf_ref)

        hh = h_ref[...]
        err = hh * _rstd(hh) * nf_ref[...] - tgt_ref[...]
        row = jnp.sum(err * err, axis=-1, keepdims=True) * (0.5 / d)
        loss_ref[...] += jnp.sum(row, axis=0, keepdims=True)
        dx, dnf = _rms_bwd(hh, nf_ref[...], err * (1.0 / d))
        dnf_ref[0:1, :] += dnf
        dh_ref[...] = dx

    row = pl.BlockSpec((tm, d), lambda i: (i, 0))
    return pl.pallas_call(
        body, name=name, grid=(t // tm,),
        in_specs=[row, pl.BlockSpec((1, d), lambda i: (0, 0)), row],
        out_specs=[pl.BlockSpec((8, 128), lambda i: (0, 0)), row, pl.BlockSpec((8, d), lambda i: (0, 0))],
        out_shape=[jax.ShapeDtypeStruct((8, 128), F32), jax.ShapeDtypeStruct((t, d), F32),
                   jax.ShapeDtypeStruct((8, d), F32)],
        compiler_params=_cparams("arbitrary"),
    )(h, nf, tgt)


def bucket_table(cfg):
    qi = jnp.arange(cfg.win)[:, None]
    sj = jnp.arange(2 * cfg.win)[None, :]
    dist = qi + cfg.win - sj
    n = jnp.maximum(dist, 0)
    max_exact = cfg.nbuckets // 2
    nf = jnp.maximum(n, max_exact).astype(F32)
    large = max_exact + (jnp.log(nf / max_exact) / math.log(cfg.max_dist / max_exact)
                         * (cfg.nbuckets - max_exact)).astype(jnp.int32)
    large = jnp.minimum(large, cfg.nbuckets - 1)
    bucket = jnp.where(n < max_exact, n, large)
    return jnp.where((dist >= 0) & (dist < cfg.win), bucket, -1).astype(jnp.int32)


def bias_build(cfg, rel_bias, buckets, name):
    w = cfg.win

    def body(rb_ref, bk_ref, o_ref):
        bk = bk_ref[...]
        for h in range(cfg.hq):
            acc = jnp.full((w, 2 * w), NEG_INF, F32)
            for b in range(cfg.nbuckets):
                acc = jnp.where(bk == b, rb_ref[b, h], acc)
            o_ref[h] = acc

    return pl.pallas_call(
        body, name=name,
        in_specs=[pl.BlockSpec(memory_space=pltpu.SMEM), pl.BlockSpec(memory_space=pltpu.VMEM)],
        out_specs=pl.BlockSpec(memory_space=pltpu.VMEM),
        out_shape=jax.ShapeDtypeStruct((cfg.hq, w, 2 * w), F32),
    )(rel_bias, buckets)


def bias_grad(cfg, dbias, buckets, name):
    w = cfg.win

    def body(db_ref, bk_ref, o_ref, rows_ref):
        bk = bk_ref[...]
        for h in range(cfg.hq):
            d = db_ref[0, h]
            for e in range(1, dbias.shape[0]):
                d = d + db_ref[e, h]
            for b in range(cfg.nbuckets):
                rows_ref[b:b + 1, :] = jnp.sum(jnp.where(bk == b, d, 0.0), axis=0, keepdims=True)
            o_ref[h] = jnp.broadcast_to(jnp.sum(rows_ref[...], axis=1, keepdims=True), (cfg.nbuckets, 128))

    return pl.pallas_call(
        body, name=name,
        in_specs=[pl.BlockSpec(memory_space=pltpu.VMEM), pl.BlockSpec(memory_space=pltpu.VMEM)],
        out_specs=pl.BlockSpec(memory_space=pltpu.VMEM),
        out_shape=jax.ShapeDtypeStruct((cfg.hq, cfg.nbuckets, 128), F32),
        scratch_shapes=[pltpu.VMEM((cfg.nbuckets, 2 * w), F32)],
    )(dbias, buckets)


def _attn_probs(cfg, qk, bias_h, sink, first_ok):
    s = qk * (1.0 / math.sqrt(cfg.hd)) + bias_h
    s = jnp.where(first_ok, s, NEG_INF)
    m = jnp.maximum(jnp.max(s, axis=-1, keepdims=True), sink)
    e = jnp.exp(s - m)
    es = jnp.exp(sink - m)
    inv = 1.0 / (jnp.sum(e, axis=-1, keepdims=True) + es)
    return e * inv, es * inv


def _attn_block_inputs(cfg, n, q_ref, kv_ref):
    w = cfg.win
    r0 = pl.multiple_of(n * w, w)
    rp = pl.multiple_of(jnp.maximum(n - 1, 0) * w, w)
    qb = q_ref[pl.ds(r0, w), :]
    kk = jnp.concatenate([kv_ref[pl.ds(rp, w), :], kv_ref[pl.ds(r0, w), :]], axis=0)
    col = lax.broadcasted_iota(jnp.int32, (w, 2 * w), 1)
    first_ok = (n > 0) | (col >= w)
    return r0, rp, qb, kk, first_ok


def _kv_col_block(cfg):
    assert cfg.qw % (2 * cfg.kvw) == 0
    return cfg.qw // (2 * cfg.kvw)


def attn_fwd(cfg, u, bias, sinks, name):
    t = u.shape[0]
    s, w, hd, g = cfg.s, cfg.win, cfg.hd, cfg.hq // cfg.hkv
    kvb = _kv_col_block(cfg)

    def body(q_ref, kv_ref, bias_ref, sink_ref, o_ref):
        def blk(n, carry):
            r0, _, qb, kk, first_ok = _attn_block_inputs(cfg, n, q_ref, kv_ref)
            heads = range(cfg.hq)
            scores = [_nt(qb[:, hd * h:hd * (h + 1)], kk[:, hd * (h // g):hd * (h // g + 1)]) for h in heads]
            probs = [_attn_probs(cfg, scores[h], bias_ref[h], sink_ref[h], first_ok)[0].astype(BF16) for h in heads]
            outs = [_nn(probs[h], kk[:, cfg.kvw + hd * (h // g):cfg.kvw + hd * (h // g + 1)]) for h in heads]
            o_ref[pl.ds(r0, w), :] = jnp.concatenate(outs, axis=1).astype(BF16)
            return carry

        lax.fori_loop(0, s // w, blk, 0)

    return pl.pallas_call(
        body, name=name, grid=(t // s,),
        in_specs=[pl.BlockSpec((s, cfg.qw), lambda b: (b, 0)), pl.BlockSpec((s, 2 * cfg.kvw), lambda b: (b, kvb)),
                  pl.BlockSpec(bias.shape, lambda b: (0, 0, 0)), pl.BlockSpec(memory_space=pltpu.SMEM)],
        out_specs=pl.BlockSpec((s, cfg.qw), lambda b: (b, 0)),
        out_shape=jax.ShapeDtypeStruct((t, cfg.qw), BF16),
        compiler_params=_cparams("arbitrary"),
    )(u, u, bias, sinks)


def attn_bwd(cfg, u, dcat, bias, sinks, name):
    t = u.shape[0]
    s, w, hd, g = cfg.s, cfg.win, cfg.hd, cfg.hq // cfg.hkv
    kvb = _kv_col_block(cfg)
    scale = 1.0 / math.sqrt(hd)
    assert cfg.hq <= 8

    def body(q_ref, kv_ref, do_ref, bias_ref, sink_ref, du_ref, dbias_ref, dsink_ref, dkv_ref):
        @pl.when(pl.program_id(0) == 0)
        def _():
            dbias_ref[...] = jnp.zeros_like(dbias_ref)
            dsink_ref[...] = jnp.zeros_like(dsink_ref)

        dkv_ref[...] = jnp.zeros_like(dkv_ref)

        def blk(n, carry):
            r0, rp, qb, kk, first_ok = _attn_block_inputs(cfg, n, q_ref, kv_ref)
            dob = do_ref[pl.ds(r0, w), :]
            heads = range(cfg.hq)
            kjs = [kk[:, hd * j:hd * (j + 1)] for j in range(cfg.hkv)]
            vjs = [kk[:, cfg.kvw + hd * j:cfg.kvw + hd * (j + 1)] for j in range(cfg.hkv)]
            qhs = [qb[:, hd * h:hd * (h + 1)] for h in heads]
            dohs = [dob[:, hd * h:hd * (h + 1)] for h in heads]
            scores = [_nt(qhs[h], kjs[h // g]) for h in heads]
            dps = [_nt(dohs[h], vjs[h // g]) for h in heads]
            pbs, dsbs = [], []
            for h in heads:
                p, ps = _attn_probs(cfg, scores[h], bias_ref[h], sink_ref[h], first_ok)
                delta = jnp.sum(p * dps[h], axis=-1, keepdims=True)
                ds = p * (dps[h] - delta)
                dsink_ref[h:h + 1, :] += jnp.broadcast_to(-jnp.sum(ps * delta, axis=0, keepdims=True), (1, 128))
                dbias_ref[h] += ds
                pbs.append(p.astype(BF16))
                dsbs.append(ds.astype(BF16))
            dqs = []
            dks = [jnp.zeros((2 * w, hd), F32) for _ in range(cfg.hkv)]
            dvs = [jnp.zeros((2 * w, hd), F32) for _ in range(cfg.hkv)]
            for h in heads:
                dqs.append(_nn(dsbs[h], kjs[h // g]) * scale)
                dks[h // g] = dks[h // g] + _tn(dsbs[h], qhs[h]) * scale
                dvs[h // g] = dvs[h // g] + _tn(pbs[h], dohs[h])
            du_ref[pl.ds(r0, w), 0:cfg.qw] = jnp.concatenate(dqs, axis=1).astype(BF16)
            dkv = jnp.concatenate(dks + dvs, axis=1)
            dkv_ref[pl.ds(rp, w), :] += dkv[:w]
            dkv_ref[pl.ds(r0, w), :] += dkv[w:]
            return carry

        lax.fori_loop(0, s // w, blk, 0)
        du_ref[:, cfg.qw:] = dkv_ref[...].astype(BF16)

    wa = cfg.qw + 2 * cfg.kvw
    return pl.pallas_call(
        body, name=name, grid=(t // s,),
        in_specs=[pl.BlockSpec((s, cfg.qw), lambda b: (b, 0)), pl.BlockSpec((s, 2 * cfg.kvw), lambda b: (b, kvb)),
                  pl.BlockSpec((s, cfg.qw), lambda b: (b, 0)),
                  pl.BlockSpec(bias.shape, lambda b: (0, 0, 0)), pl.BlockSpec(memory_space=pltpu.SMEM)],
        out_specs=[pl.BlockSpec((s, wa), lambda b: (b, 0)), pl.BlockSpec(bias.shape, lambda b: (0, 0, 0)),
                   pl.BlockSpec((8, 128), lambda b: (0, 0))],
        out_shape=[jax.ShapeDtypeStruct((t, wa), BF16), jax.ShapeDtypeStruct(bias.shape, F32),
                   jax.ShapeDtypeStruct((8, 128), F32)],
        scratch_shapes=[pltpu.VMEM((s, 2 * cfg.kvw), F32)],
        compiler_params=_cparams("arbitrary"),
    )(u, u, dcat, bias, sinks)


def _shift_views(win, rc, pad):
    return [win] + [win[j:j + rc + pad - 8] for j in range(1, 8)]


def _tap(views, off, rc):
    a = 8 * (off // 8)
    return views[off % 8][a:a + rc]


def _conv_rows(views, w_ref, cw, pad, rc, lanes=slice(None)):
    acc = None
    for k in range(cw):
        term = _tap(views, pad - (cw - 1) + k, rc) * w_ref[k:k + 1, lanes]
        acc = term if acc is None else acc + term
    return acc


def _conv_rows_t(views, w_ref, cw, rc, lanes=slice(None)):
    acc = None
    for k in range(cw):
        term = _tap(views, cw - 1 - k, rc) * w_ref[k:k + 1, lanes]
        acc = term if acc is None else acc + term
    return acc


def _group_sum(x):
    acc = x[0:8]
    for i in range(1, x.shape[0] // 8):
        acc = acc + x[8 * i:8 * i + 8]
    return acc


def _conv_wgrad(views, dy, acc_ref, cw, pad, rc, lanes=slice(None)):
    for k in range(cw):
        acc_ref[k, :, lanes] += _group_sum(dy * _tap(views, pad - (cw - 1) + k, rc))


LANE_TILE = 128


def _lane_tiles(width):
    return [slice(c0, c0 + LANE_TILE) for c0 in range(0, width, LANE_TILE)]


def _conv_rows_tiled(x_ref, r0, w_ref, cw, pad, rc):
    return jnp.concatenate([_conv_rows(_shift_views(x_ref[pl.ds(r0, rc + pad), lanes], rc, pad), w_ref, cw, pad, rc, lanes)
                            for lanes in _lane_tiles(x_ref.shape[1])], axis=1)


def _conv_rows_t_tiled(x_ref, r0, w_ref, cw, pad, rc):
    return jnp.concatenate([_conv_rows_t(_shift_views(x_ref[pl.ds(r0, rc + pad), lanes], rc, pad), w_ref, cw, rc, lanes)
                            for lanes in _lane_tiles(x_ref.shape[1])], axis=1)


def _conv_wgrad_tiled(x_ref, r0, dy, acc_ref, cw, pad, rc):
    for lanes in _lane_tiles(x_ref.shape[1]):
        _conv_wgrad(_shift_views(x_ref[pl.ds(r0, rc + pad), lanes], rc, pad), dy[:, lanes], acc_ref, cw, pad, rc, lanes)


CONV_RC = 64
CONV_PAD = 32
GLU_RC = 256


def _conv_col_blocks(cfg):
    off = cfg.qw + 2 * cfg.kvw
    bw = math.gcd(off, cfg.cc)
    assert bw % 128 == 0
    n = cfg.cc // bw
    return bw, [off // bw + i for i in range(n)], [(off + cfg.cc) // bw + i for i in range(n)]


def _glu_inputs(a_refs, b_refs, rows):
    ga = jnp.concatenate([r[rows, :] for r in a_refs], axis=1).astype(F32)
    gb = jnp.concatenate([r[rows, :] for r in b_refs], axis=1).astype(F32)
    return ga, gb


def _fill_glu(cfg, a_refs, b_refs, xp_ref):
    xp_ref[0:CONV_PAD, :] = jnp.zeros((CONV_PAD, cfg.cc), F32)

    def fill(i, carry):
        r0 = pl.multiple_of(i * GLU_RC, GLU_RC)
        ga, gb = _glu_inputs(a_refs, b_refs, pl.ds(r0, GLU_RC))
        xp_ref[pl.ds(CONV_PAD + r0, GLU_RC), :] = ga * _sigmoid(gb)
        return carry

    lax.fori_loop(0, cfg.s // GLU_RC, fill, 0)


def _layernorm_stats(cv):
    mu = jnp.mean(cv, axis=-1, keepdims=True)
    xc = cv - mu
    rstd = lax.rsqrt(jnp.mean(xc * xc, axis=-1, keepdims=True) + LN_EPS)
    return xc * rstd, rstd


def conv_fwd(cfg, u, cw_w, cb, lg, lb, name):
    t = u.shape[0]
    s, cc, cw = cfg.s, cfg.cc, cfg.cw
    bw, a_idx, b_idx = _conv_col_blocks(cfg)
    nb = len(a_idx)

    def body(*refs):
        a_refs, b_refs = refs[:nb], refs[nb:2 * nb]
        w_ref, cb_ref, lg_ref, lb_ref, o_ref, cv_ref, xp_ref = refs[2 * nb:]
        _fill_glu(cfg, a_refs, b_refs, xp_ref)

        def chunk(i, carry):
            r0 = pl.multiple_of(i * CONV_RC, CONV_RC)
            cv = _conv_rows_tiled(xp_ref, r0, w_ref, cw, CONV_PAD, CONV_RC) + cb_ref[...]
            cv_ref[pl.ds(r0, CONV_RC), :] = cv
            xhat, _ = _layernorm_stats(cv)
            ln = xhat * lg_ref[...] + lb_ref[...]
            o_ref[pl.ds(r0, CONV_RC), :] = (ln * _sigmoid(ln)).astype(BF16)
            return carry

        lax.fori_loop(0, s // CONV_RC, chunk, 0, unroll=2)

    def colspec(j):
        return pl.BlockSpec((s, bw), lambda b: (b, j))

    vec = pl.BlockSpec((1, cc), lambda b: (0, 0))
    return pl.pallas_call(
        body, name=name, grid=(t // s,),
        in_specs=[colspec(j) for j in a_idx + b_idx] + [pl.BlockSpec((cw, cc), lambda b: (0, 0)), vec, vec, vec],
        out_specs=[pl.BlockSpec((s, cc), lambda b: (b, 0))] * 2,
        out_shape=[jax.ShapeDtypeStruct((t, cc), BF16), jax.ShapeDtypeStruct((t, cc), F32)],
        scratch_shapes=[pltpu.VMEM((CONV_PAD + s, cc), F32)],
        compiler_params=_cparams("arbitrary"),
    )(*([u] * (2 * nb)), cw_w, cb, lg, lb)


def conv_bwd(cfg, u, cv_saved, dcat, cw_w, cb, lg, lb, name):
    t = u.shape[0]
    s, cc, cw = cfg.s, cfg.cc, cfg.cw
    bw, a_idx, b_idx = _conv_col_blocks(cfg)
    nb = len(a_idx)
    assert cfg.qw % cc == 0 and cw <= 32

    def body(*refs):
        a_refs, b_refs = refs[:nb], refs[nb:2 * nb]
        cv_ref, dc_ref, w_ref, cb_ref, lg_ref, lb_ref, du_ref, dw_ref, dvec_ref, xp_ref, dcv_ref, dwacc_ref = refs[2 * nb:]

        @pl.when(pl.program_id(0) == 0)
        def _():
            dw_ref[...] = jnp.zeros_like(dw_ref)
            dvec_ref[...] = jnp.zeros_like(dvec_ref)

        _fill_glu(cfg, a_refs, b_refs, xp_ref)
        dcv_ref[s:s + CONV_PAD, :] = jnp.zeros((CONV_PAD, cc), F32)
        dwacc_ref[...] = jnp.zeros_like(dwacc_ref)

        def chunk(i, carry):
            r0 = pl.multiple_of(i * CONV_RC, CONV_RC)
            xhat, rstd = _layernorm_stats(cv_ref[pl.ds(r0, CONV_RC), :])
            ln = xhat * lg_ref[...] + lb_ref[...]
            sg = _sigmoid(ln)
            dln = dc_ref[pl.ds(r0, CONV_RC), :].astype(F32) * (sg * (1.0 + ln * (1.0 - sg)))
            dxh = dln * lg_ref[...]
            dcv = rstd * (dxh - jnp.mean(dxh, axis=-1, keepdims=True)
                          - xhat * jnp.mean(dxh * xhat, axis=-1, keepdims=True))
            dcv_ref[pl.ds(r0, CONV_RC), :] = dcv
            dvec_ref[0:1, :] += jnp.sum(dcv, axis=0, keepdims=True)
            dvec_ref[1:2, :] += jnp.sum(dln * xhat, axis=0, keepdims=True)
            dvec_ref[2:3, :] += jnp.sum(dln, axis=0, keepdims=True)
            _conv_wgrad_tiled(xp_ref, r0, dcv, dwacc_ref, cw, CONV_PAD, CONV_RC)
            return carry

        lax.fori_loop(0, s // CONV_RC, chunk, 0, unroll=2)
        for k in range(cw):
            dw_ref[k:k + 1, :] += jnp.sum(dwacc_ref[k], axis=0, keepdims=True)

        def chunk2(i, carry):
            r0 = pl.multiple_of(i * CONV_RC, CONV_RC)
            dglu = _conv_rows_t_tiled(dcv_ref, r0, w_ref, cw, CONV_PAD, CONV_RC)
            ga, gb = _glu_inputs(a_refs, b_refs, pl.ds(r0, CONV_RC))
            sgb = _sigmoid(gb)
            du_ref[pl.ds(r0, CONV_RC), 0:cc] = (dglu * sgb).astype(BF16)
            du_ref[pl.ds(r0, CONV_RC), cc:2 * cc] = (dglu * ga * sgb * (1.0 - sgb)).astype(BF16)
            return carry

        lax.fori_loop(0, s // CONV_RC, chunk2, 0)

    def colspec(j):
        return pl.BlockSpec((s, bw), lambda b: (b, j))

    vec = pl.BlockSpec((1, cc), lambda b: (0, 0))
    return pl.pallas_call(
        body, name=name, grid=(t // s,),
        in_specs=[colspec(j) for j in a_idx + b_idx]
        + [pl.BlockSpec((s, cc), lambda b: (b, 0)), pl.BlockSpec((s, cc), lambda b: (b, cfg.qw // cc)),
           pl.BlockSpec((cw, cc), lambda b: (0, 0)), vec, vec, vec],
        out_specs=[pl.BlockSpec((s, 2 * cc), lambda b: (b, 0)), pl.BlockSpec((32, cc), lambda b: (0, 0)),
                   pl.BlockSpec((8, cc), lambda b: (0, 0))],
        out_shape=[jax.ShapeDtypeStruct((t, 2 * cc), BF16), jax.ShapeDtypeStruct((32, cc), F32),
                   jax.ShapeDtypeStruct((8, cc), F32)],
        scratch_shapes=[pltpu.VMEM((CONV_PAD + s, cc), F32), pltpu.VMEM((s + CONV_PAD, cc), F32),
                        pltpu.VMEM((cw, 8, cc), F32)],
        compiler_params=_cparams("arbitrary"),
    )(*([u] * (2 * nb)), cv_saved, dcat, cw_w, cb, lg, lb)


LRU_RC = 64
LRU_PAD = 8
SCAN_RC = 16
SCAN_UNROLL = 4
GELU_K = math.sqrt(2.0 / math.pi)


def _expm1_neg(z):
    return jnp.where(z > -0.05, z * (1.0 + z * (0.5 + z * (1.0 / 6.0 + z * (1.0 / 24.0)))), jnp.exp(z) - 1.0)


def _log_sigmoid(x):
    e = jnp.exp(-jnp.abs(x))
    log1p = jnp.where(e < 0.01, e * (1.0 - e * (0.5 - e * (1.0 / 3.0))), jnp.log(1.0 + e))
    return jnp.minimum(x, 0.0) - log1p


def _gelu(x):
    t = jnp.tanh(GELU_K * (x + 0.044715 * x * x * x))
    return 0.5 * x * (1.0 + t), t


def _gelu_grad(x, t):
    return 0.5 * (1.0 + t) + 0.5 * x * (1.0 - t * t) * GELU_K * (1.0 + 3.0 * 0.044715 * x * x)


def _lru_gates(xc, wa_ref, ba, wx_ref, bx, ls):
    nh = xc.shape[1] // 128
    xb = xc.astype(BF16)
    ra = jnp.concatenate([_nn(xb[:, 128 * h:128 * (h + 1)], wa_ref[h]) for h in range(nh)], axis=1) + ba
    ia = jnp.concatenate([_nn(xb[:, 128 * h:128 * (h + 1)], wx_ref[h]) for h in range(nh)], axis=1) + bx
    r = _sigmoid(ra)
    ig = _sigmoid(ia)
    log_a = RG_LRU_C * r * ls
    return r, ig, log_a


def _lru_decay(log_a):
    return jnp.exp(log_a), jnp.sqrt(-_expm1_neg(2.0 * log_a))


def _fill_padded(src_ref, dst_ref, s, ct):
    dst_ref[0:LRU_PAD, :] = jnp.zeros((LRU_PAD, ct), F32)

    def fill(i, carry):
        r0 = pl.multiple_of(i * GLU_RC, GLU_RC)
        dst_ref[pl.ds(LRU_PAD + r0, GLU_RC), :] = src_ref[pl.ds(r0, GLU_RC), :].astype(F32)
        return carry

    lax.fori_loop(0, s // GLU_RC, fill, 0)


def _lru_specs(cfg, ct):
    s, lw = cfg.s, cfg.lw
    nct = lw // ct
    nh = ct // 128
    act = [pl.BlockSpec((s, ct), lambda c, b: (b, c)), pl.BlockSpec((s, ct), lambda c, b: (b, nct + c))]
    vec = pl.BlockSpec((1, ct), lambda c, b: (0, c))
    gate_w = pl.BlockSpec((nh, 128, 128), lambda c, b: (c, 0, 0))
    params = [pl.BlockSpec((cfg.lcw, ct), lambda c, b: (0, c)), vec, gate_w, vec, gate_w, vec, vec]
    return nct, act, params


def lru_fwd(cfg, u, conv_w, conv_b, wa, ba, wx, bx, lam, name):
    t = u.shape[0]
    s, lw, lcw, ct = cfg.s, cfg.lw, cfg.lcw, cfg.ct_f
    nct, act, params = _lru_specs(cfg, ct)

    def body(gi_ref, ri_ref, cw_ref, cb_ref, wa_ref, ba_ref, wx_ref, bx_ref, lam_ref,
             y_ref, hs_ref, r_ref, ig_ref, xc_ref, la_ref, xp_ref, a_ref, b_ref):
        _fill_padded(ri_ref, xp_ref, s, ct)
        ls = _log_sigmoid(lam_ref[...])

        def chunk(i, carry):
            r0 = pl.multiple_of(i * LRU_RC, LRU_RC)
            rows = pl.ds(r0, LRU_RC)
            views = _shift_views(xp_ref[pl.ds(r0, LRU_RC + LRU_PAD), :], LRU_RC, LRU_PAD)
            xc = _conv_rows(views, cw_ref, lcw, LRU_PAD, LRU_RC) + cb_ref[...]
            r, ig, log_a = _lru_gates(xc, wa_ref, ba_ref[...], wx_ref, bx_ref[...], ls)
            a, mult = _lru_decay(log_a)
            a_ref[rows, :] = a
            b_ref[rows, :] = mult * (ig * xc)
            r_ref[rows, :] = r.astype(BF16)
            ig_ref[rows, :] = ig.astype(BF16)
            xc_ref[rows, :] = xc.astype(BF16)
            la_ref[rows, :] = log_a
            return carry

        lax.fori_loop(0, s // LRU_RC, chunk, 0, unroll=2)
        row = lax.broadcasted_iota(jnp.int32, (SCAN_RC, ct), 0)

        def scan(i, h_last):
            for sub in range(SCAN_UNROLL):
                rows = pl.ds(pl.multiple_of((i * SCAN_UNROLL + sub) * SCAN_RC, SCAN_RC), SCAN_RC)
                a = a_ref[rows, :]
                b = b_ref[rows, :]
                sft = 1
                while sft < SCAN_RC:
                    a_sh = jnp.where(row >= sft, pltpu.roll(a, sft, 0), 1.0)
                    b_sh = jnp.where(row >= sft, pltpu.roll(b, sft, 0), 0.0)
                    b = a * b_sh + b
                    a = a * a_sh
                    sft *= 2
                h = a * h_last + b
                gate, _ = _gelu(gi_ref[rows, :].astype(F32))
                y_ref[rows, :] = (gate * h).astype(BF16)
                hs_ref[rows, :] = h.astype(BF16)
                h_last = h[SCAN_RC - 1:SCAN_RC, :]
            return h_last

        lax.fori_loop(0, s // (SCAN_RC * SCAN_UNROLL), scan, jnp.zeros((1, ct), F32))

    out = pl.BlockSpec((s, ct), lambda c, b: (b, c))
    return pl.pallas_call(
        body, name=name, grid=(nct, t // s),
        in_specs=act + params,
        out_specs=[out] * 6,
        out_shape=[jax.ShapeDtypeStruct((t, lw), BF16)] * 5 + [jax.ShapeDtypeStruct((t, lw), F32)],
        scratch_shapes=[pltpu.VMEM((LRU_PAD + s, ct), F32), pltpu.VMEM((s, ct), F32), pltpu.VMEM((s, ct), F32)],
        compiler_params=_cparams("arbitrary", "arbitrary"),
    )(u, u, conv_w, conv_b, wa, ba, wx, bx, lam)


def lru_bwd(cfg, u, saved, dy, conv_w, conv_b, wa, ba, wx, bx, lam, name):
    t = u.shape[0]
    s, lw, lcw, ct = cfg.s, cfg.lw, cfg.lcw, cfg.ct_b
    nct, act, params = _lru_specs(cfg, ct)
    nh = ct // 128
    nscan = s // SCAN_RC
    assert lcw <= 8

    def body(gi_ref, ri_ref, hs_ref, r_ref, ig_ref, xc_ref, la_ref, dy_ref,
             cw_ref, cb_ref, wa_ref, ba_ref, wx_ref, bx_ref, lam_ref,
             dug_ref, dur_ref, dwa_ref, dwx_ref, dvec_ref, dcw_ref,
             xp_ref, hp_ref, a_ref, g_ref, dxc_ref, dwacc_ref):
        @pl.when(pl.program_id(1) == 0)
        def _():
            dwa_ref[...] = jnp.zeros_like(dwa_ref)
            dwx_ref[...] = jnp.zeros_like(dwx_ref)
            dvec_ref[...] = jnp.zeros_like(dvec_ref)
            dcw_ref[...] = jnp.zeros_like(dcw_ref)

        _fill_padded(ri_ref, xp_ref, s, ct)
        _fill_padded(hs_ref, hp_ref, s, ct)
        dxc_ref[s:s + LRU_PAD, :] = jnp.zeros((LRU_PAD, ct), F32)
        dwacc_ref[...] = jnp.zeros_like(dwacc_ref)
        lam = lam_ref[...]
        ls = _log_sigmoid(lam)

        def chunk(i, carry):
            r0 = pl.multiple_of(i * LRU_RC, LRU_RC)
            rows = pl.ds(r0, LRU_RC)
            a_ref[rows, :] = jnp.exp(la_ref[rows, :])
            x = gi_ref[rows, :].astype(F32)
            gate, th = _gelu(x)
            dyv = dy_ref[rows, :].astype(F32)
            g_ref[rows, :] = dyv * gate
            dug_ref[rows, :] = (dyv * hp_ref[pl.ds(LRU_PAD + r0, LRU_RC), :] * _gelu_grad(x, th)).astype(BF16)
            return carry

        lax.fori_loop(0, s // LRU_RC, chunk, 0, unroll=2)
        row = lax.broadcasted_iota(jnp.int32, (SCAN_RC, ct), 0)

        def scan(ii, carry):
            g_next, a_next = carry
            for sub in range(SCAN_UNROLL):
                step = nscan - 1 - (ii * SCAN_UNROLL + sub)
                rows = pl.ds(pl.multiple_of(step * SCAN_RC, SCAN_RC), SCAN_RC)
                a = a_ref[rows, :]
                d = g_ref[rows, :]
                c = jnp.where(row < SCAN_RC - 1, pltpu.roll(a, SCAN_RC - 1, 0), a_next)
                sft = 1
                while sft < SCAN_RC:
                    c_sh = jnp.where(row < SCAN_RC - sft, pltpu.roll(c, SCAN_RC - sft, 0), 1.0)
                    d_sh = jnp.where(row < SCAN_RC - sft, pltpu.roll(d, SCAN_RC - sft, 0), 0.0)
                    d = d + c * d_sh
                    c = c * c_sh
                    sft *= 2
                g = d + c * g_next
                g_ref[rows, :] = g
                g_next, a_next = g[0:1, :], a[0:1, :]
            return g_next, a_next

        lax.fori_loop(0, nscan // SCAN_UNROLL, scan, (jnp.zeros((1, ct), F32), jnp.zeros((1, ct), F32)))

        def chunk3(i, carry):
            r0 = pl.multiple_of(i * LRU_RC, LRU_RC)
            rows = pl.ds(r0, LRU_RC)
            r, ig, xc = r_ref[rows, :].astype(F32), ig_ref[rows, :].astype(F32), xc_ref[rows, :].astype(F32)
            a, mult = _lru_decay(la_ref[rows, :])
            g = g_ref[rows, :]
            h_prev = hp_ref[pl.ds(r0, LRU_RC + LRU_PAD), :][LRU_PAD - 1:LRU_PAD - 1 + LRU_RC]
            dix = g * mult
            di = dix * xc
            dxc = dix * ig
            da = g * h_prev - (g * ig * xc) * a / mult
            dlog_a = da * a
            dr = dlog_a * (RG_LRU_C * ls)
            dra = dr * r * (1.0 - r)
            dia = di * ig * (1.0 - ig)
            xb, drab, diab = xc.astype(BF16), dra.astype(BF16), dia.astype(BF16)
            dxg = []
            for h in range(nh):
                cols = slice(128 * h, 128 * (h + 1))
                dxg.append(_nt(drab[:, cols], wa_ref[h]) + _nt(diab[:, cols], wx_ref[h]))
                dwa_ref[h] += _tn(xb[:, cols], drab[:, cols])
                dwx_ref[h] += _tn(xb[:, cols], diab[:, cols])
            dxc = dxc + jnp.concatenate(dxg, axis=1)
            dvec_ref[0:1, :] += jnp.sum(dra, axis=0, keepdims=True)
            dvec_ref[1:2, :] += jnp.sum(dia, axis=0, keepdims=True)
            dvec_ref[2:3, :] += jnp.sum(dlog_a * r, axis=0, keepdims=True) * (RG_LRU_C * _sigmoid(-lam))
            dvec_ref[3:4, :] += jnp.sum(dxc, axis=0, keepdims=True)
            dxc_ref[rows, :] = dxc
            views = _shift_views(xp_ref[pl.ds(r0, LRU_RC + LRU_PAD), :], LRU_RC, LRU_PAD)
            _conv_wgrad(views, dxc, dwacc_ref, lcw, LRU_PAD, LRU_RC)
            return carry

        lax.fori_loop(0, s // LRU_RC, chunk3, 0, unroll=2)
        for k in range(lcw):
            dcw_ref[k:k + 1, :] += jnp.sum(dwacc_ref[k], axis=0, keepdims=True)

        def chunk4(i, carry):
            r0 = pl.multiple_of(i * LRU_RC, LRU_RC)
            views = _shift_views(dxc_ref[pl.ds(r0, LRU_RC + LRU_PAD), :], LRU_RC, LRU_PAD)
            dur_ref[pl.ds(r0, LRU_RC), :] = _conv_rows_t(views, cw_ref, lcw, LRU_RC).astype(BF16)
            return carry

        lax.fori_loop(0, s // LRU_RC, chunk4, 0, unroll=2)

    blk = pl.BlockSpec((s, ct), lambda c, b: (b, c))
    acc8 = pl.BlockSpec((8, ct), lambda c, b: (0, c))
    gate_w = pl.BlockSpec((nh, 128, 128), lambda c, b: (c, 0, 0))
    return pl.pallas_call(
        body, name=name, grid=(nct, t // s),
        in_specs=act + [blk] * 6 + params,
        out_specs=[blk, blk, gate_w, gate_w, acc8, acc8],
        out_shape=[jax.ShapeDtypeStruct((t, lw), BF16), jax.ShapeDtypeStruct((t, lw), BF16),
                   jax.ShapeDtypeStruct((cfg.lh, 128, 128), F32), jax.ShapeDtypeStruct((cfg.lh, 128, 128), F32),
                   jax.ShapeDtypeStruct((8, lw), F32), jax.ShapeDtypeStruct((8, lw), F32)],
        scratch_shapes=[pltpu.VMEM((LRU_PAD + s, ct), F32), pltpu.VMEM((LRU_PAD + s, ct), F32),
                        pltpu.VMEM((s, ct), F32), pltpu.VMEM((s, ct), F32), pltpu.VMEM((s + LRU_PAD, ct), F32),
                        pltpu.VMEM((lcw, 8, ct), F32)],
        compiler_params=_cparams("arbitrary", "arbitrary"),
    )(u, u, *saved, dy, conv_w, conv_b, wa, ba, wx, bx, lam)


def local_step(cfg, x, tgt, p, shards=None):
    buckets = bucket_table(cfg)
    bias = bias_build(cfg, p["rel_bias"], buckets, "bias_build")
    ga_w, gx_w = p["gate_a_w"].astype(BF16), p["gate_x_w"].astype(BF16)
    wf = dict(p["wf"])
    dist = shards is not None

    def ffn_forward(l, k, h, nw):
        riders = gather_riders([shards[(l + 1, k)]]) if dist and l + 1 < cfg.depth else None
        outs = ffn_fwd(cfg, h, nw, wf[(l, k)], 0, 2, f"ffn{k + 1}_fwd_{l}", riders=riders)
        if riders is not None:
            wf[(l + 1, k)] = outs[4].reshape(3, -1, cfg.d)
        return outs[:4]

    def blocks(g):
        g = g if g.ndim == 3 else g[None]
        return g.reshape(g.shape[0], N_DEV, g.shape[1] // N_DEV, g.shape[2])

    h = x
    saved = []
    for l in range(cfg.depth):
        i = l // 2
        s = {"h0": h}
        s["h1"], s["xn1"], s["g1"], s["u1"] = ffn_forward(l, 0, h, p["norm_ffn1"][l][None])
        if l % 2 == 0:
            s["um"], s["xnm"] = norm_proj(cfg, s["h1"], p["norm_mix"][l][None], p["even_in"], f"mix_in_{l}", wi=i)
            attn = attn_fwd(cfg, s["um"], bias, p["attn_sinks"][i], f"attn_fwd_{l}")
            c, s["cv"] = conv_fwd(cfg, s["um"], p["conv_b_w"][i], p["conv_b_b"][i][None], p["conv_ln_g"][i][None],
                                  p["conv_ln_b"][i][None], f"conv_fwd_{l}")
            s["parts"] = [attn, c]
            s["h2"] = proj_residual(cfg, s["h1"], s["parts"], p["even_out"], f"mix_out_{l}", wi=i)
        else:
            s["um"], s["xnm"] = norm_proj(cfg, s["h1"], p["norm_mix"][l][None], p["odd_in"], f"mix_in_{l}", wi=i)
            y, *s["saved"] = lru_fwd(cfg, s["um"], p["lru_conv_w"][i], p["lru_conv_b"][i][None], ga_w[i], p["gate_a_b"][i][None],
                                 gx_w[i], p["gate_x_b"][i][None], p["lru_lambda"][i][None], f"lru_fwd_{l}")
            s["parts"] = [y]
            s["h2"] = proj_residual(cfg, s["h1"], s["parts"], p["odd_out"], f"mix_out_{l}", wi=i)
        h, s["xn2"], s["g2"], s["u2"] = ffn_forward(l, 1, s["h2"], p["norm_ffn2"][l][None])
        saved.append(s)

    loss, dh, dnf = loss_head(cfg, h, p["norm_final"][None], tgt, "loss_head")
    big = [None] * cfg.depth
    sm = {k: [None] * cfg.depth for k in ("norm_ffn1", "norm_mix", "norm_ffn2")}
    ne, no = (cfg.depth + 1) // 2, cfg.depth // 2
    for k in ("attn_sinks", "conv_b_w", "conv_b_b", "conv_ln_g", "conv_ln_b", "dbias"):
        sm[k] = [None] * ne
    for k in ("lru_conv_w", "lru_conv_b", "gate_a_w", "gate_a_b", "gate_x_w", "gate_x_b", "lru_lambda"):
        sm[k] = [None] * no
    pending = None
    for l in reversed(range(cfg.depth)):
        i = l // 2
        s = saved[l]
        riders = scatter_riders([pending]) if pending is not None else None
        outs = ffn_bwd_x(cfg, dh, s["h2"], p["norm_ffn2"][l][None], s["g2"], s["u2"], wf[(l, 1)], 0, 2,
                         f"ffn2_bwd_x_{l}", riders=riders)
        dh, dout, dg, du, dn = outs[:5]
        if riders is not None:
            big[l + 1]["f1"] = (pending, outs[5])
        sm["norm_ffn2"][l] = dn[0]
        gf2 = blocks(ffn_bwd_w(cfg, s["xn2"], dout, s["g2"], s["u2"], dg, du, f"ffn2_bwd_w_{l}")[0])
        w_out = p["even_out"] if l % 2 == 0 else p["odd_out"]
        w_in = p["even_in"] if l % 2 == 0 else p["odd_in"]
        dcat = proj_bwd_act(cfg, dh, w_out, f"mix_out_bwd_{l}", wi=i)
        g_out = blocks(grad_weight(cfg, s["parts"], dh, f"mix_out_gw_{l}"))
        if l % 2 == 0:
            du_a, sm["dbias"][i], dsink = attn_bwd(cfg, s["um"], dcat, bias, p["attn_sinks"][i], f"attn_bwd_{l}")
            du_c, dcw, dvec = conv_bwd(cfg, s["um"], s["cv"], dcat, p["conv_b_w"][i], p["conv_b_b"][i][None],
                                       p["conv_ln_g"][i][None], p["conv_ln_b"][i][None], f"conv_bwd_{l}")
            sm["attn_sinks"][i] = dsink[:cfg.hq, 0]
            sm["conv_b_w"][i] = dcw[:cfg.cw]
            sm["conv_b_b"][i], sm["conv_ln_g"][i], sm["conv_ln_b"][i] = dvec[0], dvec[1], dvec[2]
            dparts = [du_a, du_c]
        else:
            dug, dur, dwa, dwx, dvec, dcw = lru_bwd(
                cfg, s["um"], s["saved"], dcat, p["lru_conv_w"][i], p["lru_conv_b"][i][None], ga_w[i], p["gate_a_b"][i][None],
                gx_w[i], p["gate_x_b"][i][None], p["lru_lambda"][i][None], f"lru_bwd_{l}")
            sm["gate_a_w"][i], sm["gate_x_w"][i] = dwa, dwx
            sm["gate_a_b"][i], sm["gate_x_b"][i], sm["lru_lambda"][i], sm["lru_conv_b"][i] = dvec[0], dvec[1], dvec[2], dvec[3]
            sm["lru_conv_w"][i] = dcw[:cfg.lcw]
            dparts = [dug, dur]
        g_in = blocks(grad_weight(cfg, dparts, s["xnm"], f"mix_in_gw_{l}"))
        dh, dn = norm_proj_bwd(cfg, dh, s["h1"], p["norm_mix"][l][None], dparts, w_in, f"mix_in_bwd_{l}", wi=i)
        sm["norm_mix"][l] = dn[0]
        riders = scatter_riders([gf2]) if dist else None
        outs = ffn_bwd_x(cfg, dh, s["h0"], p["norm_ffn1"][l][None], s["g1"], s["u1"], wf[(l, 0)], 0, 2,
                         f"ffn1_bwd_x_{l}", riders=riders)
        dh, dout, dg, du, dn = outs[:5]
        sm["norm_ffn1"][l] = dn[0]
        riders = scatter_riders([g_in, g_out]) if dist else None
        gouts = ffn_bwd_w(cfg, s["xn1"], dout, s["g1"], s["u1"], dg, du, f"ffn1_bwd_w_{l}", riders=riders)
        gf1 = blocks(gouts[0])
        big[l] = {"f1": (gf1, None), "f2": (gf2, outs[5] if dist else None),
                  "in": (g_in, gouts[1] if dist else None), "out": (g_out, gouts[2] if dist else None)}
        pending = gf1 if dist and l > 0 else None
    drb = bias_grad(cfg, jnp.stack(sm.pop("dbias")), buckets, "bias_grad")
    small = {k: jnp.stack(v) for k, v in sm.items()}
    small["rel_bias"] = drb[:, :, 0].T
    small["norm_final"] = dnf[0]
    return loss, dh, big, small


MESH = pl.DeviceIdType.MESH
ANY = pl.BlockSpec(memory_space=pl.ANY)


def _place():
    return lax.axis_index("x"), lax.axis_index("y"), lax.axis_index("c")


FLIPS = ((0, 0, 1), (0, 1, 0), (0, 1, 1), (1, 0, 0), (1, 0, 1), (1, 1, 0), (1, 1, 1))


def _peer(place, flip):
    return tuple(1 - v if f else v for v, f in zip(place, flip))


def _dev_index(place):
    return 4 * place[0] + 2 * place[1] + place[2]


def _remote(src, dst, send_sems, recv_sems, g, k, peer):
    return pltpu.make_async_remote_copy(src_ref=src, dst_ref=dst, send_sem=send_sems.at[g, k], recv_sem=recv_sems.at[g, k],
                                        device_id=peer, device_id_type=MESH)


def gather_riders(srcs):
    ng = len(srcs)

    def copies(in_refs, out_refs, sems):
        send_sems, recv_sems, local_sems = sems
        me = _place()
        local, sends, recvs = [], [], []
        for g in range(ng):
            mine = out_refs[g].at[:, _dev_index(me)]
            local.append(pltpu.make_async_copy(in_refs[g], mine, local_sems.at[g]))
            for k, flip in enumerate(FLIPS):
                peer = _peer(me, flip)
                sends.append(_remote(in_refs[g], mine, send_sems, recv_sems, g, k, peer))
                recvs.append(_remote(in_refs[g], out_refs[g].at[:, _dev_index(peer)], send_sems, recv_sems, g, k, peer))
        return local, sends, recvs

    def start(in_refs, out_refs, sems):
        local, sends, _ = copies(in_refs, out_refs, sems)
        for cp in local + sends:
            cp.start()

    def wait(in_refs, out_refs, sems):
        local, sends, recvs = copies(in_refs, out_refs, sems)
        for cp in sends:
            cp.wait_send()
        for cp in recvs:
            cp.wait_recv()
        for cp in local:
            cp.wait()

    return Riders(tuple(srcs), tuple(jax.ShapeDtypeStruct((s.shape[0], N_DEV) + s.shape[1:], s.dtype) for s in srcs),
                  (pltpu.SemaphoreType.DMA((ng, 7)), pltpu.SemaphoreType.DMA((ng, 7)), pltpu.SemaphoreType.DMA((ng,))),
                  start, wait)


def scatter_riders(bufs):
    ng = len(bufs)

    def copies(in_refs, out_refs, sems):
        send_sems, recv_sems = sems
        me = _place()
        return [_remote(in_refs[g].at[:, _dev_index(_peer(me, flip))], out_refs[g].at[:, k], send_sems, recv_sems, g, k,
                        _peer(me, flip)) for g in range(ng) for k, flip in enumerate(FLIPS)]

    def start(in_refs, out_refs, sems):
        for cp in copies(in_refs, out_refs, sems):
            cp.start()

    def wait(in_refs, out_refs, sems):
        for cp in copies(in_refs, out_refs, sems):
            cp.wait()

    return Riders(tuple(bufs), tuple(jax.ShapeDtypeStruct((b.shape[0], 7) + b.shape[2:], b.dtype) for b in bufs),
                  (pltpu.SemaphoreType.DMA((ng, 7)), pltpu.SemaphoreType.DMA((ng, 7))), start, wait)


def shard_sum(buf, recv, dev, name):
    n, _, r, cdim = buf.shape

    def body(dev_ref, a_ref, b_ref, o_ref):
        acc = a_ref[...].astype(F32)
        for k in range(7):
            acc = acc + b_ref[k].astype(F32)
        o_ref[...] = acc

    return pl.pallas_call(
        body, name=name,
        grid_spec=pltpu.PrefetchScalarGridSpec(
            num_scalar_prefetch=1, grid=(n,),
            in_specs=[pl.BlockSpec((None, None, r, cdim), lambda i, dev_ref: (i, dev_ref[0], 0, 0)),
                      pl.BlockSpec((None, 7, r, cdim), lambda i, dev_ref: (i, 0, 0, 0))],
            out_specs=pl.BlockSpec((None, r, cdim), lambda i, dev_ref: (i, 0, 0))),
        out_shape=jax.ShapeDtypeStruct((n, r, cdim), F32),
    )(dev, buf, recv)


def all_gather(srcs, name):
    ng = len(srcs)

    def body(*refs):
        x_refs, o_refs = refs[:ng], refs[ng:2 * ng]
        send_sems, recv_sems, local_sems = refs[2 * ng:]
        x, y, c = _place()
        me, sibling = (x, y, c), (x, y, 1 - c)
        chips = [(1 - x, y), (x, 1 - y), (1 - x, 1 - y)]

        def copy(gi, k, block, to, src=None):
            dst = o_refs[gi].at[:, 4 * block[0] + 2 * block[1] + block[2]]
            return pltpu.make_async_remote_copy(
                src_ref=dst if src is None else src, dst_ref=dst, send_sem=send_sems.at[gi, k],
                recv_sem=recv_sems.at[gi, k], device_id=to, device_id_type=MESH)

        mine = [pltpu.make_async_copy(x_refs[gi], o_refs[gi].at[:, 4 * x + 2 * y + c], local_sems.at[gi])
                for gi in range(ng)]
        for cp in mine:
            cp.start()
        first = []
        for gi in range(ng):
            first.append(copy(gi, 0, me, sibling, src=x_refs[gi]))
            first += [copy(gi, 1 + j, me, (*chip, c), src=x_refs[gi]) for j, chip in enumerate(chips)]
        for cp in first:
            cp.start()
        passed = []
        for j, chip in enumerate(chips):
            for gi in range(ng):
                copy(gi, 1 + j, (*chip, c), me).wait_recv()
                cp = copy(gi, 4 + j, (*chip, c), sibling)
                cp.start()
                passed.append(cp)
        for gi in range(ng):
            copy(gi, 0, sibling, me).wait_recv()
            for j, chip in enumerate(chips):
                copy(gi, 4 + j, (*chip, 1 - c), me).wait_recv()
        for cp in first + passed:
            cp.wait_send()
        for cp in mine:
            cp.wait()

    return pl.pallas_call(
        body, name=name,
        in_specs=[ANY] * ng, out_specs=[ANY] * ng,
        out_shape=[jax.ShapeDtypeStruct((s.shape[0], N_DEV) + s.shape[1:], s.dtype) for s in srcs],
        scratch_shapes=[pltpu.SemaphoreType.DMA((ng, 7)), pltpu.SemaphoreType.DMA((ng, 7)),
                        pltpu.SemaphoreType.DMA((ng,))],
    )(*srcs)


def pair_exchange(bufs, name):
    ng = len(bufs)

    def body(*refs):
        b_refs, o_refs = refs[:ng], refs[ng:2 * ng]
        send_sems, recv_sems = refs[2 * ng:]
        x, y, c = _place()
        copies = [pltpu.make_async_remote_copy(
            src_ref=b_refs[gi].at[:, :, 1 - c], dst_ref=o_refs[gi], send_sem=send_sems.at[gi], recv_sem=recv_sems.at[gi],
            device_id=(x, y, 1 - c), device_id_type=MESH) for gi in range(ng)]
        for cp in copies:
            cp.start()
        for cp in copies:
            cp.wait()

    return pl.pallas_call(
        body, name=name,
        in_specs=[ANY] * ng, out_specs=[ANY] * ng,
        out_shape=[jax.ShapeDtypeStruct(b.shape[:2] + b.shape[3:], b.dtype) for b in bufs],
        scratch_shapes=[pltpu.SemaphoreType.DMA((ng,)), pltpu.SemaphoreType.DMA((ng,))],
    )(*bufs)


def chip_exchange(qs, name):
    ng = len(qs)

    def body(*refs):
        q_refs, o_refs = refs[:ng], refs[ng:2 * ng]
        send_sems, recv_sems = refs[2 * ng:]
        x, y, c = _place()
        chips = [(1 - x, y), (x, 1 - y), (1 - x, 1 - y)]
        copies = [pltpu.make_async_remote_copy(
            src_ref=q_refs[gi].at[:, 2 * chip[0] + chip[1]], dst_ref=o_refs[gi].at[:, j],
            send_sem=send_sems.at[gi, j], recv_sem=recv_sems.at[gi, j],
            device_id=(*chip, c), device_id_type=MESH) for gi in range(ng) for j, chip in enumerate(chips)]
        for cp in copies:
            cp.start()
        for cp in copies:
            cp.wait()

    return pl.pallas_call(
        body, name=name,
        in_specs=[ANY] * ng, out_specs=[ANY] * ng,
        out_shape=[jax.ShapeDtypeStruct((q.shape[0], 3) + q.shape[2:], q.dtype) for q in qs],
        scratch_shapes=[pltpu.SemaphoreType.DMA((ng, 3)), pltpu.SemaphoreType.DMA((ng, 3))],
    )(*qs)


def pair_sum(buf, recv, core, name):
    n, _, _, r, cdim = buf.shape

    def body(core_ref, a_ref, b_ref, o_ref):
        o_ref[...] = (a_ref[...].astype(F32) + b_ref[...].astype(F32)).astype(BF16)

    blk = pl.BlockSpec((None, None, r, cdim), lambda i, k, core_ref: (i, k, 0, 0))
    return pl.pallas_call(
        body, name=name,
        grid_spec=pltpu.PrefetchScalarGridSpec(
            num_scalar_prefetch=1, grid=(n, 4),
            in_specs=[pl.BlockSpec((None, None, None, r, cdim), lambda i, k, core_ref: (i, k, core_ref[0], 0, 0)), blk],
            out_specs=blk),
        out_shape=jax.ShapeDtypeStruct((n, 4, r, cdim), BF16),
    )(core, buf, recv)


def chip_sum(q, recv, chip, name):
    n, _, r, cdim = q.shape

    def body(chip_ref, a_ref, b_ref, o_ref):
        acc = a_ref[...].astype(F32)
        for j in range(3):
            acc = acc + b_ref[j].astype(F32)
        o_ref[...] = acc

    return pl.pallas_call(
        body, name=name,
        grid_spec=pltpu.PrefetchScalarGridSpec(
            num_scalar_prefetch=1, grid=(n,),
            in_specs=[pl.BlockSpec((None, None, r, cdim), lambda i, chip_ref: (i, chip_ref[0], 0, 0)),
                      pl.BlockSpec((None, 3, r, cdim), lambda i, chip_ref: (i, 0, 0, 0))],
            out_specs=pl.BlockSpec((None, r, cdim), lambda i, chip_ref: (i, 0, 0))),
        out_shape=jax.ShapeDtypeStruct((n, r, cdim), F32),
    )(chip, q, recv)


def sum_blocks(a, name):
    def body(a_ref, o_ref):
        acc = a_ref[0]
        for d in range(1, a.shape[0]):
            acc = acc + a_ref[d]
        o_ref[...] = acc

    return pl.pallas_call(body, name=name, out_shape=jax.ShapeDtypeStruct(a.shape[1:], F32),
                          compiler_params=pltpu.CompilerParams(vmem_limit_bytes=VMEM_LIMIT))(a)


def adamw(w, g, m, v, name):
    c1 = 1.0 / (1.0 - ADAM_B1 ** ADAM_STEP)
    c2 = 1.0 / (1.0 - ADAM_B2 ** ADAM_STEP)

    def body(w_ref, g_ref, m_ref, v_ref, d_ref, mo_ref, vo_ref):
        gg = g_ref[...]
        m2 = ADAM_B1 * m_ref[...] + (1.0 - ADAM_B1) * gg
        v2 = ADAM_B2 * v_ref[...] + (1.0 - ADAM_B2) * (gg * gg)
        mo_ref[...] = m2
        vo_ref[...] = v2
        d_ref[...] = -ADAM_LR * ((m2 * c1) / (jnp.sqrt(v2 * c2) + ADAM_EPS) + ADAM_WD * w_ref[...])

    out_shape = [jax.ShapeDtypeStruct(w.shape, F32)] * 3
    if w.ndim == 2:
        return pl.pallas_call(body, name=name, out_shape=out_shape,
                              compiler_params=pltpu.CompilerParams(vmem_limit_bytes=VMEM_LIMIT))(w, g, m, v)
    blk = pl.BlockSpec((None,) + w.shape[1:], lambda i: (i, 0, 0))
    return pl.pallas_call(body, name=name, grid=(w.shape[0],), in_specs=[blk] * 4, out_specs=[blk] * 3,
                          out_shape=out_shape, compiler_params=_cparams("arbitrary"))(w, g, m, v)


WEIGHTS = ("norm_ffn1", "ffn1_wg", "ffn1_wu", "ffn1_wd", "norm_mix", "norm_ffn2", "ffn2_wg", "ffn2_wu", "ffn2_wd",
           "rel_bias", "even_w_in", "attn_sinks", "conv_b_w", "conv_b_b", "conv_ln_g", "conv_ln_b", "even_w_out",
           "odd_w_in", "lru_conv_w", "lru_conv_b", "gate_a_w", "gate_a_b", "gate_x_w", "gate_x_b", "lru_lambda",
           "odd_w_out", "norm_final")
BIG = ("ffn1_wg", "ffn1_wu", "ffn1_wd", "ffn2_wg", "ffn2_wu", "ffn2_wd", "even_w_in", "even_w_out", "odd_w_in", "odd_w_out")
SMALL = tuple(n for n in WEIGHTS if n not in BIG)
SMALL_SHARDED = ("conv_b_w", "lru_conv_w", "lru_conv_b", "gate_a_b", "gate_x_b", "lru_lambda")
PACK_ALIGN = 1024


def _pack(arrays):
    parts = []
    for a in arrays:
        flat = a.reshape(-1)
        parts.append(jnp.pad(flat, (0, -flat.shape[0] % PACK_ALIGN)))
    return jnp.concatenate(parts).reshape(-1, 128)


def _unpack(packed, shapes, lead=()):
    flat = packed.reshape(lead + (-1,))
    out, off = [], 0
    for shp in shapes:
        size = math.prod(shp)
        out.append(flat[..., off:off + size].reshape(lead + tuple(shp)))
        off += size + (-size % PACK_ALIGN)
    return out


def _unshard_last(blocks):
    nd = blocks.ndim
    moved = jnp.moveaxis(blocks, 0, nd - 2)
    return moved.reshape(moved.shape[:-2] + (-1,))


def _step(cfg, x, weights, loss_target, ms, vs):
    w = dict(zip(WEIGHTS, weights))
    m = dict(zip(WEIGHTS, ms))
    v = dict(zip(WEIGHTS, vs))
    px, py, pc = _place()
    dev = 4 * px + 2 * py + pc
    core = jnp.reshape(pc, (1,)).astype(jnp.int32)
    chip = jnp.reshape(2 * px + py, (1,)).astype(jnp.int32)
    d = cfg.d
    t = cfg.bl * cfg.s

    def rows(name):
        a = w[name]
        return (a if name.endswith(("wd", "w_out")) else a.transpose(0, 2, 1)).astype(BF16)

    r3 = {n: rows(n) for n in BIG[:6]}
    shards = {(l, k): jnp.stack([r3[f"ffn{k + 1}_{mat}"][l] for mat in ("wg", "wu", "wd")])
              for l in range(cfg.depth) for k in range(2)}
    small_src = _pack([w[n] for n in SMALL_SHARDED])[None]
    gathered = all_gather([shards.pop((0, 0)), shards.pop((0, 1)), rows("even_w_in"), rows("even_w_out"), rows("odd_w_in"),
                           rows("odd_w_out"), small_src], "all_gather_weights")
    full = [g.reshape(g.shape[0], -1, g.shape[-1]) for g in gathered[:6]]
    p = {n: w[n] for n in SMALL if n not in SMALL_SHARDED}
    p.update(wf={(0, 0): full[0], (0, 1): full[1]}, even_in=full[2], even_out=full[3], odd_in=full[4], odd_out=full[5])
    for n, blocks in zip(SMALL_SHARDED, _unpack(gathered[6][0], [w[n].shape for n in SMALL_SHARDED], lead=(N_DEV,))):
        p[n] = _unshard_last(blocks)

    lossp, gx, big, small = local_step(cfg, x.reshape(t, d), loss_target.reshape(t, d), p, shards)
    loss = lax.psum(lossp[0, 0], ("x", "y", "c"))

    dev1 = jnp.reshape(dev, (1,)).astype(jnp.int32)
    shard_rows = [{} for _ in range(cfg.depth)]
    for l in range(cfg.depth):
        for key, (buf, recv) in big[l].items():
            if recv is not None:
                shard_rows[l][key] = shard_sum(buf, recv, dev1, f"rs_sum_{key}_{l}")
    left = [(l, key, buf) for l in range(cfg.depth) for key, (buf, recv) in big[l].items() if recv is None]
    bufs = [buf.reshape(buf.shape[0], 4, 2, buf.shape[2], d) for _, _, buf in left]
    recv = pair_exchange(bufs, "rs_pair")
    qs = [pair_sum(b, r, core, f"rs_pair_sum{j}") for j, (b, r) in enumerate(zip(bufs, recv))]
    recv = chip_exchange(qs, "rs_chip")
    for j, ((l, key, _), q, r) in enumerate(zip(left, qs, recv)):
        shard_rows[l][key] = chip_sum(q, r, chip, f"rs_chip_sum{j}")

    g_rows = {}
    for k in range(2):
        ffn_g = jnp.stack([shard_rows[l][f"f{k + 1}"] for l in range(cfg.depth)])
        for j, mat in enumerate(("wg", "wu", "wd")):
            g_rows[f"ffn{k + 1}_{mat}"] = ffn_g[:, j]
    g_rows["even_w_in"] = jnp.stack([shard_rows[l]["in"][0] for l in range(0, cfg.depth, 2)])
    g_rows["even_w_out"] = jnp.stack([shard_rows[l]["out"][0] for l in range(0, cfg.depth, 2)])
    g_rows["odd_w_in"] = jnp.stack([shard_rows[l]["in"][0] for l in range(1, cfg.depth, 2)])
    g_rows["odd_w_out"] = jnp.stack([shard_rows[l]["out"][0] for l in range(1, cfg.depth, 2)])
    grads = {}

    full_shapes = [small[n].shape for n in SMALL]
    parts = all_gather([_pack([small[n] for n in SMALL])[None]], "all_gather_small_grads")[0][0]
    for n, g in zip(SMALL, _unpack(sum_blocks(parts, "sum_small_grads"), full_shapes)):
        if n in SMALL_SHARDED:
            width = w[n].shape[-1]
            g = lax.dynamic_slice_in_dim(g, dev * width, width, axis=g.ndim - 1)
        grads[n] = g

    delta, new_m, new_v = {}, {}, {}
    for n in BIG:
        if n.endswith(("wd", "w_out")):
            grads[n] = g_rows[n]
            delta[n], new_m[n], new_v[n] = adamw(w[n], grads[n], m[n], v[n], f"adamw_{n}")
        elif w[n].shape[-1] % 128 == 0:
            grads[n] = g_rows[n].transpose(0, 2, 1)
            delta[n], new_m[n], new_v[n] = adamw(w[n], grads[n], m[n], v[n], f"adamw_{n}")
        else:
            outs = adamw(w[n].transpose(0, 2, 1), g_rows[n], m[n].transpose(0, 2, 1), v[n].transpose(0, 2, 1), f"adamw_{n}")
            delta[n], new_m[n], new_v[n] = [o.transpose(0, 2, 1) for o in outs]
            grads[n] = g_rows[n].transpose(0, 2, 1)
    shapes = [w[n].shape for n in SMALL]
    packed = adamw(*[_pack([src[n] for n in SMALL]) for src in (w, grads, m, v)], "adamw_small")
    for out, pk in zip((delta, new_m, new_v), packed):
        out.update(zip(SMALL, _unpack(pk, shapes)))

    return (loss, gx.reshape(x.shape), *[grads[n] for n in WEIGHTS], *[delta[n] for n in WEIGHTS],
            *[new_m[n] for n in WEIGHTS], *[new_v[n] for n in WEIGHTS])


def kernel(x, norm_ffn1, ffn1_wg, ffn1_wu, ffn1_wd, norm_mix, norm_ffn2, ffn2_wg, ffn2_wu, ffn2_wd, rel_bias, even_w_in, attn_sinks, conv_b_w, conv_b_b, conv_ln_g, conv_ln_b, even_w_out, odd_w_in, lru_conv_w, lru_conv_b, gate_a_w, gate_a_b, gate_x_w, gate_x_b, lru_lambda, odd_w_out, norm_final, loss_target, m_norm_ffn1, m_ffn1_wg, m_ffn1_wu, m_ffn1_wd, m_norm_mix, m_norm_ffn2, m_ffn2_wg, m_ffn2_wu, m_ffn2_wd, m_rel_bias, m_even_w_in, m_attn_sinks, m_conv_b_w, m_conv_b_b, m_conv_ln_g, m_conv_ln_b, m_even_w_out, m_odd_w_in, m_lru_conv_w, m_lru_conv_b, m_gate_a_w, m_gate_a_b, m_gate_x_w, m_gate_x_b, m_lru_lambda, m_odd_w_out, m_norm_final, v_norm_ffn1, v_ffn1_wg, v_ffn1_wu, v_ffn1_wd, v_norm_mix, v_norm_ffn2, v_ffn2_wg, v_ffn2_wu, v_ffn2_wd, v_rel_bias, v_even_w_in, v_attn_sinks, v_conv_b_w, v_conv_b_b, v_conv_ln_g, v_conv_ln_b, v_even_w_out, v_odd_w_in, v_lru_conv_w, v_lru_conv_b, v_gate_a_w, v_gate_a_b, v_gate_x_w, v_gate_x_b, v_lru_lambda, v_odd_w_out, v_norm_final):
    weights = (norm_ffn1, ffn1_wg, ffn1_wu, ffn1_wd, norm_mix, norm_ffn2, ffn2_wg, ffn2_wu, ffn2_wd, rel_bias, even_w_in, attn_sinks, conv_b_w, conv_b_b, conv_ln_g, conv_ln_b, even_w_out, odd_w_in, lru_conv_w, lru_conv_b, gate_a_w, gate_a_b, gate_x_w, gate_x_b, lru_lambda, odd_w_out, norm_final)
    ms = (m_norm_ffn1, m_ffn1_wg, m_ffn1_wu, m_ffn1_wd, m_norm_mix, m_norm_ffn2, m_ffn2_wg, m_ffn2_wu, m_ffn2_wd, m_rel_bias, m_even_w_in, m_attn_sinks, m_conv_b_w, m_conv_b_b, m_conv_ln_g, m_conv_ln_b, m_even_w_out, m_odd_w_in, m_lru_conv_w, m_lru_conv_b, m_gate_a_w, m_gate_a_b, m_gate_x_w, m_gate_x_b, m_lru_lambda, m_odd_w_out, m_norm_final)
    vs = (v_norm_ffn1, v_ffn1_wg, v_ffn1_wu, v_ffn1_wd, v_norm_mix, v_norm_ffn2, v_ffn2_wg, v_ffn2_wu, v_ffn2_wd, v_rel_bias, v_even_w_in, v_attn_sinks, v_conv_b_w, v_conv_b_b, v_conv_ln_g, v_conv_ln_b, v_even_w_out, v_odd_w_in, v_lru_conv_w, v_lru_conv_b, v_gate_a_w, v_gate_a_b, v_gate_x_w, v_gate_x_b, v_lru_lambda, v_odd_w_out, v_norm_final)
    return _step(Cfg(), x, weights, loss_target, ms, vs)
```

```python
import math
from typing import NamedTuple

import jax
import jax.numpy as jnp
from jax import lax
from jax.experimental import pallas as pl
from jax.experimental.pallas import tpu as pltpu

F32 = jnp.float32
BF16 = jnp.bfloat16
RMS_EPS = 1e-6
LN_EPS = 1e-5
NEG_INF = -1e30
RG_LRU_C = 8.0
ADAM_LR = 0.001
ADAM_B1 = 0.9
ADAM_B2 = 0.999
ADAM_EPS = 1e-08
ADAM_WD = 0.01
ADAM_STEP = 10
N_DEV = 8
VMEM_LIMIT = 56 * 1024 * 1024


class Cfg(NamedTuple):
    d: int = 1024
    f: int = 2816
    s: int = 2048
    bl: int = 4
    hq: int = 8
    hkv: int = 2
    hd: int = 64
    win: int = 128
    cc: int = 512
    cw: int = 31
    lh: int = 8
    lb: int = 128
    lcw: int = 4
    nbuckets: int = 32
    max_dist: int = 128
    depth: int = 4
    tm: int = 512
    tm_ffn: int = 1024
    tf: int = 256
    tk_ffn: int = 512
    tf_w: int = 1408
    tk: int = 1024
    ct_f: int = 256
    ct_b: int = 256

    @property
    def qw(self):
        return self.hq * self.hd

    @property
    def kvw(self):
        return self.hkv * self.hd

    @property
    def even_in(self):
        return self.qw + 2 * self.kvw + 2 * self.cc

    @property
    def even_cat(self):
        return self.qw + self.cc

    @property
    def lw(self):
        return self.lh * self.lb


def _cparams(*sem):
    return pltpu.CompilerParams(dimension_semantics=sem, vmem_limit_bytes=VMEM_LIMIT)


def _nt(a, b):
    return lax.dot_general(a, b, (((1,), (1,)), ((), ())), preferred_element_type=F32)


def _nn(a, b):
    return lax.dot_general(a, b, (((1,), (0,)), ((), ())), preferred_element_type=F32)


def _tn(a, b):
    return lax.dot_general(a, b, (((0,), (0,)), ((), ())), preferred_element_type=F32)


def _rstd(h):
    return lax.rsqrt(jnp.mean(h * h, axis=-1, keepdims=True) + RMS_EPS)


def _rms_bwd(h, nw, dxn):
    rstd = _rstd(h)
    dyg = dxn * nw
    dnw = jnp.sum(dxn * h * rstd, axis=0, keepdims=True)
    dx = rstd * (dyg - h * (rstd * rstd) * jnp.mean(dyg * h, axis=-1, keepdims=True))
    return dx, dnw


def _sigmoid(x):
    return 0.5 * jnp.tanh(0.5 * x) + 0.5


FFN_SLABS = 4


def _ffn_wspecs(tf, d, gu, md):
    return [pl.BlockSpec((2, tf, d), lambda i, j: (gu, j, 0)), pl.BlockSpec((None, tf, d), lambda i, j: (md, j, 0))]


class Riders(NamedTuple):
    inputs: tuple
    out_shape: tuple
    scratch: tuple
    start: object
    wait: object


def _ride(riders, grid, n_in, n_out, body):
    if riders is None:
        return body, [], [], [], [], []
    ni, no, ns = len(riders.inputs), len(riders.out_shape), len(riders.scratch)

    def full(*refs):
        ins, rin = refs[:n_in], refs[n_in:n_in + ni]
        outs = refs[n_in + ni:n_in + ni + n_out]
        rout = refs[n_in + ni + n_out:n_in + ni + n_out + no]
        rest = refs[n_in + ni + n_out + no:]
        scratch, sems = rest[:len(rest) - ns], rest[len(rest) - ns:]
        first = last = None
        for axis, size in enumerate(grid):
            pid = pl.program_id(axis)
            first = (pid == 0) if first is None else first & (pid == 0)
            last = (pid == size - 1) if last is None else last & (pid == size - 1)

        @pl.when(first)
        def _():
            riders.start(rin, rout, sems)

        body(*ins, *outs, *scratch)

        @pl.when(last)
        def _():
            riders.wait(rin, rout, sems)

    any_spec = pl.BlockSpec(memory_space=pl.ANY)
    return full, list(riders.inputs), [any_spec] * ni, [any_spec] * no, list(riders.out_shape), list(riders.scratch)


def ffn_fwd(cfg, h, nw, wts, gu, md, name, riders=None):
    t, d = h.shape
    f = wts.shape[1]
    tm, tf = cfg.tm_ffn, cfg.tf
    nj = f // tf

    def body(h_ref, nw_ref, wgu_ref, wd_ref, ho_ref, xn_ref, g_ref, u_ref, acc_ref):
        j = pl.program_id(1)

        @pl.when(j == 0)
        def _():
            hh = h_ref[...]
            xn_ref[...] = (hh * _rstd(hh) * nw_ref[...]).astype(BF16)
            acc_ref[...] = jnp.zeros_like(acc_ref)

        gu = _nt(xn_ref[...], wgu_ref[...].reshape(2 * tf, d))
        g, u = gu[:, :tf], gu[:, tf:]
        g_ref[...] = g.astype(BF16)
        u_ref[...] = u.astype(BF16)
        acc_ref[...] += _nn((g * _sigmoid(g) * u).astype(BF16), wd_ref[...])

        @pl.when(j == nj - 1)
        def _():
            ho_ref[...] = h_ref[...] + 0.5 * acc_ref[...]

    row = pl.BlockSpec((tm, d), lambda i, j: (i, 0))
    hid = pl.BlockSpec((tm, tf), lambda i, j: (i, j))
    grid = (t // tm, nj)
    full, r_args, r_in, r_out, r_shape, r_scratch = _ride(riders, grid, 4, 4, body)
    return pl.pallas_call(
        full, name=name, grid=grid,
        in_specs=[row, pl.BlockSpec((1, d), lambda i, j: (0, 0))] + _ffn_wspecs(tf, d, gu, md) + r_in,
        out_specs=[row, row, hid, hid] + r_out,
        out_shape=[jax.ShapeDtypeStruct((t, d), F32), jax.ShapeDtypeStruct((t, d), BF16),
                   jax.ShapeDtypeStruct((t, f), BF16), jax.ShapeDtypeStruct((t, f), BF16)] + r_shape,
        scratch_shapes=[pltpu.VMEM((tm, d), F32)] + r_scratch,
        compiler_params=_cparams("arbitrary", "arbitrary"),
    )(h, nw, wts, wts, *r_args)


def ffn_bwd_x(cfg, dh, h, nw, g, u, wts, gu, md, name, riders=None):
    t, d = h.shape
    f = wts.shape[1]
    tm, tf = cfg.tm_ffn, cfg.tf
    nj = f // tf

    def body(dh_ref, h_ref, nw_ref, g_ref, u_ref, wgu_ref, wd_ref,
             dho_ref, dout_ref, dg_ref, du_ref, dnw_ref, acc_ref, da_ref):
        i, j = pl.program_id(0), pl.program_id(1)

        @pl.when(j == 0)
        def _():
            dout_ref[...] = (0.5 * dh_ref[...]).astype(BF16)
            acc_ref[...] = jnp.zeros_like(acc_ref)
            da_ref[1] = jnp.zeros((tm, tf), F32)

        @pl.when((i == 0) & (j == 0))
        def _():
            dnw_ref[...] = jnp.zeros_like(dnw_ref)

        slot = lax.rem(j, 2)
        da = da_ref[1 - slot]
        da_ref[slot] = _nt(dout_ref[...], wd_ref[...])
        gg = g_ref[...].astype(F32)
        sig = _sigmoid(gg)
        dg = (da * u_ref[...].astype(F32) * (sig * (1.0 + gg * (1.0 - sig)))).astype(BF16)
        du = (da * (gg * sig)).astype(BF16)
        dg_ref[...] = dg
        du_ref[...] = du
        acc_ref[...] += _nn(jnp.concatenate([dg, du], axis=1), wgu_ref[...].reshape(2 * tf, d))

        @pl.when(j == nj)
        def _():
            dx, dnw = _rms_bwd(h_ref[...], nw_ref[...], acc_ref[...])
            dnw_ref[0:1, :] += dnw
            dho_ref[...] = dh_ref[...] + dx

    row = pl.BlockSpec((tm, d), lambda i, j: (i, 0))
    prev = pl.BlockSpec((tm, tf), lambda i, j: (i, jnp.maximum(j - 1, 0)))
    wspecs = [pl.BlockSpec((2, tf, d), lambda i, j: (gu, jnp.maximum(j - 1, 0), 0)),
              pl.BlockSpec((None, tf, d), lambda i, j: (md, jnp.minimum(j, nj - 1), 0))]
    grid = (t // tm, nj + 1)
    full, r_args, r_in, r_out, r_shape, r_scratch = _ride(riders, grid, 7, 5, body)
    return pl.pallas_call(
        full, name=name, grid=grid,
        in_specs=[row, row, pl.BlockSpec((1, d), lambda i, j: (0, 0)), prev, prev] + wspecs + r_in,
        out_specs=[row, row, prev, prev, pl.BlockSpec((8, d), lambda i, j: (0, 0))] + r_out,
        out_shape=[jax.ShapeDtypeStruct((t, d), F32), jax.ShapeDtypeStruct((t, d), BF16),
                   jax.ShapeDtypeStruct((t, f), BF16), jax.ShapeDtypeStruct((t, f), BF16),
                   jax.ShapeDtypeStruct((8, d), F32)] + r_shape,
        scratch_shapes=[pltpu.VMEM((tm, d), F32), pltpu.VMEM((2, tm, tf), F32)] + r_scratch,
        compiler_params=_cparams("arbitrary", "arbitrary"),
    )(dh, h, nw, g, u, wts, wts, *r_args)


def ffn_bwd_w(cfg, xn, dout, g, u, dg, du, name, riders=None):
    t, d = xn.shape
    f = g.shape[1]
    tk, tf = cfg.tk_ffn, cfg.tf_w
    nk = t // tk

    def body(xn_ref, dout_ref, g_ref, u_ref, dg_ref, du_ref, o_ref, acc_ref):
        k = pl.program_id(1)

        @pl.when(k == 0)
        def _():
            acc_ref[...] = jnp.zeros_like(acc_ref)

        gg = g_ref[...].astype(F32)
        a = (gg * _sigmoid(gg) * u_ref[...].astype(F32)).astype(BF16)
        xn_t = xn_ref[...]
        acc_ref[0] += _tn(dg_ref[...], xn_t)
        acc_ref[1] += _tn(du_ref[...], xn_t)
        acc_ref[2] += _tn(a, dout_ref[...])

        @pl.when(k == nk - 1)
        def _():
            o_ref[...] = acc_ref[...].astype(BF16)

    row = pl.BlockSpec((tk, d), lambda j, k: (k, 0))
    hid = pl.BlockSpec((tk, tf), lambda j, k: (k, j))
    grid = (f // tf, nk)
    full, r_args, r_in, r_out, r_shape, r_scratch = _ride(riders, grid, 6, 1, body)
    return pl.pallas_call(
        full, name=name, grid=grid,
        in_specs=[row, row, hid, hid, hid, hid] + r_in,
        out_specs=[pl.BlockSpec((3, tf, d), lambda j, k: (0, j, 0), pipeline_mode=pl.Buffered(1))] + r_out,
        out_shape=[jax.ShapeDtypeStruct((3, f, d), BF16)] + r_shape,
        scratch_shapes=[pltpu.VMEM((3, tf, d), F32)] + r_scratch,
        compiler_params=_cparams("arbitrary", "arbitrary"),
    )(xn, dout, g, u, dg, du, *r_args)


def _wspec(w, wi):
    if w.ndim == 2:
        return pl.BlockSpec(w.shape, lambda i: (0, 0))
    return pl.BlockSpec((None,) + w.shape[1:], lambda i: (wi, 0, 0))


def norm_proj(cfg, h, nw, w, name, wi=0):
    t, d = h.shape
    n = w.shape[-2]
    tm = cfg.tm

    def body(h_ref, nw_ref, w_ref, u_ref, xn_ref):
        hh = h_ref[...]
        xn = (hh * _rstd(hh) * nw_ref[...]).astype(BF16)
        xn_ref[...] = xn
        u_ref[...] = _nt(xn, w_ref[...]).astype(BF16)

    return pl.pallas_call(
        body, name=name, grid=(t // tm,),
        in_specs=[pl.BlockSpec((tm, d), lambda i: (i, 0)), pl.BlockSpec((1, d), lambda i: (0, 0)),
                  _wspec(w, wi)],
        out_specs=[pl.BlockSpec((tm, n), lambda i: (i, 0)), pl.BlockSpec((tm, d), lambda i: (i, 0))],
        out_shape=[jax.ShapeDtypeStruct((t, n), BF16), jax.ShapeDtypeStruct((t, d), BF16)],
        compiler_params=_cparams("arbitrary"),
    )(h, nw, w)


def proj_residual(cfg, h, parts, w, name, wi=0):
    t, d = h.shape
    tm = cfg.tm
    ks = [p.shape[1] for p in parts]
    offs = [sum(ks[:i]) for i in range(len(ks))]
    np_ = len(parts)

    def body(*refs):
        h_ref, w_ref, ho_ref = refs[0], refs[1 + np_], refs[2 + np_]
        acc = h_ref[...]
        for p_ref, off, k in zip(refs[1:1 + np_], offs, ks):
            acc = acc + _nn(p_ref[...], w_ref[off:off + k, :])
        ho_ref[...] = acc

    return pl.pallas_call(
        body, name=name, grid=(t // tm,),
        in_specs=[pl.BlockSpec((tm, d), lambda i: (i, 0))]
        + [pl.BlockSpec((tm, k), lambda i: (i, 0)) for k in ks]
        + [_wspec(w, wi)],
        out_specs=pl.BlockSpec((tm, d), lambda i: (i, 0)),
        out_shape=jax.ShapeDtypeStruct((t, d), F32),
        compiler_params=_cparams("arbitrary"),
    )(h, *parts, w)


def proj_bwd_act(cfg, dh, w, name, wi=0):
    t, d = dh.shape
    k = w.shape[-2]
    tm = cfg.tm

    def body(dh_ref, w_ref, o_ref):
        o_ref[...] = _nt(dh_ref[...].astype(BF16), w_ref[...]).astype(BF16)

    return pl.pallas_call(
        body, name=name, grid=(t // tm,),
        in_specs=[pl.BlockSpec((tm, d), lambda i: (i, 0)), _wspec(w, wi)],
        out_specs=pl.BlockSpec((tm, k), lambda i: (i, 0)),
        out_shape=jax.ShapeDtypeStruct((t, k), BF16),
        compiler_params=_cparams("arbitrary"),
    )(dh, w)


def grad_weight(cfg, parts, b, name):
    t, d = b.shape
    tk = cfg.tk
    nk = t // tk
    ks = [p.shape[1] for p in parts]
    offs = [sum(ks[:i]) for i in range(len(ks))]
    np_ = len(parts)

    def body(*refs):
        b_ref, o_ref, acc_ref = refs[np_:]
        kk = pl.program_id(0)

        @pl.when(kk == 0)
        def _():
            acc_ref[...] = jnp.zeros_like(acc_ref)

        bb = b_ref[...].astype(BF16)
        for a_ref, off, k in zip(refs[:np_], offs, ks):
            acc_ref[off:off + k, :] += _tn(a_ref[...], bb)

        @pl.when(kk == nk - 1)
        def _():
            o_ref[...] = acc_ref[...].astype(BF16)

    return pl.pallas_call(
        body, name=name, grid=(nk,),
        in_specs=[pl.BlockSpec((tk, k), lambda kk: (kk, 0)) for k in ks] + [pl.BlockSpec((tk, d), lambda kk: (kk, 0))],
        out_specs=pl.BlockSpec((sum(ks), d), lambda kk: (0, 0)),
        out_shape=jax.ShapeDtypeStruct((sum(ks), d), BF16),
        scratch_shapes=[pltpu.VMEM((sum(ks), d), F32)],
        compiler_params=_cparams("arbitrary"),
    )(*parts, b)


def norm_proj_bwd(cfg, dh, h, nw, parts, w, name, wi=0):
    t, d = h.shape
    tm = cfg.tm
    ks = [p.shape[1] for p in parts]
    offs = [sum(ks[:i]) for i in range(len(ks))]
    np_ = len(parts)

    def body(*refs):
        dh_ref, h_ref, nw_ref = refs[:3]
        w_ref, dho_ref, dnw_ref = refs[3 + np_:]

        @pl.when(pl.program_id(0) == 0)
        def _():
            dnw_ref[...] = jnp.zeros_like(dnw_ref)

        dxn = None
        for p_ref, off, k in zip(refs[3:3 + np_], offs, ks):
            term = _nn(p_ref[...], w_ref[off:off + k, :])
            dxn = term if dxn is None else dxn + term
        dx, dnw = _rms_bwd(h_ref[...], nw_ref[...], dxn)
        dnw_ref[0:1, :] += dnw
        dho_ref[...] = dh_ref[...] + dx

    row = pl.BlockSpec((tm, d), lambda i: (i, 0))
    return pl.pallas_call(
        body, name=name, grid=(t // tm,),
        in_specs=[row, row, pl.BlockSpec((1, d), lambda i: (0, 0))]
        + [pl.BlockSpec((tm, k), lambda i: (i, 0)) for k in ks]
        + [_wspec(w, wi)],
        out_specs=[row, pl.BlockSpec((8, d), lambda i: (0, 0))],
        out_shape=[jax.ShapeDtypeStruct((t, d), F32), jax.ShapeDtypeStruct((8, d), F32)],
        compiler_params=_cparams("arbitrary"),
    )(dh, h, nw, *parts, w)


def loss_head(cfg, h, nf, tgt, name):
    t, d = h.shape
    tm = cfg.tm

    def body(h_ref, nf_ref, tgt_ref, loss_ref, dh_ref, dnf_ref):
        @pl.when(pl.program_id(0) == 0)
        def _():
            loss_ref[...] = jnp.zeros_like(loss_ref)
            dnf_ref[...] = jnp.zeros_like(dnf_ref)

        hh = h_ref[...]
        err = hh * _rstd(hh) * nf_ref[...] - tgt_ref[...]
        row = jnp.sum(err * err, axis=-1, keepdims=True) * (0.5 / d)
        loss_ref[...] += jnp.sum(row, axis=0, keepdims=True)
        dx, dnf = _rms_bwd(hh, nf_ref[...], err * (1.0 / d))
        dnf_ref[0:1, :] += dnf
        dh_ref[...] = dx

    row = pl.BlockSpec((tm, d), lambda i: (i, 0))
    return pl.pallas_call(
        body, name=name, grid=(t // tm,),
        in_specs=[row, pl.BlockSpec((1, d), lambda i: (0, 0)), row],
        out_specs=[pl.BlockSpec((8, 128), lambda i: (0, 0)), row, pl.BlockSpec((8, d), lambda i: (0, 0))],
        out_shape=[jax.ShapeDtypeStruct((8, 128), F32), jax.ShapeDtypeStruct((t, d), F32),
                   jax.ShapeDtypeStruct((8, d), F32)],
        compiler_params=_cparams("arbitrary"),
    )(h, nf, tgt)


def bucket_table(cfg):
    qi = jnp.arange(cfg.win)[:, None]
    sj = jnp.arange(2 * cfg.win)[None, :]
    dist = qi + cfg.win - sj
    n = jnp.maximum(dist, 0)
    max_exact = cfg.nbuckets // 2
    nf = jnp.maximum(n, max_exact).astype(F32)
    large = max_exact + (jnp.log(nf / max_exact) / math.log(cfg.max_dist / max_exact)
                         * (cfg.nbuckets - max_exact)).astype(jnp.int32)
    large = jnp.minimum(large, cfg.nbuckets - 1)
    bucket = jnp.where(n < max_exact, n, large)
    return jnp.where((dist >= 0) & (dist < cfg.win), bucket, -1).astype(jnp.int32)


def bias_build(cfg, rel_bias, buckets, name):
    w = cfg.win

    def body(rb_ref, bk_ref, o_ref):
        bk = bk_ref[...]
        for h in range(cfg.hq):
            acc = jnp.full((w, 2 * w), NEG_INF, F32)
            for b in range(cfg.nbuckets):
                acc = jnp.where(bk == b, rb_ref[b, h], acc)
            o_ref[h] = acc

    return pl.pallas_call(
        body, name=name,
        in_specs=[pl.BlockSpec(memory_space=pltpu.SMEM), pl.BlockSpec(memory_space=pltpu.VMEM)],
        out_specs=pl.BlockSpec(memory_space=pltpu.VMEM),
        out_shape=jax.ShapeDtypeStruct((cfg.hq, w, 2 * w), F32),
    )(rel_bias, buckets)


def bias_grad(cfg, dbias, buckets, name):
    w = cfg.win

    def body(db_ref, bk_ref, o_ref, rows_ref):
        bk = bk_ref[...]
        for h in range(cfg.hq):
            d = db_ref[0, h]
            for e in range(1, dbias.shape[0]):
                d = d + db_ref[e, h]
            for b in range(cfg.nbuckets):
                rows_ref[b:b + 1, :] = jnp.sum(jnp.where(bk == b, d, 0.0), axis=0, keepdims=True)
            o_ref[h] = jnp.broadcast_to(jnp.sum(rows_ref[...], axis=1, keepdims=True), (cfg.nbuckets, 128))

    return pl.pallas_call(
        body, name=name,
        in_specs=[pl.BlockSpec(memory_space=pltpu.VMEM), pl.BlockSpec(memory_space=pltpu.VMEM)],
        out_specs=pl.BlockSpec(memory_space=pltpu.VMEM),
        out_shape=jax.ShapeDtypeStruct((cfg.hq, cfg.nbuckets, 128), F32),
        scratch_shapes=[pltpu.VMEM((cfg.nbuckets, 2 * w), F32)],
    )(dbias, buckets)


def _attn_probs(cfg, qk, bias_h, sink, first_ok):
    s = qk * (1.0 / math.sqrt(cfg.hd)) + bias_h
    s = jnp.where(first_ok, s, NEG_INF)
    m = jnp.maximum(jnp.max(s, axis=-1, keepdims=True), sink)
    e = jnp.exp(s - m)
    es = jnp.exp(sink - m)
    inv = 1.0 / (jnp.sum(e, axis=-1, keepdims=True) + es)
    return e * inv, es * inv


def _attn_block_inputs(cfg, n, q_ref, kv_ref):
    w = cfg.win
    r0 = pl.multiple_of(n * w, w)
    rp = pl.multiple_of(jnp.maximum(n - 1, 0) * w, w)
    qb = q_ref[pl.ds(r0, w), :]
    kk = jnp.concatenate([kv_ref[pl.ds(rp, w), :], kv_ref[pl.ds(r0, w), :]], axis=0)
    col = lax.broadcasted_iota(jnp.int32, (w, 2 * w), 1)
    first_ok = (n > 0) | (col >= w)
    return r0, rp, qb, kk, first_ok


def _kv_col_block(cfg):
    assert cfg.qw % (2 * cfg.kvw) == 0
    return cfg.qw // (2 * cfg.kvw)


def attn_fwd(cfg, u, bias, sinks, name):
    t = u.shape[0]
    s, w, hd, g = cfg.s, cfg.win, cfg.hd, cfg.hq // cfg.hkv
    kvb = _kv_col_block(cfg)

    def body(q_ref, kv_ref, bias_ref, sink_ref, o_ref):
        def blk(n, carry):
            r0, _, qb, kk, first_ok = _attn_block_inputs(cfg, n, q_ref, kv_ref)
            heads = range(cfg.hq)
            scores = [_nt(qb[:, hd * h:hd * (h + 1)], kk[:, hd * (h // g):hd * (h // g + 1)]) for h in heads]
            probs = [_attn_probs(cfg, scores[h], bias_ref[h], sink_ref[h], first_ok)[0].astype(BF16) for h in heads]
            outs = [_nn(probs[h], kk[:, cfg.kvw + hd * (h // g):cfg.kvw + hd * (h // g + 1)]) for h in heads]
            o_ref[pl.ds(r0, w), :] = jnp.concatenate(outs, axis=1).astype(BF16)
            return carry

        lax.fori_loop(0, s // w, blk, 0)

    return pl.pallas_call(
        body, name=name, grid=(t // s,),
        in_specs=[pl.BlockSpec((s, cfg.qw), lambda b: (b, 0)), pl.BlockSpec((s, 2 * cfg.kvw), lambda b: (b, kvb)),
                  pl.BlockSpec(bias.shape, lambda b: (0, 0, 0)), pl.BlockSpec(memory_space=pltpu.SMEM)],
        out_specs=pl.BlockSpec((s, cfg.qw), lambda b: (b, 0)),
        out_shape=jax.ShapeDtypeStruct((t, cfg.qw), BF16),
        compiler_params=_cparams("arbitrary"),
    )(u, u, bias, sinks)


def attn_bwd(cfg, u, dcat, bias, sinks, name):
    t = u.shape[0]
    s, w, hd, g = cfg.s, cfg.win, cfg.hd, cfg.hq // cfg.hkv
    kvb = _kv_col_block(cfg)
    scale = 1.0 / math.sqrt(hd)
    assert cfg.hq <= 8

    def body(q_ref, kv_ref, do_ref, bias_ref, sink_ref, du_ref, dbias_ref, dsink_ref, dkv_ref):
        @pl.when(pl.program_id(0) == 0)
        def _():
            dbias_ref[...] = jnp.zeros_like(dbias_ref)
            dsink_ref[...] = jnp.zeros_like(dsink_ref)

        dkv_ref[...] = jnp.zeros_like(dkv_ref)

        def blk(n, carry):
            r0, rp, qb, kk, first_ok = _attn_block_inputs(cfg, n, q_ref, kv_ref)
            dob = do_ref[pl.ds(r0, w), :]
            heads = range(cfg.hq)
            kjs = [kk[:, hd * j:hd * (j + 1)] for j in range(cfg.hkv)]
            vjs = [kk[:, cfg.kvw + hd * j:cfg.kvw + hd * (j + 1)] for j in range(cfg.hkv)]
            qhs = [qb[:, hd * h:hd * (h + 1)] for h in heads]
            dohs = [dob[:, hd * h:hd * (h + 1)] for h in heads]
            scores = [_nt(qhs[h], kjs[h // g]) for h in heads]
            dps = [_nt(dohs[h], vjs[h // g]) for h in heads]
            pbs, dsbs = [], []
            for h in heads:
                p, ps = _attn_probs(cfg, scores[h], bias_ref[h], sink_ref[h], first_ok)
                delta = jnp.sum(p * dps[h], axis=-1, keepdims=True)
                ds = p * (dps[h] - delta)
                dsink_ref[h:h + 1, :] += jnp.broadcast_to(-jnp.sum(ps * delta, axis=0, keepdims=True), (1, 128))
                dbias_ref[h] += ds
                pbs.append(p.astype(BF16))
                dsbs.append(ds.astype(BF16))
            dqs = []
            dks = [jnp.zeros((2 * w, hd), F32) for _ in range(cfg.hkv)]
            dvs = [jnp.zeros((2 * w, hd), F32) for _ in range(cfg.hkv)]
            for h in heads:
                dqs.append(_nn(dsbs[h], kjs[h // g]) * scale)
                dks[h // g] = dks[h // g] + _tn(dsbs[h], qhs[h]) * scale
                dvs[h // g] = dvs[h // g] + _tn(pbs[h], dohs[h])
            du_ref[pl.ds(r0, w), 0:cfg.qw] = jnp.concatenate(dqs, axis=1).astype(BF16)
            dkv = jnp.concatenate(dks + dvs, axis=1)
            dkv_ref[pl.ds(rp, w), :] += dkv[:w]
            dkv_ref[pl.ds(r0, w), :] += dkv[w:]
            return carry

        lax.fori_loop(0, s // w, blk, 0)
        du_ref[:, cfg.qw:] = dkv_ref[...].astype(BF16)

    wa = cfg.qw + 2 * cfg.kvw
    return pl.pallas_call(
        body, name=name, grid=(t // s,),
        in_specs=[pl.BlockSpec((s, cfg.qw), lambda b: (b, 0)), pl.BlockSpec((s, 2 * cfg.kvw), lambda b: (b, kvb)),
                  pl.BlockSpec((s, cfg.qw), lambda b: (b, 0)),
                  pl.BlockSpec(bias.shape, lambda b: (0, 0, 0)), pl.BlockSpec(memory_space=pltpu.SMEM)],
        out_specs=[pl.BlockSpec((s, wa), lambda b: (b, 0)), pl.BlockSpec(bias.shape, lambda b: (0, 0, 0)),
                   pl.BlockSpec((8, 128), lambda b: (0, 0))],
        out_shape=[jax.ShapeDtypeStruct((t, wa), BF16), jax.ShapeDtypeStruct(bias.shape, F32),
                   jax.ShapeDtypeStruct((8, 128), F32)],
        scratch_shapes=[pltpu.VMEM((s, 2 * cfg.kvw), F32)],
        compiler_params=_cparams("arbitrary"),
    )(u, u, dcat, bias, sinks)


def _shift_views(win, rc, pad):
    return [win] + [win[j:j + rc + pad - 8] for j in range(1, 8)]


def _tap(views, off, rc):
    a = 8 * (off // 8)
    return views[off % 8][a:a + rc]


def _conv_rows(views, w_ref, cw, pad, rc, lanes=slice(None)):
    acc = None
    for k in range(cw):
        term = _tap(views, pad - (cw - 1) + k, rc) * w_ref[k:k + 1, lanes]
        acc = term if acc is None else acc + term
    return acc


def _conv_rows_t(views, w_ref, cw, rc, lanes=slice(None)):
    acc = None
    for k in range(cw):
        term = _tap(views, cw - 1 - k, rc) * w_ref[k:k + 1, lanes]
        acc = term if acc is None else acc + term
    return acc


def _group_sum(x):
    acc = x[0:8]
    for i in range(1, x.shape[0] // 8):
        acc = acc + x[8 * i:8 * i + 8]
    return acc


def _conv_wgrad(views, dy, acc_ref, cw, pad, rc, lanes=slice(None)):
    for k in range(cw):
        acc_ref[k, :, lanes] += _group_sum(dy * _tap(views, pad - (cw - 1) + k, rc))


LANE_TILE = 128


def _lane_tiles(width):
    return [slice(c0, c0 + LANE_TILE) for c0 in range(0, width, LANE_TILE)]


def _conv_rows_tiled(x_ref, r0, w_ref, cw, pad, rc):
    return jnp.concatenate([_conv_rows(_shift_views(x_ref[pl.ds(r0, rc + pad), lanes], rc, pad), w_ref, cw, pad, rc, lanes)
                            for lanes in _lane_tiles(x_ref.shape[1])], axis=1)


def _conv_rows_t_tiled(x_ref, r0, w_ref, cw, pad, rc):
    return jnp.concatenate([_conv_rows_t(_shift_views(x_ref[pl.ds(r0, rc + pad), lanes], rc, pad), w_ref, cw, rc, lanes)
                            for lanes in _lane_tiles(x_ref.shape[1])], axis=1)


def _conv_wgrad_tiled(x_ref, r0, dy, acc_ref, cw, pad, rc):
    for lanes in _lane_tiles(x_ref.shape[1]):
        _conv_wgrad(_shift_views(x_ref[pl.ds(r0, rc + pad), lanes], rc, pad), dy[:, lanes], acc_ref, cw, pad, rc, lanes)


CONV_RC = 64
CONV_PAD = 32
GLU_RC = 256


def _conv_col_blocks(cfg):
    off = cfg.qw + 2 * cfg.kvw
    bw = math.gcd(off, cfg.cc)
    assert bw % 128 == 0
    n = cfg.cc // bw
    return bw, [off // bw + i for i in range(n)], [(off + cfg.cc) // bw + i for i in range(n)]


def _glu_inputs(a_refs, b_refs, rows):
    ga = jnp.concatenate([r[rows, :] for r in a_refs], axis=1).astype(F32)
    gb = jnp.concatenate([r[rows, :] for r in b_refs], axis=1).astype(F32)
    return ga, gb


def _fill_glu(cfg, a_refs, b_refs, xp_ref):
    xp_ref[0:CONV_PAD, :] = jnp.zeros((CONV_PAD, cfg.cc), F32)

    def fill(i, carry):
        r0 = pl.multiple_of(i * GLU_RC, GLU_RC)
        ga, gb = _glu_inputs(a_refs, b_refs, pl.ds(r0, GLU_RC))
        xp_ref[pl.ds(CONV_PAD + r0, GLU_RC), :] = ga * _sigmoid(gb)
        return carry

    lax.fori_loop(0, cfg.s // GLU_RC, fill, 0)


def _layernorm_stats(cv):
    mu = jnp.mean(cv, axis=-1, keepdims=True)
    xc = cv - mu
    rstd = lax.rsqrt(jnp.mean(xc * xc, axis=-1, keepdims=True) + LN_EPS)
    return xc * rstd, rstd


def conv_fwd(cfg, u, cw_w, cb, lg, lb, name):
    t = u.shape[0]
    s, cc, cw = cfg.s, cfg.cc, cfg.cw
    bw, a_idx, b_idx = _conv_col_blocks(cfg)
    nb = len(a_idx)

    def body(*refs):
        a_refs, b_refs = refs[:nb], refs[nb:2 * nb]
        w_ref, cb_ref, lg_ref, lb_ref, o_ref, cv_ref, xp_ref = refs[2 * nb:]
        _fill_glu(cfg, a_refs, b_refs, xp_ref)

        def chunk(i, carry):
            r0 = pl.multiple_of(i * CONV_RC, CONV_RC)
            cv = _conv_rows_tiled(xp_ref, r0, w_ref, cw, CONV_PAD, CONV_RC) + cb_ref[...]
            cv_ref[pl.ds(r0, CONV_RC), :] = cv
            xhat, _ = _layernorm_stats(cv)
            ln = xhat * lg_ref[...] + lb_ref[...]
            o_ref[pl.ds(r0, CONV_RC), :] = (ln * _sigmoid(ln)).astype(BF16)
            return carry

        lax.fori_loop(0, s // CONV_RC, chunk, 0, unroll=2)

    def colspec(j):
        return pl.BlockSpec((s, bw), lambda b: (b, j))

    vec = pl.BlockSpec((1, cc), lambda b: (0, 0))
    return pl.pallas_call(
        body, name=name, grid=(t // s,),
        in_specs=[colspec(j) for j in a_idx + b_idx] + [pl.BlockSpec((cw, cc), lambda b: (0, 0)), vec, vec, vec],
        out_specs=[pl.BlockSpec((s, cc), lambda b: (b, 0))] * 2,
        out_shape=[jax.ShapeDtypeStruct((t, cc), BF16), jax.ShapeDtypeStruct((t, cc), F32)],
        scratch_shapes=[pltpu.VMEM((CONV_PAD + s, cc), F32)],
        compiler_params=_cparams("arbitrary"),
    )(*([u] * (2 * nb)), cw_w, cb, lg, lb)


def conv_bwd(cfg, u, cv_saved, dcat, cw_w, cb, lg, lb, name):
    t = u.shape[0]
    s, cc, cw = cfg.s, cfg.cc, cfg.cw
    bw, a_idx, b_idx = _conv_col_blocks(cfg)
    nb = len(a_idx)
    assert cfg.qw % cc == 0 and cw <= 32

    def body(*refs):
        a_refs, b_refs = refs[:nb], refs[nb:2 * nb]
        cv_ref, dc_ref, w_ref, cb_ref, lg_ref, lb_ref, du_ref, dw_ref, dvec_ref, xp_ref, dcv_ref, dwacc_ref = refs[2 * nb:]

        @pl.when(pl.program_id(0) == 0)
        def _():
            dw_ref[...] = jnp.zeros_like(dw_ref)
            dvec_ref[...] = jnp.zeros_like(dvec_ref)

        _fill_glu(cfg, a_refs, b_refs, xp_ref)
        dcv_ref[s:s + CONV_PAD, :] = jnp.zeros((CONV_PAD, cc), F32)
        dwacc_ref[...] = jnp.zeros_like(dwacc_ref)

        def chunk(i, carry):
            r0 = pl.multiple_of(i * CONV_RC, CONV_RC)
            xhat, rstd = _layernorm_stats(cv_ref[pl.ds(r0, CONV_RC), :])
            ln = xhat * lg_ref[...] + lb_ref[...]
            sg = _sigmoid(ln)
            dln = dc_ref[pl.ds(r0, CONV_RC), :].astype(F32) * (sg * (1.0 + ln * (1.0 - sg)))
            dxh = dln * lg_ref[...]
            dcv = rstd * (dxh - jnp.mean(dxh, axis=-1, keepdims=True)
                          - xhat * jnp.mean(dxh * xhat, axis=-1, keepdims=True))
            dcv_ref[pl.ds(r0, CONV_RC), :] = dcv
            dvec_ref[0:1, :] += jnp.sum(dcv, axis=0, keepdims=True)
            dvec_ref[1:2, :] += jnp.sum(dln * xhat, axis=0, keepdims=True)
            dvec_ref[2:3, :] += jnp.sum(dln, axis=0, keepdims=True)
            _conv_wgrad_tiled(xp_ref, r0, dcv, dwacc_ref, cw, CONV_PAD, CONV_RC)
            return carry

        lax.fori_loop(0, s // CONV_RC, chunk, 0, unroll=2)
        for k in range(cw):
            dw_ref[k:k + 1, :] += jnp.sum(dwacc_ref[k], axis=0, keepdims=True)

        def chunk2(i, carry):
            r0 = pl.multiple_of(i * CONV_RC, CONV_RC)
            dglu = _conv_rows_t_tiled(dcv_ref, r0, w_ref, cw, CONV_PAD, CONV_RC)
            ga, gb = _glu_inputs(a_refs, b_refs, pl.ds(r0, CONV_RC))
            sgb = _sigmoid(gb)
            du_ref[pl.ds(r0, CONV_RC), 0:cc] = (dglu * sgb).astype(BF16)
            du_ref[pl.ds(r0, CONV_RC), cc:2 * cc] = (dglu * ga * sgb * (1.0 - sgb)).astype(BF16)
            return carry

        lax.fori_loop(0, s // CONV_RC, chunk2, 0)

    def colspec(j):
        return pl.BlockSpec((s, bw), lambda b: (b, j))

    vec = pl.BlockSpec((1, cc), lambda b: (0, 0))
    return pl.pallas_call(
        body, name=name, grid=(t // s,),
        in_specs=[colspec(j) for j in a_idx + b_idx]
        + [pl.BlockSpec((s, cc), lambda b: (b, 0)), pl.BlockSpec((s, cc), lambda b: (b, cfg.qw // cc)),
           pl.BlockSpec((cw, cc), lambda b: (0, 0)), vec, vec, vec],
        out_specs=[pl.BlockSpec((s, 2 * cc), lambda b: (b, 0)), pl.BlockSpec((32, cc), lambda b: (0, 0)),
                   pl.BlockSpec((8, cc), lambda b: (0, 0))],
        out_shape=[jax.ShapeDtypeStruct((t, 2 * cc), BF16), jax.ShapeDtypeStruct((32, cc), F32),
                   jax.ShapeDtypeStruct((8, cc), F32)],
        scratch_shapes=[pltpu.VMEM((CONV_PAD + s, cc), F32), pltpu.VMEM((s + CONV_PAD, cc), F32),
                        pltpu.VMEM((cw, 8, cc), F32)],
        compiler_params=_cparams("arbitrary"),
    )(*([u] * (2 * nb)), cv_saved, dcat, cw_w, cb, lg, lb)


LRU_RC = 64
LRU_PAD = 8
SCAN_RC = 16
SCAN_UNROLL = 4
GELU_K = math.sqrt(2.0 / math.pi)


def _expm1_neg(z):
    return jnp.where(z > -0.05, z * (1.0 + z * (0.5 + z * (1.0 / 6.0 + z * (1.0 / 24.0)))), jnp.exp(z) - 1.0)


def _log_sigmoid(x):
    e = jnp.exp(-jnp.abs(x))
    log1p = jnp.where(e < 0.01, e * (1.0 - e * (0.5 - e * (1.0 / 3.0))), jnp.log(1.0 + e))
    return jnp.minimum(x, 0.0) - log1p


def _gelu(x):
    t = jnp.tanh(GELU_K * (x + 0.044715 * x * x * x))
    return 0.5 * x * (1.0 + t), t


def _gelu_grad(x, t):
    return 0.5 * (1.0 + t) + 0.5 * x * (1.0 - t * t) * GELU_K * (1.0 + 3.0 * 0.044715 * x * x)


def _lru_gates(xc, wa_ref, ba, wx_ref, bx, ls):
    nh = xc.shape[1] // 128
    xb = xc.astype(BF16)
    ra = jnp.concatenate([_nn(xb[:, 128 * h:128 * (h + 1)], wa_ref[h]) for h in range(nh)], axis=1) + ba
    ia = jnp.concatenate([_nn(xb[:, 128 * h:128 * (h + 1)], wx_ref[h]) for h in range(nh)], axis=1) + bx
    r = _sigmoid(ra)
    ig = _sigmoid(ia)
    log_a = RG_LRU_C * r * ls
    return r, ig, log_a


def _lru_decay(log_a):
    return jnp.exp(log_a), jnp.sqrt(-_expm1_neg(2.0 * log_a))


def _fill_padded(src_ref, dst_ref, s, ct):
    dst_ref[0:LRU_PAD, :] = jnp.zeros((LRU_PAD, ct), F32)

    def fill(i, carry):
        r0 = pl.multiple_of(i * GLU_RC, GLU_RC)
        dst_ref[pl.ds(LRU_PAD + r0, GLU_RC), :] = src_ref[pl.ds(r0, GLU_RC), :].astype(F32)
        return carry

    lax.fori_loop(0, s // GLU_RC, fill, 0)


def _lru_specs(cfg, ct):
    s, lw = cfg.s, cfg.lw
    nct = lw // ct
    nh = ct // 128
    act = [pl.BlockSpec((s, ct), lambda c, b: (b, c)), pl.BlockSpec((s, ct), lambda c, b: (b, nct + c))]
    vec = pl.BlockSpec((1, ct), lambda c, b: (0, c))
    gate_w = pl.BlockSpec((nh, 128, 128), lambda c, b: (c, 0, 0))
    params = [pl.BlockSpec((cfg.lcw, ct), lambda c, b: (0, c)), vec, gate_w, vec, gate_w, vec, vec]
    return nct, act, params


def lru_fwd(cfg, u, conv_w, conv_b, wa, ba, wx, bx, lam, name):
    t = u.shape[0]
    s, lw, lcw, ct = cfg.s, cfg.lw, cfg.lcw, cfg.ct_f
    nct, act, params = _lru_specs(cfg, ct)

    def body(gi_ref, ri_ref, cw_ref, cb_ref, wa_ref, ba_ref, wx_ref, bx_ref, lam_ref,
             y_ref, hs_ref, r_ref, ig_ref, xc_ref, la_ref, xp_ref, a_ref, b_ref):
        _fill_padded(ri_ref, xp_ref, s, ct)
        ls = _log_sigmoid(lam_ref[...])

        def chunk(i, carry):
            r0 = pl.multiple_of(i * LRU_RC, LRU_RC)
            rows = pl.ds(r0, LRU_RC)
            views = _shift_views(xp_ref[pl.ds(r0, LRU_RC + LRU_PAD), :], LRU_RC, LRU_PAD)
            xc = _conv_rows(views, cw_ref, lcw, LRU_PAD, LRU_RC) + cb_ref[...]
            r, ig, log_a = _lru_gates(xc, wa_ref, ba_ref[...], wx_ref, bx_ref[...], ls)
            a, mult = _lru_decay(log_a)
            a_ref[rows, :] = a
            b_ref[rows, :] = mult * (ig * xc)
            r_ref[rows, :] = r.astype(BF16)
            ig_ref[rows, :] = ig.astype(BF16)
            xc_ref[rows, :] = xc.astype(BF16)
            la_ref[rows, :] = log_a
            return carry

        lax.fori_loop(0, s // LRU_RC, chunk, 0, unroll=2)
        row = lax.broadcasted_iota(jnp.int32, (SCAN_RC, ct), 0)

        def scan(i, h_last):
            for sub in range(SCAN_UNROLL):
                rows = pl.ds(pl.multiple_of((i * SCAN_UNROLL + sub) * SCAN_RC, SCAN_RC), SCAN_RC)
                a = a_ref[rows, :]
                b = b_ref[rows, :]
                sft = 1
                while sft < SCAN_RC:
                    a_sh = jnp.where(row >= sft, pltpu.roll(a, sft, 0), 1.0)
                    b_sh = jnp.where(row >= sft, pltpu.roll(b, sft, 0), 0.0)
                    b = a * b_sh + b
                    a = a * a_sh
                    sft *= 2
                h = a * h_last + b
                gate, _ = _gelu(gi_ref[rows, :].astype(F32))
                y_ref[rows, :] = (gate * h).astype(BF16)
                hs_ref[rows, :] = h.astype(BF16)
                h_last = h[SCAN_RC - 1:SCAN_RC, :]
            return h_last

        lax.fori_loop(0, s // (SCAN_RC * SCAN_UNROLL), scan, jnp.zeros((1, ct), F32))

    out = pl.BlockSpec((s, ct), lambda c, b: (b, c))
    return pl.pallas_call(
        body, name=name, grid=(nct, t // s),
        in_specs=act + params,
        out_specs=[out] * 6,
        out_shape=[jax.ShapeDtypeStruct((t, lw), BF16)] * 5 + [jax.ShapeDtypeStruct((t, lw), F32)],
        scratch_shapes=[pltpu.VMEM((LRU_PAD + s, ct), F32), pltpu.VMEM((s, ct), F32), pltpu.VMEM((s, ct), F32)],
        compiler_params=_cparams("arbitrary", "arbitrary"),
    )(u, u, conv_w, conv_b, wa, ba, wx, bx, lam)


def lru_bwd(cfg, u, saved, dy, conv_w, conv_b, wa, ba, wx, bx, lam, name):
    t = u.shape[0]
    s, lw, lcw, ct = cfg.s, cfg.lw, cfg.lcw, cfg.ct_b
    nct, act, params = _lru_specs(cfg, ct)
    nh = ct // 128
    nscan = s // SCAN_RC
    assert lcw <= 8

    def body(gi_ref, ri_ref, hs_ref, r_ref, ig_ref, xc_ref, la_ref, dy_ref,
             cw_ref, cb_ref, wa_ref, ba_ref, wx_ref, bx_ref, lam_ref,
             dug_ref, dur_ref, dwa_ref, dwx_ref, dvec_ref, dcw_ref,
             xp_ref, hp_ref, a_ref, g_ref, dxc_ref, dwacc_ref):
        @pl.when(pl.program_id(1) == 0)
        def _():
            dwa_ref[...] = jnp.zeros_like(dwa_ref)
            dwx_ref[...] = jnp.zeros_like(dwx_ref)
            dvec_ref[...] = jnp.zeros_like(dvec_ref)
            dcw_ref[...] = jnp.zeros_like(dcw_ref)

        _fill_padded(ri_ref, xp_ref, s, ct)
        _fill_padded(hs_ref, hp_ref, s, ct)
        dxc_ref[s:s + LRU_PAD, :] = jnp.zeros((LRU_PAD, ct), F32)
        dwacc_ref[...] = jnp.zeros_like(dwacc_ref)
        lam = lam_ref[...]
        ls = _log_sigmoid(lam)

        def chunk(i, carry):
            r0 = pl.multiple_of(i * LRU_RC, LRU_RC)
            rows = pl.ds(r0, LRU_RC)
            a_ref[rows, :] = jnp.exp(la_ref[rows, :])
            x = gi_ref[rows, :].astype(F32)
            gate, th = _gelu(x)
            dyv = dy_ref[rows, :].astype(F32)
            g_ref[rows, :] = dyv * gate
            dug_ref[rows, :] = (dyv * hp_ref[pl.ds(LRU_PAD + r0, LRU_RC), :] * _gelu_grad(x, th)).astype(BF16)
            return carry

        lax.fori_loop(0, s // LRU_RC, chunk, 0, unroll=2)
        row = lax.broadcasted_iota(jnp.int32, (SCAN_RC, ct), 0)

        def scan(ii, carry):
            g_next, a_next = carry
            for sub in range(SCAN_UNROLL):
                step = nscan - 1 - (ii * SCAN_UNROLL + sub)
                rows = pl.ds(pl.multiple_of(step * SCAN_RC, SCAN_RC), SCAN_RC)
                a = a_ref[rows, :]
                d = g_ref[rows, :]
                c = jnp.where(row < SCAN_RC - 1, pltpu.roll(a, SCAN_RC - 1, 0), a_next)
                sft = 1
                while sft < SCAN_RC:
                    c_sh = jnp.where(row < SCAN_RC - sft, pltpu.roll(c, SCAN_RC - sft, 0), 1.0)
                    d_sh = jnp.where(row < SCAN_RC - sft, pltpu.roll(d, SCAN_RC - sft, 0), 0.0)
                    d = d + c * d_sh
                    c = c * c_sh
                    sft *= 2
                g = d + c * g_next
                g_ref[rows, :] = g
                g_next, a_next = g[0:1, :], a[0:1, :]
            return g_next, a_next

        lax.fori_loop(0, nscan // SCAN_UNROLL, scan, (jnp.zeros((1, ct), F32), jnp.zeros((1, ct), F32)))

        def chunk3(i, carry):
            r0 = pl.multiple_of(i * LRU_RC, LRU_RC)
            rows = pl.ds(r0, LRU_RC)
            r, ig, xc = r_ref[rows, :].astype(F32), ig_ref[rows, :].astype(F32), xc_ref[rows, :].astype(F32)
            a, mult = _lru_decay(la_ref[rows, :])
            g = g_ref[rows, :]
            h_prev = hp_ref[pl.ds(r0, LRU_RC + LRU_PAD), :][LRU_PAD - 1:LRU_PAD - 1 + LRU_RC]
            dix = g * mult
            di = dix * xc
            dxc = dix * ig
            da = g * h_prev - (g * ig * xc) * a / mult
            dlog_a = da * a
            dr = dlog_a * (RG_LRU_C * ls)
            dra = dr * r * (1.0 - r)
            dia = di * ig * (1.0 - ig)
            xb, drab, diab = xc.astype(BF16), dra.astype(BF16), dia.astype(BF16)
            dxg = []
            for h in range(nh):
                cols = slice(128 * h, 128 * (h + 1))
                dxg.append(_nt(drab[:, cols], wa_ref[h]) + _nt(diab[:, cols], wx_ref[h]))
                dwa_ref[h] += _tn(xb[:, cols], drab[:, cols])
                dwx_ref[h] += _tn(xb[:, cols], diab[:, cols])
            dxc = dxc + jnp.concatenate(dxg, axis=1)
            dvec_ref[0:1, :] += jnp.sum(dra, axis=0, keepdims=True)
            dvec_ref[1:2, :] += jnp.sum(dia, axis=0, keepdims=True)
            dvec_ref[2:3, :] += jnp.sum(dlog_a * r, axis=0, keepdims=True) * (RG_LRU_C * _sigmoid(-lam))
            dvec_ref[3:4, :] += jnp.sum(dxc, axis=0, keepdims=True)
            dxc_ref[rows, :] = dxc
            views = _shift_views(xp_ref[pl.ds(r0, LRU_RC + LRU_PAD), :], LRU_RC, LRU_PAD)
            _conv_wgrad(views, dxc, dwacc_ref, lcw, LRU_PAD, LRU_RC)
            return carry

        lax.fori_loop(0, s // LRU_RC, chunk3, 0, unroll=2)
        for k in range(lcw):
            dcw_ref[k:k + 1, :] += jnp.sum(dwacc_ref[k], axis=0, keepdims=True)

        def chunk4(i, carry):
            r0 = pl.multiple_of(i * LRU_RC, LRU_RC)
            views = _shift_views(dxc_ref[pl.ds(r0, LRU_RC + LRU_PAD), :], LRU_RC, LRU_PAD)
            dur_ref[pl.ds(r0, LRU_RC), :] = _conv_rows_t(views, cw_ref, lcw, LRU_RC).astype(BF16)
            return carry

        lax.fori_loop(0, s // LRU_RC, chunk4, 0, unroll=2)

    blk = pl.BlockSpec((s, ct), lambda c, b: (b, c))
    acc8 = pl.BlockSpec((8, ct), lambda c, b: (0, c))
    gate_w = pl.BlockSpec((nh, 128, 128), lambda c, b: (c, 0, 0))
    return pl.pallas_call(
        body, name=name, grid=(nct, t // s),
        in_specs=act + [blk] * 6 + params,
        out_specs=[blk, blk, gate_w, gate_w, acc8, acc8],
        out_shape=[jax.ShapeDtypeStruct((t, lw), BF16), jax.ShapeDtypeStruct((t, lw), BF16),
                   jax.ShapeDtypeStruct((cfg.lh, 128, 128), F32), jax.ShapeDtypeStruct((cfg.lh, 128, 128), F32),
                   jax.ShapeDtypeStruct((8, lw), F32), jax.ShapeDtypeStruct((8, lw), F32)],
        scratch_shapes=[pltpu.VMEM((LRU_PAD + s, ct), F32), pltpu.VMEM((LRU_PAD + s, ct), F32),
                        pltpu.VMEM((s, ct), F32), pltpu.VMEM((s, ct), F32), pltpu.VMEM((s + LRU_PAD, ct), F32),
                        pltpu.VMEM((lcw, 8, ct), F32)],
        compiler_params=_cparams("arbitrary", "arbitrary"),
    )(u, u, *saved, dy, conv_w, conv_b, wa, ba, wx, bx, lam)


def local_step(cfg, x, tgt, p, shards=None):
    buckets = bucket_table(cfg)
    bias = bias_build(cfg, p["rel_bias"], buckets, "bias_build")
    ga_w, gx_w = p["gate_a_w"].astype(BF16), p["gate_x_w"].astype(BF16)
    wf = dict(p["wf"])
    dist = shards is not None

    def ffn_forward(l, k, h, nw):
        riders = gather_riders([shards[(l + 1, k)]]) if dist and l + 1 < cfg.depth else None
        outs = ffn_fwd(cfg, h, nw, wf[(l, k)], 0, 2, f"ffn{k + 1}_fwd_{l}", riders=riders)
        if riders is not None:
            wf[(l + 1, k)] = outs[4].reshape(3, -1, cfg.d)
        return outs[:4]

    def blocks(g):
        g = g if g.ndim == 3 else g[None]
        return g.reshape(g.shape[0], N_DEV, g.shape[1] // N_DEV, g.shape[2])

    h = x
    saved = []
    for l in range(cfg.depth):
        i = l // 2
        s = {"h0": h}
        s["h1"], s["xn1"], s["g1"], s["u1"] = ffn_forward(l, 0, h, p["norm_ffn1"][l][None])
        if l % 2 == 0:
            s["um"], s["xnm"] = norm_proj(cfg, s["h1"], p["norm_mix"][l][None], p["even_in"], f"mix_in_{l}", wi=i)
            attn = attn_fwd(cfg, s["um"], bias, p["attn_sinks"][i], f"attn_fwd_{l}")
            c, s["cv"] = conv_fwd(cfg, s["um"], p["conv_b_w"][i], p["conv_b_b"][i][None], p["conv_ln_g"][i][None],
                                  p["conv_ln_b"][i][None], f"conv_fwd_{l}")
            s["parts"] = [attn, c]
            s["h2"] = proj_residual(cfg, s["h1"], s["parts"], p["even_out"], f"mix_out_{l}", wi=i)
        else:
            s["um"], s["xnm"] = norm_proj(cfg, s["h1"], p["norm_mix"][l][None], p["odd_in"], f"mix_in_{l}", wi=i)
            y, *s["saved"] = lru_fwd(cfg, s["um"], p["lru_conv_w"][i], p["lru_conv_b"][i][None], ga_w[i], p["gate_a_b"][i][None],
                                 gx_w[i], p["gate_x_b"][i][None], p["lru_lambda"][i][None], f"lru_fwd_{l}")
            s["parts"] = [y]
            s["h2"] = proj_residual(cfg, s["h1"], s["parts"], p["odd_out"], f"mix_out_{l}", wi=i)
        h, s["xn2"], s["g2"], s["u2"] = ffn_forward(l, 1, s["h2"], p["norm_ffn2"][l][None])
        saved.append(s)

    loss, dh, dnf = loss_head(cfg, h, p["norm_final"][None], tgt, "loss_head")
    big = [None] * cfg.depth
    sm = {k: [None] * cfg.depth for k in ("norm_ffn1", "norm_mix", "norm_ffn2")}
    ne, no = (cfg.depth + 1) // 2, cfg.depth // 2
    for k in ("attn_sinks", "conv_b_w", "conv_b_b", "conv_ln_g", "conv_ln_b", "dbias"):
        sm[k] = [None] * ne
    for k in ("lru_conv_w", "lru_conv_b", "gate_a_w", "gate_a_b", "gate_x_w", "gate_x_b", "lru_lambda"):
        sm[k] = [None] * no
    pending = None
    for l in reversed(range(cfg.depth)):
        i = l // 2
        s = saved[l]
        riders = scatter_riders([pending]) if pending is not None else None
        outs = ffn_bwd_x(cfg, dh, s["h2"], p["norm_ffn2"][l][None], s["g2"], s["u2"], wf[(l, 1)], 0, 2,
                         f"ffn2_bwd_x_{l}", riders=riders)
        dh, dout, dg, du, dn = outs[:5]
        if riders is not None:
            big[l + 1]["f1"] = (pending, outs[5])
        sm["norm_ffn2"][l] = dn[0]
        gf2 = blocks(ffn_bwd_w(cfg, s["xn2"], dout, s["g2"], s["u2"], dg, du, f"ffn2_bwd_w_{l}")[0])
        w_out = p["even_out"] if l % 2 == 0 else p["odd_out"]
        w_in = p["even_in"] if l % 2 == 0 else p["odd_in"]
        dcat = proj_bwd_act(cfg, dh, w_out, f"mix_out_bwd_{l}", wi=i)
        g_out = blocks(grad_weight(cfg, s["parts"], dh, f"mix_out_gw_{l}"))
        if l % 2 == 0:
            du_a, sm["dbias"][i], dsink = attn_bwd(cfg, s["um"], dcat, bias, p["attn_sinks"][i], f"attn_bwd_{l}")
            du_c, dcw, dvec = conv_bwd(cfg, s["um"], s["cv"], dcat, p["conv_b_w"][i], p["conv_b_b"][i][None],
                                       p["conv_ln_g"][i][None], p["conv_ln_b"][i][None], f"conv_bwd_{l}")
            sm["attn_sinks"][i] = dsink[:cfg.hq, 0]
            sm["conv_b_w"][i] = dcw[:cfg.cw]
            sm["conv_b_b"][i], sm["conv_ln_g"][i], sm["conv_ln_b"][i] = dvec[0], dvec[1], dvec[2]
            dparts = [du_a, du_c]
        else:
            dug, dur, dwa, dwx, dvec, dcw = lru_bwd(
                cfg, s["um"], s["saved"], dcat, p["lru_conv_w"][i], p["lru_conv_b"][i][None], ga_w[i], p["gate_a_b"][i][None],
                gx_w[i], p["gate_x_b"][i][None], p["lru_lambda"][i][None], f"lru_bwd_{l}")
            sm["gate_a_w"][i], sm["gate_x_w"][i] = dwa, dwx
            sm["gate_a_b"][i], sm["gate_x_b"][i], sm["lru_lambda"][i], sm["lru_conv_b"][i] = dvec[0], dvec[1], dvec[2], dvec[3]
            sm["lru_conv_w"][i] = dcw[:cfg.lcw]
            dparts = [dug, dur]
        g_in = blocks(grad_weight(cfg, dparts, s["xnm"], f"mix_in_gw_{l}"))
        dh, dn = norm_proj_bwd(cfg, dh, s["h1"], p["norm_mix"][l][None], dparts, w_in, f"mix_in_bwd_{l}", wi=i)
        sm["norm_mix"][l] = dn[0]
        riders = scatter_riders([gf2]) if dist else None
        outs = ffn_bwd_x(cfg, dh, s["h0"], p["norm_ffn1"][l][None], s["g1"], s["u1"], wf[(l, 0)], 0, 2,
                         f"ffn1_bwd_x_{l}", riders=riders)
        dh, dout, dg, du, dn = outs[:5]
        sm["norm_ffn1"][l] = dn[0]
        riders = scatter_riders([g_in, g_out]) if dist else None
        gouts = ffn_bwd_w(cfg, s["xn1"], dout, s["g1"], s["u1"], dg, du, f"ffn1_bwd_w_{l}", riders=riders)
        gf1 = blocks(gouts[0])
        big[l] = {"f1": (gf1, None), "f2": (gf2, outs[5] if dist else None),
                  "in": (g_in, gouts[1] if dist else None), "out": (g_out, gouts[2] if dist else None)}
        pending = gf1 if dist and l > 0 else None
    drb = bias_grad(cfg, jnp.stack(sm.pop("dbias")), buckets, "bias_grad")
    small = {k: jnp.stack(v) for k, v in sm.items()}
    small["rel_bias"] = drb[:, :, 0].T
    small["norm_final"] = dnf[0]
    return loss, dh, big, small


MESH = pl.DeviceIdType.MESH
ANY = pl.BlockSpec(memory_space=pl.ANY)


def _place():
    return lax.axis_index("x"), lax.axis_index("y"), lax.axis_index("c")


FLIPS = ((0, 0, 1), (0, 1, 0), (0, 1, 1), (1, 0, 0), (1, 0, 1), (1, 1, 0), (1, 1, 1))


def _peer(place, flip):
    return tuple(1 - v if f else v for v, f in zip(place, flip))


def _dev_index(place):
    return 4 * place[0] + 2 * place[1] + place[2]


def _remote(src, dst, send_sems, recv_sems, g, k, peer):
    return pltpu.make_async_remote_copy(src_ref=src, dst_ref=dst, send_sem=send_sems.at[g, k], recv_sem=recv_sems.at[g, k],
                                        device_id=peer, device_id_type=MESH)


def gather_riders(srcs):
    ng = len(srcs)

    def copies(in_refs, out_refs, sems):
        send_sems, recv_sems, local_sems = sems
        me = _place()
        local, sends, recvs = [], [], []
        for g in range(ng):
            mine = out_refs[g].at[:, _dev_index(me)]
            local.append(pltpu.make_async_copy(in_refs[g], mine, local_sems.at[g]))
            for k, flip in enumerate(FLIPS):
                peer = _peer(me, flip)
                sends.append(_remote(in_refs[g], mine, send_sems, recv_sems, g, k, peer))
                recvs.append(_remote(in_refs[g], out_refs[g].at[:, _dev_index(peer)], send_sems, recv_sems, g, k, peer))
        return local, sends, recvs

    def start(in_refs, out_refs, sems):
        local, sends, _ = copies(in_refs, out_refs, sems)
        for cp in local + sends:
            cp.start()

    def wait(in_refs, out_refs, sems):
        local, sends, recvs = copies(in_refs, out_refs, sems)
        for cp in sends:
            cp.wait_send()
        for cp in recvs:
            cp.wait_recv()
        for cp in local:
            cp.wait()

    return Riders(tuple(srcs), tuple(jax.ShapeDtypeStruct((s.shape[0], N_DEV) + s.shape[1:], s.dtype) for s in srcs),
                  (pltpu.SemaphoreType.DMA((ng, 7)), pltpu.SemaphoreType.DMA((ng, 7)), pltpu.SemaphoreType.DMA((ng,))),
                  start, wait)


def scatter_riders(bufs):
    ng = len(bufs)

    def copies(in_refs, out_refs, sems):
        send_sems, recv_sems = sems
        me = _place()
        return [_remote(in_refs[g].at[:, _dev_index(_peer(me, flip))], out_refs[g].at[:, k], send_sems, recv_sems, g, k,
                        _peer(me, flip)) for g in range(ng) for k, flip in enumerate(FLIPS)]

    def start(in_refs, out_refs, sems):
        for cp in copies(in_refs, out_refs, sems):
            cp.start()

    def wait(in_refs, out_refs, sems):
        for cp in copies(in_refs, out_refs, sems):
            cp.wait()

    return Riders(tuple(bufs), tuple(jax.ShapeDtypeStruct((b.shape[0], 7) + b.shape[2:], b.dtype) for b in bufs),
                  (pltpu.SemaphoreType.DMA((ng, 7)), pltpu.SemaphoreType.DMA((ng, 7))), start, wait)


def shard_sum(buf, recv, dev, name):
    n, _, r, cdim = buf.shape

    def body(dev_ref, a_ref, b_ref, o_ref):
        acc = a_ref[...].astype(F32)
        for k in range(7):
            acc = acc + b_ref[k].astype(F32)
        o_ref[...] = acc

    return pl.pallas_call(
        body, name=name,
        grid_spec=pltpu.PrefetchScalarGridSpec(
            num_scalar_prefetch=1, grid=(n,),
            in_specs=[pl.BlockSpec((None, None, r, cdim), lambda i, dev_ref: (i, dev_ref[0], 0, 0)),
                      pl.BlockSpec((None, 7, r, cdim), lambda i, dev_ref: (i, 0, 0, 0))],
            out_specs=pl.BlockSpec((None, r, cdim), lambda i, dev_ref: (i, 0, 0))),
        out_shape=jax.ShapeDtypeStruct((n, r, cdim), F32),
    )(dev, buf, recv)


def all_gather(srcs, name):
    ng = len(srcs)

    def body(*refs):
        x_refs, o_refs = refs[:ng], refs[ng:2 * ng]
        send_sems, recv_sems, local_sems = refs[2 * ng:]
        x, y, c = _place()
        me, sibling = (x, y, c), (x, y, 1 - c)
        chips = [(1 - x, y), (x, 1 - y), (1 - x, 1 - y)]

        def copy(gi, k, block, to, src=None):
            dst = o_refs[gi].at[:, 4 * block[0] + 2 * block[1] + block[2]]
            return pltpu.make_async_remote_copy(
                src_ref=dst if src is None else src, dst_ref=dst, send_sem=send_sems.at[gi, k],
                recv_sem=recv_sems.at[gi, k], device_id=to, device_id_type=MESH)

        mine = [pltpu.make_async_copy(x_refs[gi], o_refs[gi].at[:, 4 * x + 2 * y + c], local_sems.at[gi])
                for gi in range(ng)]
        for cp in mine:
            cp.start()
        first = []
        for gi in range(ng):
            first.append(copy(gi, 0, me, sibling, src=x_refs[gi]))
            first += [copy(gi, 1 + j, me, (*chip, c), src=x_refs[gi]) for j, chip in enumerate(chips)]
        for cp in first:
            cp.start()
        passed = []
        for j, chip in enumerate(chips):
            for gi in range(ng):
                copy(gi, 1 + j, (*chip, c), me).wait_recv()
                cp = copy(gi, 4 + j, (*chip, c), sibling)
                cp.start()
                passed.append(cp)
        for gi in range(ng):
            copy(gi, 0, sibling, me).wait_recv()
            for j, chip in enumerate(chips):
                copy(gi, 4 + j, (*chip, 1 - c), me).wait_recv()
        for cp in first + passed:
            cp.wait_send()
        for cp in mine:
            cp.wait()

    return pl.pallas_call(
        body, name=name,
        in_specs=[ANY] * ng, out_specs=[ANY] * ng,
        out_shape=[jax.ShapeDtypeStruct((s.shape[0], N_DEV) + s.shape[1:], s.dtype) for s in srcs],
        scratch_shapes=[pltpu.SemaphoreType.DMA((ng, 7)), pltpu.SemaphoreType.DMA((ng, 7)),
                        pltpu.SemaphoreType.DMA((ng,))],
    )(*srcs)


def pair_exchange(bufs, name):
    ng = len(bufs)

    def body(*refs):
        b_refs, o_refs = refs[:ng], refs[ng:2 * ng]
        send_sems, recv_sems = refs[2 * ng:]
        x, y, c = _place()
        copies = [pltpu.make_async_remote_copy(
            src_ref=b_refs[gi].at[:, :, 1 - c], dst_ref=o_refs[gi], send_sem=send_sems.at[gi], recv_sem=recv_sems.at[gi],
            device_id=(x, y, 1 - c), device_id_type=MESH) for gi in range(ng)]
        for cp in copies:
            cp.start()
        for cp in copies:
            cp.wait()

    return pl.pallas_call(
        body, name=name,
        in_specs=[ANY] * ng, out_specs=[ANY] * ng,
        out_shape=[jax.ShapeDtypeStruct(b.shape[:2] + b.shape[3:], b.dtype) for b in bufs],
        scratch_shapes=[pltpu.SemaphoreType.DMA((ng,)), pltpu.SemaphoreType.DMA((ng,))],
    )(*bufs)


def chip_exchange(qs, name):
    ng = len(qs)

    def body(*refs):
        q_refs, o_refs = refs[:ng], refs[ng:2 * ng]
        send_sems, recv_sems = refs[2 * ng:]
        x, y, c = _place()
        chips = [(1 - x, y), (x, 1 - y), (1 - x, 1 - y)]
        copies = [pltpu.make_async_remote_copy(
            src_ref=q_refs[gi].at[:, 2 * chip[0] + chip[1]], dst_ref=o_refs[gi].at[:, j],
            send_sem=send_sems.at[gi, j], recv_sem=recv_sems.at[gi, j],
            device_id=(*chip, c), device_id_type=MESH) for gi in range(ng) for j, chip in enumerate(chips)]
        for cp in copies:
            cp.start()
        for cp in copies:
            cp.wait()

    return pl.pallas_call(
        body, name=name,
        in_specs=[ANY] * ng, out_specs=[ANY] * ng,
        out_shape=[jax.ShapeDtypeStruct((q.shape[0], 3) + q.shape[2:], q.dtype) for q in qs],
        scratch_shapes=[pltpu.SemaphoreType.DMA((ng, 3)), pltpu.SemaphoreType.DMA((ng, 3))],
    )(*qs)


def pair_sum(buf, recv, core, name):
    n, _, _, r, cdim = buf.shape

    def body(core_ref, a_ref, b_ref, o_ref):
        o_ref[...] = (a_ref[...].astype(F32) + b_ref[...].astype(F32)).astype(BF16)

    blk = pl.BlockSpec((None, None, r, cdim), lambda i, k, core_ref: (i, k, 0, 0))
    return pl.pallas_call(
        body, name=name,
        grid_spec=pltpu.PrefetchScalarGridSpec(
            num_scalar_prefetch=1, grid=(n, 4),
            in_specs=[pl.BlockSpec((None, None, None, r, cdim), lambda i, k, core_ref: (i, k, core_ref[0], 0, 0)), blk],
            out_specs=blk),
        out_shape=jax.ShapeDtypeStruct((n, 4, r, cdim), BF16),
    )(core, buf, recv)


def chip_sum(q, recv, chip, name):
    n, _, r, cdim = q.shape

    def body(chip_ref, a_ref, b_ref, o_ref):
        acc = a_ref[...].astype(F32)
        for j in range(3):
            acc = acc + b_ref[j].astype(F32)
        o_ref[...] = acc

    return pl.pallas_call(
        body, name=name,
        grid_spec=pltpu.PrefetchScalarGridSpec(
            num_scalar_prefetch=1, grid=(n,),
            in_specs=[pl.BlockSpec((None, None, r, cdim), lambda i, chip_ref: (i, chip_ref[0], 0, 0)),
                      pl.BlockSpec((None, 3, r, cdim), lambda i, chip_ref: (i, 0, 0, 0))],
            out_specs=pl.BlockSpec((None, r, cdim), lambda i, chip_ref: (i, 0, 0))),
        out_shape=jax.ShapeDtypeStruct((n, r, cdim), F32),
    )(chip, q, recv)


def sum_blocks(a, name):
    def body(a_ref, o_ref):
        acc = a_ref[0]
        for d in range(1, a.shape[0]):
            acc = acc + a_ref[d]
        o_ref[...] = acc

    return pl.pallas_call(body, name=name, out_shape=jax.ShapeDtypeStruct(a.shape[1:], F32),
                          compiler_params=pltpu.CompilerParams(vmem_limit_bytes=VMEM_LIMIT))(a)


def adamw(w, g, m, v, name):
    c1 = 1.0 / (1.0 - ADAM_B1 ** ADAM_STEP)
    c2 = 1.0 / (1.0 - ADAM_B2 ** ADAM_STEP)

    def body(w_ref, g_ref, m_ref, v_ref, d_ref, mo_ref, vo_ref):
        gg = g_ref[...]
        m2 = ADAM_B1 * m_ref[...] + (1.0 - ADAM_B1) * gg
        v2 = ADAM_B2 * v_ref[...] + (1.0 - ADAM_B2) * (gg * gg)
        mo_ref[...] = m2
        vo_ref[...] = v2
        d_ref[...] = -ADAM_LR * ((m2 * c1) / (jnp.sqrt(v2 * c2) + ADAM_EPS) + ADAM_WD * w_ref[...])

    out_shape = [jax.ShapeDtypeStruct(w.shape, F32)] * 3
    if w.ndim == 2:
        return pl.pallas_call(body, name=name, out_shape=out_shape,
                              compiler_params=pltpu.CompilerParams(vmem_limit_bytes=VMEM_LIMIT))(w, g, m, v)
    blk = pl.BlockSpec((None,) + w.shape[1:], lambda i: (i, 0, 0))
    return pl.pallas_call(body, name=name, grid=(w.shape[0],), in_specs=[blk] * 4, out_specs=[blk] * 3,
                          out_shape=out_shape, compiler_params=_cparams("arbitrary"))(w, g, m, v)


WEIGHTS = ("norm_ffn1", "ffn1_wg", "ffn1_wu", "ffn1_wd", "norm_mix", "norm_ffn2", "ffn2_wg", "ffn2_wu", "ffn2_wd",
           "rel_bias", "even_w_in", "attn_sinks", "conv_b_w", "conv_b_b", "conv_ln_g", "conv_ln_b", "even_w_out",
           "odd_w_in", "lru_conv_w", "lru_conv_b", "gate_a_w", "gate_a_b", "gate_x_w", "gate_x_b", "lru_lambda",
           "odd_w_out", "norm_final")
BIG = ("ffn1_wg", "ffn1_wu", "ffn1_wd", "ffn2_wg", "ffn2_wu", "ffn2_wd", "even_w_in", "even_w_out", "odd_w_in", "odd_w_out")
SMALL = tuple(n for n in WEIGHTS if n not in BIG)
SMALL_SHARDED = ("conv_b_w", "lru_conv_w", "lru_conv_b", "gate_a_b", "gate_x_b", "lru_lambda")
PACK_ALIGN = 1024


def _pack(arrays):
    parts = []
    for a in arrays:
        flat = a.reshape(-1)
        parts.append(jnp.pad(flat, (0, -flat.shape[0] % PACK_ALIGN)))
    return jnp.concatenate(parts).reshape(-1, 128)


def _unpack(packed, shapes, lead=()):
    out, row = [], 0
    for shp in shapes:
        size = math.prod(shp)
        nrows = (size + (-size % PACK_ALIGN)) // 128
        part = packed[..., row:row + nrows, :].reshape(lead + (nrows * 128,))
        out.append(part[..., :size].reshape(lead + tuple(shp)))
        row += nrows
    return out


def _unshard_last(blocks):
    nd = blocks.ndim
    moved = jnp.moveaxis(blocks, 0, nd - 2)
    return moved.reshape(moved.shape[:-2] + (-1,))


def _step(cfg, x, weights, loss_target, ms, vs):
    w = dict(zip(WEIGHTS, weights))
    m = dict(zip(WEIGHTS, ms))
    v = dict(zip(WEIGHTS, vs))
    px, py, pc = _place()
    dev = 4 * px + 2 * py + pc
    core = jnp.reshape(pc, (1,)).astype(jnp.int32)
    chip = jnp.reshape(2 * px + py, (1,)).astype(jnp.int32)
    d = cfg.d
    t = cfg.bl * cfg.s

    def rows(name):
        a = w[name]
        return (a if name.endswith(("wd", "w_out")) else a.transpose(0, 2, 1)).astype(BF16)

    r3 = {n: rows(n) for n in BIG[:6]}
    shards = {(l, k): jnp.stack([r3[f"ffn{k + 1}_{mat}"][l] for mat in ("wg", "wu", "wd")])
              for l in range(cfg.depth) for k in range(2)}
    small_src = _pack([w[n] for n in SMALL_SHARDED])[None]
    gathered = all_gather([shards.pop((0, 0)), shards.pop((0, 1)), rows("even_w_in"), rows("even_w_out"), rows("odd_w_in"),
                           rows("odd_w_out"), small_src], "all_gather_weights")
    full = [g.reshape(g.shape[0], -1, g.shape[-1]) for g in gathered[:6]]
    p = {n: w[n] for n in SMALL if n not in SMALL_SHARDED}
    p.update(wf={(0, 0): full[0], (0, 1): full[1]}, even_in=full[2], even_out=full[3], odd_in=full[4], odd_out=full[5])
    for n, blocks in zip(SMALL_SHARDED, _unpack(gathered[6][0], [w[n].shape for n in SMALL_SHARDED], lead=(N_DEV,))):
        p[n] = _unshard_last(blocks)

    lossp, gx, big, small = local_step(cfg, x.reshape(t, d), loss_target.reshape(t, d), p, shards)
    loss = lax.psum(lossp[0, 0], ("x", "y", "c"))

    dev1 = jnp.reshape(dev, (1,)).astype(jnp.int32)
    shard_rows = [{} for _ in range(cfg.depth)]
    for l in range(cfg.depth):
        for key, (buf, recv) in big[l].items():
            if recv is not None:
                shard_rows[l][key] = shard_sum(buf, recv, dev1, f"rs_sum_{key}_{l}")
    left = [(l, key, buf) for l in range(cfg.depth) for key, (buf, recv) in big[l].items() if recv is None]
    bufs = [buf.reshape(buf.shape[0], 4, 2, buf.shape[2], d) for _, _, buf in left]
    recv = pair_exchange(bufs, "rs_pair")
    qs = [pair_sum(b, r, core, f"rs_pair_sum{j}") for j, (b, r) in enumerate(zip(bufs, recv))]
    recv = chip_exchange(qs, "rs_chip")
    for j, ((l, key, _), q, r) in enumerate(zip(left, qs, recv)):
        shard_rows[l][key] = chip_sum(q, r, chip, f"rs_chip_sum{j}")

    g_rows = {}
    for k in range(2):
        ffn_g = jnp.stack([shard_rows[l][f"f{k + 1}"] for l in range(cfg.depth)])
        for j, mat in enumerate(("wg", "wu", "wd")):
            g_rows[f"ffn{k + 1}_{mat}"] = ffn_g[:, j]
    g_rows["even_w_in"] = jnp.stack([shard_rows[l]["in"][0] for l in range(0, cfg.depth, 2)])
    g_rows["even_w_out"] = jnp.stack([shard_rows[l]["out"][0] for l in range(0, cfg.depth, 2)])
    g_rows["odd_w_in"] = jnp.stack([shard_rows[l]["in"][0] for l in range(1, cfg.depth, 2)])
    g_rows["odd_w_out"] = jnp.stack([shard_rows[l]["out"][0] for l in range(1, cfg.depth, 2)])
    grads = {}

    full_shapes = [small[n].shape for n in SMALL]
    parts = all_gather([_pack([small[n] for n in SMALL])[None]], "all_gather_small_grads")[0][0]
    for n, g in zip(SMALL, _unpack(sum_blocks(parts, "sum_small_grads"), full_shapes)):
        if n in SMALL_SHARDED:
            width = w[n].shape[-1]
            g = lax.dynamic_slice_in_dim(g, dev * width, width, axis=g.ndim - 1)
        grads[n] = g

    delta, new_m, new_v = {}, {}, {}
    for n in BIG:
        if n.endswith(("wd", "w_out")):
            grads[n] = g_rows[n]
            delta[n], new_m[n], new_v[n] = adamw(w[n], grads[n], m[n], v[n], f"adamw_{n}")
        elif w[n].shape[-1] % 128 == 0:
            grads[n] = g_rows[n].transpose(0, 2, 1)
            delta[n], new_m[n], new_v[n] = adamw(w[n], grads[n], m[n], v[n], f"adamw_{n}")
        else:
            outs = adamw(w[n].transpose(0, 2, 1), g_rows[n], m[n].transpose(0, 2, 1), v[n].transpose(0, 2, 1), f"adamw_{n}")
            delta[n], new_m[n], new_v[n] = [o.transpose(0, 2, 1) for o in outs]
            grads[n] = g_rows[n].transpose(0, 2, 1)
    shapes = [w[n].shape for n in SMALL]
    packed = adamw(*[_pack([src[n] for n in SMALL]) for src in (w, grads, m, v)], "adamw_small")
    for out, pk in zip((delta, new_m, new_v), packed):
        out.update(zip(SMALL, _unpack(pk, shapes)))

    return (loss, gx.reshape(x.shape), *[grads[n] for n in WEIGHTS], *[delta[n] for n in WEIGHTS],
            *[new_m[n] for n in WEIGHTS], *[new_v[n] for n in WEIGHTS])


def kernel(x, norm_ffn1, ffn1_wg, ffn1_wu, ffn1_wd, norm_mix, norm_ffn2, ffn2_wg, ffn2_wu, ffn2_wd, rel_bias, even_w_in, attn_sinks, conv_b_w, conv_b_b, conv_ln_g, conv_ln_b, even_w_out, odd_w_in, lru_conv_w, lru_conv_b, gate_a_w, gate_a_b, gate_x_w, gate_x_b, lru_lambda, odd_w_out, norm_final, loss_target, m_norm_ffn1, m_ffn1_wg, m_ffn1_wu, m_ffn1_wd, m_norm_mix, m_norm_ffn2, m_ffn2_wg, m_ffn2_wu, m_ffn2_wd, m_rel_bias, m_even_w_in, m_attn_sinks, m_conv_b_w, m_conv_b_b, m_conv_ln_g, m_conv_ln_b, m_even_w_out, m_odd_w_in, m_lru_conv_w, m_lru_conv_b, m_gate_a_w, m_gate_a_b, m_gate_x_w, m_gate_x_b, m_lru_lambda, m_odd_w_out, m_norm_final, v_norm_ffn1, v_ffn1_wg, v_ffn1_wu, v_ffn1_wd, v_norm_mix, v_norm_ffn2, v_ffn2_wg, v_ffn2_wu, v_ffn2_wd, v_rel_bias, v_even_w_in, v_attn_sinks, v_conv_b_w, v_conv_b_b, v_conv_ln_g, v_conv_ln_b, v_even_w_out, v_odd_w_in, v_lru_conv_w, v_lru_conv_b, v_gate_a_w, v_gate_a_b, v_gate_x_w, v_gate_x_b, v_lru_lambda, v_odd_w_out, v_norm_final):
    weights = (norm_ffn1, ffn1_wg, ffn1_wu, ffn1_wd, norm_mix, norm_ffn2, ffn2_wg, ffn2_wu, ffn2_wd, rel_bias, even_w_in, attn_sinks, conv_b_w, conv_b_b, conv_ln_g, conv_ln_b, even_w_out, odd_w_in, lru_conv_w, lru_conv_b, gate_a_w, gate_a_b, gate_x_w, gate_x_b, lru_lambda, odd_w_out, norm_final)
    ms = (m_norm_ffn1, m_ffn1_wg, m_ffn1_wu, m_ffn1_wd, m_norm_mix, m_norm_ffn2, m_ffn2_wg, m_ffn2_wu, m_ffn2_wd, m_rel_bias, m_even_w_in, m_attn_sinks, m_conv_b_w, m_conv_b_b, m_conv_ln_g, m_conv_ln_b, m_even_w_out, m_odd_w_in, m_lru_conv_w, m_lru_conv_b, m_gate_a_w, m_gate_a_b, m_gate_x_w, m_gate_x_b, m_lru_lambda, m_odd_w_out, m_norm_final)
    vs = (v_norm_ffn1, v_ffn1_wg, v_ffn1_wu, v_ffn1_wd, v_norm_mix, v_norm_ffn2, v_ffn2_wg, v_ffn2_wu, v_ffn2_wd, v_rel_bias, v_even_w_in, v_attn_sinks, v_conv_b_w, v_conv_b_b, v_conv_ln_g, v_conv_ln_b, v_even_w_out, v_odd_w_in, v_lru_conv_w, v_lru_conv_b, v_gate_a_w, v_gate_a_b, v_gate_x_w, v_gate_x_b, v_lru_lambda, v_odd_w_out, v_norm_final)
    return _step(Cfg(), x, weights, loss_target, ms, vs)
```

```python
import math
from typing import NamedTuple

import jax
import jax.numpy as jnp
from jax import lax
from jax.experimental import pallas as pl
from jax.experimental.pallas import tpu as pltpu

F32 = jnp.float32
BF16 = jnp.bfloat16
RMS_EPS = 1e-6
LN_EPS = 1e-5
NEG_INF = -1e30
RG_LRU_C = 8.0
ADAM_LR = 0.001
ADAM_B1 = 0.9
ADAM_B2 = 0.999
ADAM_EPS = 1e-08
ADAM_WD = 0.01
ADAM_STEP = 10
N_DEV = 8
VMEM_LIMIT = 56 * 1024 * 1024


class Cfg(NamedTuple):
    d: int = 1024
    f: int = 2816
    s: int = 2048
    bl: int = 4
    hq: int = 8
    hkv: int = 2
    hd: int = 64
    win: int = 128
    cc: int = 512
    cw: int = 31
    lh: int = 8
    lb: int = 128
    lcw: int = 4
    nbuckets: int = 32
    max_dist: int = 128
    depth: int = 4
    tm: int = 1024
    tm_ffn: int = 1024
    tf: int = 256
    tk_ffn: int = 512
    tf_w: int = 1408
    tk: int = 1024
    ct_f: int = 256
    ct_b: int = 256

    @property
    def qw(self):
        return self.hq * self.hd

    @property
    def kvw(self):
        return self.hkv * self.hd

    @property
    def even_in(self):
        return self.qw + 2 * self.kvw + 2 * self.cc

    @property
    def even_cat(self):
        return self.qw + self.cc

    @property
    def lw(self):
        return self.lh * self.lb


def _cparams(*sem):
    return pltpu.CompilerParams(dimension_semantics=sem, vmem_limit_bytes=VMEM_LIMIT)


def _nt(a, b):
    return lax.dot_general(a, b, (((1,), (1,)), ((), ())), preferred_element_type=F32)


def _nn(a, b):
    return lax.dot_general(a, b, (((1,), (0,)), ((), ())), preferred_element_type=F32)


def _tn(a, b):
    return lax.dot_general(a, b, (((0,), (0,)), ((), ())), preferred_element_type=F32)


def _rstd(h):
    return lax.rsqrt(jnp.mean(h * h, axis=-1, keepdims=True) + RMS_EPS)


def _rms_bwd(h, nw, dxn):
    rstd = _rstd(h)
    dyg = dxn * nw
    dnw = jnp.sum(dxn * h * rstd, axis=0, keepdims=True)
    dx = rstd * (dyg - h * (rstd * rstd) * jnp.mean(dyg * h, axis=-1, keepdims=True))
    return dx, dnw


def _sigmoid(x):
    return 0.5 * jnp.tanh(0.5 * x) + 0.5


FFN_SLABS = 4


def _ffn_wspecs(tf, d, gu, md):
    return [pl.BlockSpec((2, tf, d), lambda i, j: (gu, j, 0)), pl.BlockSpec((None, tf, d), lambda i, j: (md, j, 0))]


class Riders(NamedTuple):
    inputs: tuple
    out_shape: tuple
    scratch: tuple
    start: object
    wait: object


def _ride(riders, grid, n_in, n_out, body):
    if riders is None:
        return body, [], [], [], [], []
    ni, no, ns = len(riders.inputs), len(riders.out_shape), len(riders.scratch)

    def full(*refs):
        ins, rin = refs[:n_in], refs[n_in:n_in + ni]
        outs = refs[n_in + ni:n_in + ni + n_out]
        rout = refs[n_in + ni + n_out:n_in + ni + n_out + no]
        rest = refs[n_in + ni + n_out + no:]
        scratch, sems = rest[:len(rest) - ns], rest[len(rest) - ns:]
        first = last = None
        for axis, size in enumerate(grid):
            pid = pl.program_id(axis)
            first = (pid == 0) if first is None else first & (pid == 0)
            last = (pid == size - 1) if last is None else last & (pid == size - 1)

        @pl.when(first)
        def _():
            riders.start(rin, rout, sems)

        body(*ins, *outs, *scratch)

        @pl.when(last)
        def _():
            riders.wait(rin, rout, sems)

    any_spec = pl.BlockSpec(memory_space=pl.ANY)
    return full, list(riders.inputs), [any_spec] * ni, [any_spec] * no, list(riders.out_shape), list(riders.scratch)


def ffn_fwd(cfg, h, nw, wts, gu, md, name, riders=None):
    t, d = h.shape
    f = wts.shape[1]
    tm, tf = cfg.tm_ffn, cfg.tf
    nj = f // tf

    def body(h_ref, nw_ref, wgu_ref, wd_ref, ho_ref, xn_ref, g_ref, u_ref, acc_ref):
        j = pl.program_id(1)

        @pl.when(j == 0)
        def _():
            hh = h_ref[...]
            xn_ref[...] = (hh * _rstd(hh) * nw_ref[...]).astype(BF16)
            acc_ref[...] = jnp.zeros_like(acc_ref)

        gu = _nt(xn_ref[...], wgu_ref[...].reshape(2 * tf, d))
        g, u = gu[:, :tf], gu[:, tf:]
        g_ref[...] = g.astype(BF16)
        u_ref[...] = u.astype(BF16)
        acc_ref[...] += _nn((g * _sigmoid(g) * u).astype(BF16), wd_ref[...])

        @pl.when(j == nj - 1)
        def _():
            ho_ref[...] = h_ref[...] + 0.5 * acc_ref[...]

    row = pl.BlockSpec((tm, d), lambda i, j: (i, 0))
    hid = pl.BlockSpec((tm, tf), lambda i, j: (i, j))
    grid = (t // tm, nj)
    full, r_args, r_in, r_out, r_shape, r_scratch = _ride(riders, grid, 4, 4, body)
    return pl.pallas_call(
        full, name=name, grid=grid,
        in_specs=[row, pl.BlockSpec((1, d), lambda i, j: (0, 0))] + _ffn_wspecs(tf, d, gu, md) + r_in,
        out_specs=[row, row, hid, hid] + r_out,
        out_shape=[jax.ShapeDtypeStruct((t, d), F32), jax.ShapeDtypeStruct((t, d), BF16),
                   jax.ShapeDtypeStruct((t, f), BF16), jax.ShapeDtypeStruct((t, f), BF16)] + r_shape,
        scratch_shapes=[pltpu.VMEM((tm, d), F32)] + r_scratch,
        compiler_params=_cparams("arbitrary", "arbitrary"),
    )(h, nw, wts, wts, *r_args)


def ffn_bwd_x(cfg, dh, h, nw, g, u, wts, gu, md, name, riders=None):
    t, d = h.shape
    f = wts.shape[1]
    tm, tf = cfg.tm_ffn, cfg.tf
    nj = f // tf

    def body(dh_ref, h_ref, nw_ref, g_ref, u_ref, wgu_ref, wd_ref,
             dho_ref, dout_ref, dg_ref, du_ref, dnw_ref, acc_ref, da_ref):
        i, j = pl.program_id(0), pl.program_id(1)

        @pl.when(j == 0)
        def _():
            dout_ref[...] = (0.5 * dh_ref[...]).astype(BF16)
            acc_ref[...] = jnp.zeros_like(acc_ref)
            da_ref[1] = jnp.zeros((tm, tf), F32)

        @pl.when((i == 0) & (j == 0))
        def _():
            dnw_ref[...] = jnp.zeros_like(dnw_ref)

        slot = lax.rem(j, 2)
        da = da_ref[1 - slot]
        da_ref[slot] = _nt(dout_ref[...], wd_ref[...])
        gg = g_ref[...].astype(F32)
        sig = _sigmoid(gg)
        dg = (da * u_ref[...].astype(F32) * (sig * (1.0 + gg * (1.0 - sig)))).astype(BF16)
        du = (da * (gg * sig)).astype(BF16)
        dg_ref[...] = dg
        du_ref[...] = du
        acc_ref[...] += _nn(jnp.concatenate([dg, du], axis=1), wgu_ref[...].reshape(2 * tf, d))

        @pl.when(j == nj)
        def _():
            dx, dnw = _rms_bwd(h_ref[...], nw_ref[...], acc_ref[...])
            dnw_ref[0:1, :] += dnw
            dho_ref[...] = dh_ref[...] + dx

    row = pl.BlockSpec((tm, d), lambda i, j: (i, 0))
    prev = pl.BlockSpec((tm, tf), lambda i, j: (i, jnp.maximum(j - 1, 0)))
    wspecs = [pl.BlockSpec((2, tf, d), lambda i, j: (gu, jnp.maximum(j - 1, 0), 0)),
              pl.BlockSpec((None, tf, d), lambda i, j: (md, jnp.minimum(j, nj - 1), 0))]
    grid = (t // tm, nj + 1)
    full, r_args, r_in, r_out, r_shape, r_scratch = _ride(riders, grid, 7, 5, body)
    return pl.pallas_call(
        full, name=name, grid=grid,
        in_specs=[row, row, pl.BlockSpec((1, d), lambda i, j: (0, 0)), prev, prev] + wspecs + r_in,
        out_specs=[row, row, prev, prev, pl.BlockSpec((8, d), lambda i, j: (0, 0))] + r_out,
        out_shape=[jax.ShapeDtypeStruct((t, d), F32), jax.ShapeDtypeStruct((t, d), BF16),
                   jax.ShapeDtypeStruct((t, f), BF16), jax.ShapeDtypeStruct((t, f), BF16),
                   jax.ShapeDtypeStruct((8, d), F32)] + r_shape,
        scratch_shapes=[pltpu.VMEM((tm, d), F32), pltpu.VMEM((2, tm, tf), F32)] + r_scratch,
        compiler_params=_cparams("arbitrary", "arbitrary"),
    )(dh, h, nw, g, u, wts, wts, *r_args)


def ffn_bwd_w(cfg, xn, dout, g, u, dg, du, name, riders=None):
    t, d = xn.shape
    f = g.shape[1]
    tk, tf = cfg.tk_ffn, cfg.tf_w
    nk = t // tk

    def body(xn_ref, dout_ref, g_ref, u_ref, dg_ref, du_ref, o_ref, acc_ref):
        k = pl.program_id(1)

        @pl.when(k == 0)
        def _():
            acc_ref[...] = jnp.zeros_like(acc_ref)

        gg = g_ref[...].astype(F32)
        a = (gg * _sigmoid(gg) * u_ref[...].astype(F32)).astype(BF16)
        xn_t = xn_ref[...]
        acc_ref[0] += _tn(dg_ref[...], xn_t)
        acc_ref[1] += _tn(du_ref[...], xn_t)
        acc_ref[2] += _tn(a, dout_ref[...])

        @pl.when(k == nk - 1)
        def _():
            o_ref[...] = acc_ref[...].astype(BF16)

    row = pl.BlockSpec((tk, d), lambda j, k: (k, 0))
    hid = pl.BlockSpec((tk, tf), lambda j, k: (k, j))
    grid = (f // tf, nk)
    full, r_args, r_in, r_out, r_shape, r_scratch = _ride(riders, grid, 6, 1, body)
    return pl.pallas_call(
        full, name=name, grid=grid,
        in_specs=[row, row, hid, hid, hid, hid] + r_in,
        out_specs=[pl.BlockSpec((3, tf, d), lambda j, k: (0, j, 0), pipeline_mode=pl.Buffered(1))] + r_out,
        out_shape=[jax.ShapeDtypeStruct((3, f, d), BF16)] + r_shape,
        scratch_shapes=[pltpu.VMEM((3, tf, d), F32)] + r_scratch,
        compiler_params=_cparams("arbitrary", "arbitrary"),
    )(xn, dout, g, u, dg, du, *r_args)


def _wspec(w, wi):
    if w.ndim == 2:
        return pl.BlockSpec(w.shape, lambda i: (0, 0))
    return pl.BlockSpec((None,) + w.shape[1:], lambda i: (wi, 0, 0))


def norm_proj(cfg, h, nw, w, name, wi=0):
    t, d = h.shape
    n = w.shape[-2]
    tm = cfg.tm

    def body(h_ref, nw_ref, w_ref, u_ref, xn_ref):
        hh = h_ref[...]
        xn = (hh * _rstd(hh) * nw_ref[...]).astype(BF16)
        xn_ref[...] = xn
        u_ref[...] = _nt(xn, w_ref[...]).astype(BF16)

    return pl.pallas_call(
        body, name=name, grid=(t // tm,),
        in_specs=[pl.BlockSpec((tm, d), lambda i: (i, 0)), pl.BlockSpec((1, d), lambda i: (0, 0)),
                  _wspec(w, wi)],
        out_specs=[pl.BlockSpec((tm, n), lambda i: (i, 0)), pl.BlockSpec((tm, d), lambda i: (i, 0))],
        out_shape=[jax.ShapeDtypeStruct((t, n), BF16), jax.ShapeDtypeStruct((t, d), BF16)],
        compiler_params=_cparams("arbitrary"),
    )(h, nw, w)


def proj_residual(cfg, h, parts, w, name, wi=0):
    t, d = h.shape
    tm = cfg.tm
    ks = [p.shape[1] for p in parts]
    offs = [sum(ks[:i]) for i in range(len(ks))]
    np_ = len(parts)

    def body(*refs):
        h_ref, w_ref, ho_ref = refs[0], refs[1 + np_], refs[2 + np_]
        acc = h_ref[...]
        for p_ref, off, k in zip(refs[1:1 + np_], offs, ks):
            acc = acc + _nn(p_ref[...], w_ref[off:off + k, :])
        ho_ref[...] = acc

    return pl.pallas_call(
        body, name=name, grid=(t // tm,),
        in_specs=[pl.BlockSpec((tm, d), lambda i: (i, 0))]
        + [pl.BlockSpec((tm, k), lambda i: (i, 0)) for k in ks]
        + [_wspec(w, wi)],
        out_specs=pl.BlockSpec((tm, d), lambda i: (i, 0)),
        out_shape=jax.ShapeDtypeStruct((t, d), F32),
        compiler_params=_cparams("arbitrary"),
    )(h, *parts, w)


def proj_bwd_act(cfg, dh, w, name, wi=0):
    t, d = dh.shape
    k = w.shape[-2]
    tm = cfg.tm

    def body(dh_ref, w_ref, o_ref):
        o_ref[...] = _nt(dh_ref[...].astype(BF16), w_ref[...]).astype(BF16)

    return pl.pallas_call(
        body, name=name, grid=(t // tm,),
        in_specs=[pl.BlockSpec((tm, d), lambda i: (i, 0)), _wspec(w, wi)],
        out_specs=pl.BlockSpec((tm, k), lambda i: (i, 0)),
        out_shape=jax.ShapeDtypeStruct((t, k), BF16),
        compiler_params=_cparams("arbitrary"),
    )(dh, w)


def grad_weight(cfg, parts, b, name):
    t, d = b.shape
    tk = cfg.tk
    nk = t // tk
    ks = [p.shape[1] for p in parts]
    offs = [sum(ks[:i]) for i in range(len(ks))]
    np_ = len(parts)

    def body(*refs):
        b_ref, o_ref, acc_ref = refs[np_:]
        kk = pl.program_id(0)

        @pl.when(kk == 0)
        def _():
            acc_ref[...] = jnp.zeros_like(acc_ref)

        bb = b_ref[...].astype(BF16)
        for a_ref, off, k in zip(refs[:np_], offs, ks):
            acc_ref[off:off + k, :] += _tn(a_ref[...], bb)

        @pl.when(kk == nk - 1)
        def _():
            o_ref[...] = acc_ref[...].astype(BF16)

    return pl.pallas_call(
        body, name=name, grid=(nk,),
        in_specs=[pl.BlockSpec((tk, k), lambda kk: (kk, 0)) for k in ks] + [pl.BlockSpec((tk, d), lambda kk: (kk, 0))],
        out_specs=pl.BlockSpec((sum(ks), d), lambda kk: (0, 0)),
        out_shape=jax.ShapeDtypeStruct((sum(ks), d), BF16),
        scratch_shapes=[pltpu.VMEM((sum(ks), d), F32)],
        compiler_params=_cparams("arbitrary"),
    )(*parts, b)


def norm_proj_bwd(cfg, dh, h, nw, parts, w, name, wi=0):
    t, d = h.shape
    tm = cfg.tm
    ks = [p.shape[1] for p in parts]
    offs = [sum(ks[:i]) for i in range(len(ks))]
    np_ = len(parts)

    def body(*refs):
        dh_ref, h_ref, nw_ref = refs[:3]
        w_ref, dho_ref, dnw_ref = refs[3 + np_:]

        @pl.when(pl.program_id(0) == 0)
        def _():
            dnw_ref[...] = jnp.zeros_like(dnw_ref)

        dxn = None
        for p_ref, off, k in zip(refs[3:3 + np_], offs, ks):
            term = _nn(p_ref[...], w_ref[off:off + k, :])
            dxn = term if dxn is None else dxn + term
        dx, dnw = _rms_bwd(h_ref[...], nw_ref[...], dxn)
        dnw_ref[0:1, :] += dnw
        dho_ref[...] = dh_ref[...] + dx

    row = pl.BlockSpec((tm, d), lambda i: (i, 0))
    return pl.pallas_call(
        body, name=name, grid=(t // tm,),
        in_specs=[row, row, pl.BlockSpec((1, d), lambda i: (0, 0))]
        + [pl.BlockSpec((tm, k), lambda i: (i, 0)) for k in ks]
        + [_wspec(w, wi)],
        out_specs=[row, pl.BlockSpec((8, d), lambda i: (0, 0))],
        out_shape=[jax.ShapeDtypeStruct((t, d), F32), jax.ShapeDtypeStruct((8, d), F32)],
        compiler_params=_cparams("arbitrary"),
    )(dh, h, nw, *parts, w)


def loss_head(cfg, h, nf, tgt, name):
    t, d = h.shape
    tm = cfg.tm

    def body(h_ref, nf_ref, tgt_ref, loss_ref, dh_ref, dnf_ref):
        @pl.when(pl.program_id(0) == 0)
        def _():
            loss_ref[...] = jnp.zeros_like(loss_ref)
            dnf_ref[...] = jnp.zeros_like(dnf_ref)

        hh = h_ref[...]
        err = hh * _rstd(hh) * nf_ref[...] - tgt_ref[...]
        row = jnp.sum(err * err, axis=-1, keepdims=True) * (0.5 / d)
        loss_ref[...] += jnp.sum(row, axis=0, keepdims=True)
        dx, dnf = _rms_bwd(hh, nf_ref[...], err * (1.0 / d))
        dnf_ref[0:1, :] += dnf
        dh_ref[...] = dx

    row = pl.BlockSpec((tm, d), lambda i: (i, 0))
    return pl.pallas_call(
        body, name=name, grid=(t // tm,),
        in_specs=[row, pl.BlockSpec((1, d), lambda i: (0, 0)), row],
        out_specs=[pl.BlockSpec((8, 128), lambda i: (0, 0)), row, pl.BlockSpec((8, d), lambda i: (0, 0))],
        out_shape=[jax.ShapeDtypeStruct((8, 128), F32), jax.ShapeDtypeStruct((t, d), F32),
                   jax.ShapeDtypeStruct((8, d), F32)],
        compiler_params=_cparams("arbitrary"),
    )(h, nf, tgt)


def bucket_table(cfg):
    qi = jnp.arange(cfg.win)[:, None]
    sj = jnp.arange(2 * cfg.win)[None, :]
    dist = qi + cfg.win - sj
    n = jnp.maximum(dist, 0)
    max_exact = cfg.nbuckets // 2
    nf = jnp.maximum(n, max_exact).astype(F32)
    large = max_exact + (jnp.log(nf / max_exact) / math.log(cfg.max_dist / max_exact)
                         * (cfg.nbuckets - max_exact)).astype(jnp.int32)
    large = jnp.minimum(large, cfg.nbuckets - 1)
    bucket = jnp.where(n < max_exact, n, large)
    return jnp.where((dist >= 0) & (dist < cfg.win), bucket, -1).astype(jnp.int32)


def bias_build(cfg, rel_bias, buckets, name):
    w = cfg.win

    def body(rb_ref, bk_ref, o_ref):
        bk = bk_ref[...]
        for h in range(cfg.hq):
            acc = jnp.full((w, 2 * w), NEG_INF, F32)
            for b in range(cfg.nbuckets):
                acc = jnp.where(bk == b, rb_ref[b, h], acc)
            o_ref[h] = acc

    return pl.pallas_call(
        body, name=name,
        in_specs=[pl.BlockSpec(memory_space=pltpu.SMEM), pl.BlockSpec(memory_space=pltpu.VMEM)],
        out_specs=pl.BlockSpec(memory_space=pltpu.VMEM),
        out_shape=jax.ShapeDtypeStruct((cfg.hq, w, 2 * w), F32),
    )(rel_bias, buckets)


def bias_grad(cfg, dbias, buckets, name):
    w = cfg.win

    def body(db_ref, bk_ref, o_ref, rows_ref):
        bk = bk_ref[...]
        for h in range(cfg.hq):
            d = db_ref[0, h]
            for e in range(1, dbias.shape[0]):
                d = d + db_ref[e, h]
            for b in range(cfg.nbuckets):
                rows_ref[b:b + 1, :] = jnp.sum(jnp.where(bk == b, d, 0.0), axis=0, keepdims=True)
            o_ref[h] = jnp.broadcast_to(jnp.sum(rows_ref[...], axis=1, keepdims=True), (cfg.nbuckets, 128))

    return pl.pallas_call(
        body, name=name,
        in_specs=[pl.BlockSpec(memory_space=pltpu.VMEM), pl.BlockSpec(memory_space=pltpu.VMEM)],
        out_specs=pl.BlockSpec(memory_space=pltpu.VMEM),
        out_shape=jax.ShapeDtypeStruct((cfg.hq, cfg.nbuckets, 128), F32),
        scratch_shapes=[pltpu.VMEM((cfg.nbuckets, 2 * w), F32)],
    )(dbias, buckets)


def _attn_probs(cfg, qk, bias_h, sink, first_ok):
    s = qk * (1.0 / math.sqrt(cfg.hd)) + bias_h
    s = jnp.where(first_ok, s, NEG_INF)
    m = jnp.maximum(jnp.max(s, axis=-1, keepdims=True), sink)
    e = jnp.exp(s - m)
    es = jnp.exp(sink - m)
    inv = 1.0 / (jnp.sum(e, axis=-1, keepdims=True) + es)
    return e * inv, es * inv


def _attn_block_inputs(cfg, n, q_ref, kv_ref):
    w = cfg.win
    r0 = pl.multiple_of(n * w, w)
    rp = pl.multiple_of(jnp.maximum(n - 1, 0) * w, w)
    qb = q_ref[pl.ds(r0, w), :]
    kk = jnp.concatenate([kv_ref[pl.ds(rp, w), :], kv_ref[pl.ds(r0, w), :]], axis=0)
    col = lax.broadcasted_iota(jnp.int32, (w, 2 * w), 1)
    first_ok = (n > 0) | (col >= w)
    return r0, rp, qb, kk, first_ok


def _kv_col_block(cfg):
    assert cfg.qw % (2 * cfg.kvw) == 0
    return cfg.qw // (2 * cfg.kvw)


def attn_fwd(cfg, u, bias, sinks, name):
    t = u.shape[0]
    s, w, hd, g = cfg.s, cfg.win, cfg.hd, cfg.hq // cfg.hkv
    kvb = _kv_col_block(cfg)

    def body(q_ref, kv_ref, bias_ref, sink_ref, o_ref):
        def blk(n, carry):
            r0, _, qb, kk, first_ok = _attn_block_inputs(cfg, n, q_ref, kv_ref)
            heads = range(cfg.hq)
            scores = [_nt(qb[:, hd * h:hd * (h + 1)], kk[:, hd * (h // g):hd * (h // g + 1)]) for h in heads]
            probs = [_attn_probs(cfg, scores[h], bias_ref[h], sink_ref[h], first_ok)[0].astype(BF16) for h in heads]
            outs = [_nn(probs[h], kk[:, cfg.kvw + hd * (h // g):cfg.kvw + hd * (h // g + 1)]) for h in heads]
            o_ref[pl.ds(r0, w), :] = jnp.concatenate(outs, axis=1).astype(BF16)
            return carry

        lax.fori_loop(0, s // w, blk, 0)

    return pl.pallas_call(
        body, name=name, grid=(t // s,),
        in_specs=[pl.BlockSpec((s, cfg.qw), lambda b: (b, 0)), pl.BlockSpec((s, 2 * cfg.kvw), lambda b: (b, kvb)),
                  pl.BlockSpec(bias.shape, lambda b: (0, 0, 0)), pl.BlockSpec(memory_space=pltpu.SMEM)],
        out_specs=pl.BlockSpec((s, cfg.qw), lambda b: (b, 0)),
        out_shape=jax.ShapeDtypeStruct((t, cfg.qw), BF16),
        compiler_params=_cparams("arbitrary"),
    )(u, u, bias, sinks)


def attn_bwd(cfg, u, dcat, bias, sinks, name):
    t = u.shape[0]
    s, w, hd, g = cfg.s, cfg.win, cfg.hd, cfg.hq // cfg.hkv
    kvb = _kv_col_block(cfg)
    scale = 1.0 / math.sqrt(hd)
    assert cfg.hq <= 8

    def body(q_ref, kv_ref, do_ref, bias_ref, sink_ref, du_ref, dbias_ref, dsink_ref, dkv_ref):
        @pl.when(pl.program_id(0) == 0)
        def _():
            dbias_ref[...] = jnp.zeros_like(dbias_ref)
            dsink_ref[...] = jnp.zeros_like(dsink_ref)

        dkv_ref[...] = jnp.zeros_like(dkv_ref)

        def blk(n, carry):
            r0, rp, qb, kk, first_ok = _attn_block_inputs(cfg, n, q_ref, kv_ref)
            dob = do_ref[pl.ds(r0, w), :]
            heads = range(cfg.hq)
            kjs = [kk[:, hd * j:hd * (j + 1)] for j in range(cfg.hkv)]
            vjs = [kk[:, cfg.kvw + hd * j:cfg.kvw + hd * (j + 1)] for j in range(cfg.hkv)]
            qhs = [qb[:, hd * h:hd * (h + 1)] for h in heads]
            dohs = [dob[:, hd * h:hd * (h + 1)] for h in heads]
            scores = [_nt(qhs[h], kjs[h // g]) for h in heads]
            dps = [_nt(dohs[h], vjs[h // g]) for h in heads]
            pbs, dsbs = [], []
            for h in heads:
                p, ps = _attn_probs(cfg, scores[h], bias_ref[h], sink_ref[h], first_ok)
                delta = jnp.sum(p * dps[h], axis=-1, keepdims=True)
                ds = p * (dps[h] - delta)
                dsink_ref[h:h + 1, :] += jnp.broadcast_to(-jnp.sum(ps * delta, axis=0, keepdims=True), (1, 128))
                dbias_ref[h] += ds
                pbs.append(p.astype(BF16))
                dsbs.append(ds.astype(BF16))
            dqs = []
            dks = [jnp.zeros((2 * w, hd), F32) for _ in range(cfg.hkv)]
            dvs = [jnp.zeros((2 * w, hd), F32) for _ in range(cfg.hkv)]
            for h in heads:
                dqs.append(_nn(dsbs[h], kjs[h // g]) * scale)
                dks[h // g] = dks[h // g] + _tn(dsbs[h], qhs[h]) * scale
                dvs[h // g] = dvs[h // g] + _tn(pbs[h], dohs[h])
            du_ref[pl.ds(r0, w), 0:cfg.qw] = jnp.concatenate(dqs, axis=1).astype(BF16)
            dkv = jnp.concatenate(dks + dvs, axis=1)
            dkv_ref[pl.ds(rp, w), :] += dkv[:w]
            dkv_ref[pl.ds(r0, w), :] += dkv[w:]
            return carry

        lax.fori_loop(0, s // w, blk, 0)
        du_ref[:, cfg.qw:] = dkv_ref[...].astype(BF16)

    wa = cfg.qw + 2 * cfg.kvw
    return pl.pallas_call(
        body, name=name, grid=(t // s,),
        in_specs=[pl.BlockSpec((s, cfg.qw), lambda b: (b, 0)), pl.BlockSpec((s, 2 * cfg.kvw), lambda b: (b, kvb)),
                  pl.BlockSpec((s, cfg.qw), lambda b: (b, 0)),
                  pl.BlockSpec(bias.shape, lambda b: (0, 0, 0)), pl.BlockSpec(memory_space=pltpu.SMEM)],
        out_specs=[pl.BlockSpec((s, wa), lambda b: (b, 0)), pl.BlockSpec(bias.shape, lambda b: (0, 0, 0)),
                   pl.BlockSpec((8, 128), lambda b: (0, 0))],
        out_shape=[jax.ShapeDtypeStruct((t, wa), BF16), jax.ShapeDtypeStruct(bias.shape, F32),
                   jax.ShapeDtypeStruct((8, 128), F32)],
        scratch_shapes=[pltpu.VMEM((s, 2 * cfg.kvw), F32)],
        compiler_params=_cparams("arbitrary"),
    )(u, u, dcat, bias, sinks)


def _shift_views(win, rc, pad):
    return [win] + [win[j:j + rc + pad - 8] for j in range(1, 8)]


def _tap(views, off, rc):
    a = 8 * (off // 8)
    return views[off % 8][a:a + rc]


def _conv_rows(views, w_ref, cw, pad, rc, lanes=slice(None)):
    acc = None
    for k in range(cw):
        term = _tap(views, pad - (cw - 1) + k, rc) * w_ref[k:k + 1, lanes]
        acc = term if acc is None else acc + term
    return acc


def _conv_rows_t(views, w_ref, cw, rc, lanes=slice(None)):
    acc = None
    for k in range(cw):
        term = _tap(views, cw - 1 - k, rc) * w_ref[k:k + 1, lanes]
        acc = term if acc is None else acc + term
    return acc


def _group_sum(x):
    acc = x[0:8]
    for i in range(1, x.shape[0] // 8):
        acc = acc + x[8 * i:8 * i + 8]
    return acc


def _conv_wgrad(views, dy, acc_ref, cw, pad, rc, lanes=slice(None)):
    for k in range(cw):
        acc_ref[k, :, lanes] += _group_sum(dy * _tap(views, pad - (cw - 1) + k, rc))


LANE_TILE = 128


def _lane_tiles(width):
    return [slice(c0, c0 + LANE_TILE) for c0 in range(0, width, LANE_TILE)]


def _conv_rows_tiled(x_ref, r0, w_ref, cw, pad, rc):
    return jnp.concatenate([_conv_rows(_shift_views(x_ref[pl.ds(r0, rc + pad), lanes], rc, pad), w_ref, cw, pad, rc, lanes)
                            for lanes in _lane_tiles(x_ref.shape[1])], axis=1)


def _conv_rows_t_tiled(x_ref, r0, w_ref, cw, pad, rc):
    return jnp.concatenate([_conv_rows_t(_shift_views(x_ref[pl.ds(r0, rc + pad), lanes], rc, pad), w_ref, cw, rc, lanes)
                            for lanes in _lane_tiles(x_ref.shape[1])], axis=1)


def _conv_wgrad_tiled(x_ref, r0, dy, acc_ref, cw, pad, rc):
    for lanes in _lane_tiles(x_ref.shape[1]):
        _conv_wgrad(_shift_views(x_ref[pl.ds(r0, rc + pad), lanes], rc, pad), dy[:, lanes], acc_ref, cw, pad, rc, lanes)


CONV_RC = 64
CONV_PAD = 32
GLU_RC = 256


def _conv_col_blocks(cfg):
    off = cfg.qw + 2 * cfg.kvw
    bw = math.gcd(off, cfg.cc)
    assert bw % 128 == 0
    n = cfg.cc // bw
    return bw, [off // bw + i for i in range(n)], [(off + cfg.cc) // bw + i for i in range(n)]


def _glu_inputs(a_refs, b_refs, rows):
    ga = jnp.concatenate([r[rows, :] for r in a_refs], axis=1).astype(F32)
    gb = jnp.concatenate([r[rows, :] for r in b_refs], axis=1).astype(F32)
    return ga, gb


def _fill_glu(cfg, a_refs, b_refs, xp_ref):
    xp_ref[0:CONV_PAD, :] = jnp.zeros((CONV_PAD, cfg.cc), F32)

    def fill(i, carry):
        r0 = pl.multiple_of(i * GLU_RC, GLU_RC)
        ga, gb = _glu_inputs(a_refs, b_refs, pl.ds(r0, GLU_RC))
        xp_ref[pl.ds(CONV_PAD + r0, GLU_RC), :] = ga * _sigmoid(gb)
        return carry

    lax.fori_loop(0, cfg.s // GLU_RC, fill, 0)


def _layernorm_stats(cv):
    mu = jnp.mean(cv, axis=-1, keepdims=True)
    xc = cv - mu
    rstd = lax.rsqrt(jnp.mean(xc * xc, axis=-1, keepdims=True) + LN_EPS)
    return xc * rstd, rstd


def conv_fwd(cfg, u, cw_w, cb, lg, lb, name):
    t = u.shape[0]
    s, cc, cw = cfg.s, cfg.cc, cfg.cw
    bw, a_idx, b_idx = _conv_col_blocks(cfg)
    nb = len(a_idx)

    def body(*refs):
        a_refs, b_refs = refs[:nb], refs[nb:2 * nb]
        w_ref, cb_ref, lg_ref, lb_ref, o_ref, cv_ref, xp_ref = refs[2 * nb:]
        _fill_glu(cfg, a_refs, b_refs, xp_ref)

        def chunk(i, carry):
            r0 = pl.multiple_of(i * CONV_RC, CONV_RC)
            cv = _conv_rows_tiled(xp_ref, r0, w_ref, cw, CONV_PAD, CONV_RC) + cb_ref[...]
            cv_ref[pl.ds(r0, CONV_RC), :] = cv
            xhat, _ = _layernorm_stats(cv)
            ln = xhat * lg_ref[...] + lb_ref[...]
            o_ref[pl.ds(r0, CONV_RC), :] = (ln * _sigmoid(ln)).astype(BF16)
            return carry

        lax.fori_loop(0, s // CONV_RC, chunk, 0, unroll=2)

    def colspec(j):
        return pl.BlockSpec((s, bw), lambda b: (b, j))

    vec = pl.BlockSpec((1, cc), lambda b: (0, 0))
    return pl.pallas_call(
        body, name=name, grid=(t // s,),
        in_specs=[colspec(j) for j in a_idx + b_idx] + [pl.BlockSpec((cw, cc), lambda b: (0, 0)), vec, vec, vec],
        out_specs=[pl.BlockSpec((s, cc), lambda b: (b, 0))] * 2,
        out_shape=[jax.ShapeDtypeStruct((t, cc), BF16), jax.ShapeDtypeStruct((t, cc), F32)],
        scratch_shapes=[pltpu.VMEM((CONV_PAD + s, cc), F32)],
        compiler_params=_cparams("arbitrary"),
    )(*([u] * (2 * nb)), cw_w, cb, lg, lb)


def conv_bwd(cfg, u, cv_saved, dcat, cw_w, cb, lg, lb, name):
    t = u.shape[0]
    s, cc, cw = cfg.s, cfg.cc, cfg.cw
    bw, a_idx, b_idx = _conv_col_blocks(cfg)
    nb = len(a_idx)
    assert cfg.qw % cc == 0 and cw <= 32

    def body(*refs):
        a_refs, b_refs = refs[:nb], refs[nb:2 * nb]
        cv_ref, dc_ref, w_ref, cb_ref, lg_ref, lb_ref, du_ref, dw_ref, dvec_ref, xp_ref, dcv_ref, dwacc_ref = refs[2 * nb:]

        @pl.when(pl.program_id(0) == 0)
        def _():
            dw_ref[...] = jnp.zeros_like(dw_ref)
            dvec_ref[...] = jnp.zeros_like(dvec_ref)

        _fill_glu(cfg, a_refs, b_refs, xp_ref)
        dcv_ref[s:s + CONV_PAD, :] = jnp.zeros((CONV_PAD, cc), F32)
        dwacc_ref[...] = jnp.zeros_like(dwacc_ref)

        def chunk(i, carry):
            r0 = pl.multiple_of(i * CONV_RC, CONV_RC)
            xhat, rstd = _layernorm_stats(cv_ref[pl.ds(r0, CONV_RC), :])
            ln = xhat * lg_ref[...] + lb_ref[...]
            sg = _sigmoid(ln)
            dln = dc_ref[pl.ds(r0, CONV_RC), :].astype(F32) * (sg * (1.0 + ln * (1.0 - sg)))
            dxh = dln * lg_ref[...]
            dcv = rstd * (dxh - jnp.mean(dxh, axis=-1, keepdims=True)
                          - xhat * jnp.mean(dxh * xhat, axis=-1, keepdims=True))
            dcv_ref[pl.ds(r0, CONV_RC), :] = dcv
            dvec_ref[0:1, :] += jnp.sum(dcv, axis=0, keepdims=True)
            dvec_ref[1:2, :] += jnp.sum(dln * xhat, axis=0, keepdims=True)
            dvec_ref[2:3, :] += jnp.sum(dln, axis=0, keepdims=True)
            _conv_wgrad_tiled(xp_ref, r0, dcv, dwacc_ref, cw, CONV_PAD, CONV_RC)
            return carry

        lax.fori_loop(0, s // CONV_RC, chunk, 0, unroll=2)
        for k in range(cw):
            dw_ref[k:k + 1, :] += jnp.sum(dwacc_ref[k], axis=0, keepdims=True)

        def chunk2(i, carry):
            r0 = pl.multiple_of(i * CONV_RC, CONV_RC)
            dglu = _conv_rows_t_tiled(dcv_ref, r0, w_ref, cw, CONV_PAD, CONV_RC)
            ga, gb = _glu_inputs(a_refs, b_refs, pl.ds(r0, CONV_RC))
            sgb = _sigmoid(gb)
            du_ref[pl.ds(r0, CONV_RC), 0:cc] = (dglu * sgb).astype(BF16)
            du_ref[pl.ds(r0, CONV_RC), cc:2 * cc] = (dglu * ga * sgb * (1.0 - sgb)).astype(BF16)
            return carry

        lax.fori_loop(0, s // CONV_RC, chunk2, 0)

    def colspec(j):
        return pl.BlockSpec((s, bw), lambda b: (b, j))

    vec = pl.BlockSpec((1, cc), lambda b: (0, 0))
    return pl.pallas_call(
        body, name=name, grid=(t // s,),
        in_specs=[colspec(j) for j in a_idx + b_idx]
        + [pl.BlockSpec((s, cc), lambda b: (b, 0)), pl.BlockSpec((s, cc), lambda b: (b, cfg.qw // cc)),
           pl.BlockSpec((cw, cc), lambda b: (0, 0)), vec, vec, vec],
        out_specs=[pl.BlockSpec((s, 2 * cc), lambda b: (b, 0)), pl.BlockSpec((32, cc), lambda b: (0, 0)),
                   pl.BlockSpec((8, cc), lambda b: (0, 0))],
        out_shape=[jax.ShapeDtypeStruct((t, 2 * cc), BF16), jax.ShapeDtypeStruct((32, cc), F32),
                   jax.ShapeDtypeStruct((8, cc), F32)],
        scratch_shapes=[pltpu.VMEM((CONV_PAD + s, cc), F32), pltpu.VMEM((s + CONV_PAD, cc), F32),
                        pltpu.VMEM((cw, 8, cc), F32)],
        compiler_params=_cparams("arbitrary"),
    )(*([u] * (2 * nb)), cv_saved, dcat, cw_w, cb, lg, lb)


LRU_RC = 64
LRU_PAD = 8
SCAN_RC = 16
SCAN_UNROLL = 4
GELU_K = math.sqrt(2.0 / math.pi)


def _expm1_neg(z):
    return jnp.where(z > -0.05, z * (1.0 + z * (0.5 + z * (1.0 / 6.0 + z * (1.0 / 24.0)))), jnp.exp(z) - 1.0)


def _log_sigmoid(x):
    e = jnp.exp(-jnp.abs(x))
    log1p = jnp.where(e < 0.01, e * (1.0 - e * (0.5 - e * (1.0 / 3.0))), jnp.log(1.0 + e))
    return jnp.minimum(x, 0.0) - log1p


def _gelu(x):
    t = jnp.tanh(GELU_K * (x + 0.044715 * x * x * x))
    return 0.5 * x * (1.0 + t), t


def _gelu_grad(x, t):
    return 0.5 * (1.0 + t) + 0.5 * x * (1.0 - t * t) * GELU_K * (1.0 + 3.0 * 0.044715 * x * x)


def _lru_gates(xc, wa_ref, ba, wx_ref, bx, ls):
    nh = xc.shape[1] // 128
    xb = xc.astype(BF16)
    ra = jnp.concatenate([_nn(xb[:, 128 * h:128 * (h + 1)], wa_ref[h]) for h in range(nh)], axis=1) + ba
    ia = jnp.concatenate([_nn(xb[:, 128 * h:128 * (h + 1)], wx_ref[h]) for h in range(nh)], axis=1) + bx
    r = _sigmoid(ra)
    ig = _sigmoid(ia)
    log_a = RG_LRU_C * r * ls
    return r, ig, log_a


def _lru_decay(log_a):
    return jnp.exp(log_a), jnp.sqrt(-_expm1_neg(2.0 * log_a))


def _fill_padded(src_ref, dst_ref, s, ct):
    dst_ref[0:LRU_PAD, :] = jnp.zeros((LRU_PAD, ct), F32)

    def fill(i, carry):
        r0 = pl.multiple_of(i * GLU_RC, GLU_RC)
        dst_ref[pl.ds(LRU_PAD + r0, GLU_RC), :] = src_ref[pl.ds(r0, GLU_RC), :].astype(F32)
        return carry

    lax.fori_loop(0, s // GLU_RC, fill, 0)


def _lru_specs(cfg, ct):
    s, lw = cfg.s, cfg.lw
    nct = lw // ct
    nh = ct // 128
    act = [pl.BlockSpec((s, ct), lambda c, b: (b, c)), pl.BlockSpec((s, ct), lambda c, b: (b, nct + c))]
    vec = pl.BlockSpec((1, ct), lambda c, b: (0, c))
    gate_w = pl.BlockSpec((nh, 128, 128), lambda c, b: (c, 0, 0))
    params = [pl.BlockSpec((cfg.lcw, ct), lambda c, b: (0, c)), vec, gate_w, vec, gate_w, vec, vec]
    return nct, act, params


def lru_fwd(cfg, u, conv_w, conv_b, wa, ba, wx, bx, lam, name):
    t = u.shape[0]
    s, lw, lcw, ct = cfg.s, cfg.lw, cfg.lcw, cfg.ct_f
    nct, act, params = _lru_specs(cfg, ct)

    def body(gi_ref, ri_ref, cw_ref, cb_ref, wa_ref, ba_ref, wx_ref, bx_ref, lam_ref,
             y_ref, hs_ref, r_ref, ig_ref, xc_ref, la_ref, xp_ref, a_ref, b_ref):
        _fill_padded(ri_ref, xp_ref, s, ct)
        ls = _log_sigmoid(lam_ref[...])

        def chunk(i, carry):
            r0 = pl.multiple_of(i * LRU_RC, LRU_RC)
            rows = pl.ds(r0, LRU_RC)
            views = _shift_views(xp_ref[pl.ds(r0, LRU_RC + LRU_PAD), :], LRU_RC, LRU_PAD)
            xc = _conv_rows(views, cw_ref, lcw, LRU_PAD, LRU_RC) + cb_ref[...]
            r, ig, log_a = _lru_gates(xc, wa_ref, ba_ref[...], wx_ref, bx_ref[...], ls)
            a, mult = _lru_decay(log_a)
            a_ref[rows, :] = a
            b_ref[rows, :] = mult * (ig * xc)
            r_ref[rows, :] = r.astype(BF16)
            ig_ref[rows, :] = ig.astype(BF16)
            xc_ref[rows, :] = xc.astype(BF16)
            la_ref[rows, :] = log_a
            return carry

        lax.fori_loop(0, s // LRU_RC, chunk, 0, unroll=2)
        row = lax.broadcasted_iota(jnp.int32, (SCAN_RC, ct), 0)

        def scan(i, h_last):
            for sub in range(SCAN_UNROLL):
                rows = pl.ds(pl.multiple_of((i * SCAN_UNROLL + sub) * SCAN_RC, SCAN_RC), SCAN_RC)
                a = a_ref[rows, :]
                b = b_ref[rows, :]
                sft = 1
                while sft < SCAN_RC:
                    a_sh = jnp.where(row >= sft, pltpu.roll(a, sft, 0), 1.0)
                    b_sh = jnp.where(row >= sft, pltpu.roll(b, sft, 0), 0.0)
                    b = a * b_sh + b
                    a = a * a_sh
                    sft *= 2
                h = a * h_last + b
                gate, _ = _gelu(gi_ref[rows, :].astype(F32))
                y_ref[rows, :] = (gate * h).astype(BF16)
                hs_ref[rows, :] = h.astype(BF16)
                h_last = h[SCAN_RC - 1:SCAN_RC, :]
            return h_last

        lax.fori_loop(0, s // (SCAN_RC * SCAN_UNROLL), scan, jnp.zeros((1, ct), F32))

    out = pl.BlockSpec((s, ct), lambda c, b: (b, c))
    return pl.pallas_call(
        body, name=name, grid=(nct, t // s),
        in_specs=act + params,
        out_specs=[out] * 6,
        out_shape=[jax.ShapeDtypeStruct((t, lw), BF16)] * 5 + [jax.ShapeDtypeStruct((t, lw), F32)],
        scratch_shapes=[pltpu.VMEM((LRU_PAD + s, ct), F32), pltpu.VMEM((s, ct), F32), pltpu.VMEM((s, ct), F32)],
        compiler_params=_cparams("arbitrary", "arbitrary"),
    )(u, u, conv_w, conv_b, wa, ba, wx, bx, lam)


def lru_bwd(cfg, u, saved, dy, conv_w, conv_b, wa, ba, wx, bx, lam, name):
    t = u.shape[0]
    s, lw, lcw, ct = cfg.s, cfg.lw, cfg.lcw, cfg.ct_b
    nct, act, params = _lru_specs(cfg, ct)
    nh = ct // 128
    nscan = s // SCAN_RC
    assert lcw <= 8

    def body(gi_ref, ri_ref, hs_ref, r_ref, ig_ref, xc_ref, la_ref, dy_ref,
             cw_ref, cb_ref, wa_ref, ba_ref, wx_ref, bx_ref, lam_ref,
             dug_ref, dur_ref, dwa_ref, dwx_ref, dvec_ref, dcw_ref,
             xp_ref, hp_ref, a_ref, g_ref, dxc_ref, dwacc_ref):
        @pl.when(pl.program_id(1) == 0)
        def _():
            dwa_ref[...] = jnp.zeros_like(dwa_ref)
            dwx_ref[...] = jnp.zeros_like(dwx_ref)
            dvec_ref[...] = jnp.zeros_like(dvec_ref)
            dcw_ref[...] = jnp.zeros_like(dcw_ref)

        _fill_padded(ri_ref, xp_ref, s, ct)
        _fill_padded(hs_ref, hp_ref, s, ct)
        dxc_ref[s:s + LRU_PAD, :] = jnp.zeros((LRU_PAD, ct), F32)
        dwacc_ref[...] = jnp.zeros_like(dwacc_ref)
        lam = lam_ref[...]
        ls = _log_sigmoid(lam)

        def chunk(i, carry):
            r0 = pl.multiple_of(i * LRU_RC, LRU_RC)
            rows = pl.ds(r0, LRU_RC)
            a_ref[rows, :] = jnp.exp(la_ref[rows, :])
            x = gi_ref[rows, :].astype(F32)
            gate, th = _gelu(x)
            dyv = dy_ref[rows, :].astype(F32)
            g_ref[rows, :] = dyv * gate
            dug_ref[rows, :] = (dyv * hp_ref[pl.ds(LRU_PAD + r0, LRU_RC), :] * _gelu_grad(x, th)).astype(BF16)
            return carry

        lax.fori_loop(0, s // LRU_RC, chunk, 0, unroll=2)
        row = lax.broadcasted_iota(jnp.int32, (SCAN_RC, ct), 0)

        def scan(ii, carry):
            g_next, a_next = carry
            for sub in range(SCAN_UNROLL):
                step = nscan - 1 - (ii * SCAN_UNROLL + sub)
                rows = pl.ds(pl.multiple_of(step * SCAN_RC, SCAN_RC), SCAN_RC)
                a = a_ref[rows, :]
                d = g_ref[rows, :]
                c = jnp.where(row < SCAN_RC - 1, pltpu.roll(a, SCAN_RC - 1, 0), a_next)
                sft = 1
                while sft < SCAN_RC:
                    c_sh = jnp.where(row < SCAN_RC - sft, pltpu.roll(c, SCAN_RC - sft, 0), 1.0)
                    d_sh = jnp.where(row < SCAN_RC - sft, pltpu.roll(d, SCAN_RC - sft, 0), 0.0)
                    d = d + c * d_sh
                    c = c * c_sh
                    sft *= 2
                g = d + c * g_next
                g_ref[rows, :] = g
                g_next, a_next = g[0:1, :], a[0:1, :]
            return g_next, a_next

        lax.fori_loop(0, nscan // SCAN_UNROLL, scan, (jnp.zeros((1, ct), F32), jnp.zeros((1, ct), F32)))

        def chunk3(i, carry):
            r0 = pl.multiple_of(i * LRU_RC, LRU_RC)
            rows = pl.ds(r0, LRU_RC)
            r, ig, xc = r_ref[rows, :].astype(F32), ig_ref[rows, :].astype(F32), xc_ref[rows, :].astype(F32)
            a, mult = _lru_decay(la_ref[rows, :])
            g = g_ref[rows, :]
            h_prev = hp_ref[pl.ds(r0, LRU_RC + LRU_PAD), :][LRU_PAD - 1:LRU_PAD - 1 + LRU_RC]
            dix = g * mult
            di = dix * xc
            dxc = dix * ig
            da = g * h_prev - (g * ig * xc) * a / mult
            dlog_a = da * a
            dr = dlog_a * (RG_LRU_C * ls)
            dra = dr * r * (1.0 - r)
            dia = di * ig * (1.0 - ig)
            xb, drab, diab = xc.astype(BF16), dra.astype(BF16), dia.astype(BF16)
            dxg = []
            for h in range(nh):
                cols = slice(128 * h, 128 * (h + 1))
                dxg.append(_nt(drab[:, cols], wa_ref[h]) + _nt(diab[:, cols], wx_ref[h]))
                dwa_ref[h] += _tn(xb[:, cols], drab[:, cols])
                dwx_ref[h] += _tn(xb[:, cols], diab[:, cols])
            dxc = dxc + jnp.concatenate(dxg, axis=1)
            dvec_ref[0:1, :] += jnp.sum(dra, axis=0, keepdims=True)
            dvec_ref[1:2, :] += jnp.sum(dia, axis=0, keepdims=True)
            dvec_ref[2:3, :] += jnp.sum(dlog_a * r, axis=0, keepdims=True) * (RG_LRU_C * _sigmoid(-lam))
            dvec_ref[3:4, :] += jnp.sum(dxc, axis=0, keepdims=True)
            dxc_ref[rows, :] = dxc
            views = _shift_views(xp_ref[pl.ds(r0, LRU_RC + LRU_PAD), :], LRU_RC, LRU_PAD)
            _conv_wgrad(views, dxc, dwacc_ref, lcw, LRU_PAD, LRU_RC)
            return carry

        lax.fori_loop(0, s // LRU_RC, chunk3, 0, unroll=2)
        for k in range(lcw):
            dcw_ref[k:k + 1, :] += jnp.sum(dwacc_ref[k], axis=0, keepdims=True)

        def chunk4(i, carry):
            r0 = pl.multiple_of(i * LRU_RC, LRU_RC)
            views = _shift_views(dxc_ref[pl.ds(r0, LRU_RC + LRU_PAD), :], LRU_RC, LRU_PAD)
            dur_ref[pl.ds(r0, LRU_RC), :] = _conv_rows_t(views, cw_ref, lcw, LRU_RC).astype(BF16)
            return carry

        lax.fori_loop(0, s // LRU_RC, chunk4, 0, unroll=2)

    blk = pl.BlockSpec((s, ct), lambda c, b: (b, c))
    acc8 = pl.BlockSpec((8, ct), lambda c, b: (0, c))
    gate_w = pl.BlockSpec((nh, 128, 128), lambda c, b: (c, 0, 0))
    return pl.pallas_call(
        body, name=name, grid=(nct, t // s),
        in_specs=act + [blk] * 6 + params,
        out_specs=[blk, blk, gate_w, gate_w, acc8, acc8],
        out_shape=[jax.ShapeDtypeStruct((t, lw), BF16), jax.ShapeDtypeStruct((t, lw), BF16),
                   jax.ShapeDtypeStruct((cfg.lh, 128, 128), F32), jax.ShapeDtypeStruct((cfg.lh, 128, 128), F32),
                   jax.ShapeDtypeStruct((8, lw), F32), jax.ShapeDtypeStruct((8, lw), F32)],
        scratch_shapes=[pltpu.VMEM((LRU_PAD + s, ct), F32), pltpu.VMEM((LRU_PAD + s, ct), F32),
                        pltpu.VMEM((s, ct), F32), pltpu.VMEM((s, ct), F32), pltpu.VMEM((s + LRU_PAD, ct), F32),
                        pltpu.VMEM((lcw, 8, ct), F32)],
        compiler_params=_cparams("arbitrary", "arbitrary"),
    )(u, u, *saved, dy, conv_w, conv_b, wa, ba, wx, bx, lam)


def local_step(cfg, x, tgt, p, shards=None):
    buckets = bucket_table(cfg)
    bias = bias_build(cfg, p["rel_bias"], buckets, "bias_build")
    ga_w, gx_w = p["gate_a_w"].astype(BF16), p["gate_x_w"].astype(BF16)
    wf = dict(p["wf"])
    dist = shards is not None

    def ffn_forward(l, k, h, nw):
        riders = gather_riders([shards[(l + 1, k)]]) if dist and l + 1 < cfg.depth else None
        outs = ffn_fwd(cfg, h, nw, wf[(l, k)], 0, 2, f"ffn{k + 1}_fwd_{l}", riders=riders)
        if riders is not None:
            wf[(l + 1, k)] = outs[4].reshape(3, -1, cfg.d)
        return outs[:4]

    def blocks(g):
        g = g if g.ndim == 3 else g[None]
        return g.reshape(g.shape[0], N_DEV, g.shape[1] // N_DEV, g.shape[2])

    h = x
    saved = []
    for l in range(cfg.depth):
        i = l // 2
        s = {"h0": h}
        s["h1"], s["xn1"], s["g1"], s["u1"] = ffn_forward(l, 0, h, p["norm_ffn1"][l][None])
        if l % 2 == 0:
            s["um"], s["xnm"] = norm_proj(cfg, s["h1"], p["norm_mix"][l][None], p["even_in"], f"mix_in_{l}", wi=i)
            attn = attn_fwd(cfg, s["um"], bias, p["attn_sinks"][i], f"attn_fwd_{l}")
            c, s["cv"] = conv_fwd(cfg, s["um"], p["conv_b_w"][i], p["conv_b_b"][i][None], p["conv_ln_g"][i][None],
                                  p["conv_ln_b"][i][None], f"conv_fwd_{l}")
            s["parts"] = [attn, c]
            s["h2"] = proj_residual(cfg, s["h1"], s["parts"], p["even_out"], f"mix_out_{l}", wi=i)
        else:
            s["um"], s["xnm"] = norm_proj(cfg, s["h1"], p["norm_mix"][l][None], p["odd_in"], f"mix_in_{l}", wi=i)
            y, *s["saved"] = lru_fwd(cfg, s["um"], p["lru_conv_w"][i], p["lru_conv_b"][i][None], ga_w[i], p["gate_a_b"][i][None],
                                 gx_w[i], p["gate_x_b"][i][None], p["lru_lambda"][i][None], f"lru_fwd_{l}")
            s["parts"] = [y]
            s["h2"] = proj_residual(cfg, s["h1"], s["parts"], p["odd_out"], f"mix_out_{l}", wi=i)
        h, s["xn2"], s["g2"], s["u2"] = ffn_forward(l, 1, s["h2"], p["norm_ffn2"][l][None])
        saved.append(s)

    loss, dh, dnf = loss_head(cfg, h, p["norm_final"][None], tgt, "loss_head")
    big = [None] * cfg.depth
    sm = {k: [None] * cfg.depth for k in ("norm_ffn1", "norm_mix", "norm_ffn2")}
    ne, no = (cfg.depth + 1) // 2, cfg.depth // 2
    for k in ("attn_sinks", "conv_b_w", "conv_b_b", "conv_ln_g", "conv_ln_b", "dbias"):
        sm[k] = [None] * ne
    for k in ("lru_conv_w", "lru_conv_b", "gate_a_w", "gate_a_b", "gate_x_w", "gate_x_b", "lru_lambda"):
        sm[k] = [None] * no
    pending = None
    for l in reversed(range(cfg.depth)):
        i = l // 2
        s = saved[l]
        riders = scatter_riders([pending]) if pending is not None else None
        outs = ffn_bwd_x(cfg, dh, s["h2"], p["norm_ffn2"][l][None], s["g2"], s["u2"], wf[(l, 1)], 0, 2,
                         f"ffn2_bwd_x_{l}", riders=riders)
        dh, dout, dg, du, dn = outs[:5]
        if riders is not None:
            big[l + 1]["f1"] = (pending, outs[5])
        sm["norm_ffn2"][l] = dn[0]
        gf2 = blocks(ffn_bwd_w(cfg, s["xn2"], dout, s["g2"], s["u2"], dg, du, f"ffn2_bwd_w_{l}")[0])
        w_out = p["even_out"] if l % 2 == 0 else p["odd_out"]
        w_in = p["even_in"] if l % 2 == 0 else p["odd_in"]
        dcat = proj_bwd_act(cfg, dh, w_out, f"mix_out_bwd_{l}", wi=i)
        g_out = blocks(grad_weight(cfg, s["parts"], dh, f"mix_out_gw_{l}"))
        if l % 2 == 0:
            du_a, sm["dbias"][i], dsink = attn_bwd(cfg, s["um"], dcat, bias, p["attn_sinks"][i], f"attn_bwd_{l}")
            du_c, dcw, dvec = conv_bwd(cfg, s["um"], s["cv"], dcat, p["conv_b_w"][i], p["conv_b_b"][i][None],
                                       p["conv_ln_g"][i][None], p["conv_ln_b"][i][None], f"conv_bwd_{l}")
            sm["attn_sinks"][i] = dsink[:cfg.hq, 0]
            sm["conv_b_w"][i] = dcw[:cfg.cw]
            sm["conv_b_b"][i], sm["conv_ln_g"][i], sm["conv_ln_b"][i] = dvec[0], dvec[1], dvec[2]
            dparts = [du_a, du_c]
        else:
            dug, dur, dwa, dwx, dvec, dcw = lru_bwd(
                cfg, s["um"], s["saved"], dcat, p["lru_conv_w"][i], p["lru_conv_b"][i][None], ga_w[i], p["gate_a_b"][i][None],
                gx_w[i], p["gate_x_b"][i][None], p["lru_lambda"][i][None], f"lru_bwd_{l}")
            sm["gate_a_w"][i], sm["gate_x_w"][i] = dwa, dwx
            sm["gate_a_b"][i], sm["gate_x_b"][i], sm["lru_lambda"][i], sm["lru_conv_b"][i] = dvec[0], dvec[1], dvec[2], dvec[3]
            sm["lru_conv_w"][i] = dcw[:cfg.lcw]
            dparts = [dug, dur]
        g_in = blocks(grad_weight(cfg, dparts, s["xnm"], f"mix_in_gw_{l}"))
        dh, dn = norm_proj_bwd(cfg, dh, s["h1"], p["norm_mix"][l][None], dparts, w_in, f"mix_in_bwd_{l}", wi=i)
        sm["norm_mix"][l] = dn[0]
        riders = scatter_riders([gf2]) if dist else None
        outs = ffn_bwd_x(cfg, dh, s["h0"], p["norm_ffn1"][l][None], s["g1"], s["u1"], wf[(l, 0)], 0, 2,
                         f"ffn1_bwd_x_{l}", riders=riders)
        dh, dout, dg, du, dn = outs[:5]
        sm["norm_ffn1"][l] = dn[0]
        riders = scatter_riders([g_in, g_out]) if dist else None
        gouts = ffn_bwd_w(cfg, s["xn1"], dout, s["g1"], s["u1"], dg, du, f"ffn1_bwd_w_{l}", riders=riders)
        gf1 = blocks(gouts[0])
        big[l] = {"f1": (gf1, None), "f2": (gf2, outs[5] if dist else None),
                  "in": (g_in, gouts[1] if dist else None), "out": (g_out, gouts[2] if dist else None)}
        pending = gf1 if dist and l > 0 else None
    drb = bias_grad(cfg, jnp.stack(sm.pop("dbias")), buckets, "bias_grad")
    small = {k: jnp.stack(v) for k, v in sm.items()}
    small["rel_bias"] = drb[:, :, 0].T
    small["norm_final"] = dnf[0]
    return loss, dh, big, small


MESH = pl.DeviceIdType.MESH
ANY = pl.BlockSpec(memory_space=pl.ANY)


def _place():
    return lax.axis_index("x"), lax.axis_index("y"), lax.axis_index("c")


FLIPS = ((0, 0, 1), (0, 1, 0), (0, 1, 1), (1, 0, 0), (1, 0, 1), (1, 1, 0), (1, 1, 1))


def _peer(place, flip):
    return tuple(1 - v if f else v for v, f in zip(place, flip))


def _dev_index(place):
    return 4 * place[0] + 2 * place[1] + place[2]


def _remote(src, dst, send_sems, recv_sems, g, k, peer):
    return pltpu.make_async_remote_copy(src_ref=src, dst_ref=dst, send_sem=send_sems.at[g, k], recv_sem=recv_sems.at[g, k],
                                        device_id=peer, device_id_type=MESH)


def gather_riders(srcs):
    ng = len(srcs)

    def copies(in_refs, out_refs, sems):
        send_sems, recv_sems, local_sems = sems
        me = _place()
        local, sends, recvs = [], [], []
        for g in range(ng):
            mine = out_refs[g].at[:, _dev_index(me)]
            local.append(pltpu.make_async_copy(in_refs[g], mine, local_sems.at[g]))
            for k, flip in enumerate(FLIPS):
                peer = _peer(me, flip)
                sends.append(_remote(in_refs[g], mine, send_sems, recv_sems, g, k, peer))
                recvs.append(_remote(in_refs[g], out_refs[g].at[:, _dev_index(peer)], send_sems, recv_sems, g, k, peer))
        return local, sends, recvs

    def start(in_refs, out_refs, sems):
        local, sends, _ = copies(in_refs, out_refs, sems)
        for cp in local + sends:
            cp.start()

    def wait(in_refs, out_refs, sems):
        local, sends, recvs = copies(in_refs, out_refs, sems)
        for cp in sends:
            cp.wait_send()
        for cp in recvs:
            cp.wait_recv()
        for cp in local:
            cp.wait()

    return Riders(tuple(srcs), tuple(jax.ShapeDtypeStruct((s.shape[0], N_DEV) + s.shape[1:], s.dtype) for s in srcs),
                  (pltpu.SemaphoreType.DMA((ng, 7)), pltpu.SemaphoreType.DMA((ng, 7)), pltpu.SemaphoreType.DMA((ng,))),
                  start, wait)


def scatter_riders(bufs):
    ng = len(bufs)

    def copies(in_refs, out_refs, sems):
        send_sems, recv_sems = sems
        me = _place()
        return [_remote(in_refs[g].at[:, _dev_index(_peer(me, flip))], out_refs[g].at[:, k], send_sems, recv_sems, g, k,
                        _peer(me, flip)) for g in range(ng) for k, flip in enumerate(FLIPS)]

    def start(in_refs, out_refs, sems):
        for cp in copies(in_refs, out_refs, sems):
            cp.start()

    def wait(in_refs, out_refs, sems):
        for cp in copies(in_refs, out_refs, sems):
            cp.wait()

    return Riders(tuple(bufs), tuple(jax.ShapeDtypeStruct((b.shape[0], 7) + b.shape[2:], b.dtype) for b in bufs),
                  (pltpu.SemaphoreType.DMA((ng, 7)), pltpu.SemaphoreType.DMA((ng, 7))), start, wait)


def shard_sum(buf, recv, dev, name):
    n, _, r, cdim = buf.shape

    def body(dev_ref, a_ref, b_ref, o_ref):
        acc = a_ref[...].astype(F32)
        for k in range(7):
            acc = acc + b_ref[k].astype(F32)
        o_ref[...] = acc

    return pl.pallas_call(
        body, name=name,
        grid_spec=pltpu.PrefetchScalarGridSpec(
            num_scalar_prefetch=1, grid=(n,),
            in_specs=[pl.BlockSpec((None, None, r, cdim), lambda i, dev_ref: (i, dev_ref[0], 0, 0)),
                      pl.BlockSpec((None, 7, r, cdim), lambda i, dev_ref: (i, 0, 0, 0))],
            out_specs=pl.BlockSpec((None, r, cdim), lambda i, dev_ref: (i, 0, 0))),
        out_shape=jax.ShapeDtypeStruct((n, r, cdim), F32),
    )(dev, buf, recv)


def all_gather(srcs, name):
    ng = len(srcs)

    def body(*refs):
        x_refs, o_refs = refs[:ng], refs[ng:2 * ng]
        send_sems, recv_sems, local_sems = refs[2 * ng:]
        x, y, c = _place()
        me, sibling = (x, y, c), (x, y, 1 - c)
        chips = [(1 - x, y), (x, 1 - y), (1 - x, 1 - y)]

        def copy(gi, k, block, to, src=None):
            dst = o_refs[gi].at[:, 4 * block[0] + 2 * block[1] + block[2]]
            return pltpu.make_async_remote_copy(
                src_ref=dst if src is None else src, dst_ref=dst, send_sem=send_sems.at[gi, k],
                recv_sem=recv_sems.at[gi, k], device_id=to, device_id_type=MESH)

        mine = [pltpu.make_async_copy(x_refs[gi], o_refs[gi].at[:, 4 * x + 2 * y + c], local_sems.at[gi])
                for gi in range(ng)]
        for cp in mine:
            cp.start()
        first = []
        for gi in range(ng):
            first.append(copy(gi, 0, me, sibling, src=x_refs[gi]))
            first += [copy(gi, 1 + j, me, (*chip, c), src=x_refs[gi]) for j, chip in enumerate(chips)]
        for cp in first:
            cp.start()
        passed = []
        for j, chip in enumerate(chips):
            for gi in range(ng):
                copy(gi, 1 + j, (*chip, c), me).wait_recv()
                cp = copy(gi, 4 + j, (*chip, c), sibling)
                cp.start()
                passed.append(cp)
        for gi in range(ng):
            copy(gi, 0, sibling, me).wait_recv()
            for j, chip in enumerate(chips):
                copy(gi, 4 + j, (*chip, 1 - c), me).wait_recv()
        for cp in first + passed:
            cp.wait_send()
        for cp in mine:
            cp.wait()

    return pl.pallas_call(
        body, name=name,
        in_specs=[ANY] * ng, out_specs=[ANY] * ng,
        out_shape=[jax.ShapeDtypeStruct((s.shape[0], N_DEV) + s.shape[1:], s.dtype) for s in srcs],
        scratch_shapes=[pltpu.SemaphoreType.DMA((ng, 7)), pltpu.SemaphoreType.DMA((ng, 7)),
                        pltpu.SemaphoreType.DMA((ng,))],
    )(*srcs)


def pair_exchange(bufs, name):
    ng = len(bufs)

    def body(*refs):
        b_refs, o_refs = refs[:ng], refs[ng:2 * ng]
        send_sems, recv_sems = refs[2 * ng:]
        x, y, c = _place()
        copies = [pltpu.make_async_remote_copy(
            src_ref=b_refs[gi].at[:, :, 1 - c], dst_ref=o_refs[gi], send_sem=send_sems.at[gi], recv_sem=recv_sems.at[gi],
            device_id=(x, y, 1 - c), device_id_type=MESH) for gi in range(ng)]
        for cp in copies:
            cp.start()
        for cp in copies:
            cp.wait()

    return pl.pallas_call(
        body, name=name,
        in_specs=[ANY] * ng, out_specs=[ANY] * ng,
        out_shape=[jax.ShapeDtypeStruct(b.shape[:2] + b.shape[3:], b.dtype) for b in bufs],
        scratch_shapes=[pltpu.SemaphoreType.DMA((ng,)), pltpu.SemaphoreType.DMA((ng,))],
    )(*bufs)


def chip_exchange(qs, name):
    ng = len(qs)

    def body(*refs):
        q_refs, o_refs = refs[:ng], refs[ng:2 * ng]
        send_sems, recv_sems = refs[2 * ng:]
        x, y, c = _place()
        chips = [(1 - x, y), (x, 1 - y), (1 - x, 1 - y)]
        copies = [pltpu.make_async_remote_copy(
            src_ref=q_refs[gi].at[:, 2 * chip[0] + chip[1]], dst_ref=o_refs[gi].at[:, j],
            send_sem=send_sems.at[gi, j], recv_sem=recv_sems.at[gi, j],
            device_id=(*chip, c), device_id_type=MESH) for gi in range(ng) for j, chip in enumerate(chips)]
        for cp in copies:
            cp.start()
        for cp in copies:
            cp.wait()

    return pl.pallas_call(
        body, name=name,
        in_specs=[ANY] * ng, out_specs=[ANY] * ng,
        out_shape=[jax.ShapeDtypeStruct((q.shape[0], 3) + q.shape[2:], q.dtype) for q in qs],
        scratch_shapes=[pltpu.SemaphoreType.DMA((ng, 3)), pltpu.SemaphoreType.DMA((ng, 3))],
    )(*qs)


def pair_sum(buf, recv, core, name):
    n, _, _, r, cdim = buf.shape

    def body(core_ref, a_ref, b_ref, o_ref):
        o_ref[...] = (a_ref[...].astype(F32) + b_ref[...].astype(F32)).astype(BF16)

    blk = pl.BlockSpec((None, None, r, cdim), lambda i, k, core_ref: (i, k, 0, 0))
    return pl.pallas_call(
        body, name=name,
        grid_spec=pltpu.PrefetchScalarGridSpec(
            num_scalar_prefetch=1, grid=(n, 4),
            in_specs=[pl.BlockSpec((None, None, None, r, cdim), lambda i, k, core_ref: (i, k, core_ref[0], 0, 0)), blk],
            out_specs=blk),
        out_shape=jax.ShapeDtypeStruct((n, 4, r, cdim), BF16),
    )(core, buf, recv)


def chip_sum(q, recv, chip, name):
    n, _, r, cdim = q.shape

    def body(chip_ref, a_ref, b_ref, o_ref):
        acc = a_ref[...].astype(F32)
        for j in range(3):
            acc = acc + b_ref[j].astype(F32)
        o_ref[...] = acc

    return pl.pallas_call(
        body, name=name,
        grid_spec=pltpu.PrefetchScalarGridSpec(
            num_scalar_prefetch=1, grid=(n,),
            in_specs=[pl.BlockSpec((None, None, r, cdim), lambda i, chip_ref: (i, chip_ref[0], 0, 0)),
                      pl.BlockSpec((None, 3, r, cdim), lambda i, chip_ref: (i, 0, 0, 0))],
            out_specs=pl.BlockSpec((None, r, cdim), lambda i, chip_ref: (i, 0, 0))),
        out_shape=jax.ShapeDtypeStruct((n, r, cdim), F32),
    )(chip, q, recv)


def sum_blocks(a, name):
    def body(a_ref, o_ref):
        acc = a_ref[0]
        for d in range(1, a.shape[0]):
            acc = acc + a_ref[d]
        o_ref[...] = acc

    return pl.pallas_call(body, name=name, out_shape=jax.ShapeDtypeStruct(a.shape[1:], F32),
                          compiler_params=pltpu.CompilerParams(vmem_limit_bytes=VMEM_LIMIT))(a)


def adamw(w, g, m, v, name):
    c1 = 1.0 / (1.0 - ADAM_B1 ** ADAM_STEP)
    c2 = 1.0 / (1.0 - ADAM_B2 ** ADAM_STEP)

    def body(w_ref, g_ref, m_ref, v_ref, d_ref, mo_ref, vo_ref):
        gg = g_ref[...]
        m2 = ADAM_B1 * m_ref[...] + (1.0 - ADAM_B1) * gg
        v2 = ADAM_B2 * v_ref[...] + (1.0 - ADAM_B2) * (gg * gg)
        mo_ref[...] = m2
        vo_ref[...] = v2
        d_ref[...] = -ADAM_LR * ((m2 * c1) / (jnp.sqrt(v2 * c2) + ADAM_EPS) + ADAM_WD * w_ref[...])

    out_shape = [jax.ShapeDtypeStruct(w.shape, F32)] * 3
    if w.ndim == 2:
        return pl.pallas_call(body, name=name, out_shape=out_shape,
                              compiler_params=pltpu.CompilerParams(vmem_limit_bytes=VMEM_LIMIT))(w, g, m, v)
    blk = pl.BlockSpec((None,) + w.shape[1:], lambda i: (i, 0, 0))
    return pl.pallas_call(body, name=name, grid=(w.shape[0],), in_specs=[blk] * 4, out_specs=[blk] * 3,
                          out_shape=out_shape, compiler_params=_cparams("arbitrary"))(w, g, m, v)


WEIGHTS = ("norm_ffn1", "ffn1_wg", "ffn1_wu", "ffn1_wd", "norm_mix", "norm_ffn2", "ffn2_wg", "ffn2_wu", "ffn2_wd",
           "rel_bias", "even_w_in", "attn_sinks", "conv_b_w", "conv_b_b", "conv_ln_g", "conv_ln_b", "even_w_out",
           "odd_w_in", "lru_conv_w", "lru_conv_b", "gate_a_w", "gate_a_b", "gate_x_w", "gate_x_b", "lru_lambda",
           "odd_w_out", "norm_final")
BIG = ("ffn1_wg", "ffn1_wu", "ffn1_wd", "ffn2_wg", "ffn2_wu", "ffn2_wd", "even_w_in", "even_w_out", "odd_w_in", "odd_w_out")
SMALL = tuple(n for n in WEIGHTS if n not in BIG)
SMALL_SHARDED = ("conv_b_w", "lru_conv_w", "lru_conv_b", "gate_a_b", "gate_x_b", "lru_lambda")
PACK_ALIGN = 1024


def _pack(arrays):
    parts = []
    for a in arrays:
        flat = a.reshape(-1)
        parts.append(jnp.pad(flat, (0, -flat.shape[0] % PACK_ALIGN)))
    return jnp.concatenate(parts).reshape(-1, 128)


def _unpack(packed, shapes, lead=()):
    out, row = [], 0
    for shp in shapes:
        size = math.prod(shp)
        nrows = (size + (-size % PACK_ALIGN)) // 128
        part = packed[..., row:row + nrows, :].reshape(lead + (nrows * 128,))
        out.append(part[..., :size].reshape(lead + tuple(shp)))
        row += nrows
    return out


def _unshard_last(blocks):
    nd = blocks.ndim
    moved = jnp.moveaxis(blocks, 0, nd - 2)
    return moved.reshape(moved.shape[:-2] + (-1,))


def _step(cfg, x, weights, loss_target, ms, vs):
    w = dict(zip(WEIGHTS, weights))
    m = dict(zip(WEIGHTS, ms))
    v = dict(zip(WEIGHTS, vs))
    px, py, pc = _place()
    dev = 4 * px + 2 * py + pc
    core = jnp.reshape(pc, (1,)).astype(jnp.int32)
    chip = jnp.reshape(2 * px + py, (1,)).astype(jnp.int32)
    d = cfg.d
    t = cfg.bl * cfg.s

    def rows(name):
        a = w[name]
        return (a if name.endswith(("wd", "w_out")) else a.transpose(0, 2, 1)).astype(BF16)

    r3 = {n: rows(n) for n in BIG[:6]}
    shards = {(l, k): jnp.stack([r3[f"ffn{k + 1}_{mat}"][l] for mat in ("wg", "wu", "wd")])
              for l in range(cfg.depth) for k in range(2)}
    small_src = _pack([w[n] for n in SMALL_SHARDED])[None]
    gathered = all_gather([shards.pop((0, 0)), shards.pop((0, 1)), rows("even_w_in"), rows("even_w_out"), rows("odd_w_in"),
                           rows("odd_w_out"), small_src], "all_gather_weights")
    full = [g.reshape(g.shape[0], -1, g.shape[-1]) for g in gathered[:6]]
    p = {n: w[n] for n in SMALL if n not in SMALL_SHARDED}
    p.update(wf={(0, 0): full[0], (0, 1): full[1]}, even_in=full[2], even_out=full[3], odd_in=full[4], odd_out=full[5])
    for n, blocks in zip(SMALL_SHARDED, _unpack(gathered[6][0], [w[n].shape for n in SMALL_SHARDED], lead=(N_DEV,))):
        p[n] = _unshard_last(blocks)

    lossp, gx, big, small = local_step(cfg, x.reshape(t, d), loss_target.reshape(t, d), p, shards)
    loss = lax.psum(lossp[0, 0], ("x", "y", "c"))

    dev1 = jnp.reshape(dev, (1,)).astype(jnp.int32)
    shard_rows = [{} for _ in range(cfg.depth)]
    for l in range(cfg.depth):
        for key, (buf, recv) in big[l].items():
            if recv is not None:
                shard_rows[l][key] = shard_sum(buf, recv, dev1, f"rs_sum_{key}_{l}")
    left = [(l, key, buf) for l in range(cfg.depth) for key, (buf, recv) in big[l].items() if recv is None]
    bufs = [buf.reshape(buf.shape[0], 4, 2, buf.shape[2], d) for _, _, buf in left]
    recv = pair_exchange(bufs, "rs_pair")
    qs = [pair_sum(b, r, core, f"rs_pair_sum{j}") for j, (b, r) in enumerate(zip(bufs, recv))]
    recv = chip_exchange(qs, "rs_chip")
    for j, ((l, key, _), q, r) in enumerate(zip(left, qs, recv)):
        shard_rows[l][key] = chip_sum(q, r, chip, f"rs_chip_sum{j}")

    g_rows = {}
    for k in range(2):
        ffn_g = jnp.stack([shard_rows[l][f"f{k + 1}"] for l in range(cfg.depth)])
        for j, mat in enumerate(("wg", "wu", "wd")):
            g_rows[f"ffn{k + 1}_{mat}"] = ffn_g[:, j]
    g_rows["even_w_in"] = jnp.stack([shard_rows[l]["in"][0] for l in range(0, cfg.depth, 2)])
    g_rows["even_w_out"] = jnp.stack([shard_rows[l]["out"][0] for l in range(0, cfg.depth, 2)])
    g_rows["odd_w_in"] = jnp.stack([shard_rows[l]["in"][0] for l in range(1, cfg.depth, 2)])
    g_rows["odd_w_out"] = jnp.stack([shard_rows[l]["out"][0] for l in range(1, cfg.depth, 2)])
    grads = {}

    full_shapes = [small[n].shape for n in SMALL]
    parts = all_gather([_pack([small[n] for n in SMALL])[None]], "all_gather_small_grads")[0][0]
    for n, g in zip(SMALL, _unpack(sum_blocks(parts, "sum_small_grads"), full_shapes)):
        if n in SMALL_SHARDED:
            width = w[n].shape[-1]
            g = lax.dynamic_slice_in_dim(g, dev * width, width, axis=g.ndim - 1)
        grads[n] = g

    delta, new_m, new_v = {}, {}, {}
    for n in BIG:
        if n.endswith(("wd", "w_out")):
            grads[n] = g_rows[n]
            delta[n], new_m[n], new_v[n] = adamw(w[n], grads[n], m[n], v[n], f"adamw_{n}")
        elif w[n].shape[-1] % 128 == 0:
            grads[n] = g_rows[n].transpose(0, 2, 1)
            delta[n], new_m[n], new_v[n] = adamw(w[n], grads[n], m[n], v[n], f"adamw_{n}")
        else:
            outs = adamw(w[n].transpose(0, 2, 1), g_rows[n], m[n].transpose(0, 2, 1), v[n].transpose(0, 2, 1), f"adamw_{n}")
            delta[n], new_m[n], new_v[n] = [o.transpose(0, 2, 1) for o in outs]
            grads[n] = g_rows[n].transpose(0, 2, 1)
    shapes = [w[n].shape for n in SMALL]
    packed = adamw(*[_pack([src[n] for n in SMALL]) for src in (w, grads, m, v)], "adamw_small")
    for out, pk in zip((delta, new_m, new_v), packed):
        out.update(zip(SMALL, _unpack(pk, shapes)))

    return (loss, gx.reshape(x.shape), *[grads[n] for n in WEIGHTS], *[delta[n] for n in WEIGHTS],
            *[new_m[n] for n in WEIGHTS], *[new_v[n] for n in WEIGHTS])


def kernel(x, norm_ffn1, ffn1_wg, ffn1_wu, ffn1_wd, norm_mix, norm_ffn2, ffn2_wg, ffn2_wu, ffn2_wd, rel_bias, even_w_in, attn_sinks, conv_b_w, conv_b_b, conv_ln_g, conv_ln_b, even_w_out, odd_w_in, lru_conv_w, lru_conv_b, gate_a_w, gate_a_b, gate_x_w, gate_x_b, lru_lambda, odd_w_out, norm_final, loss_target, m_norm_ffn1, m_ffn1_wg, m_ffn1_wu, m_ffn1_wd, m_norm_mix, m_norm_ffn2, m_ffn2_wg, m_ffn2_wu, m_ffn2_wd, m_rel_bias, m_even_w_in, m_attn_sinks, m_conv_b_w, m_conv_b_b, m_conv_ln_g, m_conv_ln_b, m_even_w_out, m_odd_w_in, m_lru_conv_w, m_lru_conv_b, m_gate_a_w, m_gate_a_b, m_gate_x_w, m_gate_x_b, m_lru_lambda, m_odd_w_out, m_norm_final, v_norm_ffn1, v_ffn1_wg, v_ffn1_wu, v_ffn1_wd, v_norm_mix, v_norm_ffn2, v_ffn2_wg, v_ffn2_wu, v_ffn2_wd, v_rel_bias, v_even_w_in, v_attn_sinks, v_conv_b_w, v_conv_b_b, v_conv_ln_g, v_conv_ln_b, v_even_w_out, v_odd_w_in, v_lru_conv_w, v_lru_conv_b, v_gate_a_w, v_gate_a_b, v_gate_x_w, v_gate_x_b, v_lru_lambda, v_odd_w_out, v_norm_final):
    weights = (norm_ffn1, ffn1_wg, ffn1_wu, ffn1_wd, norm_mix, norm_ffn2, ffn2_wg, ffn2_wu, ffn2_wd, rel_bias, even_w_in, attn_sinks, conv_b_w, conv_b_b, conv_ln_g, conv_ln_b, even_w_out, odd_w_in, lru_conv_w, lru_conv_b, gate_a_w, gate_a_b, gate_x_w, gate_x_b, lru_lambda, odd_w_out, norm_final)
    ms = (m_norm_ffn1, m_ffn1_wg, m_ffn1_wu, m_ffn1_wd, m_norm_mix, m_norm_ffn2, m_ffn2_wg, m_ffn2_wu, m_ffn2_wd, m_rel_bias, m_even_w_in, m_attn_sinks, m_conv_b_w, m_conv_b_b, m_conv_ln_g, m_conv_ln_b, m_even_w_out, m_odd_w_in, m_lru_conv_w, m_lru_conv_b, m_gate_a_w, m_gate_a_b, m_gate_x_w, m_gate_x_b, m_lru_lambda, m_odd_w_out, m_norm_final)
    vs = (v_norm_ffn1, v_ffn1_wg, v_ffn1_wu, v_ffn1_wd, v_norm_mix, v_norm_ffn2, v_ffn2_wg, v_ffn2_wu, v_ffn2_wd, v_rel_bias, v_even_w_in, v_attn_sinks, v_conv_b_w, v_conv_b_b, v_conv_ln_g, v_conv_ln_b, v_even_w_out, v_odd_w_in, v_lru_conv_w, v_lru_conv_b, v_gate_a_w, v_gate_a_b, v_gate_x_w, v_gate_x_b, v_lru_lambda, v_odd_w_out, v_norm_final)
    return _step(Cfg(), x, weights, loss_target, ms, vs)
```

```python
import math
from typing import NamedTuple

import jax
import jax.numpy as jnp
from jax import lax
from jax.experimental import pallas as pl
from jax.experimental.pallas import tpu as pltpu

F32 = jnp.float32
BF16 = jnp.bfloat16
RMS_EPS = 1e-6
LN_EPS = 1e-5
NEG_INF = -1e30
RG_LRU_C = 8.0
ADAM_LR = 0.001
ADAM_B1 = 0.9
ADAM_B2 = 0.999
ADAM_EPS = 1e-08
ADAM_WD = 0.01
ADAM_STEP = 10
N_DEV = 8
VMEM_LIMIT = 56 * 1024 * 1024


class Cfg(NamedTuple):
    d: int = 1024
    f: int = 2816
    s: int = 2048
    bl: int = 4
    hq: int = 8
    hkv: int = 2
    hd: int = 64
    win: int = 128
    cc: int = 512
    cw: int = 31
    lh: int = 8
    lb: int = 128
    lcw: int = 4
    nbuckets: int = 32
    max_dist: int = 128
    depth: int = 4
    tm: int = 1024
    tm_ffn: int = 1024
    tf: int = 256
    tk_ffn: int = 512
    tf_w: int = 1408
    tk: int = 1024
    ct_f: int = 256
    ct_b: int = 256

    @property
    def qw(self):
        return self.hq * self.hd

    @property
    def kvw(self):
        return self.hkv * self.hd

    @property
    def even_in(self):
        return self.qw + 2 * self.kvw + 2 * self.cc

    @property
    def even_cat(self):
        return self.qw + self.cc

    @property
    def lw(self):
        return self.lh * self.lb


def _cparams(*sem):
    return pltpu.CompilerParams(dimension_semantics=sem, vmem_limit_bytes=VMEM_LIMIT)


def _nt(a, b):
    return lax.dot_general(a, b, (((1,), (1,)), ((), ())), preferred_element_type=F32)


def _nn(a, b):
    return lax.dot_general(a, b, (((1,), (0,)), ((), ())), preferred_element_type=F32)


def _tn(a, b):
    return lax.dot_general(a, b, (((0,), (0,)), ((), ())), preferred_element_type=F32)


def _rstd(h):
    return lax.rsqrt(jnp.mean(h * h, axis=-1, keepdims=True) + RMS_EPS)


def _rms_bwd(h, nw, dxn):
    rstd = _rstd(h)
    dyg = dxn * nw
    dnw = jnp.sum(dxn * h * rstd, axis=0, keepdims=True)
    dx = rstd * (dyg - h * (rstd * rstd) * jnp.mean(dyg * h, axis=-1, keepdims=True))
    return dx, dnw


def _sigmoid(x):
    return 0.5 * jnp.tanh(0.5 * x) + 0.5


FFN_RING = 3


def _ffn_wspecs(tf, d, gu, md):
    return [pl.BlockSpec((2, tf, d), lambda i, j: (gu, j, 0)), pl.BlockSpec((None, tf, d), lambda i, j: (md, j, 0))]


class Riders(NamedTuple):
    inputs: tuple
    out_shape: tuple
    scratch: tuple
    start: object
    wait: object


def _ride(riders, grid, n_in, n_out, body):
    if riders is None:
        return body, [], [], [], [], []
    ni, no, ns = len(riders.inputs), len(riders.out_shape), len(riders.scratch)

    def full(*refs):
        ins, rin = refs[:n_in], refs[n_in:n_in + ni]
        outs = refs[n_in + ni:n_in + ni + n_out]
        rout = refs[n_in + ni + n_out:n_in + ni + n_out + no]
        rest = refs[n_in + ni + n_out + no:]
        scratch, sems = rest[:len(rest) - ns], rest[len(rest) - ns:]
        first = last = None
        for axis, size in enumerate(grid):
            pid = pl.program_id(axis)
            first = (pid == 0) if first is None else first & (pid == 0)
            last = (pid == size - 1) if last is None else last & (pid == size - 1)

        @pl.when(first)
        def _():
            riders.start(rin, rout, sems)

        body(*ins, *outs, *scratch)

        @pl.when(last)
        def _():
            riders.wait(rin, rout, sems)

    any_spec = pl.BlockSpec(memory_space=pl.ANY)
    return full, list(riders.inputs), [any_spec] * ni, [any_spec] * no, list(riders.out_shape), list(riders.scratch)


def ffn_fwd(cfg, h, nw, wts, gu, md, name, riders=None):
    t, d = h.shape
    f = wts.shape[1]
    tm, tf = cfg.tm_ffn, cfg.tf
    nj = f // tf

    ni = t // tm
    total = ni * nj

    def body(h_ref, nw_ref, w_hbm, ho_ref, xn_ref, g_ref, u_ref, acc_ref, wbuf, wsem):
        i, j = pl.program_id(0), pl.program_id(1)
        step = i * nj + j

        def tile_copies(s):
            slot = lax.rem(s, FFN_RING)
            rows = pl.ds(pl.multiple_of(lax.rem(s, nj) * tf, tf), tf)
            return (pltpu.make_async_copy(w_hbm.at[2 * gu:2 * gu + 2, rows, :], wbuf.at[slot, 0:2], wsem.at[slot, 0]),
                    pltpu.make_async_copy(w_hbm.at[md, rows, :], wbuf.at[slot, 2], wsem.at[slot, 1]))

        @pl.when(step == 0)
        def _():
            for s in range(FFN_RING - 1):
                for cp in tile_copies(jnp.int32(s)):
                    cp.start()

        @pl.when(step + FFN_RING - 1 < total)
        def _():
            for cp in tile_copies(step + FFN_RING - 1):
                cp.start()

        @pl.when(j == 0)
        def _():
            hh = h_ref[...]
            xn_ref[...] = (hh * _rstd(hh) * nw_ref[...]).astype(BF16)
            acc_ref[...] = jnp.zeros_like(acc_ref)

        for cp in tile_copies(step):
            cp.wait()
        slot = lax.rem(step, FFN_RING)
        gu_t = _nt(xn_ref[...], wbuf[slot, 0:2].reshape(2 * tf, d))
        g, u = gu_t[:, :tf], gu_t[:, tf:]
        g_ref[...] = g.astype(BF16)
        u_ref[...] = u.astype(BF16)
        acc_ref[...] += _nn((g * _sigmoid(g) * u).astype(BF16), wbuf[slot, 2])

        @pl.when(j == nj - 1)
        def _():
            ho_ref[...] = h_ref[...] + 0.5 * acc_ref[...]

    row = pl.BlockSpec((tm, d), lambda i, j: (i, 0))
    hid = pl.BlockSpec((tm, tf), lambda i, j: (i, j))
    grid = (ni, nj)
    full, r_args, r_in, r_out, r_shape, r_scratch = _ride(riders, grid, 3, 4, body)
    return pl.pallas_call(
        full, name=name, grid=grid,
        in_specs=[row, pl.BlockSpec((1, d), lambda i, j: (0, 0)), pl.BlockSpec(memory_space=pl.ANY)] + r_in,
        out_specs=[row, row, hid, hid] + r_out,
        out_shape=[jax.ShapeDtypeStruct((t, d), F32), jax.ShapeDtypeStruct((t, d), BF16),
                   jax.ShapeDtypeStruct((t, f), BF16), jax.ShapeDtypeStruct((t, f), BF16)] + r_shape,
        scratch_shapes=[pltpu.VMEM((tm, d), F32), pltpu.VMEM((FFN_RING, 3, tf, d), BF16),
                        pltpu.SemaphoreType.DMA((FFN_RING, 2))] + r_scratch,
        compiler_params=_cparams("arbitrary", "arbitrary"),
    )(h, nw, wts, *r_args)


def ffn_bwd_x(cfg, dh, h, nw, g, u, wts, gu, md, name, riders=None):
    t, d = h.shape
    f = wts.shape[1]
    tm, tf = cfg.tm_ffn, cfg.tf
    nj = f // tf

    def body(dh_ref, h_ref, nw_ref, g_ref, u_ref, wgu_ref, wd_ref,
             dho_ref, dout_ref, dg_ref, du_ref, dnw_ref, acc_ref, da_ref):
        i, j = pl.program_id(0), pl.program_id(1)

        @pl.when(j == 0)
        def _():
            dout_ref[...] = (0.5 * dh_ref[...]).astype(BF16)
            acc_ref[...] = jnp.zeros_like(acc_ref)
            da_ref[1] = jnp.zeros((tm, tf), F32)

        @pl.when((i == 0) & (j == 0))
        def _():
            dnw_ref[...] = jnp.zeros_like(dnw_ref)

        slot = lax.rem(j, 2)
        da = da_ref[1 - slot]
        da_ref[slot] = _nt(dout_ref[...], wd_ref[...])
        gg = g_ref[...].astype(F32)
        sig = _sigmoid(gg)
        dg = (da * u_ref[...].astype(F32) * (sig * (1.0 + gg * (1.0 - sig)))).astype(BF16)
        du = (da * (gg * sig)).astype(BF16)
        dg_ref[...] = dg
        du_ref[...] = du
        acc_ref[...] += _nn(jnp.concatenate([dg, du], axis=1), wgu_ref[...].reshape(2 * tf, d))

        @pl.when(j == nj)
        def _():
            dx, dnw = _rms_bwd(h_ref[...], nw_ref[...], acc_ref[...])
            dnw_ref[0:1, :] += dnw
            dho_ref[...] = dh_ref[...] + dx

    row = pl.BlockSpec((tm, d), lambda i, j: (i, 0))
    prev = pl.BlockSpec((tm, tf), lambda i, j: (i, jnp.maximum(j - 1, 0)))
    wspecs = [pl.BlockSpec((2, tf, d), lambda i, j: (gu, jnp.maximum(j - 1, 0), 0)),
              pl.BlockSpec((None, tf, d), lambda i, j: (md, jnp.minimum(j, nj - 1), 0))]
    grid = (t // tm, nj + 1)
    full, r_args, r_in, r_out, r_shape, r_scratch = _ride(riders, grid, 7, 5, body)
    return pl.pallas_call(
        full, name=name, grid=grid,
        in_specs=[row, row, pl.BlockSpec((1, d), lambda i, j: (0, 0)), prev, prev] + wspecs + r_in,
        out_specs=[row, row, prev, prev, pl.BlockSpec((8, d), lambda i, j: (0, 0))] + r_out,
        out_shape=[jax.ShapeDtypeStruct((t, d), F32), jax.ShapeDtypeStruct((t, d), BF16),
                   jax.ShapeDtypeStruct((t, f), BF16), jax.ShapeDtypeStruct((t, f), BF16),
                   jax.ShapeDtypeStruct((8, d), F32)] + r_shape,
        scratch_shapes=[pltpu.VMEM((tm, d), F32), pltpu.VMEM((2, tm, tf), F32)] + r_scratch,
        compiler_params=_cparams("arbitrary", "arbitrary"),
    )(dh, h, nw, g, u, wts, wts, *r_args)


def ffn_bwd_w(cfg, xn, dout, g, u, dg, du, name, riders=None):
    t, d = xn.shape
    f = g.shape[1]
    tk, tf = cfg.tk_ffn, cfg.tf_w
    nk = t // tk

    def body(xn_ref, dout_ref, g_ref, u_ref, dg_ref, du_ref, o_ref, acc_ref):
        k = pl.program_id(1)

        @pl.when(k == 0)
        def _():
            acc_ref[...] = jnp.zeros_like(acc_ref)

        gg = g_ref[...].astype(F32)
        a = (gg * _sigmoid(gg) * u_ref[...].astype(F32)).astype(BF16)
        xn_t = xn_ref[...]
        acc_ref[0] += _tn(dg_ref[...], xn_t)
        acc_ref[1] += _tn(du_ref[...], xn_t)
        acc_ref[2] += _tn(a, dout_ref[...])

        @pl.when(k == nk - 1)
        def _():
            o_ref[...] = acc_ref[...].astype(BF16)

    row = pl.BlockSpec((tk, d), lambda j, k: (k, 0))
    hid = pl.BlockSpec((tk, tf), lambda j, k: (k, j))
    grid = (f // tf, nk)
    full, r_args, r_in, r_out, r_shape, r_scratch = _ride(riders, grid, 6, 1, body)
    return pl.pallas_call(
        full, name=name, grid=grid,
        in_specs=[row, row, hid, hid, hid, hid] + r_in,
        out_specs=[pl.BlockSpec((3, tf, d), lambda j, k: (0, j, 0), pipeline_mode=pl.Buffered(1))] + r_out,
        out_shape=[jax.ShapeDtypeStruct((3, f, d), BF16)] + r_shape,
        scratch_shapes=[pltpu.VMEM((3, tf, d), F32)] + r_scratch,
        compiler_params=_cparams("arbitrary", "arbitrary"),
    )(xn, dout, g, u, dg, du, *r_args)


def _wspec(w, wi):
    if w.ndim == 2:
        return pl.BlockSpec(w.shape, lambda i: (0, 0))
    return pl.BlockSpec((None,) + w.shape[1:], lambda i: (wi, 0, 0))


def norm_proj(cfg, h, nw, w, name, wi=0):
    t, d = h.shape
    n = w.shape[-2]
    tm = cfg.tm

    def body(h_ref, nw_ref, w_ref, u_ref, xn_ref):
        hh = h_ref[...]
        xn = (hh * _rstd(hh) * nw_ref[...]).astype(BF16)
        xn_ref[...] = xn
        u_ref[...] = _nt(xn, w_ref[...]).astype(BF16)

    return pl.pallas_call(
        body, name=name, grid=(t // tm,),
        in_specs=[pl.BlockSpec((tm, d), lambda i: (i, 0)), pl.BlockSpec((1, d), lambda i: (0, 0)),
                  _wspec(w, wi)],
        out_specs=[pl.BlockSpec((tm, n), lambda i: (i, 0)), pl.BlockSpec((tm, d), lambda i: (i, 0))],
        out_shape=[jax.ShapeDtypeStruct((t, n), BF16), jax.ShapeDtypeStruct((t, d), BF16)],
        compiler_params=_cparams("arbitrary"),
    )(h, nw, w)


def proj_residual(cfg, h, parts, w, name, wi=0):
    t, d = h.shape
    tm = cfg.tm
    ks = [p.shape[1] for p in parts]
    offs = [sum(ks[:i]) for i in range(len(ks))]
    np_ = len(parts)

    def body(*refs):
        h_ref, w_ref, ho_ref = refs[0], refs[1 + np_], refs[2 + np_]
        acc = h_ref[...]
        for p_ref, off, k in zip(refs[1:1 + np_], offs, ks):
            acc = acc + _nn(p_ref[...], w_ref[off:off + k, :])
        ho_ref[...] = acc

    return pl.pallas_call(
        body, name=name, grid=(t // tm,),
        in_specs=[pl.BlockSpec((tm, d), lambda i: (i, 0))]
        + [pl.BlockSpec((tm, k), lambda i: (i, 0)) for k in ks]
        + [_wspec(w, wi)],
        out_specs=pl.BlockSpec((tm, d), lambda i: (i, 0)),
        out_shape=jax.ShapeDtypeStruct((t, d), F32),
        compiler_params=_cparams("arbitrary"),
    )(h, *parts, w)


def proj_bwd_act(cfg, dh, w, name, wi=0):
    t, d = dh.shape
    k = w.shape[-2]
    tm = cfg.tm

    def body(dh_ref, w_ref, o_ref):
        o_ref[...] = _nt(dh_ref[...].astype(BF16), w_ref[...]).astype(BF16)

    return pl.pallas_call(
        body, name=name, grid=(t // tm,),
        in_specs=[pl.BlockSpec((tm, d), lambda i: (i, 0)), _wspec(w, wi)],
        out_specs=pl.BlockSpec((tm, k), lambda i: (i, 0)),
        out_shape=jax.ShapeDtypeStruct((t, k), BF16),
        compiler_params=_cparams("arbitrary"),
    )(dh, w)


def grad_weight(cfg, parts, b, name):
    t, d = b.shape
    tk = cfg.tk
    nk = t // tk
    ks = [p.shape[1] for p in parts]
    offs = [sum(ks[:i]) for i in range(len(ks))]
    np_ = len(parts)

    def body(*refs):
        b_ref, o_ref, acc_ref = refs[np_:]
        kk = pl.program_id(0)

        @pl.when(kk == 0)
        def _():
            acc_ref[...] = jnp.zeros_like(acc_ref)

        bb = b_ref[...].astype(BF16)
        for a_ref, off, k in zip(refs[:np_], offs, ks):
            acc_ref[off:off + k, :] += _tn(a_ref[...], bb)

        @pl.when(kk == nk - 1)
        def _():
            o_ref[...] = acc_ref[...].astype(BF16)

    return pl.pallas_call(
        body, name=name, grid=(nk,),
        in_specs=[pl.BlockSpec((tk, k), lambda kk: (kk, 0)) for k in ks] + [pl.BlockSpec((tk, d), lambda kk: (kk, 0))],
        out_specs=pl.BlockSpec((sum(ks), d), lambda kk: (0, 0)),
        out_shape=jax.ShapeDtypeStruct((sum(ks), d), BF16),
        scratch_shapes=[pltpu.VMEM((sum(ks), d), F32)],
        compiler_params=_cparams("arbitrary"),
    )(*parts, b)


def norm_proj_bwd(cfg, dh, h, nw, parts, w, name, wi=0):
    t, d = h.shape
    tm = cfg.tm
    ks = [p.shape[1] for p in parts]
    offs = [sum(ks[:i]) for i in range(len(ks))]
    np_ = len(parts)

    def body(*refs):
        dh_ref, h_ref, nw_ref = refs[:3]
        w_ref, dho_ref, dnw_ref = refs[3 + np_:]

        @pl.when(pl.program_id(0) == 0)
        def _():
            dnw_ref[...] = jnp.zeros_like(dnw_ref)

        dxn = None
        for p_ref, off, k in zip(refs[3:3 + np_], offs, ks):
            term = _nn(p_ref[...], w_ref[off:off + k, :])
            dxn = term if dxn is None else dxn + term
        dx, dnw = _rms_bwd(h_ref[...], nw_ref[...], dxn)
        dnw_ref[0:1, :] += dnw
        dho_ref[...] = dh_ref[...] + dx

    row = pl.BlockSpec((tm, d), lambda i: (i, 0))
    return pl.pallas_call(
        body, name=name, grid=(t // tm,),
        in_specs=[row, row, pl.BlockSpec((1, d), lambda i: (0, 0))]
        + [pl.BlockSpec((tm, k), lambda i: (i, 0)) for k in ks]
        + [_wspec(w, wi)],
        out_specs=[row, pl.BlockSpec((8, d), lambda i: (0, 0))],
        out_shape=[jax.ShapeDtypeStruct((t, d), F32), jax.ShapeDtypeStruct((8, d), F32)],
        compiler_params=_cparams("arbitrary"),
    )(dh, h, nw, *parts, w)


def loss_head(cfg, h, nf, tgt, name):
    t, d = h.shape
    tm = cfg.tm

    def body(h_ref, nf_ref, tgt_ref, loss_ref, dh_ref, dnf_ref):
        @pl.when(pl.program_id(0) == 0)
        def _():
            loss_ref[...] = jnp.zeros_like(loss_ref)
            dnf_ref[...] = jnp.zeros_like(dnf_ref)

        hh = h_ref[...]
        err = hh * _rstd(hh) * nf_ref[...] - tgt_ref[...]
        row = jnp.sum(err * err, axis=-1, keepdims=True) * (0.5 / d)
        loss_ref[...] += jnp.sum(row, axis=0, keepdims=True)
        dx, dnf = _rms_bwd(hh, nf_ref[...], err * (1.0 / d))
        dnf_ref[0:1, :] += dnf
        dh_ref[...] = dx

    row = pl.BlockSpec((tm, d), lambda i: (i, 0))
    return pl.pallas_call(
        body, name=name, grid=(t // tm,),
        in_specs=[row, pl.BlockSpec((1, d), lambda i: (0, 0)), row],
        out_specs=[pl.BlockSpec((8, 128), lambda i: (0, 0)), row, pl.BlockSpec((8, d), lambda i: (0, 0))],
        out_shape=[jax.ShapeDtypeStruct((8, 128), F32), jax.ShapeDtypeStruct((t, d), F32),
                   jax.ShapeDtypeStruct((8, d), F32)],
        compiler_params=_cparams("arbitrary"),
    )(h, nf, tgt)


def bucket_table(cfg):
    qi = jnp.arange(cfg.win)[:, None]
    sj = jnp.arange(2 * cfg.win)[None, :]
    dist = qi + cfg.win - sj
    n = jnp.maximum(dist, 0)
    max_exact = cfg.nbuckets // 2
    nf = jnp.maximum(n, max_exact).astype(F32)
    large = max_exact + (jnp.log(nf / max_exact) / math.log(cfg.max_dist / max_exact)
                         * (cfg.nbuckets - max_exact)).astype(jnp.int32)
    large = jnp.minimum(large, cfg.nbuckets - 1)
    bucket = jnp.where(n < max_exact, n, large)
    return jnp.where((dist >= 0) & (dist < cfg.win), bucket, -1).astype(jnp.int32)


def bias_build(cfg, rel_bias, buckets, name):
    w = cfg.win

    def body(rb_ref, bk_ref, o_ref):
        bk = bk_ref[...]
        for h in range(cfg.hq):
            acc = jnp.full((w, 2 * w), NEG_INF, F32)
            for b in range(cfg.nbuckets):
                acc = jnp.where(bk == b, rb_ref[b, h], acc)
            o_ref[h] = acc

    return pl.pallas_call(
        body, name=name,
        in_specs=[pl.BlockSpec(memory_space=pltpu.SMEM), pl.BlockSpec(memory_space=pltpu.VMEM)],
        out_specs=pl.BlockSpec(memory_space=pltpu.VMEM),
        out_shape=jax.ShapeDtypeStruct((cfg.hq, w, 2 * w), F32),
    )(rel_bias, buckets)


def bias_grad(cfg, dbias, buckets, name):
    w = cfg.win

    def body(db_ref, bk_ref, o_ref, rows_ref):
        bk = bk_ref[...]
        for h in range(cfg.hq):
            d = db_ref[0, h]
            for e in range(1, dbias.shape[0]):
                d = d + db_ref[e, h]
            for b in range(cfg.nbuckets):
                rows_ref[b:b + 1, :] = jnp.sum(jnp.where(bk == b, d, 0.0), axis=0, keepdims=True)
            o_ref[h] = jnp.broadcast_to(jnp.sum(rows_ref[...], axis=1, keepdims=True), (cfg.nbuckets, 128))

    return pl.pallas_call(
        body, name=name,
        in_specs=[pl.BlockSpec(memory_space=pltpu.VMEM), pl.BlockSpec(memory_space=pltpu.VMEM)],
        out_specs=pl.BlockSpec(memory_space=pltpu.VMEM),
        out_shape=jax.ShapeDtypeStruct((cfg.hq, cfg.nbuckets, 128), F32),
        scratch_shapes=[pltpu.VMEM((cfg.nbuckets, 2 * w), F32)],
    )(dbias, buckets)


def _attn_probs(cfg, qk, bias_h, sink, first_ok):
    s = qk * (1.0 / math.sqrt(cfg.hd)) + bias_h
    s = jnp.where(first_ok, s, NEG_INF)
    m = jnp.maximum(jnp.max(s, axis=-1, keepdims=True), sink)
    e = jnp.exp(s - m)
    es = jnp.exp(sink - m)
    inv = 1.0 / (jnp.sum(e, axis=-1, keepdims=True) + es)
    return e * inv, es * inv


def _attn_block_inputs(cfg, n, q_ref, kv_ref):
    w = cfg.win
    r0 = pl.multiple_of(n * w, w)
    rp = pl.multiple_of(jnp.maximum(n - 1, 0) * w, w)
    qb = q_ref[pl.ds(r0, w), :]
    kk = jnp.concatenate([kv_ref[pl.ds(rp, w), :], kv_ref[pl.ds(r0, w), :]], axis=0)
    col = lax.broadcasted_iota(jnp.int32, (w, 2 * w), 1)
    first_ok = (n > 0) | (col >= w)
    return r0, rp, qb, kk, first_ok


def _kv_col_block(cfg):
    assert cfg.qw % (2 * cfg.kvw) == 0
    return cfg.qw // (2 * cfg.kvw)


def attn_fwd(cfg, u, bias, sinks, name):
    t = u.shape[0]
    s, w, hd, g = cfg.s, cfg.win, cfg.hd, cfg.hq // cfg.hkv
    kvb = _kv_col_block(cfg)

    def body(q_ref, kv_ref, bias_ref, sink_ref, o_ref):
        def blk(n, carry):
            r0, _, qb, kk, first_ok = _attn_block_inputs(cfg, n, q_ref, kv_ref)
            heads = range(cfg.hq)
            scores = [_nt(qb[:, hd * h:hd * (h + 1)], kk[:, hd * (h // g):hd * (h // g + 1)]) for h in heads]
            probs = [_attn_probs(cfg, scores[h], bias_ref[h], sink_ref[h], first_ok)[0].astype(BF16) for h in heads]
            outs = [_nn(probs[h], kk[:, cfg.kvw + hd * (h // g):cfg.kvw + hd * (h // g + 1)]) for h in heads]
            o_ref[pl.ds(r0, w), :] = jnp.concatenate(outs, axis=1).astype(BF16)
            return carry

        lax.fori_loop(0, s // w, blk, 0)

    return pl.pallas_call(
        body, name=name, grid=(t // s,),
        in_specs=[pl.BlockSpec((s, cfg.qw), lambda b: (b, 0)), pl.BlockSpec((s, 2 * cfg.kvw), lambda b: (b, kvb)),
                  pl.BlockSpec(bias.shape, lambda b: (0, 0, 0)), pl.BlockSpec(memory_space=pltpu.SMEM)],
        out_specs=pl.BlockSpec((s, cfg.qw), lambda b: (b, 0)),
        out_shape=jax.ShapeDtypeStruct((t, cfg.qw), BF16),
        compiler_params=_cparams("arbitrary"),
    )(u, u, bias, sinks)


def attn_bwd(cfg, u, dcat, bias, sinks, name):
    t = u.shape[0]
    s, w, hd, g = cfg.s, cfg.win, cfg.hd, cfg.hq // cfg.hkv
    kvb = _kv_col_block(cfg)
    scale = 1.0 / math.sqrt(hd)
    assert cfg.hq <= 8

    def body(q_ref, kv_ref, do_ref, bias_ref, sink_ref, du_ref, dbias_ref, dsink_ref, dkv_ref):
        @pl.when(pl.program_id(0) == 0)
        def _():
            dbias_ref[...] = jnp.zeros_like(dbias_ref)
            dsink_ref[...] = jnp.zeros_like(dsink_ref)

        dkv_ref[...] = jnp.zeros_like(dkv_ref)

        def blk(n, carry):
            r0, rp, qb, kk, first_ok = _attn_block_inputs(cfg, n, q_ref, kv_ref)
            dob = do_ref[pl.ds(r0, w), :]
            heads = range(cfg.hq)
            kjs = [kk[:, hd * j:hd * (j + 1)] for j in range(cfg.hkv)]
            vjs = [kk[:, cfg.kvw + hd * j:cfg.kvw + hd * (j + 1)] for j in range(cfg.hkv)]
            qhs = [qb[:, hd * h:hd * (h + 1)] for h in heads]
            dohs = [dob[:, hd * h:hd * (h + 1)] for h in heads]
            scores = [_nt(qhs[h], kjs[h // g]) for h in heads]
            dps = [_nt(dohs[h], vjs[h // g]) for h in heads]
            pbs, dsbs = [], []
            for h in heads:
                p, ps = _attn_probs(cfg, scores[h], bias_ref[h], sink_ref[h], first_ok)
                delta = jnp.sum(p * dps[h], axis=-1, keepdims=True)
                ds = p * (dps[h] - delta)
                dsink_ref[h:h + 1, :] += jnp.broadcast_to(-jnp.sum(ps * delta, axis=0, keepdims=True), (1, 128))
                dbias_ref[h] += ds
                pbs.append(p.astype(BF16))
                dsbs.append(ds.astype(BF16))
            dqs = []
            dks = [jnp.zeros((2 * w, hd), F32) for _ in range(cfg.hkv)]
            dvs = [jnp.zeros((2 * w, hd), F32) for _ in range(cfg.hkv)]
            for h in heads:
                dqs.append(_nn(dsbs[h], kjs[h // g]) * scale)
                dks[h // g] = dks[h // g] + _tn(dsbs[h], qhs[h]) * scale
                dvs[h // g] = dvs[h // g] + _tn(pbs[h], dohs[h])
            du_ref[pl.ds(r0, w), 0:cfg.qw] = jnp.concatenate(dqs, axis=1).astype(BF16)
            dkv = jnp.concatenate(dks + dvs, axis=1)
            dkv_ref[pl.ds(rp, w), :] += dkv[:w]
            dkv_ref[pl.ds(r0, w), :] += dkv[w:]
            return carry

        lax.fori_loop(0, s // w, blk, 0)
        du_ref[:, cfg.qw:] = dkv_ref[...].astype(BF16)

    wa = cfg.qw + 2 * cfg.kvw
    return pl.pallas_call(
        body, name=name, grid=(t // s,),
        in_specs=[pl.BlockSpec((s, cfg.qw), lambda b: (b, 0)), pl.BlockSpec((s, 2 * cfg.kvw), lambda b: (b, kvb)),
                  pl.BlockSpec((s, cfg.qw), lambda b: (b, 0)),
                  pl.BlockSpec(bias.shape, lambda b: (0, 0, 0)), pl.BlockSpec(memory_space=pltpu.SMEM)],
        out_specs=[pl.BlockSpec((s, wa), lambda b: (b, 0)), pl.BlockSpec(bias.shape, lambda b: (0, 0, 0)),
                   pl.BlockSpec((8, 128), lambda b: (0, 0))],
        out_shape=[jax.ShapeDtypeStruct((t, wa), BF16), jax.ShapeDtypeStruct(bias.shape, F32),
                   jax.ShapeDtypeStruct((8, 128), F32)],
        scratch_shapes=[pltpu.VMEM((s, 2 * cfg.kvw), F32)],
        compiler_params=_cparams("arbitrary"),
    )(u, u, dcat, bias, sinks)


def _shift_views(win, rc, pad):
    return [win] + [win[j:j + rc + pad - 8] for j in range(1, 8)]


def _tap(views, off, rc):
    a = 8 * (off // 8)
    return views[off % 8][a:a + rc]


def _conv_rows(views, w_ref, cw, pad, rc, lanes=slice(None)):
    acc = None
    for k in range(cw):
        term = _tap(views, pad - (cw - 1) + k, rc) * w_ref[k:k + 1, lanes]
        acc = term if acc is None else acc + term
    return acc


def _conv_rows_t(views, w_ref, cw, rc, lanes=slice(None)):
    acc = None
    for k in range(cw):
        term = _tap(views, cw - 1 - k, rc) * w_ref[k:k + 1, lanes]
        acc = term if acc is None else acc + term
    return acc


def _group_sum(x):
    acc = x[0:8]
    for i in range(1, x.shape[0] // 8):
        acc = acc + x[8 * i:8 * i + 8]
    return acc


def _conv_wgrad(views, dy, acc_ref, cw, pad, rc, lanes=slice(None)):
    for k in range(cw):
        acc_ref[k, :, lanes] += _group_sum(dy * _tap(views, pad - (cw - 1) + k, rc))


LANE_TILE = 128


def _lane_tiles(width):
    return [slice(c0, c0 + LANE_TILE) for c0 in range(0, width, LANE_TILE)]


def _conv_rows_tiled(x_ref, r0, w_ref, cw, pad, rc):
    return jnp.concatenate([_conv_rows(_shift_views(x_ref[pl.ds(r0, rc + pad), lanes], rc, pad), w_ref, cw, pad, rc, lanes)
                            for lanes in _lane_tiles(x_ref.shape[1])], axis=1)


def _conv_rows_t_tiled(x_ref, r0, w_ref, cw, pad, rc):
    return jnp.concatenate([_conv_rows_t(_shift_views(x_ref[pl.ds(r0, rc + pad), lanes], rc, pad), w_ref, cw, rc, lanes)
                            for lanes in _lane_tiles(x_ref.shape[1])], axis=1)


def _conv_wgrad_tiled(x_ref, r0, dy, acc_ref, cw, pad, rc):
    for lanes in _lane_tiles(x_ref.shape[1]):
        _conv_wgrad(_shift_views(x_ref[pl.ds(r0, rc + pad), lanes], rc, pad), dy[:, lanes], acc_ref, cw, pad, rc, lanes)


CONV_RC = 64
CONV_PAD = 32
GLU_RC = 256


def _conv_col_blocks(cfg):
    off = cfg.qw + 2 * cfg.kvw
    bw = math.gcd(off, cfg.cc)
    assert bw % 128 == 0
    n = cfg.cc // bw
    return bw, [off // bw + i for i in range(n)], [(off + cfg.cc) // bw + i for i in range(n)]


def _glu_inputs(a_refs, b_refs, rows):
    ga = jnp.concatenate([r[rows, :] for r in a_refs], axis=1).astype(F32)
    gb = jnp.concatenate([r[rows, :] for r in b_refs], axis=1).astype(F32)
    return ga, gb


def _fill_glu(cfg, a_refs, b_refs, xp_ref):
    xp_ref[0:CONV_PAD, :] = jnp.zeros((CONV_PAD, cfg.cc), F32)

    def fill(i, carry):
        r0 = pl.multiple_of(i * GLU_RC, GLU_RC)
        ga, gb = _glu_inputs(a_refs, b_refs, pl.ds(r0, GLU_RC))
        xp_ref[pl.ds(CONV_PAD + r0, GLU_RC), :] = ga * _sigmoid(gb)
        return carry

    lax.fori_loop(0, cfg.s // GLU_RC, fill, 0)


def _layernorm_stats(cv):
    mu = jnp.mean(cv, axis=-1, keepdims=True)
    xc = cv - mu
    rstd = lax.rsqrt(jnp.mean(xc * xc, axis=-1, keepdims=True) + LN_EPS)
    return xc * rstd, rstd


def conv_fwd(cfg, u, cw_w, cb, lg, lb, name):
    t = u.shape[0]
    s, cc, cw = cfg.s, cfg.cc, cfg.cw
    bw, a_idx, b_idx = _conv_col_blocks(cfg)
    nb = len(a_idx)

    def body(*refs):
        a_refs, b_refs = refs[:nb], refs[nb:2 * nb]
        w_ref, cb_ref, lg_ref, lb_ref, o_ref, cv_ref, xp_ref = refs[2 * nb:]
        _fill_glu(cfg, a_refs, b_refs, xp_ref)

        def chunk(i, carry):
            r0 = pl.multiple_of(i * CONV_RC, CONV_RC)
            cv = _conv_rows_tiled(xp_ref, r0, w_ref, cw, CONV_PAD, CONV_RC) + cb_ref[...]
            cv_ref[pl.ds(r0, CONV_RC), :] = cv
            xhat, _ = _layernorm_stats(cv)
            ln = xhat * lg_ref[...] + lb_ref[...]
            o_ref[pl.ds(r0, CONV_RC), :] = (ln * _sigmoid(ln)).astype(BF16)
            return carry

        lax.fori_loop(0, s // CONV_RC, chunk, 0, unroll=2)

    def colspec(j):
        return pl.BlockSpec((s, bw), lambda b: (b, j))

    vec = pl.BlockSpec((1, cc), lambda b: (0, 0))
    return pl.pallas_call(
        body, name=name, grid=(t // s,),
        in_specs=[colspec(j) for j in a_idx + b_idx] + [pl.BlockSpec((cw, cc), lambda b: (0, 0)), vec, vec, vec],
        out_specs=[pl.BlockSpec((s, cc), lambda b: (b, 0))] * 2,
        out_shape=[jax.ShapeDtypeStruct((t, cc), BF16), jax.ShapeDtypeStruct((t, cc), F32)],
        scratch_shapes=[pltpu.VMEM((CONV_PAD + s, cc), F32)],
        compiler_params=_cparams("arbitrary"),
    )(*([u] * (2 * nb)), cw_w, cb, lg, lb)


def conv_bwd(cfg, u, cv_saved, dcat, cw_w, cb, lg, lb, name):
    t = u.shape[0]
    s, cc, cw = cfg.s, cfg.cc, cfg.cw
    bw, a_idx, b_idx = _conv_col_blocks(cfg)
    nb = len(a_idx)
    assert cfg.qw % cc == 0 and cw <= 32

    def body(*refs):
        a_refs, b_refs = refs[:nb], refs[nb:2 * nb]
        cv_ref, dc_ref, w_ref, cb_ref, lg_ref, lb_ref, du_ref, dw_ref, dvec_ref, xp_ref, dcv_ref, dwacc_ref = refs[2 * nb:]

        @pl.when(pl.program_id(0) == 0)
        def _():
            dw_ref[...] = jnp.zeros_like(dw_ref)
            dvec_ref[...] = jnp.zeros_like(dvec_ref)

        _fill_glu(cfg, a_refs, b_refs, xp_ref)
        dcv_ref[s:s + CONV_PAD, :] = jnp.zeros((CONV_PAD, cc), F32)
        dwacc_ref[...] = jnp.zeros_like(dwacc_ref)

        def chunk(i, carry):
            r0 = pl.multiple_of(i * CONV_RC, CONV_RC)
            xhat, rstd = _layernorm_stats(cv_ref[pl.ds(r0, CONV_RC), :])
            ln = xhat * lg_ref[...] + lb_ref[...]
            sg = _sigmoid(ln)
            dln = dc_ref[pl.ds(r0, CONV_RC), :].astype(F32) * (sg * (1.0 + ln * (1.0 - sg)))
            dxh = dln * lg_ref[...]
            dcv = rstd * (dxh - jnp.mean(dxh, axis=-1, keepdims=True)
                          - xhat * jnp.mean(dxh * xhat, axis=-1, keepdims=True))
            dcv_ref[pl.ds(r0, CONV_RC), :] = dcv
            dvec_ref[0:1, :] += jnp.sum(dcv, axis=0, keepdims=True)
            dvec_ref[1:2, :] += jnp.sum(dln * xhat, axis=0, keepdims=True)
            dvec_ref[2:3, :] += jnp.sum(dln, axis=0, keepdims=True)
            _conv_wgrad_tiled(xp_ref, r0, dcv, dwacc_ref, cw, CONV_PAD, CONV_RC)
            return carry

        lax.fori_loop(0, s // CONV_RC, chunk, 0, unroll=2)
        for k in range(cw):
            dw_ref[k:k + 1, :] += jnp.sum(dwacc_ref[k], axis=0, keepdims=True)

        def chunk2(i, carry):
            r0 = pl.multiple_of(i * CONV_RC, CONV_RC)
            dglu = _conv_rows_t_tiled(dcv_ref, r0, w_ref, cw, CONV_PAD, CONV_RC)
            ga, gb = _glu_inputs(a_refs, b_refs, pl.ds(r0, CONV_RC))
            sgb = _sigmoid(gb)
            du_ref[pl.ds(r0, CONV_RC), 0:cc] = (dglu * sgb).astype(BF16)
            du_ref[pl.ds(r0, CONV_RC), cc:2 * cc] = (dglu * ga * sgb * (1.0 - sgb)).astype(BF16)
            return carry

        lax.fori_loop(0, s // CONV_RC, chunk2, 0)

    def colspec(j):
        return pl.BlockSpec((s, bw), lambda b: (b, j))

    vec = pl.BlockSpec((1, cc), lambda b: (0, 0))
    return pl.pallas_call(
        body, name=name, grid=(t // s,),
        in_specs=[colspec(j) for j in a_idx + b_idx]
        + [pl.BlockSpec((s, cc), lambda b: (b, 0)), pl.BlockSpec((s, cc), lambda b: (b, cfg.qw // cc)),
           pl.BlockSpec((cw, cc), lambda b: (0, 0)), vec, vec, vec],
        out_specs=[pl.BlockSpec((s, 2 * cc), lambda b: (b, 0)), pl.BlockSpec((32, cc), lambda b: (0, 0)),
                   pl.BlockSpec((8, cc), lambda b: (0, 0))],
        out_shape=[jax.ShapeDtypeStruct((t, 2 * cc), BF16), jax.ShapeDtypeStruct((32, cc), F32),
                   jax.ShapeDtypeStruct((8, cc), F32)],
        scratch_shapes=[pltpu.VMEM((CONV_PAD + s, cc), F32), pltpu.VMEM((s + CONV_PAD, cc), F32),
                        pltpu.VMEM((cw, 8, cc), F32)],
        compiler_params=_cparams("arbitrary"),
    )(*([u] * (2 * nb)), cv_saved, dcat, cw_w, cb, lg, lb)


LRU_RC = 64
LRU_PAD = 8
SCAN_RC = 16
SCAN_UNROLL = 4
GELU_K = math.sqrt(2.0 / math.pi)


def _expm1_neg(z):
    return jnp.where(z > -0.05, z * (1.0 + z * (0.5 + z * (1.0 / 6.0 + z * (1.0 / 24.0)))), jnp.exp(z) - 1.0)


def _log_sigmoid(x):
    e = jnp.exp(-jnp.abs(x))
    log1p = jnp.where(e < 0.01, e * (1.0 - e * (0.5 - e * (1.0 / 3.0))), jnp.log(1.0 + e))
    return jnp.minimum(x, 0.0) - log1p


def _gelu(x):
    t = jnp.tanh(GELU_K * (x + 0.044715 * x * x * x))
    return 0.5 * x * (1.0 + t), t


def _gelu_grad(x, t):
    return 0.5 * (1.0 + t) + 0.5 * x * (1.0 - t * t) * GELU_K * (1.0 + 3.0 * 0.044715 * x * x)


def _lru_gates(xc, wa_ref, ba, wx_ref, bx, ls):
    nh = xc.shape[1] // 128
    xb = xc.astype(BF16)
    ra = jnp.concatenate([_nn(xb[:, 128 * h:128 * (h + 1)], wa_ref[h]) for h in range(nh)], axis=1) + ba
    ia = jnp.concatenate([_nn(xb[:, 128 * h:128 * (h + 1)], wx_ref[h]) for h in range(nh)], axis=1) + bx
    r = _sigmoid(ra)
    ig = _sigmoid(ia)
    log_a = RG_LRU_C * r * ls
    return r, ig, log_a


def _lru_decay(log_a):
    return jnp.exp(log_a), jnp.sqrt(-_expm1_neg(2.0 * log_a))


def _fill_padded(src_ref, dst_ref, s, ct):
    dst_ref[0:LRU_PAD, :] = jnp.zeros((LRU_PAD, ct), F32)

    def fill(i, carry):
        r0 = pl.multiple_of(i * GLU_RC, GLU_RC)
        dst_ref[pl.ds(LRU_PAD + r0, GLU_RC), :] = src_ref[pl.ds(r0, GLU_RC), :].astype(F32)
        return carry

    lax.fori_loop(0, s // GLU_RC, fill, 0)


def _lru_specs(cfg, ct):
    s, lw = cfg.s, cfg.lw
    nct = lw // ct
    nh = ct // 128
    act = [pl.BlockSpec((s, ct), lambda c, b: (b, c)), pl.BlockSpec((s, ct), lambda c, b: (b, nct + c))]
    vec = pl.BlockSpec((1, ct), lambda c, b: (0, c))
    gate_w = pl.BlockSpec((nh, 128, 128), lambda c, b: (c, 0, 0))
    params = [pl.BlockSpec((cfg.lcw, ct), lambda c, b: (0, c)), vec, gate_w, vec, gate_w, vec, vec]
    return nct, act, params


def lru_fwd(cfg, u, conv_w, conv_b, wa, ba, wx, bx, lam, name):
    t = u.shape[0]
    s, lw, lcw, ct = cfg.s, cfg.lw, cfg.lcw, cfg.ct_f
    nct, act, params = _lru_specs(cfg, ct)

    def body(gi_ref, ri_ref, cw_ref, cb_ref, wa_ref, ba_ref, wx_ref, bx_ref, lam_ref,
             y_ref, hs_ref, r_ref, ig_ref, xc_ref, la_ref, xp_ref, a_ref, b_ref):
        _fill_padded(ri_ref, xp_ref, s, ct)
        ls = _log_sigmoid(lam_ref[...])

        def chunk(i, carry):
            r0 = pl.multiple_of(i * LRU_RC, LRU_RC)
            rows = pl.ds(r0, LRU_RC)
            views = _shift_views(xp_ref[pl.ds(r0, LRU_RC + LRU_PAD), :], LRU_RC, LRU_PAD)
            xc = _conv_rows(views, cw_ref, lcw, LRU_PAD, LRU_RC) + cb_ref[...]
            r, ig, log_a = _lru_gates(xc, wa_ref, ba_ref[...], wx_ref, bx_ref[...], ls)
            a, mult = _lru_decay(log_a)
            a_ref[rows, :] = a
            b_ref[rows, :] = mult * (ig * xc)
            r_ref[rows, :] = r.astype(BF16)
            ig_ref[rows, :] = ig.astype(BF16)
            xc_ref[rows, :] = xc.astype(BF16)
            la_ref[rows, :] = log_a
            return carry

        lax.fori_loop(0, s // LRU_RC, chunk, 0, unroll=2)
        row = lax.broadcasted_iota(jnp.int32, (SCAN_RC, ct), 0)

        def scan(i, h_last):
            for sub in range(SCAN_UNROLL):
                rows = pl.ds(pl.multiple_of((i * SCAN_UNROLL + sub) * SCAN_RC, SCAN_RC), SCAN_RC)
                a = a_ref[rows, :]
                b = b_ref[rows, :]
                sft = 1
                while sft < SCAN_RC:
                    a_sh = jnp.where(row >= sft, pltpu.roll(a, sft, 0), 1.0)
                    b_sh = jnp.where(row >= sft, pltpu.roll(b, sft, 0), 0.0)
                    b = a * b_sh + b
                    a = a * a_sh
                    sft *= 2
                h = a * h_last + b
                gate, _ = _gelu(gi_ref[rows, :].astype(F32))
                y_ref[rows, :] = (gate * h).astype(BF16)
                hs_ref[rows, :] = h.astype(BF16)
                h_last = h[SCAN_RC - 1:SCAN_RC, :]
            return h_last

        lax.fori_loop(0, s // (SCAN_RC * SCAN_UNROLL), scan, jnp.zeros((1, ct), F32))

    out = pl.BlockSpec((s, ct), lambda c, b: (b, c))
    return pl.pallas_call(
        body, name=name, grid=(nct, t // s),
        in_specs=act + params,
        out_specs=[out] * 6,
        out_shape=[jax.ShapeDtypeStruct((t, lw), BF16)] * 5 + [jax.ShapeDtypeStruct((t, lw), F32)],
        scratch_shapes=[pltpu.VMEM((LRU_PAD + s, ct), F32), pltpu.VMEM((s, ct), F32), pltpu.VMEM((s, ct), F32)],
        compiler_params=_cparams("arbitrary", "arbitrary"),
    )(u, u, conv_w, conv_b, wa, ba, wx, bx, lam)


def lru_bwd(cfg, u, saved, dy, conv_w, conv_b, wa, ba, wx, bx, lam, name):
    t = u.shape[0]
    s, lw, lcw, ct = cfg.s, cfg.lw, cfg.lcw, cfg.ct_b
    nct, act, params = _lru_specs(cfg, ct)
    nh = ct // 128
    nscan = s // SCAN_RC
    assert lcw <= 8

    def body(gi_ref, ri_ref, hs_ref, r_ref, ig_ref, xc_ref, la_ref, dy_ref,
             cw_ref, cb_ref, wa_ref, ba_ref, wx_ref, bx_ref, lam_ref,
             dug_ref, dur_ref, dwa_ref, dwx_ref, dvec_ref, dcw_ref,
             xp_ref, hp_ref, a_ref, g_ref, dxc_ref, dwacc_ref):
        @pl.when(pl.program_id(1) == 0)
        def _():
            dwa_ref[...] = jnp.zeros_like(dwa_ref)
            dwx_ref[...] = jnp.zeros_like(dwx_ref)
            dvec_ref[...] = jnp.zeros_like(dvec_ref)
            dcw_ref[...] = jnp.zeros_like(dcw_ref)

        _fill_padded(ri_ref, xp_ref, s, ct)
        _fill_padded(hs_ref, hp_ref, s, ct)
        dxc_ref[s:s + LRU_PAD, :] = jnp.zeros((LRU_PAD, ct), F32)
        dwacc_ref[...] = jnp.zeros_like(dwacc_ref)
        lam = lam_ref[...]
        ls = _log_sigmoid(lam)

        def chunk(i, carry):
            r0 = pl.multiple_of(i * LRU_RC, LRU_RC)
            rows = pl.ds(r0, LRU_RC)
            a_ref[rows, :] = jnp.exp(la_ref[rows, :])
            x = gi_ref[rows, :].astype(F32)
            gate, th = _gelu(x)
            dyv = dy_ref[rows, :].astype(F32)
            g_ref[rows, :] = dyv * gate
            dug_ref[rows, :] = (dyv * hp_ref[pl.ds(LRU_PAD + r0, LRU_RC), :] * _gelu_grad(x, th)).astype(BF16)
            return carry

        lax.fori_loop(0, s // LRU_RC, chunk, 0, unroll=2)
        row = lax.broadcasted_iota(jnp.int32, (SCAN_RC, ct), 0)

        def scan(ii, carry):
            g_next, a_next = carry
            for sub in range(SCAN_UNROLL):
                step = nscan - 1 - (ii * SCAN_UNROLL + sub)
                rows = pl.ds(pl.multiple_of(step * SCAN_RC, SCAN_RC), SCAN_RC)
                a = a_ref[rows, :]
                d = g_ref[rows, :]
                c = jnp.where(row < SCAN_RC - 1, pltpu.roll(a, SCAN_RC - 1, 0), a_next)
                sft = 1
                while sft < SCAN_RC:
                    c_sh = jnp.where(row < SCAN_RC - sft, pltpu.roll(c, SCAN_RC - sft, 0), 1.0)
                    d_sh = jnp.where(row < SCAN_RC - sft, pltpu.roll(d, SCAN_RC - sft, 0), 0.0)
                    d = d + c * d_sh
                    c = c * c_sh
                    sft *= 2
                g = d + c * g_next
                g_ref[rows, :] = g
                g_next, a_next = g[0:1, :], a[0:1, :]
            return g_next, a_next

        lax.fori_loop(0, nscan // SCAN_UNROLL, scan, (jnp.zeros((1, ct), F32), jnp.zeros((1, ct), F32)))

        def chunk3(i, carry):
            r0 = pl.multiple_of(i * LRU_RC, LRU_RC)
            rows = pl.ds(r0, LRU_RC)
            r, ig, xc = r_ref[rows, :].astype(F32), ig_ref[rows, :].astype(F32), xc_ref[rows, :].astype(F32)
            a, mult = _lru_decay(la_ref[rows, :])
            g = g_ref[rows, :]
            h_prev = hp_ref[pl.ds(r0, LRU_RC + LRU_PAD), :][LRU_PAD - 1:LRU_PAD - 1 + LRU_RC]
            dix = g * mult
            di = dix * xc
            dxc = dix * ig
            da = g * h_prev - (g * ig * xc) * a / mult
            dlog_a = da * a
            dr = dlog_a * (RG_LRU_C * ls)
            dra = dr * r * (1.0 - r)
            dia = di * ig * (1.0 - ig)
            xb, drab, diab = xc.astype(BF16), dra.astype(BF16), dia.astype(BF16)
            dxg = []
            for h in range(nh):
                cols = slice(128 * h, 128 * (h + 1))
                dxg.append(_nt(drab[:, cols], wa_ref[h]) + _nt(diab[:, cols], wx_ref[h]))
                dwa_ref[h] += _tn(xb[:, cols], drab[:, cols])
                dwx_ref[h] += _tn(xb[:, cols], diab[:, cols])
            dxc = dxc + jnp.concatenate(dxg, axis=1)
            dvec_ref[0:1, :] += jnp.sum(dra, axis=0, keepdims=True)
            dvec_ref[1:2, :] += jnp.sum(dia, axis=0, keepdims=True)
            dvec_ref[2:3, :] += jnp.sum(dlog_a * r, axis=0, keepdims=True) * (RG_LRU_C * _sigmoid(-lam))
            dvec_ref[3:4, :] += jnp.sum(dxc, axis=0, keepdims=True)
            dxc_ref[rows, :] = dxc
            views = _shift_views(xp_ref[pl.ds(r0, LRU_RC + LRU_PAD), :], LRU_RC, LRU_PAD)
            _conv_wgrad(views, dxc, dwacc_ref, lcw, LRU_PAD, LRU_RC)
            return carry

        lax.fori_loop(0, s // LRU_RC, chunk3, 0, unroll=2)
        for k in range(lcw):
            dcw_ref[k:k + 1, :] += jnp.sum(dwacc_ref[k], axis=0, keepdims=True)

        def chunk4(i, carry):
            r0 = pl.multiple_of(i * LRU_RC, LRU_RC)
            views = _shift_views(dxc_ref[pl.ds(r0, LRU_RC + LRU_PAD), :], LRU_RC, LRU_PAD)
            dur_ref[pl.ds(r0, LRU_RC), :] = _conv_rows_t(views, cw_ref, lcw, LRU_RC).astype(BF16)
            return carry

        lax.fori_loop(0, s // LRU_RC, chunk4, 0, unroll=2)

    blk = pl.BlockSpec((s, ct), lambda c, b: (b, c))
    acc8 = pl.BlockSpec((8, ct), lambda c, b: (0, c))
    gate_w = pl.BlockSpec((nh, 128, 128), lambda c, b: (c, 0, 0))
    return pl.pallas_call(
        body, name=name, grid=(nct, t // s),
        in_specs=act + [blk] * 6 + params,
        out_specs=[blk, blk, gate_w, gate_w, acc8, acc8],
        out_shape=[jax.ShapeDtypeStruct((t, lw), BF16), jax.ShapeDtypeStruct((t, lw), BF16),
                   jax.ShapeDtypeStruct((cfg.lh, 128, 128), F32), jax.ShapeDtypeStruct((cfg.lh, 128, 128), F32),
                   jax.ShapeDtypeStruct((8, lw), F32), jax.ShapeDtypeStruct((8, lw), F32)],
        scratch_shapes=[pltpu.VMEM((LRU_PAD + s, ct), F32), pltpu.VMEM((LRU_PAD + s, ct), F32),
                        pltpu.VMEM((s, ct), F32), pltpu.VMEM((s, ct), F32), pltpu.VMEM((s + LRU_PAD, ct), F32),
                        pltpu.VMEM((lcw, 8, ct), F32)],
        compiler_params=_cparams("arbitrary", "arbitrary"),
    )(u, u, *saved, dy, conv_w, conv_b, wa, ba, wx, bx, lam)


def local_step(cfg, x, tgt, p, shards=None):
    buckets = bucket_table(cfg)
    bias = bias_build(cfg, p["rel_bias"], buckets, "bias_build")
    ga_w, gx_w = p["gate_a_w"].astype(BF16), p["gate_x_w"].astype(BF16)
    wf = dict(p["wf"])
    dist = shards is not None

    def ffn_forward(l, k, h, nw):
        riders = gather_riders([shards[(l + 1, k)]]) if dist and l + 1 < cfg.depth else None
        outs = ffn_fwd(cfg, h, nw, wf[(l, k)], 0, 2, f"ffn{k + 1}_fwd_{l}", riders=riders)
        if riders is not None:
            wf[(l + 1, k)] = outs[4].reshape(3, -1, cfg.d)
        return outs[:4]

    def blocks(g):
        g = g if g.ndim == 3 else g[None]
        return g.reshape(g.shape[0], N_DEV, g.shape[1] // N_DEV, g.shape[2])

    h = x
    saved = []
    for l in range(cfg.depth):
        i = l // 2
        s = {"h0": h}
        s["h1"], s["xn1"], s["g1"], s["u1"] = ffn_forward(l, 0, h, p["norm_ffn1"][l][None])
        if l % 2 == 0:
            s["um"], s["xnm"] = norm_proj(cfg, s["h1"], p["norm_mix"][l][None], p["even_in"], f"mix_in_{l}", wi=i)
            attn = attn_fwd(cfg, s["um"], bias, p["attn_sinks"][i], f"attn_fwd_{l}")
            c, s["cv"] = conv_fwd(cfg, s["um"], p["conv_b_w"][i], p["conv_b_b"][i][None], p["conv_ln_g"][i][None],
                                  p["conv_ln_b"][i][None], f"conv_fwd_{l}")
            s["parts"] = [attn, c]
            s["h2"] = proj_residual(cfg, s["h1"], s["parts"], p["even_out"], f"mix_out_{l}", wi=i)
        else:
            s["um"], s["xnm"] = norm_proj(cfg, s["h1"], p["norm_mix"][l][None], p["odd_in"], f"mix_in_{l}", wi=i)
            y, *s["saved"] = lru_fwd(cfg, s["um"], p["lru_conv_w"][i], p["lru_conv_b"][i][None], ga_w[i], p["gate_a_b"][i][None],
                                 gx_w[i], p["gate_x_b"][i][None], p["lru_lambda"][i][None], f"lru_fwd_{l}")
            s["parts"] = [y]
            s["h2"] = proj_residual(cfg, s["h1"], s["parts"], p["odd_out"], f"mix_out_{l}", wi=i)
        h, s["xn2"], s["g2"], s["u2"] = ffn_forward(l, 1, s["h2"], p["norm_ffn2"][l][None])
        saved.append(s)

    loss, dh, dnf = loss_head(cfg, h, p["norm_final"][None], tgt, "loss_head")
    big = [None] * cfg.depth
    sm = {k: [None] * cfg.depth for k in ("norm_ffn1", "norm_mix", "norm_ffn2")}
    ne, no = (cfg.depth + 1) // 2, cfg.depth // 2
    for k in ("attn_sinks", "conv_b_w", "conv_b_b", "conv_ln_g", "conv_ln_b", "dbias"):
        sm[k] = [None] * ne
    for k in ("lru_conv_w", "lru_conv_b", "gate_a_w", "gate_a_b", "gate_x_w", "gate_x_b", "lru_lambda"):
        sm[k] = [None] * no
    pending = None
    for l in reversed(range(cfg.depth)):
        i = l // 2
        s = saved[l]
        riders = scatter_riders([pending]) if pending is not None else None
        outs = ffn_bwd_x(cfg, dh, s["h2"], p["norm_ffn2"][l][None], s["g2"], s["u2"], wf[(l, 1)], 0, 2,
                         f"ffn2_bwd_x_{l}", riders=riders)
        dh, dout, dg, du, dn = outs[:5]
        if riders is not None:
            big[l + 1]["f1"] = (pending, outs[5])
        sm["norm_ffn2"][l] = dn[0]
        gf2 = blocks(ffn_bwd_w(cfg, s["xn2"], dout, s["g2"], s["u2"], dg, du, f"ffn2_bwd_w_{l}")[0])
        w_out = p["even_out"] if l % 2 == 0 else p["odd_out"]
        w_in = p["even_in"] if l % 2 == 0 else p["odd_in"]
        dcat = proj_bwd_act(cfg, dh, w_out, f"mix_out_bwd_{l}", wi=i)
        g_out = blocks(grad_weight(cfg, s["parts"], dh, f"mix_out_gw_{l}"))
        if l % 2 == 0:
            du_a, sm["dbias"][i], dsink = attn_bwd(cfg, s["um"], dcat, bias, p["attn_sinks"][i], f"attn_bwd_{l}")
            du_c, dcw, dvec = conv_bwd(cfg, s["um"], s["cv"], dcat, p["conv_b_w"][i], p["conv_b_b"][i][None],
                                       p["conv_ln_g"][i][None], p["conv_ln_b"][i][None], f"conv_bwd_{l}")
            sm["attn_sinks"][i] = dsink[:cfg.hq, 0]
            sm["conv_b_w"][i] = dcw[:cfg.cw]
            sm["conv_b_b"][i], sm["conv_ln_g"][i], sm["conv_ln_b"][i] = dvec[0], dvec[1], dvec[2]
            dparts = [du_a, du_c]
        else:
            dug, dur, dwa, dwx, dvec, dcw = lru_bwd(
                cfg, s["um"], s["saved"], dcat, p["lru_conv_w"][i], p["lru_conv_b"][i][None], ga_w[i], p["gate_a_b"][i][None],
                gx_w[i], p["gate_x_b"][i][None], p["lru_lambda"][i][None], f"lru_bwd_{l}")
            sm["gate_a_w"][i], sm["gate_x_w"][i] = dwa, dwx
            sm["gate_a_b"][i], sm["gate_x_b"][i], sm["lru_lambda"][i], sm["lru_conv_b"][i] = dvec[0], dvec[1], dvec[2], dvec[3]
            sm["lru_conv_w"][i] = dcw[:cfg.lcw]
            dparts = [dug, dur]
        g_in = blocks(grad_weight(cfg, dparts, s["xnm"], f"mix_in_gw_{l}"))
        dh, dn = norm_proj_bwd(cfg, dh, s["h1"], p["norm_mix"][l][None], dparts, w_in, f"mix_in_bwd_{l}", wi=i)
        sm["norm_mix"][l] = dn[0]
        riders = scatter_riders([gf2]) if dist else None
        outs = ffn_bwd_x(cfg, dh, s["h0"], p["norm_ffn1"][l][None], s["g1"], s["u1"], wf[(l, 0)], 0, 2,
                         f"ffn1_bwd_x_{l}", riders=riders)
        dh, dout, dg, du, dn = outs[:5]
        sm["norm_ffn1"][l] = dn[0]
        riders = scatter_riders([g_in, g_out]) if dist else None
        gouts = ffn_bwd_w(cfg, s["xn1"], dout, s["g1"], s["u1"], dg, du, f"ffn1_bwd_w_{l}", riders=riders)
        gf1 = blocks(gouts[0])
        big[l] = {"f1": (gf1, None), "f2": (gf2, outs[5] if dist else None),
                  "in": (g_in, gouts[1] if dist else None), "out": (g_out, gouts[2] if dist else None)}
        pending = gf1 if dist and l > 0 else None
    drb = bias_grad(cfg, jnp.stack(sm.pop("dbias")), buckets, "bias_grad")
    small = {k: jnp.stack(v) for k, v in sm.items()}
    small["rel_bias"] = drb[:, :, 0].T
    small["norm_final"] = dnf[0]
    return loss, dh, big, small


MESH = pl.DeviceIdType.MESH
ANY = pl.BlockSpec(memory_space=pl.ANY)


def _place():
    return lax.axis_index("x"), lax.axis_index("y"), lax.axis_index("c")


FLIPS = ((0, 0, 1), (0, 1, 0), (0, 1, 1), (1, 0, 0), (1, 0, 1), (1, 1, 0), (1, 1, 1))


def _peer(place, flip):
    return tuple(1 - v if f else v for v, f in zip(place, flip))


def _dev_index(place):
    return 4 * place[0] + 2 * place[1] + place[2]


def _remote(src, dst, send_sems, recv_sems, g, k, peer):
    return pltpu.make_async_remote_copy(src_ref=src, dst_ref=dst, send_sem=send_sems.at[g, k], recv_sem=recv_sems.at[g, k],
                                        device_id=peer, device_id_type=MESH)


def gather_riders(srcs):
    ng = len(srcs)

    def copies(in_refs, out_refs, sems):
        send_sems, recv_sems, local_sems = sems
        me = _place()
        local, sends, recvs = [], [], []
        for g in range(ng):
            mine = out_refs[g].at[:, _dev_index(me)]
            local.append(pltpu.make_async_copy(in_refs[g], mine, local_sems.at[g]))
            for k, flip in enumerate(FLIPS):
                peer = _peer(me, flip)
                sends.append(_remote(in_refs[g], mine, send_sems, recv_sems, g, k, peer))
                recvs.append(_remote(in_refs[g], out_refs[g].at[:, _dev_index(peer)], send_sems, recv_sems, g, k, peer))
        return local, sends, recvs

    def start(in_refs, out_refs, sems):
        local, sends, _ = copies(in_refs, out_refs, sems)
        for cp in local + sends:
            cp.start()

    def wait(in_refs, out_refs, sems):
        local, sends, recvs = copies(in_refs, out_refs, sems)
        for cp in sends:
            cp.wait_send()
        for cp in recvs:
            cp.wait_recv()
        for cp in local:
            cp.wait()

    return Riders(tuple(srcs), tuple(jax.ShapeDtypeStruct((s.shape[0], N_DEV) + s.shape[1:], s.dtype) for s in srcs),
                  (pltpu.SemaphoreType.DMA((ng, 7)), pltpu.SemaphoreType.DMA((ng, 7)), pltpu.SemaphoreType.DMA((ng,))),
                  start, wait)


def scatter_riders(bufs):
    ng = len(bufs)

    def copies(in_refs, out_refs, sems):
        send_sems, recv_sems = sems
        me = _place()
        return [_remote(in_refs[g].at[:, _dev_index(_peer(me, flip))], out_refs[g].at[:, k], send_sems, recv_sems, g, k,
                        _peer(me, flip)) for g in range(ng) for k, flip in enumerate(FLIPS)]

    def start(in_refs, out_refs, sems):
        for cp in copies(in_refs, out_refs, sems):
            cp.start()

    def wait(in_refs, out_refs, sems):
        for cp in copies(in_refs, out_refs, sems):
            cp.wait()

    return Riders(tuple(bufs), tuple(jax.ShapeDtypeStruct((b.shape[0], 7) + b.shape[2:], b.dtype) for b in bufs),
                  (pltpu.SemaphoreType.DMA((ng, 7)), pltpu.SemaphoreType.DMA((ng, 7))), start, wait)


def shard_sum(buf, recv, dev, name):
    n, _, r, cdim = buf.shape

    def body(dev_ref, a_ref, b_ref, o_ref):
        acc = a_ref[...].astype(F32)
        for k in range(7):
            acc = acc + b_ref[k].astype(F32)
        o_ref[...] = acc

    return pl.pallas_call(
        body, name=name,
        grid_spec=pltpu.PrefetchScalarGridSpec(
            num_scalar_prefetch=1, grid=(n,),
            in_specs=[pl.BlockSpec((None, None, r, cdim), lambda i, dev_ref: (i, dev_ref[0], 0, 0)),
                      pl.BlockSpec((None, 7, r, cdim), lambda i, dev_ref: (i, 0, 0, 0))],
            out_specs=pl.BlockSpec((None, r, cdim), lambda i, dev_ref: (i, 0, 0))),
        out_shape=jax.ShapeDtypeStruct((n, r, cdim), F32),
    )(dev, buf, recv)


def all_gather(srcs, name):
    ng = len(srcs)

    def body(*refs):
        x_refs, o_refs = refs[:ng], refs[ng:2 * ng]
        send_sems, recv_sems, local_sems = refs[2 * ng:]
        x, y, c = _place()
        me, sibling = (x, y, c), (x, y, 1 - c)
        chips = [(1 - x, y), (x, 1 - y), (1 - x, 1 - y)]

        def copy(gi, k, block, to, src=None):
            dst = o_refs[gi].at[:, 4 * block[0] + 2 * block[1] + block[2]]
            return pltpu.make_async_remote_copy(
                src_ref=dst if src is None else src, dst_ref=dst, send_sem=send_sems.at[gi, k],
                recv_sem=recv_sems.at[gi, k], device_id=to, device_id_type=MESH)

        mine = [pltpu.make_async_copy(x_refs[gi], o_refs[gi].at[:, 4 * x + 2 * y + c], local_sems.at[gi])
                for gi in range(ng)]
        for cp in mine:
            cp.start()
        first = []
        for gi in range(ng):
            first.append(copy(gi, 0, me, sibling, src=x_refs[gi]))
            first += [copy(gi, 1 + j, me, (*chip, c), src=x_refs[gi]) for j, chip in enumerate(chips)]
        for cp in first:
            cp.start()
        passed = []
        for j, chip in enumerate(chips):
            for gi in range(ng):
                copy(gi, 1 + j, (*chip, c), me).wait_recv()
                cp = copy(gi, 4 + j, (*chip, c), sibling)
                cp.start()
                passed.append(cp)
        for gi in range(ng):
            copy(gi, 0, sibling, me).wait_recv()
            for j, chip in enumerate(chips):
                copy(gi, 4 + j, (*chip, 1 - c), me).wait_recv()
        for cp in first + passed:
            cp.wait_send()
        for cp in mine:
            cp.wait()

    return pl.pallas_call(
        body, name=name,
        in_specs=[ANY] * ng, out_specs=[ANY] * ng,
        out_shape=[jax.ShapeDtypeStruct((s.shape[0], N_DEV) + s.shape[1:], s.dtype) for s in srcs],
        scratch_shapes=[pltpu.SemaphoreType.DMA((ng, 7)), pltpu.SemaphoreType.DMA((ng, 7)),
                        pltpu.SemaphoreType.DMA((ng,))],
    )(*srcs)


def pair_exchange(bufs, name):
    ng = len(bufs)

    def body(*refs):
        b_refs, o_refs = refs[:ng], refs[ng:2 * ng]
        send_sems, recv_sems = refs[2 * ng:]
        x, y, c = _place()
        copies = [pltpu.make_async_remote_copy(
            src_ref=b_refs[gi].at[:, :, 1 - c], dst_ref=o_refs[gi], send_sem=send_sems.at[gi], recv_sem=recv_sems.at[gi],
            device_id=(x, y, 1 - c), device_id_type=MESH) for gi in range(ng)]
        for cp in copies:
            cp.start()
        for cp in copies:
            cp.wait()

    return pl.pallas_call(
        body, name=name,
        in_specs=[ANY] * ng, out_specs=[ANY] * ng,
        out_shape=[jax.ShapeDtypeStruct(b.shape[:2] + b.shape[3:], b.dtype) for b in bufs],
        scratch_shapes=[pltpu.SemaphoreType.DMA((ng,)), pltpu.SemaphoreType.DMA((ng,))],
    )(*bufs)


def chip_exchange(qs, name):
    ng = len(qs)

    def body(*refs):
        q_refs, o_refs = refs[:ng], refs[ng:2 * ng]
        send_sems, recv_sems = refs[2 * ng:]
        x, y, c = _place()
        chips = [(1 - x, y), (x, 1 - y), (1 - x, 1 - y)]
        copies = [pltpu.make_async_remote_copy(
            src_ref=q_refs[gi].at[:, 2 * chip[0] + chip[1]], dst_ref=o_refs[gi].at[:, j],
            send_sem=send_sems.at[gi, j], recv_sem=recv_sems.at[gi, j],
            device_id=(*chip, c), device_id_type=MESH) for gi in range(ng) for j, chip in enumerate(chips)]
        for cp in copies:
            cp.start()
        for cp in copies:
            cp.wait()

    return pl.pallas_call(
        body, name=name,
        in_specs=[ANY] * ng, out_specs=[ANY] * ng,
        out_shape=[jax.ShapeDtypeStruct((q.shape[0], 3) + q.shape[2:], q.dtype) for q in qs],
        scratch_shapes=[pltpu.SemaphoreType.DMA((ng, 3)), pltpu.SemaphoreType.DMA((ng, 3))],
    )(*qs)


def pair_sum(buf, recv, core, name):
    n, _, _, r, cdim = buf.shape

    def body(core_ref, a_ref, b_ref, o_ref):
        o_ref[...] = (a_ref[...].astype(F32) + b_ref[...].astype(F32)).astype(BF16)

    blk = pl.BlockSpec((None, None, r, cdim), lambda i, k, core_ref: (i, k, 0, 0))
    return pl.pallas_call(
        body, name=name,
        grid_spec=pltpu.PrefetchScalarGridSpec(
            num_scalar_prefetch=1, grid=(n, 4),
            in_specs=[pl.BlockSpec((None, None, None, r, cdim), lambda i, k, core_ref: (i, k, core_ref[0], 0, 0)), blk],
            out_specs=blk),
        out_shape=jax.ShapeDtypeStruct((n, 4, r, cdim), BF16),
    )(core, buf, recv)


def chip_sum(q, recv, chip, name):
    n, _, r, cdim = q.shape

    def body(chip_ref, a_ref, b_ref, o_ref):
        acc = a_ref[...].astype(F32)
        for j in range(3):
            acc = acc + b_ref[j].astype(F32)
        o_ref[...] = acc

    return pl.pallas_call(
        body, name=name,
        grid_spec=pltpu.PrefetchScalarGridSpec(
            num_scalar_prefetch=1, grid=(n,),
            in_specs=[pl.BlockSpec((None, None, r, cdim), lambda i, chip_ref: (i, chip_ref[0], 0, 0)),
                      pl.BlockSpec((None, 3, r, cdim), lambda i, chip_ref: (i, 0, 0, 0))],
            out_specs=pl.BlockSpec((None, r, cdim), lambda i, chip_ref: (i, 0, 0))),
        out_shape=jax.ShapeDtypeStruct((n, r, cdim), F32),
    )(chip, q, recv)


def sum_blocks(a, name):
    def body(a_ref, o_ref):
        acc = a_ref[0]
        for d in range(1, a.shape[0]):
            acc = acc + a_ref[d]
        o_ref[...] = acc

    return pl.pallas_call(body, name=name, out_shape=jax.ShapeDtypeStruct(a.shape[1:], F32),
                          compiler_params=pltpu.CompilerParams(vmem_limit_bytes=VMEM_LIMIT))(a)


def adamw(w, g, m, v, name):
    c1 = 1.0 / (1.0 - ADAM_B1 ** ADAM_STEP)
    c2 = 1.0 / (1.0 - ADAM_B2 ** ADAM_STEP)

    def body(w_ref, g_ref, m_ref, v_ref, d_ref, mo_ref, vo_ref):
        gg = g_ref[...]
        m2 = ADAM_B1 * m_ref[...] + (1.0 - ADAM_B1) * gg
        v2 = ADAM_B2 * v_ref[...] + (1.0 - ADAM_B2) * (gg * gg)
        mo_ref[...] = m2
        vo_ref[...] = v2
        d_ref[...] = -ADAM_LR * ((m2 * c1) / (jnp.sqrt(v2 * c2) + ADAM_EPS) + ADAM_WD * w_ref[...])

    out_shape = [jax.ShapeDtypeStruct(w.shape, F32)] * 3
    if w.ndim == 2:
        return pl.pallas_call(body, name=name, out_shape=out_shape,
                              compiler_params=pltpu.CompilerParams(vmem_limit_bytes=VMEM_LIMIT))(w, g, m, v)
    blk = pl.BlockSpec((None,) + w.shape[1:], lambda i: (i, 0, 0))
    return pl.pallas_call(body, name=name, grid=(w.shape[0],), in_specs=[blk] * 4, out_specs=[blk] * 3,
                          out_shape=out_shape, compiler_params=_cparams("arbitrary"))(w, g, m, v)


WEIGHTS = ("norm_ffn1", "ffn1_wg", "ffn1_wu", "ffn1_wd", "norm_mix", "norm_ffn2", "ffn2_wg", "ffn2_wu", "ffn2_wd",
           "rel_bias", "even_w_in", "attn_sinks", "conv_b_w", "conv_b_b", "conv_ln_g", "conv_ln_b", "even_w_out",
           "odd_w_in", "lru_conv_w", "lru_conv_b", "gate_a_w", "gate_a_b", "gate_x_w", "gate_x_b", "lru_lambda",
           "odd_w_out", "norm_final")
BIG = ("ffn1_wg", "ffn1_wu", "ffn1_wd", "ffn2_wg", "ffn2_wu", "ffn2_wd", "even_w_in", "even_w_out", "odd_w_in", "odd_w_out")
SMALL = tuple(n for n in WEIGHTS if n not in BIG)
SMALL_SHARDED = ("conv_b_w", "lru_conv_w", "lru_conv_b", "gate_a_b", "gate_x_b", "lru_lambda")
PACK_ALIGN = 1024


def _pack(arrays):
    parts = []
    for a in arrays:
        flat = a.reshape(-1)
        parts.append(jnp.pad(flat, (0, -flat.shape[0] % PACK_ALIGN)))
    return jnp.concatenate(parts).reshape(-1, 128)


def _unpack(packed, shapes, lead=()):
    out, row = [], 0
    for shp in shapes:
        size = math.prod(shp)
        nrows = (size + (-size % PACK_ALIGN)) // 128
        part = packed[..., row:row + nrows, :].reshape(lead + (nrows * 128,))
        out.append(part[..., :size].reshape(lead + tuple(shp)))
        row += nrows
    return out


def _unshard_last(blocks):
    nd = blocks.ndim
    moved = jnp.moveaxis(blocks, 0, nd - 2)
    return moved.reshape(moved.shape[:-2] + (-1,))


def _step(cfg, x, weights, loss_target, ms, vs):
    w = dict(zip(WEIGHTS, weights))
    m = dict(zip(WEIGHTS, ms))
    v = dict(zip(WEIGHTS, vs))
    px, py, pc = _place()
    dev = 4 * px + 2 * py + pc
    core = jnp.reshape(pc, (1,)).astype(jnp.int32)
    chip = jnp.reshape(2 * px + py, (1,)).astype(jnp.int32)
    d = cfg.d
    t = cfg.bl * cfg.s

    def rows(name):
        a = w[name]
        return (a if name.endswith(("wd", "w_out")) else a.transpose(0, 2, 1)).astype(BF16)

    r3 = {n: rows(n) for n in BIG[:6]}
    shards = {(l, k): jnp.stack([r3[f"ffn{k + 1}_{mat}"][l] for mat in ("wg", "wu", "wd")])
              for l in range(cfg.depth) for k in range(2)}
    small_src = _pack([w[n] for n in SMALL_SHARDED])[None]
    gathered = all_gather([shards.pop((0, 0)), shards.pop((0, 1)), rows("even_w_in"), rows("even_w_out"), rows("odd_w_in"),
                           rows("odd_w_out"), small_src], "all_gather_weights")
    full = [g.reshape(g.shape[0], -1, g.shape[-1]) for g in gathered[:6]]
    p = {n: w[n] for n in SMALL if n not in SMALL_SHARDED}
    p.update(wf={(0, 0): full[0], (0, 1): full[1]}, even_in=full[2], even_out=full[3], odd_in=full[4], odd_out=full[5])
    for n, blocks in zip(SMALL_SHARDED, _unpack(gathered[6][0], [w[n].shape for n in SMALL_SHARDED], lead=(N_DEV,))):
        p[n] = _unshard_last(blocks)

    lossp, gx, big, small = local_step(cfg, x.reshape(t, d), loss_target.reshape(t, d), p, shards)
    loss = lax.psum(lossp[0, 0], ("x", "y", "c"))

    dev1 = jnp.reshape(dev, (1,)).astype(jnp.int32)
    shard_rows = [{} for _ in range(cfg.depth)]
    for l in range(cfg.depth):
        for key, (buf, recv) in big[l].items():
            if recv is not None:
                shard_rows[l][key] = shard_sum(buf, recv, dev1, f"rs_sum_{key}_{l}")
    left = [(l, key, buf) for l in range(cfg.depth) for key, (buf, recv) in big[l].items() if recv is None]
    bufs = [buf.reshape(buf.shape[0], 4, 2, buf.shape[2], d) for _, _, buf in left]
    recv = pair_exchange(bufs, "rs_pair")
    qs = [pair_sum(b, r, core, f"rs_pair_sum{j}") for j, (b, r) in enumerate(zip(bufs, recv))]
    recv = chip_exchange(qs, "rs_chip")
    for j, ((l, key, _), q, r) in enumerate(zip(left, qs, recv)):
        shard_rows[l][key] = chip_sum(q, r, chip, f"rs_chip_sum{j}")

    g_rows = {}
    for k in range(2):
        ffn_g = jnp.stack([shard_rows[l][f"f{k + 1}"] for l in range(cfg.depth)])
        for j, mat in enumerate(("wg", "wu", "wd")):
            g_rows[f"ffn{k + 1}_{mat}"] = ffn_g[:, j]
    g_rows["even_w_in"] = jnp.stack([shard_rows[l]["in"][0] for l in range(0, cfg.depth, 2)])
    g_rows["even_w_out"] = jnp.stack([shard_rows[l]["out"][0] for l in range(0, cfg.depth, 2)])
    g_rows["odd_w_in"] = jnp.stack([shard_rows[l]["in"][0] for l in range(1, cfg.depth, 2)])
    g_rows["odd_w_out"] = jnp.stack([shard_rows[l]["out"][0] for l in range(1, cfg.depth, 2)])
    grads = {}

    full_shapes = [small[n].shape for n in SMALL]
    parts = all_gather([_pack([small[n] for n in SMALL])[None]], "all_gather_small_grads")[0][0]
    for n, g in zip(SMALL, _unpack(sum_blocks(parts, "sum_small_grads"), full_shapes)):
        if n in SMALL_SHARDED:
            width = w[n].shape[-1]
            g = lax.dynamic_slice_in_dim(g, dev * width, width, axis=g.ndim - 1)
        grads[n] = g

    delta, new_m, new_v = {}, {}, {}
    for n in BIG:
        if n.endswith(("wd", "w_out")):
            grads[n] = g_rows[n]
            delta[n], new_m[n], new_v[n] = adamw(w[n], grads[n], m[n], v[n], f"adamw_{n}")
        elif w[n].shape[-1] % 128 == 0:
            grads[n] = g_rows[n].transpose(0, 2, 1)
            delta[n], new_m[n], new_v[n] = adamw(w[n], grads[n], m[n], v[n], f"adamw_{n}")
        else:
            outs = adamw(w[n].transpose(0, 2, 1), g_rows[n], m[n].transpose(0, 2, 1), v[n].transpose(0, 2, 1), f"adamw_{n}")
            delta[n], new_m[n], new_v[n] = [o.transpose(0, 2, 1) for o in outs]
            grads[n] = g_rows[n].transpose(0, 2, 1)
    shapes = [w[n].shape for n in SMALL]
    packed = adamw(*[_pack([src[n] for n in SMALL]) for src in (w, grads, m, v)], "adamw_small")
    for out, pk in zip((delta, new_m, new_v), packed):
        out.update(zip(SMALL, _unpack(pk, shapes)))

    return (loss, gx.reshape(x.shape), *[grads[n] for n in WEIGHTS], *[delta[n] for n in WEIGHTS],
            *[new_m[n] for n in WEIGHTS], *[new_v[n] for n in WEIGHTS])


def kernel(x, norm_ffn1, ffn1_wg, ffn1_wu, ffn1_wd, norm_mix, norm_ffn2, ffn2_wg, ffn2_wu, ffn2_wd, rel_bias, even_w_in, attn_sinks, conv_b_w, conv_b_b, conv_ln_g, conv_ln_b, even_w_out, odd_w_in, lru_conv_w, lru_conv_b, gate_a_w, gate_a_b, gate_x_w, gate_x_b, lru_lambda, odd_w_out, norm_final, loss_target, m_norm_ffn1, m_ffn1_wg, m_ffn1_wu, m_ffn1_wd, m_norm_mix, m_norm_ffn2, m_ffn2_wg, m_ffn2_wu, m_ffn2_wd, m_rel_bias, m_even_w_in, m_attn_sinks, m_conv_b_w, m_conv_b_b, m_conv_ln_g, m_conv_ln_b, m_even_w_out, m_odd_w_in, m_lru_conv_w, m_lru_conv_b, m_gate_a_w, m_gate_a_b, m_gate_x_w, m_gate_x_b, m_lru_lambda, m_odd_w_out, m_norm_final, v_norm_ffn1, v_ffn1_wg, v_ffn1_wu, v_ffn1_wd, v_norm_mix, v_norm_ffn2, v_ffn2_wg, v_ffn2_wu, v_ffn2_wd, v_rel_bias, v_even_w_in, v_attn_sinks, v_conv_b_w, v_conv_b_b, v_conv_ln_g, v_conv_ln_b, v_even_w_out, v_odd_w_in, v_lru_conv_w, v_lru_conv_b, v_gate_a_w, v_gate_a_b, v_gate_x_w, v_gate_x_b, v_lru_lambda, v_odd_w_out, v_norm_final):
    weights = (norm_ffn1, ffn1_wg, ffn1_wu, ffn1_wd, norm_mix, norm_ffn2, ffn2_wg, ffn2_wu, ffn2_wd, rel_bias, even_w_in, attn_sinks, conv_b_w, conv_b_b, conv_ln_g, conv_ln_b, even_w_out, odd_w_in, lru_conv_w, lru_conv_b, gate_a_w, gate_a_b, gate_x_w, gate_x_b, lru_lambda, odd_w_out, norm_final)
    ms = (m_norm_ffn1, m_ffn1_wg, m_ffn1_wu, m_ffn1_wd, m_norm_mix, m_norm_ffn2, m_ffn2_wg, m_ffn2_wu, m_ffn2_wd, m_rel_bias, m_even_w_in, m_attn_sinks, m_conv_b_w, m_conv_b_b, m_conv_ln_g, m_conv_ln_b, m_even_w_out, m_odd_w_in, m_lru_conv_w, m_lru_conv_b, m_gate_a_w, m_gate_a_b, m_gate_x_w, m_gate_x_b, m_lru_lambda, m_odd_w_out, m_norm_final)
    vs = (v_norm_ffn1, v_ffn1_wg, v_ffn1_wu, v_ffn1_wd, v_norm_mix, v_norm_ffn2, v_ffn2_wg, v_ffn2_wu, v_ffn2_wd, v_rel_bias, v_even_w_in, v_attn_sinks, v_conv_b_w, v_conv_b_b, v_conv_ln_g, v_conv_ln_b, v_even_w_out, v_odd_w_in, v_lru_conv_w, v_lru_conv_b, v_gate_a_w, v_gate_a_b, v_gate_x_w, v_gate_x_b, v_lru_lambda, v_odd_w_out, v_norm_final)
    return _step(Cfg(), x, weights, loss_target, ms, vs)
```
